```python
import jax, jax.numpy as jnp
from jax import lax
import numpy as np

D_MODEL = 1024
BATCH = 2
SEQ = 8192
DEPTH = 2
DEC_BATCH = 32
DEC_SEQ = 32
PAST_LEN = 1024

CHUNK = 64
EPS = 1e-6
NEG_INF = -1e30
ROPE_BASE = 10000.0
Q_BLOCK = 128
RET_HEADS = 4
RET_DK = 64
RET_DV = 64
RET_W = RET_HEADS * RET_DV
MLA_HEADS = 8
MLA_NOPE = 64
MLA_ROPE = 32
MLA_V = 64
MLA_QK = MLA_NOPE + MLA_ROPE
Q_LORA = 256
KV_LORA = 128
MLA_W = MLA_HEADS * MLA_V
BAND_HEADS = 4
BAND_DH = 64
BAND_W = BAND_HEADS * BAND_DH
BAND_PREV_CHUNKS = 8
BAND_PAST = BAND_PREV_CHUNKS * CHUNK
BAND_KEYS = BAND_PAST + CHUNK
MAX_REL = 128
N_REL = 2 * MAX_REL + 1
D_MIX = RET_W + MLA_W + BAND_W
_SEG = (RET_HEADS * RET_DK, RET_HEADS * RET_DK, RET_W, RET_W,
        Q_LORA, KV_LORA, MLA_ROPE, MLA_W,
        BAND_W, BAND_W, BAND_W, BAND_W)
D_IN = sum(_SEG)

kernel_name = 'hybrid_retention_mla_chunkband_stream_step'


def _rmsnorm(x, g):
    xf = x.astype(jnp.float32)
    y = xf * lax.rsqrt(jnp.mean(xf * xf, axis=-1, keepdims=True) + EPS)
    return (y * g.astype(jnp.float32)).astype(x.dtype)


def _rope(x, pos):
    half = x.shape[-1] // 2
    inv = ROPE_BASE ** (-jnp.arange(half, dtype=jnp.float32) / half)
    ang = pos.astype(jnp.float32)[:, None] * inv[None, :]
    cos = jnp.cos(ang)[None, :, None, :]
    sin = jnp.sin(ang)[None, :, None, :]
    xf = x.astype(jnp.float32)
    x1, x2 = xf[..., :half], xf[..., half:]
    return jnp.concatenate([x1 * cos - x2 * sin, x1 * sin + x2 * cos], axis=-1).astype(x.dtype)


def _project(x, norm_g, w_in):
    h = _rmsnorm(x, norm_g)
    z = jnp.einsum('btd,de->bte', h, w_in)
    cuts, acc = [], 0
    for s in _SEG[:-1]:
        acc += s
        cuts.append(acc)
    return jnp.split(z, cuts, axis=-1)


def _ret_log_gamma():
    return jnp.log1p(-jnp.exp2(-5.0 - jnp.arange(RET_HEADS, dtype=jnp.float32)))


def _retention_block(q, k, v, S):
    q = q.astype(jnp.float32)
    k = k.astype(jnp.float32)
    v = v.astype(jnp.float32)
    S = S.astype(jnp.float32)
    L = q.shape[1]
    lg = _ret_log_gamma()
    idx = jnp.arange(L, dtype=jnp.float32)
    diff = idx[:, None] - idx[None, :]
    dmask = jnp.where(diff >= 0, jnp.exp(lg[:, None, None] * jnp.maximum(diff, 0.0)), 0.0)
    inner = jnp.einsum('blhd,bmhd->bhlm', q, k) * dmask[None]
    o = jnp.einsum('bhlm,bmhe->blhe', inner, v)
    xi = jnp.exp(lg[None, :] * (idx[:, None] + 1.0))
    o = o + jnp.einsum('blhd,bhde->blhe', q, S) * xi[None, :, :, None]
    zeta = jnp.exp(lg[None, :] * (L - 1.0 - idx)[:, None])
    S_new = (jnp.exp(lg * L)[None, :, None, None] * S
             + jnp.einsum('blhd,blhe->bhde', k * zeta[None, :, :, None], v))
    return o, S_new


def _retention_prompt(q, k, v):
    B, S, H, dk = q.shape
    nc = S // CHUNK

    def to_chunks(t):
        return t.reshape(B, nc, CHUNK, H, t.shape[-1]).swapaxes(0, 1)

    S0 = jnp.zeros((B, H, dk, v.shape[-1]), jnp.float32)

    def step(state, qkv):
        o, state = _retention_block(qkv[0], qkv[1], qkv[2], state)
        return state, o

    S_fin, o = lax.scan(step, S0, (to_chunks(q), to_chunks(k), to_chunks(v)))
    return o.swapaxes(0, 1).reshape(B, S, H, v.shape[-1]), S_fin


def _mla_prompt_attn(q, k, v):
    B, S, H, dq = q.shape
    nb = S // Q_BLOCK
    scale = dq ** -0.5
    key_chunk = jnp.arange(S) // CHUNK
    qb = q.reshape(B, nb, Q_BLOCK, H, dq).swapaxes(0, 1)

    def one_block(args):
        q_blk, b_idx = args
        q_chunk = (b_idx * Q_BLOCK + jnp.arange(Q_BLOCK)) // CHUNK
        s = jnp.einsum('bqhd,bkhd->bhqk', q_blk, k).astype(jnp.float32) * scale
        s = jnp.where(key_chunk[None, :] <= q_chunk[:, None], s, NEG_INF)
        p = jax.nn.softmax(s, axis=-1).astype(v.dtype)
        return jnp.einsum('bhqk,bkhd->bqhd', p, v)

    o = lax.map(one_block, (qb, jnp.arange(nb)))
    return o.swapaxes(0, 1).reshape(B, S, H, v.shape[-1])


def _dense_attn(q, k, v, bias):
    s = jnp.einsum('bqhd,bkhd->bhqk', q, k).astype(jnp.float32) * (q.shape[-1] ** -0.5)
    if bias is not None:
        s = s + bias[None].astype(jnp.float32)
    p = jax.nn.softmax(s, axis=-1).astype(v.dtype)
    return jnp.einsum('bhqk,bkhd->bqhd', p, v)


def _band_prompt_attn(q, k, v, band_bias):
    B, S, H, d = q.shape
    nc = S // CHUNK

    def gather_band(t):
        tc = t.reshape(B, nc, CHUNK, H, d)
        tp = jnp.pad(tc, ((0, 0), (BAND_PREV_CHUNKS, 0), (0, 0), (0, 0), (0, 0)))
        return jnp.concatenate([tp[:, j:j + nc] for j in range(BAND_PREV_CHUNKS + 1)], axis=2)

    kb, vb = gather_band(k), gather_band(v)
    qc = q.reshape(B, nc, CHUNK, H, d)
    s = jnp.einsum('bcqhd,bckhd->bchqk', qc, kb).astype(jnp.float32) * (d ** -0.5)
    qi = jnp.arange(CHUNK)
    kj = jnp.arange(BAND_KEYS)
    dist = qi[:, None] + BAND_PAST - kj[None, :]
    bias = band_bias[:, jnp.clip(dist, -MAX_REL, MAX_REL) + MAX_REL]
    s = s + bias[None, None].astype(jnp.float32)
    key_chunk = jnp.arange(nc)[:, None] - BAND_PREV_CHUNKS + kj[None, :] // CHUNK
    s = jnp.where((key_chunk >= 0)[None, :, None, None, :], s, NEG_INF)
    p = jax.nn.softmax(s, axis=-1).astype(v.dtype)
    o = jnp.einsum('bchqk,bckhd->bcqhd', p, vb)
    return o.reshape(B, S, H, d)


def _layer(x, pos, norm_g, w_in, ret_gn_g, mla_qa_g, mla_w_uq, mla_qn_g, mla_qr_g,
           mla_kva_g, mla_kr_g, mla_w_ukv, mla_kn_g, band_qn_g, band_kn_g, band_bias,
           w_out, past):
    B, L, _ = x.shape
    (a_q, a_k, a_v, a_g, b_cq, b_ckv, b_kr, b_g,
     c_q, c_k, c_v, c_g) = _project(x, norm_g, w_in)

    q = _rope(a_q.reshape(B, L, RET_HEADS, RET_DK), pos)
    k = _rope(a_k.reshape(B, L, RET_HEADS, RET_DK), pos) * (RET_DK ** -0.5)
    v = a_v.reshape(B, L, RET_HEADS, RET_DV)
    if past is None:
        ret_o, ret_s = _retention_prompt(q, k, v)
    else:
        ret_o, ret_s = _retention_block(q, k, v, past[0])
    ret_o = _rmsnorm(ret_o, ret_gn_g.reshape(RET_HEADS, RET_DV)).astype(x.dtype).reshape(B, L, RET_W)

    cq = _rmsnorm(b_cq, mla_qa_g)
    qf = jnp.einsum('btr,re->bte', cq, mla_w_uq).reshape(B, L, MLA_HEADS, MLA_QK)
    q_m = jnp.concatenate([_rmsnorm(qf[..., :MLA_NOPE], mla_qn_g),
                           _rope(_rmsnorm(qf[..., MLA_NOPE:], mla_qr_g), pos)], axis=-1)
    ckv_new = _rmsnorm(b_ckv, mla_kva_g)
    kr_new = _rope(_rmsnorm(b_kr, mla_kr_g)[:, :, None, :], pos)[:, :, 0, :]
    if past is None:
        ckv_all, kr_all = ckv_new, kr_new
    else:
        ckv_all = jnp.concatenate([past[1], ckv_new], axis=1)
        kr_all = jnp.concatenate([past[2], kr_new], axis=1)
    T = ckv_all.shape[1]
    kv = jnp.einsum('btc,ce->bte', ckv_all, mla_w_ukv).reshape(B, T, MLA_HEADS, MLA_NOPE + MLA_V)
    k_m = jnp.concatenate([_rmsnorm(kv[..., :MLA_NOPE], mla_kn_g),
                           jnp.broadcast_to(kr_all[:, :, None, :], (B, T, MLA_HEADS, MLA_ROPE))], axis=-1)
    v_m = kv[..., MLA_NOPE:]
    if past is None:
        mla_o = _mla_prompt_attn(q_m, k_m, v_m)
    else:
        mla_o = _dense_attn(q_m, k_m, v_m, None)
    mla_o = mla_o.reshape(B, L, MLA_W)

    qb = _rmsnorm(c_q.reshape(B, L, BAND_HEADS, BAND_DH), band_qn_g)
    kb = _rmsnorm(c_k.reshape(B, L, BAND_HEADS, BAND_DH), band_kn_g)
    vb = c_v.reshape(B, L, BAND_HEADS, BAND_DH)
    if past is None:
        band_o = _band_prompt_attn(qb, kb, vb, band_bias)
        n_keep = min(BAND_PAST, L)
        band_k_state, band_v_state = kb[:, L - n_keep:], vb[:, L - n_keep:]
    else:
        n_c = past[3].shape[1]
        k_all = jnp.concatenate([past[3], kb], axis=1)
        v_all = jnp.concatenate([past[4], vb], axis=1)
        kpos = jnp.concatenate([pos[0] - n_c + jnp.arange(n_c), pos])
        dist = pos[:, None] - kpos[None, :]
        bias = band_bias[:, jnp.clip(dist, -MAX_REL, MAX_REL) + MAX_REL]
        band_o = _dense_attn(qb, k_all, v_all, bias)
        band_k_state, band_v_state = kb, vb
    band_o = band_o.reshape(B, L, BAND_W)

    mix = jnp.concatenate([ret_o * jax.nn.silu(a_g),
                           mla_o * jax.nn.silu(b_g),
                           band_o * jax.nn.silu(c_g)], axis=-1)
    y = x + jnp.einsum('bte,ed->btd', mix, w_out)
    return y, (ret_s.astype(x.dtype), ckv_new, kr_new, band_k_state, band_v_state)


def setup_inputs(seed: int = 0) -> dict:
    key = jax.random.key(seed)
    ks = jax.random.split(key, 24)

    def nrm(k, shape, s=1.0):
        return s * jax.random.normal(k, shape, jnp.float32)

    def gain(k, shape):
        return 1.0 + 0.1 * jax.random.normal(k, shape, jnp.float32)

    band_cache = min(BAND_PAST, PAST_LEN)
    return {
        'x_prompt': nrm(ks[0], (BATCH, SEQ, D_MODEL)),
        'x_sample': nrm(ks[1], (DEC_BATCH, DEC_SEQ, D_MODEL)),
        'state_ret': nrm(ks[2], (DEPTH, DEC_BATCH, RET_HEADS, RET_DK, RET_DV), 0.3),
        'cache_mla_ckv': nrm(ks[3], (DEPTH, DEC_BATCH, PAST_LEN, KV_LORA)),
        'cache_mla_krope': nrm(ks[4], (DEPTH, DEC_BATCH, PAST_LEN, MLA_ROPE)),
        'cache_band_k': nrm(ks[5], (DEPTH, DEC_BATCH, band_cache, BAND_HEADS, BAND_DH)),
        'cache_band_v': nrm(ks[6], (DEPTH, DEC_BATCH, band_cache, BAND_HEADS, BAND_DH)),
        'norm_g': gain(ks[7], (DEPTH, D_MODEL)),
        'w_in': nrm(ks[8], (DEPTH, D_MODEL, D_IN), D_MODEL ** -0.5),
        'ret_gn_g': gain(ks[9], (DEPTH, RET_W)),
        'mla_qa_g': gain(ks[10], (DEPTH, Q_LORA)),
        'mla_w_uq': nrm(ks[11], (DEPTH, Q_LORA, MLA_HEADS * MLA_QK), Q_LORA ** -0.5),
        'mla_qn_g': gain(ks[12], (DEPTH, MLA_NOPE)),
        'mla_qr_g': gain(ks[13], (DEPTH, MLA_ROPE)),
        'mla_kva_g': gain(ks[14], (DEPTH, KV_LORA)),
        'mla_kr_g': gain(ks[15], (DEPTH, MLA_ROPE)),
        'mla_w_ukv': nrm(ks[16], (DEPTH, KV_LORA, MLA_HEADS * (MLA_NOPE + MLA_V)), KV_LORA ** -0.5),
        'mla_kn_g': gain(ks[17], (DEPTH, MLA_NOPE)),
        'band_qn_g': gain(ks[18], (DEPTH, BAND_DH)),
        'band_kn_g': gain(ks[19], (DEPTH, BAND_DH)),
        'band_bias': nrm(ks[20], (DEPTH, BAND_HEADS, N_REL), 0.1),
        'w_out': nrm(ks[21], (DEPTH, D_MIX, D_MODEL), D_MIX ** -0.5),
    }


def reference(x_prompt, x_sample, state_ret, cache_mla_ckv, cache_mla_krope, cache_band_k,
              cache_band_v, norm_g, w_in, ret_gn_g, mla_qa_g, mla_w_uq, mla_qn_g, mla_qr_g,
              mla_kva_g, mla_kr_g, mla_w_ukv, mla_kn_g, band_qn_g, band_kn_g, band_bias, w_out):
    past_len = cache_mla_ckv.shape[2]
    pos_p = jnp.arange(x_prompt.shape[1])
    pos_s = past_len + jnp.arange(x_sample.shape[1])
    xp, xs = x_prompt, x_sample
    p_st, s_st = [], []
    for l in range(DEPTH):
        w = (norm_g[l], w_in[l], ret_gn_g[l], mla_qa_g[l], mla_w_uq[l], mla_qn_g[l], mla_qr_g[l],
             mla_kva_g[l], mla_kr_g[l], mla_w_ukv[l], mla_kn_g[l], band_qn_g[l], band_kn_g[l],
             band_bias[l], w_out[l])
        xp, sp = _layer(xp, pos_p, *w, None)
        xs, ss = _layer(xs, pos_s, *w, (state_ret[l], cache_mla_ckv[l], cache_mla_krope[l],
                                        cache_band_k[l], cache_band_v[l]))
        p_st.append(sp)
        s_st.append(ss)
    p_state_ret = jnp.stack([s[0] for s in p_st])
    p_mla_ckv = jnp.stack([s[1] for s in p_st])
    p_mla_krope = jnp.stack([s[2] for s in p_st])
    p_band_k = jnp.stack([s[3] for s in p_st])
    p_band_v = jnp.stack([s[4] for s in p_st])
    s_state_ret = jnp.stack([s[0] for s in s_st])
    s_mla_ckv = jnp.stack([s[1] for s in s_st])
    s_mla_krope = jnp.stack([s[2] for s in s_st])
    s_band_k = jnp.stack([s[3] for s in s_st])
    s_band_v = jnp.stack([s[4] for s in s_st])
    return (xp, xs, p_state_ret, p_mla_ckv, p_mla_krope, p_band_k, p_band_v,
            s_state_ret, s_mla_ckv, s_mla_krope, s_band_k, s_band_v)
```

```python
import functools

import jax
import jax.numpy as jnp
import numpy as np
from jax import lax
from jax.experimental import pallas as pl
from jax.experimental.pallas import tpu as pltpu

F32 = jnp.float32
BF16 = jnp.bfloat16

D_MODEL = 1024
DEPTH = 2
CHUNK = 64
EPS = 1e-6
NEG_INF = -1e30
ROPE_BASE = 10000.0
RET_HEADS, RET_DK, RET_DV = 4, 64, 64
RET_W = RET_HEADS * RET_DV
MLA_HEADS, MLA_NOPE, MLA_ROPE, MLA_V = 8, 64, 32, 64
MLA_QK = MLA_NOPE + MLA_ROPE
Q_LORA, KV_LORA = 256, 128
MLA_W = MLA_HEADS * MLA_V
BAND_HEADS, BAND_DH = 4, 64
BAND_W = BAND_HEADS * BAND_DH
BAND_PREV_CHUNKS = 8
BAND_PAST = BAND_PREV_CHUNKS * CHUNK
MAX_REL = 128
D_MIX = RET_W + MLA_W + BAND_W
SEG = (RET_HEADS * RET_DK, RET_HEADS * RET_DK, RET_W, RET_W, Q_LORA, KV_LORA, MLA_ROPE, MLA_W,
       BAND_W, BAND_W, BAND_W, BAND_W)

LANES = 128
HEAD_PAD = LANES
MLA_QW = MLA_HEADS * HEAD_PAD
C_AQ, C_AK, C_AV, C_CQ, C_CKV, C_KR, C_BQ, C_BK, C_BV, C_G, C_END = (
    0, 256, 512, 768, 1024, 1152, 1280, 1536, 1792, 2048, 3072)
VMEM_LIMIT = 56 * 1024 * 1024


def _dot(a, b):
    return jnp.dot(a, b, preferred_element_type=F32)


def _dot_nt(a, b):
    return lax.dot_general(a, b, (((1,), (1,)), ((), ())), preferred_element_type=F32)


def _dot_tn(a, b):
    return lax.dot_general(a, b, (((0,), (0,)), ((), ())), preferred_element_type=F32)


def _group_inv_rms(x, red, exp, inv_n):
    ss = _dot((x * x).astype(BF16), red)
    inv = lax.rsqrt(ss * inv_n + EPS)
    hi = inv.astype(BF16)
    lo = (inv - hi.astype(F32)).astype(BF16)
    return _dot(hi, exp) + _dot(lo, exp)


def _swap_halves(x, half):
    w = x.shape[-1]
    lane = lax.broadcasted_iota(jnp.int32, (1, w), 1)
    nxt = pltpu.roll(x, w - half, axis=1)
    prv = pltpu.roll(x, half, axis=1)
    return jnp.where((lane & half) == 0, nxt, prv)


def _rope(x, cos, sin_signed, half):
    return x * cos + _swap_halves(x, half) * sin_signed


def _silu(g):
    return g * (1.0 / (1.0 + jnp.exp(-g)))


def _full(shape):
    nd = len(shape)
    return pl.BlockSpec(shape, lambda *_: (0,) * nd)


def _params(sem):
    return pltpu.CompilerParams(dimension_semantics=sem, vmem_limit_bytes=VMEM_LIMIT)


def _k1_body(x_ref, ng_ref, win_ref, wuq_ref, qag_ref, kvag_ref, krg_ref, qgain_ref, bqg_ref, bkg_ref,
             cret_ref, sret_ref, cq_ref, sq_ref, ckr_ref, skr_ref,
             redq_ref, expq_ref, invnq_ref, redb_ref, expb_ref, invnb_ref,
             rq_ref, rk_ref, rv_ref, gate_ref, qm_ref, ckv_ref, kr_ref, bq_ref, bk_ref, bv_ref,
             bks_ref, bvs_ref):
    x = x_ref[...]
    h = x * lax.rsqrt(jnp.mean(x * x, axis=-1, keepdims=True) + EPS) * ng_ref[...]
    hb = h.astype(BF16)

    def seg(lo, hi):
        return _dot(hb, win_ref[:, lo:hi])

    cret, sret = cret_ref[...], sret_ref[...]
    rq_ref[...] = _rope(seg(C_AQ, C_AK), cret, sret, RET_DK // 2).astype(BF16)
    rk_ref[...] = _rope(seg(C_AK, C_AV), cret, sret, RET_DK // 2).astype(BF16)
    rv_ref[...] = seg(C_AV, C_CQ).astype(BF16)

    gate_ref[...] = _silu(seg(C_G, C_END)).astype(BF16)

    cq = seg(C_CQ, C_CKV)
    cq = cq * lax.rsqrt(jnp.mean(cq * cq, axis=-1, keepdims=True) + EPS) * qag_ref[...]
    qf = _dot(cq.astype(BF16), wuq_ref[...])
    qn = qf * _group_inv_rms(qf, redq_ref[...], expq_ref[...], invnq_ref[...]) * qgain_ref[...]
    cq_t = jnp.concatenate([cq_ref[...]] * MLA_HEADS, axis=1)
    sq_t = jnp.concatenate([sq_ref[...]] * MLA_HEADS, axis=1)
    qm_ref[...] = _rope(qn, cq_t, sq_t, MLA_ROPE // 2).astype(BF16)

    ckv = seg(C_CKV, C_KR)
    ckv_ref[...] = ckv * lax.rsqrt(jnp.mean(ckv * ckv, axis=-1, keepdims=True) + EPS) * kvag_ref[...]
    kr = seg(C_KR, C_BQ)
    kr = kr * lax.rsqrt(jnp.sum(kr * kr, axis=-1, keepdims=True) * (1.0 / MLA_ROPE) + EPS) * krg_ref[...]
    kr = _rope(kr, ckr_ref[...], skr_ref[...], MLA_ROPE // 2)
    kr_ref[...] = kr[:, :MLA_ROPE]

    redb, expb, invnb = redb_ref[...], expb_ref[...], invnb_ref[...]
    bq = seg(C_BQ, C_BK)
    bq_ref[...] = (bq * _group_inv_rms(bq, redb, expb, invnb) * bqg_ref[...]).astype(BF16)
    bk = seg(C_BK, C_BV)
    bk = bk * _group_inv_rms(bk, redb, expb, invnb) * bkg_ref[...]
    bk_ref[...] = bk.astype(BF16)
    bks_ref[...] = bk
    bv = seg(C_BV, C_G)
    bv_ref[...] = bv.astype(BF16)
    bvs_ref[...] = bv


def _k1(x2d, lw, tabs, consts, *, tm, rows_per_seq, n_keep):
    T = x2d.shape[0]
    nb = rows_per_seq // tm
    nkb = n_keep // tm
    n_seq = T // rows_per_seq

    row = lambda w: pl.BlockSpec((tm, w), lambda i: (i, 0))
    tab = lambda w: pl.BlockSpec((tm, w), lambda i: (i % nb, 0))
    keep = pl.BlockSpec((tm, BAND_W), lambda i: ((i // nb) * nkb + jnp.maximum(i % nb - (nb - nkb), 0), 0))

    ins = [
        (x2d, row(D_MODEL)), (lw['norm_g'], _full((1, D_MODEL))), (lw['w_in'], _full((D_MODEL, C_END))),
        (lw['w_uq'], _full((Q_LORA, MLA_QW))), (lw['qa_g'], _full((1, Q_LORA))),
        (lw['kva_g'], _full((1, KV_LORA))), (lw['kr_g'], _full((1, LANES))),
        (lw['q_gain'], _full((1, MLA_QW))), (lw['bq_gain'], _full((1, BAND_W))),
        (lw['bk_gain'], _full((1, BAND_W))),
        (tabs['cret'], tab(RET_W)), (tabs['sret'], tab(RET_W)), (tabs['cq'], tab(LANES)),
        (tabs['sq'], tab(LANES)), (tabs['ckr'], tab(LANES)), (tabs['skr'], tab(LANES)),
        (consts['red_q'], _full((MLA_QW, LANES))), (consts['exp_q'], _full((LANES, MLA_QW))),
        (consts['invn_q'], _full((1, LANES))), (consts['red_b'], _full((BAND_W, LANES))),
        (consts['exp_b'], _full((LANES, BAND_W))), (consts['invn_b'], _full((1, LANES))),
    ]
    outs = [
        ((T, RET_W), BF16, row(RET_W)), ((T, RET_W), BF16, row(RET_W)), ((T, RET_W), BF16, row(RET_W)),
        ((T, D_MIX), BF16, row(D_MIX)), ((T, MLA_QW), BF16, row(MLA_QW)),
        ((T, KV_LORA), F32, row(KV_LORA)), ((T, MLA_ROPE), F32, row(MLA_ROPE)),
        ((T, BAND_W), BF16, row(BAND_W)), ((T, BAND_W), BF16, row(BAND_W)), ((T, BAND_W), BF16, row(BAND_W)),
        ((n_seq * n_keep, BAND_W), F32, keep), ((n_seq * n_keep, BAND_W), F32, keep),
    ]
    return pl.pallas_call(
        _k1_body,
        grid=(T // tm,),
        in_specs=[s for _, s in ins],
        out_specs=[s for _, _, s in outs],
        out_shape=[jax.ShapeDtypeStruct(sh, dt) for sh, dt, _ in outs],
        compiler_params=_params(("arbitrary",)),
        name="k1_proj",
    )(*[a for a, _ in ins])


def _kkv_body(ckv_ref, kr_ref, wk_ref, wv_ref, redk_ref, expk_ref, invnk_ref, kgain_ref, place_ref,
              k_ref, v_ref):
    c = ckv_ref[...].astype(BF16)
    kn = _dot(c, wk_ref[...])
    kn = kn * _group_inv_rms(kn, redk_ref[...], expk_ref[...], invnk_ref[...]) * kgain_ref[...]
    k_ref[...] = (kn + _dot(kr_ref[...].astype(BF16), place_ref[...])).astype(BF16)
    v_ref[...] = _dot(c, wv_ref[...]).astype(BF16)


def _kkv(ckv2d, kr2d, lw, consts, *, tm):
    T = ckv2d.shape[0]
    row = lambda w: pl.BlockSpec((tm, w), lambda i: (i, 0))
    return pl.pallas_call(
        _kkv_body,
        grid=(T // tm,),
        in_specs=[row(KV_LORA), row(MLA_ROPE), _full((KV_LORA, MLA_QW)), _full((KV_LORA, MLA_W)),
                  _full((MLA_QW, LANES)), _full((LANES, MLA_QW)), _full((1, LANES)), _full((1, MLA_QW)),
                  _full((MLA_ROPE, MLA_QW))],
        out_specs=[row(MLA_QW), row(MLA_W)],
        out_shape=[jax.ShapeDtypeStruct((T, MLA_QW), BF16), jax.ShapeDtypeStruct((T, MLA_W), BF16)],
        compiler_params=_params(("arbitrary",)),
        name="kkv_up",
    )(ckv2d, kr2d, lw['w_uk'], lw['w_uv'], consts['red_k'], consts['exp_k'], consts['invn_k'],
      lw['k_gain'], consts['place_kr'])


def _mla_prompt_body(q_ref, k_ref, v_ref, o_ref, *, tq):
    qi = pl.program_id(1)
    lane = lax.broadcasted_iota(jnp.int32, (1, LANES), 1)
    rc = lax.broadcasted_iota(jnp.int32, (tq, tq), 0) // CHUNK
    cc = lax.broadcasted_iota(jnp.int32, (tq, tq), 1) // CHUNK
    diag_visible = cc <= rc

    outs = []
    for h in range(MLA_HEADS):
        q = q_ref[:, h * HEAD_PAD:(h + 1) * HEAD_PAD]
        v_lanes = (lane // MLA_V) == (h % 2)
        pair = h // 2

        def step(kb, carry, masked, q=q, h=h, v_lanes=v_lanes, pair=pair):
            m, l, acc = carry
            start = pl.multiple_of(kb * tq, tq)
            k = k_ref[pl.ds(start, tq), h * HEAD_PAD:(h + 1) * HEAD_PAD]
            s = _dot_nt(q, k)
            if masked:
                s = jnp.where(diag_visible, s, NEG_INF)
            m_new = jnp.maximum(m, jnp.max(s, axis=-1, keepdims=True))
            alpha = jnp.exp(m - m_new)
            p = jnp.exp(s - m_new)
            l = alpha * l + jnp.sum(p, axis=-1, keepdims=True)
            vp = v_ref[pl.ds(start, tq), pair * LANES:(pair + 1) * LANES]
            vm = jnp.where(v_lanes, vp, jnp.zeros_like(vp))
            acc = alpha * acc + _dot(p.astype(BF16), vm)
            return m_new, l, acc

        init = (jnp.full((tq, 1), NEG_INF, F32), jnp.zeros((tq, 1), F32), jnp.zeros((tq, LANES), F32))
        carry = lax.fori_loop(0, qi, functools.partial(step, masked=False), init)
        _, l, acc = step(qi, carry, True)
        outs.append(acc * (1.0 / l))
    for j in range(MLA_HEADS // 2):
        o_ref[:, j * LANES:(j + 1) * LANES] = (outs[2 * j] + outs[2 * j + 1]).astype(BF16)


def _mla_prompt(q, k, v, *, tq):
    B, S, _ = q.shape
    resident = lambda w: pl.BlockSpec((None, S, w), lambda b, i: (b, 0, 0), pipeline_mode=pl.Buffered(1))
    return pl.pallas_call(
        functools.partial(_mla_prompt_body, tq=tq),
        grid=(B, S // tq),
        in_specs=[pl.BlockSpec((None, tq, MLA_QW), lambda b, i: (b, i, 0)), resident(MLA_QW), resident(MLA_W)],
        out_specs=pl.BlockSpec((None, tq, MLA_W), lambda b, i: (b, i, 0)),
        out_shape=jax.ShapeDtypeStruct((B, S, MLA_W), BF16),
        compiler_params=_params(("arbitrary", "arbitrary")),
        name="mla_prompt",
    )(q, k, v)


def _mla_sample_body(q_ref, kp_ref, vp_ref, kn_ref, vn_ref, o_ref):
    lane = lax.broadcasted_iota(jnp.int32, (1, LANES), 1)
    outs = []
    for h in range(MLA_HEADS):
        hs = slice(h * HEAD_PAD, (h + 1) * HEAD_PAD)
        ps = slice((h // 2) * LANES, (h // 2 + 1) * LANES)
        v_lanes = (lane // MLA_V) == (h % 2)
        q = q_ref[:, hs]
        s1 = _dot_nt(q, kp_ref[:, hs])
        s2 = _dot_nt(q, kn_ref[:, hs])
        m = jnp.maximum(jnp.max(s1, axis=-1, keepdims=True), jnp.max(s2, axis=-1, keepdims=True))
        p1 = jnp.exp(s1 - m)
        p2 = jnp.exp(s2 - m)
        l = jnp.sum(p1, axis=-1, keepdims=True) + jnp.sum(p2, axis=-1, keepdims=True)
        v1 = vp_ref[:, ps]
        v2 = vn_ref[:, ps]
        acc = (_dot(p1.astype(BF16), jnp.where(v_lanes, v1, jnp.zeros_like(v1)))
               + _dot(p2.astype(BF16), jnp.where(v_lanes, v2, jnp.zeros_like(v2))))
        outs.append(acc * (1.0 / l))
    for j in range(MLA_HEADS // 2):
        o_ref[:, j * LANES:(j + 1) * LANES] = (outs[2 * j] + outs[2 * j + 1]).astype(BF16)


def _mla_sample(q, kp, vp, kn, vn):
    B, L, _ = q.shape
    P = kp.shape[1]
    blk = lambda n, w: pl.BlockSpec((None, n, w), lambda b: (b, 0, 0))
    return pl.pallas_call(
        _mla_sample_body,
        grid=(B,),
        in_specs=[blk(L, MLA_QW), blk(P, MLA_QW), blk(P, MLA_W), blk(L, MLA_QW), blk(L, MLA_W)],
        out_specs=blk(L, MLA_W),
        out_shape=jax.ShapeDtypeStruct((B, L, MLA_W), BF16),
        compiler_params=_params(("arbitrary",)),
        name="mla_sample",
    )(q, kp, vp, kn, vn)


def _band_heads(q, pieces, biases, out_dtype):
    lane = lax.broadcasted_iota(jnp.int32, (1, BAND_W), 1)
    out = jnp.zeros((q.shape[0], BAND_W), F32)
    for h in range(BAND_HEADS):
        sel = (lane // BAND_DH) == h
        qh = jnp.where(sel, q, jnp.zeros_like(q))
        ss = []
        for (k, _, gate), b_ref in zip(pieces, biases):
            s = _dot_nt(qh, k) + b_ref[h]
            if gate is not None:
                s = jnp.where(gate, s, NEG_INF)
            ss.append(s)
        m = functools.reduce(jnp.maximum, [jnp.max(s, axis=-1, keepdims=True) for s in ss])
        ps = [jnp.exp(s - m) for s in ss]
        l = functools.reduce(jnp.add, [jnp.sum(p, axis=-1, keepdims=True) for p in ps])
        acc = functools.reduce(jnp.add, [
            _dot(p.astype(BF16), jnp.where(sel, v, jnp.zeros_like(v))) for p, (_, v, _) in zip(ps, pieces)])
        out = out + acc * (1.0 / l)
    return out.astype(out_dtype)


def _band_prompt_body(q_ref, kp_ref, kc_ref, vp_ref, vc_ref, bp_ref, bc_ref, o_ref):
    has_prev = pl.program_id(1) > 0
    o_ref[...] = _band_heads(q_ref[...],
                             [(kp_ref[...], vp_ref[...], has_prev), (kc_ref[...], vc_ref[...], None)],
                             [bp_ref, bc_ref], BF16)


def _band_prompt(q, k, v, bias_prev, bias_cur, *, tq):
    B, S, _ = q.shape
    cur = pl.BlockSpec((None, tq, BAND_W), lambda b, i: (b, i, 0))
    prev = pl.BlockSpec((None, tq, BAND_W), lambda b, i: (b, jnp.maximum(i - 1, 0), 0))
    bias = _full((BAND_HEADS, tq, tq))
    return pl.pallas_call(
        _band_prompt_body,
        grid=(B, S // tq),
        in_specs=[cur, prev, cur, prev, cur, bias, bias],
        out_specs=cur,
        out_shape=jax.ShapeDtypeStruct((B, S, BAND_W), BF16),
        compiler_params=_params(("arbitrary", "arbitrary")),
        name="band_prompt",
    )(q, k, k, v, v, bias_prev, bias_cur)


def _band_sample_body(q_ref, kp_ref, vp_ref, kn_ref, vn_ref, bp_ref, bn_ref, o_ref):
    o_ref[...] = _band_heads(q_ref[...],
                             [(kp_ref[...].astype(BF16), vp_ref[...].astype(BF16), None),
                              (kn_ref[...], vn_ref[...], None)],
                             [bp_ref, bn_ref], BF16)


def _band_sample(q, kp, vp, kn, vn, bias_past, bias_new):
    B, L, _ = q.shape
    P = kp.shape[1]
    blk = lambda n: pl.BlockSpec((None, n, BAND_W), lambda b: (b, 0, 0))
    return pl.pallas_call(
        _band_sample_body,
        grid=(B,),
        in_specs=[blk(L), blk(P), blk(P), blk(L), blk(L), _full((BAND_HEADS, L, P)), _full((BAND_HEADS, L, L))],
        out_specs=blk(L),
        out_shape=jax.ShapeDtypeStruct((B, L, BAND_W), BF16),
        compiler_params=_params(("arbitrary",)),
        name="band_sample",
    )(q, kp, vp, kn, vn, bias_past, bias_new)


def _ret_body(q_ref, k_ref, v_ref, s0_ref, d_ref, xi_ref, zeta_ref, gam_ref, bd_ref,
              red_ref, exp_ref, invn_ref, gain_ref, o_ref, sfin_ref, s_sc):
    @pl.when(pl.program_id(1) == 0)
    def _():
        s_sc[...] = s0_ref[...]

    q, k, v = q_ref[...], k_ref[...], v_ref[...]
    state = s_sc[...]
    lane = lax.broadcasted_iota(jnp.int32, (1, RET_W), 1)
    o = _dot(q, state.astype(BF16)) * xi_ref[...]
    for h in range(RET_HEADS):
        sel = (lane // RET_DV) == h
        a = _dot_nt(jnp.where(sel, q, jnp.zeros_like(q)), k) * d_ref[h]
        o = o + _dot(a.astype(BF16), jnp.where(sel, v, jnp.zeros_like(v)))
    kz = (k.astype(F32) * zeta_ref[...]).astype(BF16)
    s_new = gam_ref[...] * state + bd_ref[...] * _dot_tn(kz, v)
    s_sc[...] = s_new
    sfin_ref[...] = s_new
    o_ref[...] = (o * _group_inv_rms(o, red_ref[...], exp_ref[...], invn_ref[...]) * gain_ref[...]).astype(BF16)


def _retention(q, k, v, s0, rt, consts, gain, *, c):
    B, S, _ = q.shape
    blk = pl.BlockSpec((None, c, RET_W), lambda b, t: (b, t, 0))
    st = pl.BlockSpec((None, RET_W, RET_W), lambda b, t: (b, 0, 0))
    return pl.pallas_call(
        _ret_body,
        grid=(B, S // c),
        in_specs=[blk, blk, blk, st, _full((RET_HEADS, c, c)), _full((c, RET_W)), _full((c, RET_W)),
                  _full((RET_W, RET_W)), _full((RET_W, RET_W)), _full((RET_W, LANES)), _full((LANES, RET_W)),
                  _full((1, LANES)), _full((1, RET_W))],
        out_specs=[blk, st],
        out_shape=[jax.ShapeDtypeStruct((B, S, RET_W), BF16), jax.ShapeDtypeStruct((B, RET_W, RET_W), F32)],
        scratch_shapes=[pltpu.VMEM((RET_W, RET_W), F32)],
        compiler_params=_params(("arbitrary", "arbitrary")),
        name="retention",
    )(q, k, v, s0, rt['decay'], rt['xi'], rt['zeta'], rt['gamma'], consts['bd_mask'],
      consts['red_b'], consts['exp_b'], consts['invn_b'], gain)


def _kout_body(x_ref, ro_ref, mo_ref, bo_ref, g_ref, w_ref, y_ref):
    g = g_ref[...]
    y = x_ref[...]
    y = y + _dot(ro_ref[...] * g[:, :RET_W], w_ref[:RET_W, :])
    y = y + _dot(mo_ref[...] * g[:, RET_W:RET_W + MLA_W], w_ref[RET_W:RET_W + MLA_W, :])
    y = y + _dot(bo_ref[...] * g[:, RET_W + MLA_W:], w_ref[RET_W + MLA_W:, :])
    y_ref[...] = y


def _kout(x2d, ro, mo, bo, gate, w_out, *, tm):
    T = x2d.shape[0]
    row = lambda w: pl.BlockSpec((tm, w), lambda i: (i, 0))
    return pl.pallas_call(
        _kout_body,
        grid=(T // tm,),
        in_specs=[row(D_MODEL), row(RET_W), row(MLA_W), row(BAND_W), row(D_MIX), _full((D_MIX, D_MODEL))],
        out_specs=row(D_MODEL),
        out_shape=jax.ShapeDtypeStruct((T, D_MODEL), F32),
        compiler_params=_params(("arbitrary",)),
        name="kout_proj",
    )(x2d, ro, mo, bo, gate, w_out)


def _membership(width, groups):
    red = np.zeros((width, LANES), np.float32)
    inv_n = np.zeros((1, LANES), np.float32)
    for g, (start, size) in enumerate(groups):
        red[start:start + size, g] = 1.0
        inv_n[0, g] = 1.0 / size
    return jnp.asarray(red, BF16), jnp.asarray(red.T, BF16), jnp.asarray(inv_n)


def _constants():
    c = {}
    q_groups = []
    for h in range(MLA_HEADS):
        q_groups += [(h * HEAD_PAD, MLA_NOPE), (h * HEAD_PAD + MLA_NOPE, MLA_ROPE)]
    c['red_q'], c['exp_q'], c['invn_q'] = _membership(MLA_QW, q_groups)
    c['red_k'], c['exp_k'], c['invn_k'] = _membership(MLA_QW, [(h * HEAD_PAD, MLA_NOPE) for h in range(MLA_HEADS)])
    c['red_b'], c['exp_b'], c['invn_b'] = _membership(BAND_W, [(h * BAND_DH, BAND_DH) for h in range(BAND_HEADS)])
    place = np.zeros((MLA_ROPE, MLA_QW), np.float32)
    for h in range(MLA_HEADS):
        place[np.arange(MLA_ROPE), h * HEAD_PAD + MLA_NOPE + np.arange(MLA_ROPE)] = 1.0
    c['place_kr'] = jnp.asarray(place, BF16)
    bd = np.kron(np.eye(RET_HEADS, dtype=np.float32), np.ones((RET_DK, RET_DV), np.float32))
    c['bd_mask'] = jnp.asarray(bd)
    return c


def _rope_tables(pos):
    pos = pos.astype(F32)

    def cs(half):
        inv = ROPE_BASE ** (-jnp.arange(half, dtype=F32) / half)
        ang = pos[:, None] * inv[None, :]
        c, s = jnp.cos(ang), jnp.sin(ang)
        return jnp.concatenate([c, c], axis=-1), jnp.concatenate([-s, s], axis=-1)

    T = pos.shape[0]
    c32, s32 = cs(RET_DK // 2)
    c16, s16 = cs(MLA_ROPE // 2)
    ones = lambda w: jnp.ones((T, w), F32)
    zeros = lambda w: jnp.zeros((T, w), F32)
    pad = HEAD_PAD - MLA_QK
    return {
        'cret': jnp.tile(c32, (1, RET_HEADS)), 'sret': jnp.tile(s32, (1, RET_HEADS)),
        'cq': jnp.concatenate([ones(MLA_NOPE), c16, ones(pad)], axis=-1),
        'sq': jnp.concatenate([zeros(MLA_NOPE), s16, zeros(pad)], axis=-1),
        'ckr': jnp.concatenate([c16, ones(LANES - MLA_ROPE)], axis=-1),
        'skr': jnp.concatenate([s16, zeros(LANES - MLA_ROPE)], axis=-1),
    }


def _retention_tables(c):
    lg = jnp.log1p(-jnp.exp2(-5.0 - jnp.arange(RET_HEADS, dtype=F32)))
    idx = jnp.arange(c, dtype=F32)
    diff = idx[:, None] - idx[None, :]
    decay = jnp.where(diff >= 0, jnp.exp(lg[:, None, None] * jnp.maximum(diff, 0.0)), 0.0)
    per_lane = lambda t: jnp.repeat(t, RET_DV, axis=-1)
    xi = per_lane(jnp.exp(lg[None, :] * (idx[:, None] + 1.0)))
    zeta = per_lane(jnp.exp(lg[None, :] * (c - 1.0 - idx)[:, None]))
    gamma = jnp.broadcast_to(per_lane(jnp.exp(lg * c)[None, :]).T, (RET_W, RET_W))
    return {'decay': decay, 'xi': xi, 'zeta': zeta, 'gamma': gamma}


def _band_bias_prompt(band_bias, tq):
    qi = jnp.arange(tq)[:, None]
    kj = jnp.arange(tq)[None, :]

    def table(key_offset):
        dist = qi - (kj + key_offset)
        qc = qi // CHUNK
        kc = (kj + key_offset) // CHUNK
        vis = (kc <= qc) & (kc >= qc - BAND_PREV_CHUNKS)
        b = band_bias[:, jnp.clip(dist, -MAX_REL, MAX_REL) + MAX_REL]
        return jnp.where(vis[None], b, NEG_INF).astype(F32)

    return table(-tq), table(0)


def _band_bias_sample(band_bias, n_new, n_past):
    qi = jnp.arange(n_new)[:, None]
    past = band_bias[:, jnp.clip(qi + n_past - jnp.arange(n_past)[None, :], -MAX_REL, MAX_REL) + MAX_REL]
    new = band_bias[:, jnp.clip(qi - jnp.arange(n_new)[None, :], -MAX_REL, MAX_REL) + MAX_REL]
    return past.astype(F32), new.astype(F32)


def _layer_weights(l, norm_g, w_in, ret_gn_g, mla_qa_g, mla_w_uq, mla_qn_g, mla_qr_g, mla_kva_g, mla_kr_g,
                   mla_w_ukv, mla_kn_g, band_qn_g, band_kn_g, w_out):
    cuts = np.cumsum(SEG)[:-1].tolist()
    a_q, a_k, a_v, a_g, b_cq, b_ckv, b_kr, b_g, c_q, c_k, c_v, c_g = jnp.split(w_in[l], cuts, axis=-1)
    b_kr = jnp.pad(b_kr, ((0, 0), (0, LANES - MLA_ROPE)))
    w_in_p = jnp.concatenate([a_q, a_k * (RET_DK ** -0.5), a_v, b_cq, b_ckv, b_kr, c_q, c_k, c_v, a_g, b_g, c_g],
                             axis=-1).astype(BF16)
    pad = HEAD_PAD - MLA_QK
    w_uq = jnp.pad(mla_w_uq[l].reshape(Q_LORA, MLA_HEADS, MLA_QK), ((0, 0), (0, 0), (0, pad)))
    ukv = mla_w_ukv[l].reshape(KV_LORA, MLA_HEADS, MLA_NOPE + MLA_V)
    w_uk = jnp.pad(ukv[:, :, :MLA_NOPE], ((0, 0), (0, 0), (0, HEAD_PAD - MLA_NOPE)))
    zpad = jnp.zeros((pad,), F32)
    q_gain = jnp.tile(jnp.concatenate([mla_qn_g[l], mla_qr_g[l], zpad]) * (MLA_QK ** -0.5), MLA_HEADS)
    k_gain = jnp.tile(jnp.concatenate([mla_kn_g[l], jnp.zeros((HEAD_PAD - MLA_NOPE,), F32)]), MLA_HEADS)
    return {
        'norm_g': norm_g[l][None], 'w_in': w_in_p,
        'w_uq': w_uq.reshape(Q_LORA, MLA_QW).astype(BF16), 'qa_g': mla_qa_g[l][None],
        'kva_g': mla_kva_g[l][None],
        'kr_g': jnp.concatenate([mla_kr_g[l], jnp.zeros((LANES - MLA_ROPE,), F32)])[None],
        'q_gain': q_gain[None],
        'bq_gain': (jnp.tile(band_qn_g[l], BAND_HEADS) * (BAND_DH ** -0.5))[None],
        'bk_gain': jnp.tile(band_kn_g[l], BAND_HEADS)[None],
        'w_uk': w_uk.reshape(KV_LORA, MLA_QW).astype(BF16),
        'w_uv': ukv[:, :, MLA_NOPE:].reshape(KV_LORA, MLA_W).astype(BF16),
        'k_gain': k_gain[None], 'ret_gain': ret_gn_g[l][None], 'w_out': w_out[l].astype(BF16),
    }


def _block_diag_state(s):
    B = s.shape[0]
    eye = jnp.eye(RET_HEADS, dtype=s.dtype)
    return jnp.einsum('bhde,hg->bhdge', s, eye).reshape(B, RET_W, RET_W)


def _diag_blocks(s_bd):
    B = s_bd.shape[0]
    s = s_bd.reshape(B, RET_HEADS, RET_DK, RET_HEADS, RET_DV)
    return jnp.stack([s[:, h, :, h, :] for h in range(RET_HEADS)], axis=1)


def kernel(x_prompt, x_sample, state_ret, cache_mla_ckv, cache_mla_krope, cache_band_k, cache_band_v, norm_g, w_in, ret_gn_g, mla_qa_g, mla_w_uq, mla_qn_g, mla_qr_g, mla_kva_g, mla_kr_g, mla_w_ukv, mla_kn_g, band_qn_g, band_kn_g, band_bias, w_out):
    B, S, _ = x_prompt.shape
    DB, L, _ = x_sample.shape
    past_len = cache_mla_ckv.shape[2]
    n_band_past = cache_band_k.shape[2]
    n_keep_p = min(BAND_PAST, S)

    TM_P, TM_S, TQ, RET_C = 512, 256, 512, 256
    consts = _constants()
    tabs_p = _rope_tables(jnp.arange(S))
    tabs_s = {k: jnp.tile(v, (DB, 1)) for k, v in _rope_tables(past_len + jnp.arange(L)).items()}
    rt_p = _retention_tables(RET_C)
    rt_s = _retention_tables(L)

    xp = x_prompt.reshape(B * S, D_MODEL)
    xs = x_sample.reshape(DB * L, D_MODEL)
    zeros_state = jnp.zeros((B, RET_W, RET_W), F32)
    p_st, s_st = [], []
    for l in range(DEPTH):
        lw = _layer_weights(l, norm_g, w_in, ret_gn_g, mla_qa_g, mla_w_uq, mla_qn_g, mla_qr_g, mla_kva_g,
                            mla_kr_g, mla_w_ukv, mla_kn_g, band_qn_g, band_kn_g, w_out)

        rq, rk, rv, gate, qm, ckv, kr, bq, bk, bv, bks, bvs = _k1(
            xp, lw, tabs_p, consts, tm=TM_P, rows_per_seq=S, n_keep=n_keep_p)
        km, vm = _kkv(ckv, kr, lw, consts, tm=TM_P)
        mla_o = _mla_prompt(qm.reshape(B, S, MLA_QW), km.reshape(B, S, MLA_QW), vm.reshape(B, S, MLA_W), tq=TQ)
        bias_prev, bias_cur = _band_bias_prompt(band_bias[l], TQ)
        band_o = _band_prompt(bq.reshape(B, S, BAND_W), bk.reshape(B, S, BAND_W), bv.reshape(B, S, BAND_W),
                              bias_prev, bias_cur, tq=TQ)
        ret_o, ret_s = _retention(rq.reshape(B, S, RET_W), rk.reshape(B, S, RET_W), rv.reshape(B, S, RET_W),
                                  zeros_state, rt_p, consts, lw['ret_gain'], c=RET_C)
        xp = _kout(xp, ret_o.reshape(B * S, RET_W), mla_o.reshape(B * S, MLA_W), band_o.reshape(B * S, BAND_W),
                   gate, lw['w_out'], tm=TM_P)
        p_st.append((_diag_blocks(ret_s), ckv.reshape(B, S, KV_LORA), kr.reshape(B, S, MLA_ROPE),
                     bks.reshape(B, n_keep_p, BAND_HEADS, BAND_DH), bvs.reshape(B, n_keep_p, BAND_HEADS, BAND_DH)))

        rq, rk, rv, gate, qm, ckv, kr, bq, bk, bv, bks, bvs = _k1(
            xs, lw, tabs_s, consts, tm=TM_S, rows_per_seq=DB * L, n_keep=DB * L)
        kn, vn = _kkv(ckv, kr, lw, consts, tm=TM_S)
        kp, vp = _kkv(cache_mla_ckv[l].reshape(DB * past_len, KV_LORA),
                      cache_mla_krope[l].reshape(DB * past_len, MLA_ROPE), lw, consts, tm=TM_P)
        mla_o = _mla_sample(qm.reshape(DB, L, MLA_QW), kp.reshape(DB, past_len, MLA_QW),
                            vp.reshape(DB, past_len, MLA_W), kn.reshape(DB, L, MLA_QW), vn.reshape(DB, L, MLA_W))
        bias_past, bias_new = _band_bias_sample(band_bias[l], L, n_band_past)
        band_o = _band_sample(bq.reshape(DB, L, BAND_W), cache_band_k[l].reshape(DB, n_band_past, BAND_W),
                              cache_band_v[l].reshape(DB, n_band_past, BAND_W), bk.reshape(DB, L, BAND_W),
                              bv.reshape(DB, L, BAND_W), bias_past, bias_new)
        ret_o, ret_s = _retention(rq.reshape(DB, L, RET_W), rk.reshape(DB, L, RET_W), rv.reshape(DB, L, RET_W),
                                  _block_diag_state(state_ret[l]), rt_s, consts, lw['ret_gain'], c=L)
        xs = _kout(xs, ret_o.reshape(DB * L, RET_W), mla_o.reshape(DB * L, MLA_W), band_o.reshape(DB * L, BAND_W),
                   gate, lw['w_out'], tm=TM_S)
        s_st.append((_diag_blocks(ret_s), ckv.reshape(DB, L, KV_LORA), kr.reshape(DB, L, MLA_ROPE),
                     bks.reshape(DB, L, BAND_HEADS, BAND_DH), bvs.reshape(DB, L, BAND_HEADS, BAND_DH)))

    stack = lambda sts, i: jnp.stack([s[i] for s in sts])
    return (xp.reshape(B, S, D_MODEL), xs.reshape(DB, L, D_MODEL),
            stack(p_st, 0), stack(p_st, 1), stack(p_st, 2), stack(p_st, 3), stack(p_st, 4),
            stack(s_st, 0), stack(s_st, 1), stack(s_st, 2), stack(s_st, 3), stack(s_st, 4))
```

```python
import functools

import jax
import jax.numpy as jnp
import numpy as np
from jax import lax
from jax.experimental import pallas as pl
from jax.experimental.pallas import tpu as pltpu

F32 = jnp.float32
BF16 = jnp.bfloat16

D_MODEL = 1024
DEPTH = 2
CHUNK = 64
EPS = 1e-6
NEG_INF = -1e30
ROPE_BASE = 10000.0
LOG2E = 1.4426950408889634
RET_HEADS, RET_DK, RET_DV = 4, 64, 64
RET_W = RET_HEADS * RET_DV
MLA_HEADS, MLA_NOPE, MLA_ROPE, MLA_V = 8, 64, 32, 64
MLA_QK = MLA_NOPE + MLA_ROPE
Q_LORA, KV_LORA = 256, 128
MLA_W = MLA_HEADS * MLA_V
BAND_HEADS, BAND_DH = 4, 64
BAND_W = BAND_HEADS * BAND_DH
BAND_PREV_CHUNKS = 8
BAND_PAST = BAND_PREV_CHUNKS * CHUNK
MAX_REL = 128
D_MIX = RET_W + MLA_W + BAND_W
SEG = (RET_HEADS * RET_DK, RET_HEADS * RET_DK, RET_W, RET_W, Q_LORA, KV_LORA, MLA_ROPE, MLA_W,
       BAND_W, BAND_W, BAND_W, BAND_W)

LANES = 128
HEAD_PAD = LANES
MLA_QW = MLA_HEADS * HEAD_PAD
C_AQ, C_AK, C_AV, C_CQ, C_CKV, C_KR, C_BQ, C_BK, C_BV, C_G, C_END = (
    0, 256, 512, 768, 1024, 1152, 1280, 1536, 1792, 2048, 3072)
VMEM_LIMIT = 56 * 1024 * 1024


def _dot(a, b):
    return jnp.dot(a, b, preferred_element_type=F32)


def _dot_nt(a, b):
    return lax.dot_general(a, b, (((1,), (1,)), ((), ())), preferred_element_type=F32)


def _dot_tn(a, b):
    return lax.dot_general(a, b, (((0,), (0,)), ((), ())), preferred_element_type=F32)


def _group_inv_rms(x, red, exp, inv_n):
    ss = _dot((x * x).astype(BF16), red)
    inv = lax.rsqrt(ss * inv_n + EPS)
    hi = inv.astype(BF16)
    lo = (inv - hi.astype(F32)).astype(BF16)
    return _dot(hi, exp) + _dot(lo, exp)


def _swap_halves(x, half):
    w = x.shape[-1]
    lane = lax.broadcasted_iota(jnp.int32, (1, w), 1)
    nxt = pltpu.roll(x, w - half, axis=1)
    prv = pltpu.roll(x, half, axis=1)
    return jnp.where((lane & half) == 0, nxt, prv)


def _rope(x, cos, sin_signed, half):
    return x * cos + _swap_halves(x, half) * sin_signed


def _silu(g):
    return g * (1.0 / (1.0 + jnp.exp(-g)))


def _full(shape):
    nd = len(shape)
    return pl.BlockSpec(shape, lambda *_: (0,) * nd)


def _params(sem):
    return pltpu.CompilerParams(dimension_semantics=sem, vmem_limit_bytes=VMEM_LIMIT)


def _k1_body(x_ref, ng_ref, win_ref, wuq_ref, qag_ref, kvag_ref, krg_ref, qgain_ref, bqg_ref, bkg_ref,
             cret_ref, sret_ref, cq_ref, sq_ref, ckr_ref, skr_ref,
             redq_ref, expq_ref, invnq_ref, redb_ref, expb_ref, invnb_ref,
             rq_ref, rk_ref, rv_ref, gate_ref, qm_ref, ckv_ref, kr_ref, bq_ref, bk_ref, bv_ref,
             bks_ref, bvs_ref):
    x = x_ref[...]
    h = x * lax.rsqrt(jnp.mean(x * x, axis=-1, keepdims=True) + EPS) * ng_ref[...]
    hb = h.astype(BF16)

    def seg(lo, hi):
        return _dot(hb, win_ref[:, lo:hi])

    cret, sret = cret_ref[...], sret_ref[...]
    rq_ref[...] = _rope(seg(C_AQ, C_AK), cret, sret, RET_DK // 2).astype(BF16)
    rk_ref[...] = _rope(seg(C_AK, C_AV), cret, sret, RET_DK // 2).astype(BF16)
    rv_ref[...] = seg(C_AV, C_CQ).astype(BF16)

    gate_ref[...] = _silu(seg(C_G, C_END)).astype(BF16)

    cq = seg(C_CQ, C_CKV)
    cq = cq * lax.rsqrt(jnp.mean(cq * cq, axis=-1, keepdims=True) + EPS) * qag_ref[...]
    qf = _dot(cq.astype(BF16), wuq_ref[...])
    qn = qf * _group_inv_rms(qf, redq_ref[...], expq_ref[...], invnq_ref[...]) * qgain_ref[...]
    cq_t = jnp.concatenate([cq_ref[...]] * MLA_HEADS, axis=1)
    sq_t = jnp.concatenate([sq_ref[...]] * MLA_HEADS, axis=1)
    qm_ref[...] = _rope(qn, cq_t, sq_t, MLA_ROPE // 2).astype(BF16)

    ckv = seg(C_CKV, C_KR)
    ckv_ref[...] = ckv * lax.rsqrt(jnp.mean(ckv * ckv, axis=-1, keepdims=True) + EPS) * kvag_ref[...]
    kr = seg(C_KR, C_BQ)
    kr = kr * lax.rsqrt(jnp.sum(kr * kr, axis=-1, keepdims=True) * (1.0 / MLA_ROPE) + EPS) * krg_ref[...]
    kr = _rope(kr, ckr_ref[...], skr_ref[...], MLA_ROPE // 2)
    kr_ref[...] = kr[:, :MLA_ROPE]

    redb, expb, invnb = redb_ref[...], expb_ref[...], invnb_ref[...]
    bq = seg(C_BQ, C_BK)
    bq_ref[...] = (bq * _group_inv_rms(bq, redb, expb, invnb) * bqg_ref[...]).astype(BF16)
    bk = seg(C_BK, C_BV)
    bk = bk * _group_inv_rms(bk, redb, expb, invnb) * bkg_ref[...]
    bk_ref[...] = bk.astype(BF16)
    bks_ref[...] = bk
    bv = seg(C_BV, C_G)
    bv_ref[...] = bv.astype(BF16)
    bvs_ref[...] = bv


def _k1(x2d, lw, tabs, consts, *, tm, rows_per_seq, n_keep):
    T = x2d.shape[0]
    nb = rows_per_seq // tm
    nkb = n_keep // tm
    n_seq = T // rows_per_seq

    row = lambda w: pl.BlockSpec((tm, w), lambda i: (i, 0))
    tab = lambda w: pl.BlockSpec((tm, w), lambda i: (i % nb, 0))
    keep = pl.BlockSpec((tm, BAND_W), lambda i: ((i // nb) * nkb + jnp.maximum(i % nb - (nb - nkb), 0), 0))

    ins = [
        (x2d, row(D_MODEL)), (lw['norm_g'], _full((1, D_MODEL))), (lw['w_in'], _full((D_MODEL, C_END))),
        (lw['w_uq'], _full((Q_LORA, MLA_QW))), (lw['qa_g'], _full((1, Q_LORA))),
        (lw['kva_g'], _full((1, KV_LORA))), (lw['kr_g'], _full((1, LANES))),
        (lw['q_gain'], _full((1, MLA_QW))), (lw['bq_gain'], _full((1, BAND_W))),
        (lw['bk_gain'], _full((1, BAND_W))),
        (tabs['cret'], tab(RET_W)), (tabs['sret'], tab(RET_W)), (tabs['cq'], tab(LANES)),
        (tabs['sq'], tab(LANES)), (tabs['ckr'], tab(LANES)), (tabs['skr'], tab(LANES)),
        (consts['red_q'], _full((MLA_QW, LANES))), (consts['exp_q'], _full((LANES, MLA_QW))),
        (consts['invn_q'], _full((1, LANES))), (consts['red_b'], _full((BAND_W, LANES))),
        (consts['exp_b'], _full((LANES, BAND_W))), (consts['invn_b'], _full((1, LANES))),
    ]
    outs = [
        ((T, RET_W), BF16, row(RET_W)), ((T, RET_W), BF16, row(RET_W)), ((T, RET_W), BF16, row(RET_W)),
        ((T, D_MIX), BF16, row(D_MIX)), ((T, MLA_QW), BF16, row(MLA_QW)),
        ((T, KV_LORA), F32, row(KV_LORA)), ((T, MLA_ROPE), F32, row(MLA_ROPE)),
        ((T, BAND_W), BF16, row(BAND_W)), ((T, BAND_W), BF16, row(BAND_W)), ((T, BAND_W), BF16, row(BAND_W)),
        ((n_seq * n_keep, BAND_W), F32, keep), ((n_seq * n_keep, BAND_W), F32, keep),
    ]
    return pl.pallas_call(
        _k1_body,
        grid=(T // tm,),
        in_specs=[s for _, s in ins],
        out_specs=[s for _, _, s in outs],
        out_shape=[jax.ShapeDtypeStruct(sh, dt) for sh, dt, _ in outs],
        compiler_params=_params(("arbitrary",)),
        name="k1_proj",
    )(*[a for a, _ in ins])


def _kkv_body(ckv_ref, kr_ref, wk_ref, wv_ref, redk_ref, expk_ref, invnk_ref, kgain_ref, place_ref,
              k_ref, v_ref):
    c = ckv_ref[...].astype(BF16)
    kn = _dot(c, wk_ref[...])
    kn = kn * _group_inv_rms(kn, redk_ref[...], expk_ref[...], invnk_ref[...]) * kgain_ref[...]
    k_ref[...] = (kn + _dot(kr_ref[...].astype(BF16), place_ref[...])).astype(BF16)
    v_ref[...] = _dot(c, wv_ref[...]).astype(BF16)


def _kkv(ckv2d, kr2d, lw, consts, *, tm):
    T = ckv2d.shape[0]
    row = lambda w: pl.BlockSpec((tm, w), lambda i: (i, 0))
    return pl.pallas_call(
        _kkv_body,
        grid=(T // tm,),
        in_specs=[row(KV_LORA), row(MLA_ROPE), _full((KV_LORA, MLA_QW)), _full((KV_LORA, MLA_W)),
                  _full((MLA_QW, LANES)), _full((LANES, MLA_QW)), _full((1, LANES)), _full((1, MLA_QW)),
                  _full((MLA_ROPE, MLA_QW))],
        out_specs=[row(MLA_QW), row(MLA_W)],
        out_shape=[jax.ShapeDtypeStruct((T, MLA_QW), BF16), jax.ShapeDtypeStruct((T, MLA_W), BF16)],
        compiler_params=_params(("arbitrary",)),
        name="kkv_up",
    )(ckv2d, kr2d, lw['w_uk'], lw['w_uv'], consts['red_k'], consts['exp_k'], consts['invn_k'],
      lw['k_gain'], consts['place_kr'])


def _mla_prompt_body(q_ref, k_ref, v_ref, o_ref, m_sc, l_sc, acc_sc, *, tq):
    qi = pl.program_id(1)
    lane = lax.broadcasted_iota(jnp.int32, (1, LANES), 1)
    m_sc[...] = jnp.full(m_sc.shape, NEG_INF, F32)
    l_sc[...] = jnp.zeros(l_sc.shape, F32)
    acc_sc[...] = jnp.zeros(acc_sc.shape, F32)

    def tile(kb, masked):
        start = pl.multiple_of(kb * tq, tq)
        if masked:
            rc = lax.broadcasted_iota(jnp.int32, (tq, tq), 0) // CHUNK
            cc = lax.broadcasted_iota(jnp.int32, (tq, tq), 1) // CHUNK
            visible = cc <= rc
        for h in range(MLA_HEADS):
            q = q_ref[:, h * HEAD_PAD:(h + 1) * HEAD_PAD]
            k = k_ref[pl.ds(start, tq), h * HEAD_PAD:(h + 1) * HEAD_PAD]
            s = _dot_nt(q, k)
            if masked:
                s = jnp.where(visible, s, NEG_INF)
            m_prev = m_sc[h]
            m_new = jnp.maximum(m_prev, jnp.max(s, axis=-1, keepdims=True))
            alpha = jnp.exp2(m_prev - m_new)
            p = jnp.exp2(s - jnp.concatenate([m_new] * (tq // LANES), axis=1))
            l_sc[h] = alpha * l_sc[h] + jnp.sum(p, axis=-1, keepdims=True)
            vp = v_ref[pl.ds(start, tq), (h // 2) * LANES:(h // 2 + 1) * LANES]
            vm = jnp.where((lane // MLA_V) == (h % 2), vp, jnp.zeros_like(vp))
            acc_sc[h] = alpha * acc_sc[h] + _dot(p.astype(BF16), vm)
            m_sc[h] = m_new

    def body(kb, carry):
        tile(kb, False)
        return carry

    lax.fori_loop(0, qi, body, 0)
    tile(qi, True)
    for j in range(MLA_HEADS // 2):
        o_ref[:, j * LANES:(j + 1) * LANES] = (
            acc_sc[2 * j] * (1.0 / l_sc[2 * j]) + acc_sc[2 * j + 1] * (1.0 / l_sc[2 * j + 1])).astype(BF16)


def _mla_prompt(q, k, v, *, tq):
    B, S, _ = q.shape
    resident = lambda w: pl.BlockSpec((None, S, w), lambda b, i: (b, 0, 0), pipeline_mode=pl.Buffered(1))
    stat = pltpu.VMEM((MLA_HEADS, tq, LANES), F32)
    return pl.pallas_call(
        functools.partial(_mla_prompt_body, tq=tq),
        grid=(B, S // tq),
        in_specs=[pl.BlockSpec((None, tq, MLA_QW), lambda b, i: (b, i, 0)), resident(MLA_QW), resident(MLA_W)],
        out_specs=pl.BlockSpec((None, tq, MLA_W), lambda b, i: (b, i, 0)),
        out_shape=jax.ShapeDtypeStruct((B, S, MLA_W), BF16),
        scratch_shapes=[stat, stat, stat],
        compiler_params=_params(("arbitrary", "arbitrary")),
        name="mla_prompt",
    )(q, k, v)


def _mla_sample_body(q_ref, kp_ref, vp_ref, kn_ref, vn_ref, o_ref):
    lane = lax.broadcasted_iota(jnp.int32, (1, LANES), 1)
    outs = []
    for h in range(MLA_HEADS):
        hs = slice(h * HEAD_PAD, (h + 1) * HEAD_PAD)
        ps = slice((h // 2) * LANES, (h // 2 + 1) * LANES)
        v_lanes = (lane // MLA_V) == (h % 2)
        q = q_ref[:, hs]
        s1 = _dot_nt(q, kp_ref[:, hs])
        s2 = _dot_nt(q, kn_ref[:, hs])
        m = jnp.maximum(jnp.max(s1, axis=-1, keepdims=True), jnp.max(s2, axis=-1, keepdims=True))
        p1 = jnp.exp2(s1 - m)
        p2 = jnp.exp2(s2 - m)
        l = jnp.sum(p1, axis=-1, keepdims=True) + jnp.sum(p2, axis=-1, keepdims=True)
        v1 = vp_ref[:, ps]
        v2 = vn_ref[:, ps]
        acc = (_dot(p1.astype(BF16), jnp.where(v_lanes, v1, jnp.zeros_like(v1)))
               + _dot(p2.astype(BF16), jnp.where(v_lanes, v2, jnp.zeros_like(v2))))
        outs.append(acc * (1.0 / l))
    for j in range(MLA_HEADS // 2):
        o_ref[:, j * LANES:(j + 1) * LANES] = (outs[2 * j] + outs[2 * j + 1]).astype(BF16)


def _mla_sample(q, kp, vp, kn, vn):
    B, L, _ = q.shape
    P = kp.shape[1]
    blk = lambda n, w: pl.BlockSpec((None, n, w), lambda b: (b, 0, 0))
    return pl.pallas_call(
        _mla_sample_body,
        grid=(B,),
        in_specs=[blk(L, MLA_QW), blk(P, MLA_QW), blk(P, MLA_W), blk(L, MLA_QW), blk(L, MLA_W)],
        out_specs=blk(L, MLA_W),
        out_shape=jax.ShapeDtypeStruct((B, L, MLA_W), BF16),
        compiler_params=_params(("arbitrary",)),
        name="mla_sample",
    )(q, kp, vp, kn, vn)


def _band_heads(q, pieces, biases, out_dtype):
    lane = lax.broadcasted_iota(jnp.int32, (1, BAND_W), 1)
    out = jnp.zeros((q.shape[0], BAND_W), F32)
    for h in range(BAND_HEADS):
        sel = (lane // BAND_DH) == h
        qh = jnp.where(sel, q, jnp.zeros_like(q))
        ss = []
        for (k, _, gate), b_ref in zip(pieces, biases):
            s = _dot_nt(qh, k) + b_ref[h]
            if gate is not None:
                s = jnp.where(gate, s, NEG_INF)
            ss.append(s)
        m = functools.reduce(jnp.maximum, [jnp.max(s, axis=-1, keepdims=True) for s in ss])
        ps = [jnp.exp(s - m) for s in ss]
        l = functools.reduce(jnp.add, [jnp.sum(p, axis=-1, keepdims=True) for p in ps])
        acc = functools.reduce(jnp.add, [
            _dot(p.astype(BF16), jnp.where(sel, v, jnp.zeros_like(v))) for p, (_, v, _) in zip(ps, pieces)])
        out = out + acc * (1.0 / l)
    return out.astype(out_dtype)


def _band_prompt_body(q_ref, kp_ref, kc_ref, vp_ref, vc_ref, bp_ref, bc_ref, o_ref):
    has_prev = pl.program_id(1) > 0
    o_ref[...] = _band_heads(q_ref[...],
                             [(kp_ref[...], vp_ref[...], has_prev), (kc_ref[...], vc_ref[...], None)],
                             [bp_ref, bc_ref], BF16)


def _band_prompt(q, k, v, bias_prev, bias_cur, *, tq):
    B, S, _ = q.shape
    cur = pl.BlockSpec((None, tq, BAND_W), lambda b, i: (b, i, 0))
    prev = pl.BlockSpec((None, tq, BAND_W), lambda b, i: (b, jnp.maximum(i - 1, 0), 0))
    bias = _full((BAND_HEADS, tq, tq))
    return pl.pallas_call(
        _band_prompt_body,
        grid=(B, S // tq),
        in_specs=[cur, prev, cur, prev, cur, bias, bias],
        out_specs=cur,
        out_shape=jax.ShapeDtypeStruct((B, S, BAND_W), BF16),
        compiler_params=_params(("arbitrary", "arbitrary")),
        name="band_prompt",
    )(q, k, k, v, v, bias_prev, bias_cur)


def _band_sample_body(q_ref, kp_ref, vp_ref, kn_ref, vn_ref, bp_ref, bn_ref, o_ref):
    o_ref[...] = _band_heads(q_ref[...],
                             [(kp_ref[...].astype(BF16), vp_ref[...].astype(BF16), None),
                              (kn_ref[...], vn_ref[...], None)],
                             [bp_ref, bn_ref], BF16)


def _band_sample(q, kp, vp, kn, vn, bias_past, bias_new):
    B, L, _ = q.shape
    P = kp.shape[1]
    blk = lambda n: pl.BlockSpec((None, n, BAND_W), lambda b: (b, 0, 0))
    return pl.pallas_call(
        _band_sample_body,
        grid=(B,),
        in_specs=[blk(L), blk(P), blk(P), blk(L), blk(L), _full((BAND_HEADS, L, P)), _full((BAND_HEADS, L, L))],
        out_specs=blk(L),
        out_shape=jax.ShapeDtypeStruct((B, L, BAND_W), BF16),
        compiler_params=_params(("arbitrary",)),
        name="band_sample",
    )(q, kp, vp, kn, vn, bias_past, bias_new)


def _ret_body(q_ref, k_ref, v_ref, s0_ref, d_ref, xi_ref, zeta_ref, gam_ref, bd_ref,
              red_ref, exp_ref, invn_ref, gain_ref, o_ref, sfin_ref, s_sc):
    @pl.when(pl.program_id(1) == 0)
    def _():
        s_sc[...] = s0_ref[...]

    q, k, v = q_ref[...], k_ref[...], v_ref[...]
    state = s_sc[...]
    lane = lax.broadcasted_iota(jnp.int32, (1, RET_W), 1)
    o = _dot(q, state.astype(BF16)) * xi_ref[...]
    for h in range(RET_HEADS):
        sel = (lane // RET_DV) == h
        a = _dot_nt(jnp.where(sel, q, jnp.zeros_like(q)), k) * d_ref[h]
        o = o + _dot(a.astype(BF16), jnp.where(sel, v, jnp.zeros_like(v)))
    kz = (k.astype(F32) * zeta_ref[...]).astype(BF16)
    s_new = gam_ref[...] * state + bd_ref[...] * _dot_tn(kz, v)
    s_sc[...] = s_new
    sfin_ref[...] = s_new
    o_ref[...] = (o * _group_inv_rms(o, red_ref[...], exp_ref[...], invn_ref[...]) * gain_ref[...]).astype(BF16)


def _retention(q, k, v, s0, rt, consts, gain, *, c):
    B, S, _ = q.shape
    blk = pl.BlockSpec((None, c, RET_W), lambda b, t: (b, t, 0))
    st = pl.BlockSpec((None, RET_W, RET_W), lambda b, t: (b, 0, 0))
    return pl.pallas_call(
        _ret_body,
        grid=(B, S // c),
        in_specs=[blk, blk, blk, st, _full((RET_HEADS, c, c)), _full((c, RET_W)), _full((c, RET_W)),
                  _full((RET_W, RET_W)), _full((RET_W, RET_W)), _full((RET_W, LANES)), _full((LANES, RET_W)),
                  _full((1, LANES)), _full((1, RET_W))],
        out_specs=[blk, st],
        out_shape=[jax.ShapeDtypeStruct((B, S, RET_W), BF16), jax.ShapeDtypeStruct((B, RET_W, RET_W), F32)],
        scratch_shapes=[pltpu.VMEM((RET_W, RET_W), F32)],
        compiler_params=_params(("arbitrary", "arbitrary")),
        name="retention",
    )(q, k, v, s0, rt['decay'], rt['xi'], rt['zeta'], rt['gamma'], consts['bd_mask'],
      consts['red_b'], consts['exp_b'], consts['invn_b'], gain)


def _kout_body(x_ref, ro_ref, mo_ref, bo_ref, g_ref, w_ref, y_ref):
    g = g_ref[...]
    y = x_ref[...]
    y = y + _dot(ro_ref[...] * g[:, :RET_W], w_ref[:RET_W, :])
    y = y + _dot(mo_ref[...] * g[:, RET_W:RET_W + MLA_W], w_ref[RET_W:RET_W + MLA_W, :])
    y = y + _dot(bo_ref[...] * g[:, RET_W + MLA_W:], w_ref[RET_W + MLA_W:, :])
    y_ref[...] = y


def _kout(x2d, ro, mo, bo, gate, w_out, *, tm):
    T = x2d.shape[0]
    row = lambda w: pl.BlockSpec((tm, w), lambda i: (i, 0))
    return pl.pallas_call(
        _kout_body,
        grid=(T // tm,),
        in_specs=[row(D_MODEL), row(RET_W), row(MLA_W), row(BAND_W), row(D_MIX), _full((D_MIX, D_MODEL))],
        out_specs=row(D_MODEL),
        out_shape=jax.ShapeDtypeStruct((T, D_MODEL), F32),
        compiler_params=_params(("arbitrary",)),
        name="kout_proj",
    )(x2d, ro, mo, bo, gate, w_out)


def _membership(width, groups):
    red = np.zeros((width, LANES), np.float32)
    inv_n = np.zeros((1, LANES), np.float32)
    for g, (start, size) in enumerate(groups):
        red[start:start + size, g] = 1.0
        inv_n[0, g] = 1.0 / size
    return jnp.asarray(red, BF16), jnp.asarray(red.T, BF16), jnp.asarray(inv_n)


def _constants():
    c = {}
    q_groups = []
    for h in range(MLA_HEADS):
        q_groups += [(h * HEAD_PAD, MLA_NOPE), (h * HEAD_PAD + MLA_NOPE, MLA_ROPE)]
    c['red_q'], c['exp_q'], c['invn_q'] = _membership(MLA_QW, q_groups)
    c['red_k'], c['exp_k'], c['invn_k'] = _membership(MLA_QW, [(h * HEAD_PAD, MLA_NOPE) for h in range(MLA_HEADS)])
    c['red_b'], c['exp_b'], c['invn_b'] = _membership(BAND_W, [(h * BAND_DH, BAND_DH) for h in range(BAND_HEADS)])
    place = np.zeros((MLA_ROPE, MLA_QW), np.float32)
    for h in range(MLA_HEADS):
        place[np.arange(MLA_ROPE), h * HEAD_PAD + MLA_NOPE + np.arange(MLA_ROPE)] = 1.0
    c['place_kr'] = jnp.asarray(place, BF16)
    bd = np.kron(np.eye(RET_HEADS, dtype=np.float32), np.ones((RET_DK, RET_DV), np.float32))
    c['bd_mask'] = jnp.asarray(bd)
    return c


def _rope_tables(pos):
    pos = pos.astype(F32)

    def cs(half):
        inv = ROPE_BASE ** (-jnp.arange(half, dtype=F32) / half)
        ang = pos[:, None] * inv[None, :]
        c, s = jnp.cos(ang), jnp.sin(ang)
        return jnp.concatenate([c, c], axis=-1), jnp.concatenate([-s, s], axis=-1)

    T = pos.shape[0]
    c32, s32 = cs(RET_DK // 2)
    c16, s16 = cs(MLA_ROPE // 2)
    ones = lambda w: jnp.ones((T, w), F32)
    zeros = lambda w: jnp.zeros((T, w), F32)
    pad = HEAD_PAD - MLA_QK
    return {
        'cret': jnp.tile(c32, (1, RET_HEADS)), 'sret': jnp.tile(s32, (1, RET_HEADS)),
        'cq': jnp.concatenate([ones(MLA_NOPE), c16, ones(pad)], axis=-1),
        'sq': jnp.concatenate([zeros(MLA_NOPE), s16, zeros(pad)], axis=-1),
        'ckr': jnp.concatenate([c16, ones(LANES - MLA_ROPE)], axis=-1),
        'skr': jnp.concatenate([s16, zeros(LANES - MLA_ROPE)], axis=-1),
    }


def _retention_tables(c):
    lg = jnp.log1p(-jnp.exp2(-5.0 - jnp.arange(RET_HEADS, dtype=F32)))
    idx = jnp.arange(c, dtype=F32)
    diff = idx[:, None] - idx[None, :]
    decay = jnp.where(diff >= 0, jnp.exp(lg[:, None, None] * jnp.maximum(diff, 0.0)), 0.0)
    per_lane = lambda t: jnp.repeat(t, RET_DV, axis=-1)
    xi = per_lane(jnp.exp(lg[None, :] * (idx[:, None] + 1.0)))
    zeta = per_lane(jnp.exp(lg[None, :] * (c - 1.0 - idx)[:, None]))
    gamma = jnp.broadcast_to(per_lane(jnp.exp(lg * c)[None, :]).T, (RET_W, RET_W))
    return {'decay': decay, 'xi': xi, 'zeta': zeta, 'gamma': gamma}


def _band_tables_body(rp_ref, rc_ref, prev_ref, cur_ref, past_ref, new_ref, *, tq, n_new):
    w = 2 * tq
    tp = pltpu.roll(jnp.broadcast_to(rp_ref[...], (tq, w)), 0, 1, stride=1, stride_axis=0)[:, :tq]
    tc = pltpu.roll(jnp.broadcast_to(rc_ref[...], (tq, w)), 0, 1, stride=1, stride_axis=0)[:, :tq]
    qc = lax.broadcasted_iota(jnp.int32, (tq, tq), 0) // CHUNK
    kc = lax.broadcasted_iota(jnp.int32, (tq, tq), 1) // CHUNK
    prev_ref[...] = jnp.where(kc >= qc, tp, NEG_INF)
    cur_ref[...] = jnp.where(kc <= qc, tc, NEG_INF)
    past_ref[...] = tp[:n_new, :]
    new_ref[...] = tc[:n_new, :n_new]


def _band_tables(band_bias, *, tq, n_new):
    H = band_bias.shape[0]
    lo, mid, hi = band_bias[:, :1], band_bias[:, MAX_REL + 1:2 * MAX_REL], band_bias[:, 2 * MAX_REL:]
    rep = lambda col, n: jnp.broadcast_to(col, (H, n))
    r_cur = jnp.concatenate([band_bias[:, MAX_REL::-1], rep(lo, tq - MAX_REL - 1), rep(hi, tq - MAX_REL),
                             band_bias[:, :MAX_REL:-1]], axis=1)
    r_prev = jnp.concatenate([rep(hi, tq - MAX_REL + 1), mid[:, ::-1], rep(hi, tq)], axis=1)
    row = pl.BlockSpec((None, 1, 2 * tq), lambda h: (h, 0, 0))
    tile = lambda n, m: pl.BlockSpec((None, n, m), lambda h: (h, 0, 0))
    return pl.pallas_call(
        functools.partial(_band_tables_body, tq=tq, n_new=n_new),
        grid=(H,),
        in_specs=[row, row],
        out_specs=[tile(tq, tq), tile(tq, tq), tile(n_new, tq), tile(n_new, n_new)],
        out_shape=[jax.ShapeDtypeStruct((H, tq, tq), F32), jax.ShapeDtypeStruct((H, tq, tq), F32),
                   jax.ShapeDtypeStruct((H, n_new, tq), F32), jax.ShapeDtypeStruct((H, n_new, n_new), F32)],
        compiler_params=_params(("arbitrary",)),
        name="band_tables",
    )(r_prev[:, None, :], r_cur[:, None, :])


def _layer_weights(l, norm_g, w_in, ret_gn_g, mla_qa_g, mla_w_uq, mla_qn_g, mla_qr_g, mla_kva_g, mla_kr_g,
                   mla_w_ukv, mla_kn_g, band_qn_g, band_kn_g, w_out):
    cuts = np.cumsum(SEG)[:-1].tolist()
    a_q, a_k, a_v, a_g, b_cq, b_ckv, b_kr, b_g, c_q, c_k, c_v, c_g = jnp.split(w_in[l], cuts, axis=-1)
    b_kr = jnp.pad(b_kr, ((0, 0), (0, LANES - MLA_ROPE)))
    w_in_p = jnp.concatenate([a_q, a_k * (RET_DK ** -0.5), a_v, b_cq, b_ckv, b_kr, c_q, c_k, c_v, a_g, b_g, c_g],
                             axis=-1).astype(BF16)
    pad = HEAD_PAD - MLA_QK
    w_uq = jnp.pad(mla_w_uq[l].reshape(Q_LORA, MLA_HEADS, MLA_QK), ((0, 0), (0, 0), (0, pad)))
    ukv = mla_w_ukv[l].reshape(KV_LORA, MLA_HEADS, MLA_NOPE + MLA_V)
    w_uk = jnp.pad(ukv[:, :, :MLA_NOPE], ((0, 0), (0, 0), (0, HEAD_PAD - MLA_NOPE)))
    zpad = jnp.zeros((pad,), F32)
    q_gain = jnp.tile(jnp.concatenate([mla_qn_g[l], mla_qr_g[l], zpad]) * (MLA_QK ** -0.5 * LOG2E), MLA_HEADS)
    k_gain = jnp.tile(jnp.concatenate([mla_kn_g[l], jnp.zeros((HEAD_PAD - MLA_NOPE,), F32)]), MLA_HEADS)
    return {
        'norm_g': norm_g[l][None], 'w_in': w_in_p,
        'w_uq': w_uq.reshape(Q_LORA, MLA_QW).astype(BF16), 'qa_g': mla_qa_g[l][None],
        'kva_g': mla_kva_g[l][None],
        'kr_g': jnp.concatenate([mla_kr_g[l], jnp.zeros((LANES - MLA_ROPE,), F32)])[None],
        'q_gain': q_gain[None],
        'bq_gain': (jnp.tile(band_qn_g[l], BAND_HEADS) * (BAND_DH ** -0.5))[None],
        'bk_gain': jnp.tile(band_kn_g[l], BAND_HEADS)[None],
        'w_uk': w_uk.reshape(KV_LORA, MLA_QW).astype(BF16),
        'w_uv': ukv[:, :, MLA_NOPE:].reshape(KV_LORA, MLA_W).astype(BF16),
        'k_gain': k_gain[None], 'ret_gain': ret_gn_g[l][None], 'w_out': w_out[l].astype(BF16),
    }


def _block_diag_state(s):
    B = s.shape[0]
    eye = jnp.eye(RET_HEADS, dtype=s.dtype)
    return jnp.einsum('bhde,hg->bhdge', s, eye).reshape(B, RET_W, RET_W)


def _diag_blocks(s_bd):
    B = s_bd.shape[0]
    s = s_bd.reshape(B, RET_HEADS, RET_DK, RET_HEADS, RET_DV)
    return jnp.stack([s[:, h, :, h, :] for h in range(RET_HEADS)], axis=1)


def kernel(x_prompt, x_sample, state_ret, cache_mla_ckv, cache_mla_krope, cache_band_k, cache_band_v, norm_g, w_in, ret_gn_g, mla_qa_g, mla_w_uq, mla_qn_g, mla_qr_g, mla_kva_g, mla_kr_g, mla_w_ukv, mla_kn_g, band_qn_g, band_kn_g, band_bias, w_out):
    B, S, _ = x_prompt.shape
    DB, L, _ = x_sample.shape
    past_len = cache_mla_ckv.shape[2]
    n_band_past = cache_band_k.shape[2]
    n_keep_p = min(BAND_PAST, S)

    TM_P, TM_S, TQ, RET_C = 512, 256, 512, 256
    consts = _constants()
    tabs_p = _rope_tables(jnp.arange(S))
    tabs_s = {k: jnp.tile(v, (DB, 1)) for k, v in _rope_tables(past_len + jnp.arange(L)).items()}
    rt_p = _retention_tables(RET_C)
    rt_s = _retention_tables(L)

    xp = x_prompt.reshape(B * S, D_MODEL)
    xs = x_sample.reshape(DB * L, D_MODEL)
    zeros_state = jnp.zeros((B, RET_W, RET_W), F32)
    p_st, s_st = [], []
    for l in range(DEPTH):
        lw = _layer_weights(l, norm_g, w_in, ret_gn_g, mla_qa_g, mla_w_uq, mla_qn_g, mla_qr_g, mla_kva_g,
                            mla_kr_g, mla_w_ukv, mla_kn_g, band_qn_g, band_kn_g, w_out)

        rq, rk, rv, gate, qm, ckv, kr, bq, bk, bv, bks, bvs = _k1(
            xp, lw, tabs_p, consts, tm=TM_P, rows_per_seq=S, n_keep=n_keep_p)
        km, vm = _kkv(ckv, kr, lw, consts, tm=TM_P)
        mla_o = _mla_prompt(qm.reshape(B, S, MLA_QW), km.reshape(B, S, MLA_QW), vm.reshape(B, S, MLA_W), tq=TQ)
        assert TQ == BAND_PAST == n_band_past
        bias_prev, bias_cur, bias_past, bias_new = _band_tables(band_bias[l], tq=TQ, n_new=L)
        band_o = _band_prompt(bq.reshape(B, S, BAND_W), bk.reshape(B, S, BAND_W), bv.reshape(B, S, BAND_W),
                              bias_prev, bias_cur, tq=TQ)
        ret_o, ret_s = _retention(rq.reshape(B, S, RET_W), rk.reshape(B, S, RET_W), rv.reshape(B, S, RET_W),
                                  zeros_state, rt_p, consts, lw['ret_gain'], c=RET_C)
        xp = _kout(xp, ret_o.reshape(B * S, RET_W), mla_o.reshape(B * S, MLA_W), band_o.reshape(B * S, BAND_W),
                   gate, lw['w_out'], tm=TM_P)
        p_st.append((_diag_blocks(ret_s), ckv.reshape(B, S, KV_LORA), kr.reshape(B, S, MLA_ROPE),
                     bks.reshape(B, n_keep_p, BAND_HEADS, BAND_DH), bvs.reshape(B, n_keep_p, BAND_HEADS, BAND_DH)))

        rq, rk, rv, gate, qm, ckv, kr, bq, bk, bv, bks, bvs = _k1(
            xs, lw, tabs_s, consts, tm=TM_S, rows_per_seq=DB * L, n_keep=DB * L)
        kn, vn = _kkv(ckv, kr, lw, consts, tm=TM_S)
        kp, vp = _kkv(cache_mla_ckv[l].reshape(DB * past_len, KV_LORA),
                      cache_mla_krope[l].reshape(DB * past_len, MLA_ROPE), lw, consts, tm=TM_P)
        mla_o = _mla_sample(qm.reshape(DB, L, MLA_QW), kp.reshape(DB, past_len, MLA_QW),
                            vp.reshape(DB, past_len, MLA_W), kn.reshape(DB, L, MLA_QW), vn.reshape(DB, L, MLA_W))
        band_o = _band_sample(bq.reshape(DB, L, BAND_W), cache_band_k[l].reshape(DB, n_band_past, BAND_W),
                              cache_band_v[l].reshape(DB, n_band_past, BAND_W), bk.reshape(DB, L, BAND_W),
                              bv.reshape(DB, L, BAND_W), bias_past, bias_new)
        ret_o, ret_s = _retention(rq.reshape(DB, L, RET_W), rk.reshape(DB, L, RET_W), rv.reshape(DB, L, RET_W),
                                  _block_diag_state(state_ret[l]), rt_s, consts, lw['ret_gain'], c=L)
        xs = _kout(xs, ret_o.reshape(DB * L, RET_W), mla_o.reshape(DB * L, MLA_W), band_o.reshape(DB * L, BAND_W),
                   gate, lw['w_out'], tm=TM_S)
        s_st.append((_diag_blocks(ret_s), ckv.reshape(DB, L, KV_LORA), kr.reshape(DB, L, MLA_ROPE),
                     bks.reshape(DB, L, BAND_HEADS, BAND_DH), bvs.reshape(DB, L, BAND_HEADS, BAND_DH)))

    stack = lambda sts, i: jnp.stack([s[i] for s in sts])
    return (xp.reshape(B, S, D_MODEL), xs.reshape(DB, L, D_MODEL),
            stack(p_st, 0), stack(p_st, 1), stack(p_st, 2), stack(p_st, 3), stack(p_st, 4),
            stack(s_st, 0), stack(s_st, 1), stack(s_st, 2), stack(s_st, 3), stack(s_st, 4))
```

```python
import functools

import jax
import jax.numpy as jnp
import numpy as np
from jax import lax
from jax.experimental import pallas as pl
from jax.experimental.pallas import tpu as pltpu

F32 = jnp.float32
BF16 = jnp.bfloat16

D_MODEL = 1024
DEPTH = 2
CHUNK = 64
EPS = 1e-6
NEG_INF = -1e30
ROPE_BASE = 10000.0
LOG2E = 1.4426950408889634
RET_HEADS, RET_DK, RET_DV = 4, 64, 64
RET_W = RET_HEADS * RET_DV
MLA_HEADS, MLA_NOPE, MLA_ROPE, MLA_V = 8, 64, 32, 64
MLA_QK = MLA_NOPE + MLA_ROPE
Q_LORA, KV_LORA = 256, 128
MLA_W = MLA_HEADS * MLA_V
BAND_HEADS, BAND_DH = 4, 64
BAND_W = BAND_HEADS * BAND_DH
BAND_PREV_CHUNKS = 8
BAND_PAST = BAND_PREV_CHUNKS * CHUNK
MAX_REL = 128
D_MIX = RET_W + MLA_W + BAND_W
SEG = (RET_HEADS * RET_DK, RET_HEADS * RET_DK, RET_W, RET_W, Q_LORA, KV_LORA, MLA_ROPE, MLA_W,
       BAND_W, BAND_W, BAND_W, BAND_W)

LANES = 128
HEAD_PAD = LANES
MLA_QW = MLA_HEADS * HEAD_PAD
C_AQ, C_AK, C_AV, C_CQ, C_CKV, C_KR, C_BQ, C_BK, C_BV, C_G, C_END = (
    0, 256, 512, 768, 1024, 1152, 1280, 1536, 1792, 2048, 3072)
VMEM_LIMIT = 56 * 1024 * 1024


def _dot(a, b):
    return jnp.dot(a, b, preferred_element_type=F32)


def _dot_nt(a, b):
    return lax.dot_general(a, b, (((1,), (1,)), ((), ())), preferred_element_type=F32)


def _dot_tn(a, b):
    return lax.dot_general(a, b, (((0,), (0,)), ((), ())), preferred_element_type=F32)


def _group_rms_normed(x, split, n_lo, n_hi):
    lane = lax.broadcasted_iota(jnp.int32, (1, LANES), 1)
    lo = lane < split
    outs = []
    for b in range(x.shape[-1] // LANES):
        xb = x[:, b * LANES:(b + 1) * LANES]
        sq = xb * xb
        if n_hi:
            s_lo = jnp.sum(jnp.where(lo, sq, 0.0), axis=-1, keepdims=True)
            s_hi = jnp.sum(jnp.where(lo, 0.0, sq), axis=-1, keepdims=True)
            inv = jnp.where(lo, lax.rsqrt(s_lo * (1.0 / n_lo) + EPS), lax.rsqrt(s_hi * (1.0 / n_hi) + EPS))
        else:
            inv = lax.rsqrt(jnp.sum(sq, axis=-1, keepdims=True) * (1.0 / n_lo) + EPS)
        outs.append(xb * inv)
    return jnp.concatenate(outs, axis=1)


def _swap_halves(x, half):
    w = x.shape[-1]
    lane = lax.broadcasted_iota(jnp.int32, (1, w), 1)
    nxt = pltpu.roll(x, w - half, axis=1)
    prv = pltpu.roll(x, half, axis=1)
    return jnp.where((lane & half) == 0, nxt, prv)


def _rope(x, cos, sin_signed, half):
    return x * cos + _swap_halves(x, half) * sin_signed


def _silu(g):
    return g * (1.0 / (1.0 + jnp.exp(-g)))


def _full(shape):
    nd = len(shape)
    return pl.BlockSpec(shape, lambda *_: (0,) * nd)


def _params(sem):
    return pltpu.CompilerParams(dimension_semantics=sem, vmem_limit_bytes=VMEM_LIMIT)


def _k1_body(x_ref, ng_ref, win_ref, wuq_ref, qag_ref, kvag_ref, krg_ref, qgain_ref, bqg_ref, bkg_ref,
             cret_ref, sret_ref, cq_ref, sq_ref, ckr_ref, skr_ref,
             rq_ref, rk_ref, rv_ref, gate_ref, qm_ref, ckv_ref, kr_ref, bq_ref, bk_ref, bv_ref,
             bks_ref, bvs_ref):
    x = x_ref[...]
    h = x * lax.rsqrt(jnp.mean(x * x, axis=-1, keepdims=True) + EPS) * ng_ref[...]
    hb = h.astype(BF16)

    def seg(lo, hi):
        return _dot(hb, win_ref[:, lo:hi])

    cq = seg(C_CQ, C_CKV)
    bq = seg(C_BQ, C_BK)
    bk = seg(C_BK, C_BV)

    cq = cq * lax.rsqrt(jnp.mean(cq * cq, axis=-1, keepdims=True) + EPS) * qag_ref[...]
    aq = seg(C_AQ, C_AK)
    ak = seg(C_AK, C_AV)
    qf = _dot(cq.astype(BF16), wuq_ref[...])

    bq_ref[...] = (_group_rms_normed(bq, BAND_DH, BAND_DH, BAND_DH) * bqg_ref[...]).astype(BF16)
    bk = _group_rms_normed(bk, BAND_DH, BAND_DH, BAND_DH) * bkg_ref[...]
    bk_ref[...] = bk.astype(BF16)
    bks_ref[...] = bk

    cret, sret = cret_ref[...], sret_ref[...]
    rq_ref[...] = _rope(aq, cret, sret, RET_DK // 2).astype(BF16)
    rk_ref[...] = _rope(ak, cret, sret, RET_DK // 2).astype(BF16)

    g = seg(C_G, C_END)

    gate_ref[...] = _silu(g).astype(BF16)
    av = seg(C_AV, C_CQ)
    ckv = seg(C_CKV, C_KR)
    kr = seg(C_KR, C_BQ)
    bv = seg(C_BV, C_G)

    qn = _group_rms_normed(qf, MLA_NOPE, MLA_NOPE, MLA_ROPE) * qgain_ref[...]
    cq_t = jnp.concatenate([cq_ref[...]] * MLA_HEADS, axis=1)
    sq_t = jnp.concatenate([sq_ref[...]] * MLA_HEADS, axis=1)
    qm_ref[...] = _rope(qn, cq_t, sq_t, MLA_ROPE // 2).astype(BF16)

    rv_ref[...] = av.astype(BF16)
    ckv_ref[...] = ckv * lax.rsqrt(jnp.mean(ckv * ckv, axis=-1, keepdims=True) + EPS) * kvag_ref[...]
    kr = kr * lax.rsqrt(jnp.sum(kr * kr, axis=-1, keepdims=True) * (1.0 / MLA_ROPE) + EPS) * krg_ref[...]
    kr = _rope(kr, ckr_ref[...], skr_ref[...], MLA_ROPE // 2)
    kr_ref[...] = kr[:, :MLA_ROPE]
    bv_ref[...] = bv.astype(BF16)
    bvs_ref[...] = bv


def _k1(x2d, lw, tabs, consts, *, tm, rows_per_seq, n_keep):
    T = x2d.shape[0]
    nb = rows_per_seq // tm
    nkb = n_keep // tm
    n_seq = T // rows_per_seq

    row = lambda w: pl.BlockSpec((tm, w), lambda i: (i, 0))
    tab = lambda w: pl.BlockSpec((tm, w), lambda i: (i % nb, 0))
    keep = pl.BlockSpec((tm, BAND_W), lambda i: ((i // nb) * nkb + jnp.maximum(i % nb - (nb - nkb), 0), 0))

    ins = [
        (x2d, row(D_MODEL)), (lw['norm_g'], _full((1, D_MODEL))), (lw['w_in'], _full((D_MODEL, C_END))),
        (lw['w_uq'], _full((Q_LORA, MLA_QW))), (lw['qa_g'], _full((1, Q_LORA))),
        (lw['kva_g'], _full((1, KV_LORA))), (lw['kr_g'], _full((1, LANES))),
        (lw['q_gain'], _full((1, MLA_QW))), (lw['bq_gain'], _full((1, BAND_W))),
        (lw['bk_gain'], _full((1, BAND_W))),
        (tabs['cret'], tab(RET_W)), (tabs['sret'], tab(RET_W)), (tabs['cq'], tab(LANES)),
        (tabs['sq'], tab(LANES)), (tabs['ckr'], tab(LANES)), (tabs['skr'], tab(LANES)),
    ]
    outs = [
        ((T, RET_W), BF16, row(RET_W)), ((T, RET_W), BF16, row(RET_W)), ((T, RET_W), BF16, row(RET_W)),
        ((T, D_MIX), BF16, row(D_MIX)), ((T, MLA_QW), BF16, row(MLA_QW)),
        ((T, KV_LORA), F32, row(KV_LORA)), ((T, MLA_ROPE), F32, row(MLA_ROPE)),
        ((T, BAND_W), BF16, row(BAND_W)), ((T, BAND_W), BF16, row(BAND_W)), ((T, BAND_W), BF16, row(BAND_W)),
        ((n_seq * n_keep, BAND_W), F32, keep), ((n_seq * n_keep, BAND_W), F32, keep),
    ]
    return pl.pallas_call(
        _k1_body,
        grid=(T // tm,),
        in_specs=[s for _, s in ins],
        out_specs=[s for _, _, s in outs],
        out_shape=[jax.ShapeDtypeStruct(sh, dt) for sh, dt, _ in outs],
        compiler_params=_params(("arbitrary",)),
        name="k1_proj",
    )(*[a for a, _ in ins])


def _kkv_body(ckv_ref, kr_ref, wk_ref, wv_ref, kgain_ref, k_ref, v_ref, *, transpose_v):
    c = ckv_ref[...].astype(BF16)
    kn = _dot(c, wk_ref[...])
    kn = _group_rms_normed(kn, LANES, MLA_NOPE, 0) * kgain_ref[...]
    kr = kr_ref[...]
    rows = kr.shape[0]
    kr_block = jnp.concatenate([jnp.zeros((rows, MLA_NOPE), F32), kr,
                                jnp.zeros((rows, HEAD_PAD - MLA_QK), F32)], axis=1)
    k_ref[...] = (kn + jnp.concatenate([kr_block] * MLA_HEADS, axis=1)).astype(BF16)
    if transpose_v:
        v_ref[...] = _dot_nt(wv_ref[...], c).astype(BF16)
    else:
        v_ref[...] = _dot(c, wv_ref[...]).astype(BF16)


def _kkv(ckv2d, kr2d, lw, *, tm, seq=None):
    T = ckv2d.shape[0]
    row = lambda w: pl.BlockSpec((tm, w), lambda i: (i, 0))
    if seq is None:
        wv, wv_spec = lw['w_uv'], _full((KV_LORA, MLA_W))
        v_spec, v_shape = row(MLA_W), (T, MLA_W)
    else:
        nb = seq // tm
        wv, wv_spec = lw['w_uv_t'], _full((MLA_W, KV_LORA))
        v_spec = pl.BlockSpec((None, MLA_W, tm), lambda i: (i // nb, 0, i % nb))
        v_shape = (T // seq, MLA_W, seq)
    return pl.pallas_call(
        functools.partial(_kkv_body, transpose_v=seq is not None),
        grid=(T // tm,),
        in_specs=[row(KV_LORA), row(MLA_ROPE), _full((KV_LORA, MLA_QW)), wv_spec, _full((1, MLA_QW))],
        out_specs=[row(MLA_QW), v_spec],
        out_shape=[jax.ShapeDtypeStruct((T, MLA_QW), BF16), jax.ShapeDtypeStruct(v_shape, BF16)],
        compiler_params=_params(("arbitrary",)),
        name="kkv_up",
    )(ckv2d, kr2d, lw['w_uk'], wv, lw['k_gain'])


def _mla_prompt_body(q_ref, k_ref, vt_ref, o_ref, m_sc, l_sc, acc_sc, *, tq):
    qi = pl.program_id(1)
    m_sc[...] = jnp.full(m_sc.shape, NEG_INF, F32)
    l_sc[...] = jnp.zeros(l_sc.shape, F32)
    acc_sc[...] = jnp.zeros(acc_sc.shape, F32)

    def tile(kb, masked):
        start = pl.multiple_of(kb * tq, tq)
        if masked:
            kc = lax.broadcasted_iota(jnp.int32, (tq, tq), 0) // CHUNK
            qc = lax.broadcasted_iota(jnp.int32, (tq, tq), 1) // CHUNK
            visible = kc <= qc

        def scores(h):
            q = q_ref[:, h * HEAD_PAD:(h + 1) * HEAD_PAD]
            k = k_ref[pl.ds(start, tq), h * HEAD_PAD:(h + 1) * HEAD_PAD]
            st = _dot_nt(k, q)
            return jnp.where(visible, st, NEG_INF) if masked else st

        def softmax(h, st):
            m_prev = m_sc[h:h + 1, :]
            m_new = jnp.maximum(m_prev, jnp.max(st, axis=0, keepdims=True))
            alpha = jnp.exp2(m_prev - m_new)
            p = jnp.exp2(st - m_new)
            l_sc[h:h + 1, :] = alpha * l_sc[h:h + 1, :] + jnp.sum(p, axis=0, keepdims=True)
            m_sc[h:h + 1, :] = m_new
            return p.astype(BF16), alpha

        def values(h, p, alpha):
            vt = vt_ref[h * MLA_V:(h + 1) * MLA_V, pl.ds(start, tq)]
            acc_sc[h] = alpha * acc_sc[h] + _dot(vt, p)

        st = {0: scores(0), 1: scores(1)}
        pa = {}
        for h in range(MLA_HEADS):
            pa[h] = softmax(h, st.pop(h))
            if h >= 1:
                values(h - 1, *pa.pop(h - 1))
            if h + 2 < MLA_HEADS:
                st[h + 2] = scores(h + 2)
        values(MLA_HEADS - 1, *pa.pop(MLA_HEADS - 1))

    def body(kb, carry):
        tile(kb, False)
        return carry

    lax.fori_loop(0, qi, body, 0)
    tile(qi, True)
    for j in range(MLA_HEADS // 2):
        pair = jnp.concatenate([acc_sc[2 * j] * (1.0 / l_sc[2 * j:2 * j + 1, :]),
                                acc_sc[2 * j + 1] * (1.0 / l_sc[2 * j + 1:2 * j + 2, :])], axis=0)
        o_ref[:, j * LANES:(j + 1) * LANES] = pair.T.astype(BF16)


def _mla_prompt(q, k, vt, *, tq):
    B, S, _ = q.shape
    once = pl.Buffered(1)
    stat = pltpu.VMEM((MLA_HEADS, tq), F32)
    return pl.pallas_call(
        functools.partial(_mla_prompt_body, tq=tq),
        grid=(B, S // tq),
        in_specs=[pl.BlockSpec((None, tq, MLA_QW), lambda b, i: (b, i, 0)),
                  pl.BlockSpec((None, S, MLA_QW), lambda b, i: (b, 0, 0), pipeline_mode=once),
                  pl.BlockSpec((None, MLA_W, S), lambda b, i: (b, 0, 0), pipeline_mode=once)],
        out_specs=pl.BlockSpec((None, tq, MLA_W), lambda b, i: (b, i, 0)),
        out_shape=jax.ShapeDtypeStruct((B, S, MLA_W), BF16),
        scratch_shapes=[stat, stat, pltpu.VMEM((MLA_HEADS, MLA_V, tq), F32)],
        compiler_params=_params(("arbitrary", "arbitrary")),
        name="mla_prompt",
    )(q, k, vt)


def _mla_sample_body(q_ref, kp_ref, vp_ref, kn_ref, vn_ref, o_ref):
    lane = lax.broadcasted_iota(jnp.int32, (1, LANES), 1)
    outs = []
    for h in range(MLA_HEADS):
        hs = slice(h * HEAD_PAD, (h + 1) * HEAD_PAD)
        ps = slice((h // 2) * LANES, (h // 2 + 1) * LANES)
        v_lanes = (lane // MLA_V) == (h % 2)
        q = q_ref[:, hs]
        s1 = _dot_nt(q, kp_ref[:, hs])
        s2 = _dot_nt(q, kn_ref[:, hs])
        m = jnp.maximum(jnp.max(s1, axis=-1, keepdims=True), jnp.max(s2, axis=-1, keepdims=True))
        p1 = jnp.exp2(s1 - m)
        p2 = jnp.exp2(s2 - m)
        l = jnp.sum(p1, axis=-1, keepdims=True) + jnp.sum(p2, axis=-1, keepdims=True)
        v1 = vp_ref[:, ps]
        v2 = vn_ref[:, ps]
        acc = (_dot(p1.astype(BF16), jnp.where(v_lanes, v1, jnp.zeros_like(v1)))
               + _dot(p2.astype(BF16), jnp.where(v_lanes, v2, jnp.zeros_like(v2))))
        outs.append(acc * (1.0 / l))
    for j in range(MLA_HEADS // 2):
        o_ref[:, j * LANES:(j + 1) * LANES] = (outs[2 * j] + outs[2 * j + 1]).astype(BF16)


def _mla_sample(q, kp, vp, kn, vn):
    B, L, _ = q.shape
    P = kp.shape[1]
    blk = lambda n, w: pl.BlockSpec((None, n, w), lambda b: (b, 0, 0))
    return pl.pallas_call(
        _mla_sample_body,
        grid=(B,),
        in_specs=[blk(L, MLA_QW), blk(P, MLA_QW), blk(P, MLA_W), blk(L, MLA_QW), blk(L, MLA_W)],
        out_specs=blk(L, MLA_W),
        out_shape=jax.ShapeDtypeStruct((B, L, MLA_W), BF16),
        compiler_params=_params(("arbitrary",)),
        name="mla_sample",
    )(q, kp, vp, kn, vn)


def _band_heads(q, pieces, biases, out_dtype):
    lane = lax.broadcasted_iota(jnp.int32, (1, BAND_W), 1)
    out = jnp.zeros((q.shape[0], BAND_W), F32)
    for h in range(BAND_HEADS):
        sel = (lane // BAND_DH) == h
        qh = jnp.where(sel, q, jnp.zeros_like(q))
        ss = []
        for (k, _, gate), b_ref in zip(pieces, biases):
            s = _dot_nt(qh, k) + b_ref[h]
            if gate is not None:
                s = jnp.where(gate, s, NEG_INF)
            ss.append(s)
        m = functools.reduce(jnp.maximum, [jnp.max(s, axis=-1, keepdims=True) for s in ss])
        ps = [jnp.exp(s - m) for s in ss]
        l = functools.reduce(jnp.add, [jnp.sum(p, axis=-1, keepdims=True) for p in ps])
        acc = functools.reduce(jnp.add, [
            _dot(p.astype(BF16), jnp.where(sel, v, jnp.zeros_like(v))) for p, (_, v, _) in zip(ps, pieces)])
        out = out + acc * (1.0 / l)
    return out.astype(out_dtype)


def _band_prompt_body(q_ref, kp_ref, kc_ref, vp_ref, vc_ref, bp_ref, bc_ref, o_ref):
    has_prev = pl.program_id(1) > 0
    o_ref[...] = _band_heads(q_ref[...],
                             [(kp_ref[...], vp_ref[...], has_prev), (kc_ref[...], vc_ref[...], None)],
                             [bp_ref, bc_ref], BF16)


def _band_prompt(q, k, v, bias_prev, bias_cur, *, tq):
    B, S, _ = q.shape
    cur = pl.BlockSpec((None, tq, BAND_W), lambda b, i: (b, i, 0))
    prev = pl.BlockSpec((None, tq, BAND_W), lambda b, i: (b, jnp.maximum(i - 1, 0), 0))
    bias = _full((BAND_HEADS, tq, tq))
    return pl.pallas_call(
        _band_prompt_body,
        grid=(B, S // tq),
        in_specs=[cur, prev, cur, prev, cur, bias, bias],
        out_specs=cur,
        out_shape=jax.ShapeDtypeStruct((B, S, BAND_W), BF16),
        compiler_params=_params(("arbitrary", "arbitrary")),
        name="band_prompt",
    )(q, k, k, v, v, bias_prev, bias_cur)


def _band_sample_body(q_ref, kp_ref, vp_ref, kn_ref, vn_ref, bp_ref, bn_ref, o_ref):
    o_ref[...] = _band_heads(q_ref[...],
                             [(kp_ref[...].astype(BF16), vp_ref[...].astype(BF16), None),
                              (kn_ref[...], vn_ref[...], None)],
                             [bp_ref, bn_ref], BF16)


def _band_sample(q, kp, vp, kn, vn, bias_past, bias_new):
    B, L, _ = q.shape
    P = kp.shape[1]
    blk = lambda n: pl.BlockSpec((None, n, BAND_W), lambda b: (b, 0, 0))
    return pl.pallas_call(
        _band_sample_body,
        grid=(B,),
        in_specs=[blk(L), blk(P), blk(P), blk(L), blk(L), _full((BAND_HEADS, L, P)), _full((BAND_HEADS, L, L))],
        out_specs=blk(L),
        out_shape=jax.ShapeDtypeStruct((B, L, BAND_W), BF16),
        compiler_params=_params(("arbitrary",)),
        name="band_sample",
    )(q, kp, vp, kn, vn, bias_past, bias_new)


def _ret_body(q_ref, k_ref, v_ref, s0_ref, d_ref, xi_ref, zeta_ref, gam_ref, bd_ref, gain_ref,
              o_ref, sfin_ref, s_sc):
    @pl.when(pl.program_id(1) == 0)
    def _():
        s_sc[...] = s0_ref[...]

    q, k, v = q_ref[...], k_ref[...], v_ref[...]
    state = s_sc[...]
    lane = lax.broadcasted_iota(jnp.int32, (1, RET_W), 1)
    o = _dot(q, state.astype(BF16)) * xi_ref[...]
    for h in range(RET_HEADS):
        sel = (lane // RET_DV) == h
        a = _dot_nt(jnp.where(sel, q, jnp.zeros_like(q)), k) * d_ref[h]
        o = o + _dot(a.astype(BF16), jnp.where(sel, v, jnp.zeros_like(v)))
    kz = (k.astype(F32) * zeta_ref[...]).astype(BF16)
    s_new = gam_ref[...] * state + bd_ref[...] * _dot_tn(kz, v)
    s_sc[...] = s_new
    sfin_ref[...] = s_new
    o_ref[...] = (_group_rms_normed(o, RET_DV, RET_DV, RET_DV) * gain_ref[...]).astype(BF16)


def _retention(q, k, v, s0, rt, consts, gain, *, c):
    B, S, _ = q.shape
    blk = pl.BlockSpec((None, c, RET_W), lambda b, t: (b, t, 0))
    st = pl.BlockSpec((None, RET_W, RET_W), lambda b, t: (b, 0, 0))
    return pl.pallas_call(
        _ret_body,
        grid=(B, S // c),
        in_specs=[blk, blk, blk, st, _full((RET_HEADS, c, c)), _full((c, RET_W)), _full((c, RET_W)),
                  _full((RET_W, RET_W)), _full((RET_W, RET_W)), _full((1, RET_W))],
        out_specs=[blk, st],
        out_shape=[jax.ShapeDtypeStruct((B, S, RET_W), BF16), jax.ShapeDtypeStruct((B, RET_W, RET_W), F32)],
        scratch_shapes=[pltpu.VMEM((RET_W, RET_W), F32)],
        compiler_params=_params(("arbitrary", "arbitrary")),
        name="retention",
    )(q, k, v, s0, rt['decay'], rt['xi'], rt['zeta'], rt['gamma'], consts['bd_mask'], gain)


def _kout_body(x_ref, ro_ref, mo_ref, bo_ref, g_ref, w_ref, y_ref):
    g = g_ref[...]
    y = x_ref[...]
    y = y + _dot(ro_ref[...] * g[:, :RET_W], w_ref[:RET_W, :])
    y = y + _dot(mo_ref[...] * g[:, RET_W:RET_W + MLA_W], w_ref[RET_W:RET_W + MLA_W, :])
    y = y + _dot(bo_ref[...] * g[:, RET_W + MLA_W:], w_ref[RET_W + MLA_W:, :])
    y_ref[...] = y


def _kout(x2d, ro, mo, bo, gate, w_out, *, tm):
    T = x2d.shape[0]
    row = lambda w: pl.BlockSpec((tm, w), lambda i: (i, 0))
    return pl.pallas_call(
        _kout_body,
        grid=(T // tm,),
        in_specs=[row(D_MODEL), row(RET_W), row(MLA_W), row(BAND_W), row(D_MIX), _full((D_MIX, D_MODEL))],
        out_specs=row(D_MODEL),
        out_shape=jax.ShapeDtypeStruct((T, D_MODEL), F32),
        compiler_params=_params(("arbitrary",)),
        name="kout_proj",
    )(x2d, ro, mo, bo, gate, w_out)


def _constants():
    bd = np.kron(np.eye(RET_HEADS, dtype=np.float32), np.ones((RET_DK, RET_DV), np.float32))
    return {'bd_mask': jnp.asarray(bd)}


def _rope_tables(pos):
    pos = pos.astype(F32)

    def cs(half):
        inv = ROPE_BASE ** (-jnp.arange(half, dtype=F32) / half)
        ang = pos[:, None] * inv[None, :]
        c, s = jnp.cos(ang), jnp.sin(ang)
        return jnp.concatenate([c, c], axis=-1), jnp.concatenate([-s, s], axis=-1)

    T = pos.shape[0]
    c32, s32 = cs(RET_DK // 2)
    c16, s16 = cs(MLA_ROPE // 2)
    ones = lambda w: jnp.ones((T, w), F32)
    zeros = lambda w: jnp.zeros((T, w), F32)
    pad = HEAD_PAD - MLA_QK
    return {
        'cret': jnp.tile(c32, (1, RET_HEADS)), 'sret': jnp.tile(s32, (1, RET_HEADS)),
        'cq': jnp.concatenate([ones(MLA_NOPE), c16, ones(pad)], axis=-1),
        'sq': jnp.concatenate([zeros(MLA_NOPE), s16, zeros(pad)], axis=-1),
        'ckr': jnp.concatenate([c16, ones(LANES - MLA_ROPE)], axis=-1),
        'skr': jnp.concatenate([s16, zeros(LANES - MLA_ROPE)], axis=-1),
    }


def _retention_tables(c):
    lg = jnp.log1p(-jnp.exp2(-5.0 - jnp.arange(RET_HEADS, dtype=F32)))
    idx = jnp.arange(c, dtype=F32)
    diff = idx[:, None] - idx[None, :]
    decay = jnp.where(diff >= 0, jnp.exp(lg[:, None, None] * jnp.maximum(diff, 0.0)), 0.0)
    per_lane = lambda t: jnp.repeat(t, RET_DV, axis=-1)
    xi = per_lane(jnp.exp(lg[None, :] * (idx[:, None] + 1.0)))
    zeta = per_lane(jnp.exp(lg[None, :] * (c - 1.0 - idx)[:, None]))
    gamma = jnp.broadcast_to(per_lane(jnp.exp(lg * c)[None, :]).T, (RET_W, RET_W))
    return {'decay': decay, 'xi': xi, 'zeta': zeta, 'gamma': gamma}


def _band_tables_body(rp_ref, rc_ref, prev_ref, cur_ref, past_ref, new_ref, *, tq, n_new):
    w = 2 * tq
    tp = pltpu.roll(jnp.broadcast_to(rp_ref[...], (tq, w)), 0, 1, stride=1, stride_axis=0)[:, :tq]
    tc = pltpu.roll(jnp.broadcast_to(rc_ref[...], (tq, w)), 0, 1, stride=1, stride_axis=0)[:, :tq]
    qc = lax.broadcasted_iota(jnp.int32, (tq, tq), 0) // CHUNK
    kc = lax.broadcasted_iota(jnp.int32, (tq, tq), 1) // CHUNK
    prev_ref[...] = jnp.where(kc >= qc, tp, NEG_INF)
    cur_ref[...] = jnp.where(kc <= qc, tc, NEG_INF)
    past_ref[...] = tp[:n_new, :]
    new_ref[...] = tc[:n_new, :n_new]


def _band_tables(band_bias, *, tq, n_new):
    H = band_bias.shape[0]
    lo, mid, hi = band_bias[:, :1], band_bias[:, MAX_REL + 1:2 * MAX_REL], band_bias[:, 2 * MAX_REL:]
    rep = lambda col, n: jnp.broadcast_to(col, (H, n))
    r_cur = jnp.concatenate([band_bias[:, MAX_REL::-1], rep(lo, tq - MAX_REL - 1), rep(hi, tq - MAX_REL),
                             band_bias[:, :MAX_REL:-1]], axis=1)
    r_prev = jnp.concatenate([rep(hi, tq - MAX_REL + 1), mid[:, ::-1], rep(hi, tq)], axis=1)
    row = pl.BlockSpec((None, 1, 2 * tq), lambda h: (h, 0, 0))
    tile = lambda n, m: pl.BlockSpec((None, n, m), lambda h: (h, 0, 0))
    return pl.pallas_call(
        functools.partial(_band_tables_body, tq=tq, n_new=n_new),
        grid=(H,),
        in_specs=[row, row],
        out_specs=[tile(tq, tq), tile(tq, tq), tile(n_new, tq), tile(n_new, n_new)],
        out_shape=[jax.ShapeDtypeStruct((H, tq, tq), F32), jax.ShapeDtypeStruct((H, tq, tq), F32),
                   jax.ShapeDtypeStruct((H, n_new, tq), F32), jax.ShapeDtypeStruct((H, n_new, n_new), F32)],
        compiler_params=_params(("arbitrary",)),
        name="band_tables",
    )(r_prev[:, None, :], r_cur[:, None, :])


def _layer_weights(l, norm_g, w_in, ret_gn_g, mla_qa_g, mla_w_uq, mla_qn_g, mla_qr_g, mla_kva_g, mla_kr_g,
                   mla_w_ukv, mla_kn_g, band_qn_g, band_kn_g, w_out):
    cuts = np.cumsum(SEG)[:-1].tolist()
    a_q, a_k, a_v, a_g, b_cq, b_ckv, b_kr, b_g, c_q, c_k, c_v, c_g = jnp.split(w_in[l], cuts, axis=-1)
    b_kr = jnp.pad(b_kr, ((0, 0), (0, LANES - MLA_ROPE)))
    w_in_p = jnp.concatenate([a_q, a_k * (RET_DK ** -0.5), a_v, b_cq, b_ckv, b_kr, c_q, c_k, c_v, a_g, b_g, c_g],
                             axis=-1).astype(BF16)
    pad = HEAD_PAD - MLA_QK
    w_uq = jnp.pad(mla_w_uq[l].reshape(Q_LORA, MLA_HEADS, MLA_QK), ((0, 0), (0, 0), (0, pad)))
    ukv = mla_w_ukv[l].reshape(KV_LORA, MLA_HEADS, MLA_NOPE + MLA_V)
    w_uk = jnp.pad(ukv[:, :, :MLA_NOPE], ((0, 0), (0, 0), (0, HEAD_PAD - MLA_NOPE)))
    zpad = jnp.zeros((pad,), F32)
    q_gain = jnp.tile(jnp.concatenate([mla_qn_g[l], mla_qr_g[l], zpad]) * (MLA_QK ** -0.5 * LOG2E), MLA_HEADS)
    k_gain = jnp.tile(jnp.concatenate([mla_kn_g[l], jnp.zeros((HEAD_PAD - MLA_NOPE,), F32)]), MLA_HEADS)
    return {
        'norm_g': norm_g[l][None], 'w_in': w_in_p,
        'w_uq': w_uq.reshape(Q_LORA, MLA_QW).astype(BF16), 'qa_g': mla_qa_g[l][None],
        'kva_g': mla_kva_g[l][None],
        'kr_g': jnp.concatenate([mla_kr_g[l], jnp.zeros((LANES - MLA_ROPE,), F32)])[None],
        'q_gain': q_gain[None],
        'bq_gain': (jnp.tile(band_qn_g[l], BAND_HEADS) * (BAND_DH ** -0.5))[None],
        'bk_gain': jnp.tile(band_kn_g[l], BAND_HEADS)[None],
        'w_uk': w_uk.reshape(KV_LORA, MLA_QW).astype(BF16),
        'w_uv': ukv[:, :, MLA_NOPE:].reshape(KV_LORA, MLA_W).astype(BF16),
        'w_uv_t': ukv[:, :, MLA_NOPE:].reshape(KV_LORA, MLA_W).T.astype(BF16),
        'k_gain': k_gain[None], 'ret_gain': ret_gn_g[l][None], 'w_out': w_out[l].astype(BF16),
    }


def _block_diag_state(s):
    B = s.shape[0]
    eye = jnp.eye(RET_HEADS, dtype=s.dtype)
    return jnp.einsum('bhde,hg->bhdge', s, eye).reshape(B, RET_W, RET_W)


def _diag_blocks(s_bd):
    B = s_bd.shape[0]
    s = s_bd.reshape(B, RET_HEADS, RET_DK, RET_HEADS, RET_DV)
    return jnp.stack([s[:, h, :, h, :] for h in range(RET_HEADS)], axis=1)


def kernel(x_prompt, x_sample, state_ret, cache_mla_ckv, cache_mla_krope, cache_band_k, cache_band_v, norm_g, w_in, ret_gn_g, mla_qa_g, mla_w_uq, mla_qn_g, mla_qr_g, mla_kva_g, mla_kr_g, mla_w_ukv, mla_kn_g, band_qn_g, band_kn_g, band_bias, w_out):
    B, S, _ = x_prompt.shape
    DB, L, _ = x_sample.shape
    past_len = cache_mla_ckv.shape[2]
    n_band_past = cache_band_k.shape[2]
    n_keep_p = min(BAND_PAST, S)

    TM_P, TM_S, TQ, RET_C = 512, 256, 512, 256
    consts = _constants()
    tabs_p = _rope_tables(jnp.arange(S))
    tabs_s = {k: jnp.tile(v, (DB, 1)) for k, v in _rope_tables(past_len + jnp.arange(L)).items()}
    rt_p = _retention_tables(RET_C)
    rt_s = _retention_tables(L)

    xp = x_prompt.reshape(B * S, D_MODEL)
    xs = x_sample.reshape(DB * L, D_MODEL)
    zeros_state = jnp.zeros((B, RET_W, RET_W), F32)
    p_st, s_st = [], []
    for l in range(DEPTH):
        lw = _layer_weights(l, norm_g, w_in, ret_gn_g, mla_qa_g, mla_w_uq, mla_qn_g, mla_qr_g, mla_kva_g,
                            mla_kr_g, mla_w_ukv, mla_kn_g, band_qn_g, band_kn_g, w_out)

        rq, rk, rv, gate, qm, ckv, kr, bq, bk, bv, bks, bvs = _k1(
            xp, lw, tabs_p, consts, tm=TM_P, rows_per_seq=S, n_keep=n_keep_p)
        km, vt = _kkv(ckv, kr, lw, tm=TM_P, seq=S)
        mla_o = _mla_prompt(qm.reshape(B, S, MLA_QW), km.reshape(B, S, MLA_QW), vt, tq=TQ)
        assert TQ == BAND_PAST == n_band_past
        bias_prev, bias_cur, bias_past, bias_new = _band_tables(band_bias[l], tq=TQ, n_new=L)
        band_o = _band_prompt(bq.reshape(B, S, BAND_W), bk.reshape(B, S, BAND_W), bv.reshape(B, S, BAND_W),
                              bias_prev, bias_cur, tq=TQ)
        ret_o, ret_s = _retention(rq.reshape(B, S, RET_W), rk.reshape(B, S, RET_W), rv.reshape(B, S, RET_W),
                                  zeros_state, rt_p, consts, lw['ret_gain'], c=RET_C)
        xp = _kout(xp, ret_o.reshape(B * S, RET_W), mla_o.reshape(B * S, MLA_W), band_o.reshape(B * S, BAND_W),
                   gate, lw['w_out'], tm=TM_P)
        p_st.append((_diag_blocks(ret_s), ckv.reshape(B, S, KV_LORA), kr.reshape(B, S, MLA_ROPE),
                     bks.reshape(B, n_keep_p, BAND_HEADS, BAND_DH), bvs.reshape(B, n_keep_p, BAND_HEADS, BAND_DH)))

        rq, rk, rv, gate, qm, ckv, kr, bq, bk, bv, bks, bvs = _k1(
            xs, lw, tabs_s, consts, tm=TM_S, rows_per_seq=DB * L, n_keep=DB * L)
        kn, vn = _kkv(ckv, kr, lw, tm=TM_S)
        kp, vp = _kkv(cache_mla_ckv[l].reshape(DB * past_len, KV_LORA),
                      cache_mla_krope[l].reshape(DB * past_len, MLA_ROPE), lw, tm=TM_P)
        mla_o = _mla_sample(qm.reshape(DB, L, MLA_QW), kp.reshape(DB, past_len, MLA_QW),
                            vp.reshape(DB, past_len, MLA_W), kn.reshape(DB, L, MLA_QW), vn.reshape(DB, L, MLA_W))
        band_o = _band_sample(bq.reshape(DB, L, BAND_W), cache_band_k[l].reshape(DB, n_band_past, BAND_W),
                              cache_band_v[l].reshape(DB, n_band_past, BAND_W), bk.reshape(DB, L, BAND_W),
                              bv.reshape(DB, L, BAND_W), bias_past, bias_new)
        ret_o, ret_s = _retention(rq.reshape(DB, L, RET_W), rk.reshape(DB, L, RET_W), rv.reshape(DB, L, RET_W),
                                  _block_diag_state(state_ret[l]), rt_s, consts, lw['ret_gain'], c=L)
        xs = _kout(xs, ret_o.reshape(DB * L, RET_W), mla_o.reshape(DB * L, MLA_W), band_o.reshape(DB * L, BAND_W),
                   gate, lw['w_out'], tm=TM_S)
        s_st.append((_diag_blocks(ret_s), ckv.reshape(DB, L, KV_LORA), kr.reshape(DB, L, MLA_ROPE),
                     bks.reshape(DB, L, BAND_HEADS, BAND_DH), bvs.reshape(DB, L, BAND_HEADS, BAND_DH)))

    stack = lambda sts, i: jnp.stack([s[i] for s in sts])
    return (xp.reshape(B, S, D_MODEL), xs.reshape(DB, L, D_MODEL),
            stack(p_st, 0), stack(p_st, 1), stack(p_st, 2), stack(p_st, 3), stack(p_st, 4),
            stack(s_st, 0), stack(s_st, 1), stack(s_st, 2), stack(s_st, 3), stack(s_st, 4))
```

```python
import functools

import jax
import jax.numpy as jnp
import numpy as np
from jax import lax
from jax.experimental import pallas as pl
from jax.experimental.pallas import tpu as pltpu

F32 = jnp.float32
BF16 = jnp.bfloat16

D_MODEL = 1024
DEPTH = 2
CHUNK = 64
EPS = 1e-6
NEG_INF = -1e30
ROPE_BASE = 10000.0
LOG2E = 1.4426950408889634
RET_HEADS, RET_DK, RET_DV = 4, 64, 64
RET_W = RET_HEADS * RET_DV
MLA_HEADS, MLA_NOPE, MLA_ROPE, MLA_V = 8, 64, 32, 64
MLA_QK = MLA_NOPE + MLA_ROPE
Q_LORA, KV_LORA = 256, 128
MLA_W = MLA_HEADS * MLA_V
BAND_HEADS, BAND_DH = 4, 64
BAND_W = BAND_HEADS * BAND_DH
BAND_PREV_CHUNKS = 8
BAND_PAST = BAND_PREV_CHUNKS * CHUNK
MAX_REL = 128
D_MIX = RET_W + MLA_W + BAND_W
SEG = (RET_HEADS * RET_DK, RET_HEADS * RET_DK, RET_W, RET_W, Q_LORA, KV_LORA, MLA_ROPE, MLA_W,
       BAND_W, BAND_W, BAND_W, BAND_W)

LANES = 128
HEAD_PAD = LANES
MLA_QW = MLA_HEADS * HEAD_PAD
MLA_VROWS = MLA_V + 16
MLA_VT = MLA_HEADS * MLA_VROWS
C_AQ, C_AK, C_AV, C_CQ, C_CKV, C_KR, C_BQ, C_BK, C_BV, C_G, C_END = (
    0, 256, 512, 768, 1024, 1152, 1280, 1536, 1792, 2048, 3072)
VMEM_LIMIT = 56 * 1024 * 1024


def _dot(a, b):
    return jnp.dot(a, b, preferred_element_type=F32)


def _dot_nt(a, b):
    return lax.dot_general(a, b, (((1,), (1,)), ((), ())), preferred_element_type=F32)


def _dot_tn(a, b):
    return lax.dot_general(a, b, (((0,), (0,)), ((), ())), preferred_element_type=F32)


def _group_rms_normed(x, split, n_lo, n_hi):
    lane = lax.broadcasted_iota(jnp.int32, (1, LANES), 1)
    lo = lane < split
    outs = []
    for b in range(x.shape[-1] // LANES):
        xb = x[:, b * LANES:(b + 1) * LANES]
        sq = xb * xb
        if n_hi:
            s_lo = jnp.sum(jnp.where(lo, sq, 0.0), axis=-1, keepdims=True)
            s_hi = jnp.sum(jnp.where(lo, 0.0, sq), axis=-1, keepdims=True)
            inv = jnp.where(lo, lax.rsqrt(s_lo * (1.0 / n_lo) + EPS), lax.rsqrt(s_hi * (1.0 / n_hi) + EPS))
        else:
            inv = lax.rsqrt(jnp.sum(sq, axis=-1, keepdims=True) * (1.0 / n_lo) + EPS)
        outs.append(xb * inv)
    return jnp.concatenate(outs, axis=1)


def _swap_halves(x, half):
    w = x.shape[-1]
    lane = lax.broadcasted_iota(jnp.int32, (1, w), 1)
    nxt = pltpu.roll(x, w - half, axis=1)
    prv = pltpu.roll(x, half, axis=1)
    return jnp.where((lane & half) == 0, nxt, prv)


def _rope(x, cos, sin_signed, half):
    return x * cos + _swap_halves(x, half) * sin_signed


def _silu(g):
    return g * (1.0 / (1.0 + jnp.exp(-g)))


def _full(shape):
    nd = len(shape)
    return pl.BlockSpec(shape, lambda *_: (0,) * nd)


def _params(sem):
    return pltpu.CompilerParams(dimension_semantics=sem, vmem_limit_bytes=VMEM_LIMIT)


def _k1_body(x_ref, ng_ref, win_ref, wuq_ref, qag_ref, kvag_ref, krg_ref, qgain_ref, bqg_ref, bkg_ref,
             cret_ref, sret_ref, cq_ref, sq_ref, ckr_ref, skr_ref,
             rq_ref, rk_ref, rv_ref, gate_ref, qm_ref, ckv_ref, kr_ref, bq_ref, bk_ref, bv_ref,
             bks_ref, bvs_ref):
    x = x_ref[...]
    h = x * lax.rsqrt(jnp.mean(x * x, axis=-1, keepdims=True) + EPS) * ng_ref[...]
    hb = h.astype(BF16)

    def seg(lo, hi):
        return _dot(hb, win_ref[:, lo:hi])

    cq = seg(C_CQ, C_CKV)
    bq = seg(C_BQ, C_BK)
    bk = seg(C_BK, C_BV)

    cq = cq * lax.rsqrt(jnp.mean(cq * cq, axis=-1, keepdims=True) + EPS) * qag_ref[...]
    aq = seg(C_AQ, C_AK)
    ak = seg(C_AK, C_AV)
    qf = _dot(cq.astype(BF16), wuq_ref[...])

    bq_ref[...] = (_group_rms_normed(bq, BAND_DH, BAND_DH, BAND_DH) * bqg_ref[...]).astype(BF16)
    bk = _group_rms_normed(bk, BAND_DH, BAND_DH, BAND_DH) * bkg_ref[...]
    bk_ref[...] = bk.astype(BF16)
    bks_ref[...] = bk

    cret, sret = cret_ref[...], sret_ref[...]
    rq_ref[...] = _rope(aq, cret, sret, RET_DK // 2).astype(BF16)
    rk_ref[...] = _rope(ak, cret, sret, RET_DK // 2).astype(BF16)

    g = seg(C_G, C_END)

    gate_ref[...] = _silu(g).astype(BF16)
    av = seg(C_AV, C_CQ)
    ckv = seg(C_CKV, C_KR)
    kr = seg(C_KR, C_BQ)
    bv = seg(C_BV, C_G)

    qn = _group_rms_normed(qf, MLA_NOPE, MLA_NOPE, MLA_ROPE) * qgain_ref[...]
    cq_t = jnp.concatenate([cq_ref[...]] * MLA_HEADS, axis=1)
    sq_t = jnp.concatenate([sq_ref[...]] * MLA_HEADS, axis=1)
    qm_ref[...] = _rope(qn, cq_t, sq_t, MLA_ROPE // 2).astype(BF16)

    rv_ref[...] = av.astype(BF16)
    ckv_ref[...] = ckv * lax.rsqrt(jnp.mean(ckv * ckv, axis=-1, keepdims=True) + EPS) * kvag_ref[...]
    kr = kr * lax.rsqrt(jnp.sum(kr * kr, axis=-1, keepdims=True) * (1.0 / MLA_ROPE) + EPS) * krg_ref[...]
    kr = _rope(kr, ckr_ref[...], skr_ref[...], MLA_ROPE // 2)
    kr_ref[...] = kr[:, :MLA_ROPE]
    bv_ref[...] = bv.astype(BF16)
    bvs_ref[...] = bv


def _k1(x2d, lw, tabs, consts, *, tm, rows_per_seq, n_keep):
    T = x2d.shape[0]
    nb = rows_per_seq // tm
    nkb = n_keep // tm
    n_seq = T // rows_per_seq

    row = lambda w: pl.BlockSpec((tm, w), lambda i: (i, 0))
    tab = lambda w: pl.BlockSpec((tm, w), lambda i: (i % nb, 0))
    keep = pl.BlockSpec((tm, BAND_W), lambda i: ((i // nb) * nkb + jnp.maximum(i % nb - (nb - nkb), 0), 0))

    ins = [
        (x2d, row(D_MODEL)), (lw['norm_g'], _full((1, D_MODEL))), (lw['w_in'], _full((D_MODEL, C_END))),
        (lw['w_uq'], _full((Q_LORA, MLA_QW))), (lw['qa_g'], _full((1, Q_LORA))),
        (lw['kva_g'], _full((1, KV_LORA))), (lw['kr_g'], _full((1, LANES))),
        (lw['q_gain'], _full((1, MLA_QW))), (lw['bq_gain'], _full((1, BAND_W))),
        (lw['bk_gain'], _full((1, BAND_W))),
        (tabs['cret'], tab(RET_W)), (tabs['sret'], tab(RET_W)), (tabs['cq'], tab(LANES)),
        (tabs['sq'], tab(LANES)), (tabs['ckr'], tab(LANES)), (tabs['skr'], tab(LANES)),
    ]
    outs = [
        ((T, RET_W), BF16, row(RET_W)), ((T, RET_W), BF16, row(RET_W)), ((T, RET_W), BF16, row(RET_W)),
        ((T, D_MIX), BF16, row(D_MIX)), ((T, MLA_QW), BF16, row(MLA_QW)),
        ((T, KV_LORA), F32, row(KV_LORA)), ((T, MLA_ROPE), F32, row(MLA_ROPE)),
        ((T, BAND_W), BF16, row(BAND_W)), ((T, BAND_W), BF16, row(BAND_W)), ((T, BAND_W), BF16, row(BAND_W)),
        ((n_seq * n_keep, BAND_W), F32, keep), ((n_seq * n_keep, BAND_W), F32, keep),
    ]
    return pl.pallas_call(
        _k1_body,
        grid=(T // tm,),
        in_specs=[s for _, s in ins],
        out_specs=[s for _, _, s in outs],
        out_shape=[jax.ShapeDtypeStruct(sh, dt) for sh, dt, _ in outs],
        compiler_params=_params(("arbitrary",)),
        name="k1_proj",
    )(*[a for a, _ in ins])


def _kkv_body(ckv_ref, kr_ref, wk_ref, wv_ref, kgain_ref, *rest, transpose_v):
    if transpose_v:
        ones_ref, k_ref, v_ref = rest
    else:
        k_ref, v_ref = rest
    c = ckv_ref[...].astype(BF16)
    kn = _dot(c, wk_ref[...])
    kn = _group_rms_normed(kn, LANES, MLA_NOPE, 0) * kgain_ref[...]
    kr = kr_ref[...]
    rows = kr.shape[0]
    kr_block = jnp.concatenate([jnp.zeros((rows, MLA_NOPE), F32), kr,
                                jnp.zeros((rows, HEAD_PAD - MLA_QK), F32)], axis=1)
    k_ref[...] = (kn + jnp.concatenate([kr_block] * MLA_HEADS, axis=1)).astype(BF16)
    if transpose_v:
        v_ref[...] = (_dot_nt(wv_ref[...], c) + ones_ref[...]).astype(BF16)
    else:
        v_ref[...] = _dot(c, wv_ref[...]).astype(BF16)


def _kkv(ckv2d, kr2d, lw, *, tm, seq=None):
    T = ckv2d.shape[0]
    row = lambda w: pl.BlockSpec((tm, w), lambda i: (i, 0))
    extra_in = []
    if seq is None:
        wv, wv_spec = lw['w_uv'], _full((KV_LORA, MLA_W))
        v_spec, v_shape = row(MLA_W), (T, MLA_W)
    else:
        nb = seq // tm
        wv, wv_spec = lw['w_uv_t'], _full((MLA_VT, KV_LORA))
        v_spec = pl.BlockSpec((None, MLA_VT, tm), lambda i: (i // nb, 0, i % nb))
        v_shape = (T // seq, MLA_VT, seq)
        ones = np.zeros((MLA_HEADS, MLA_VROWS, 1), np.float32)
        ones[:, MLA_V] = 1.0
        extra_in = [(jnp.asarray(ones.reshape(MLA_VT, 1)), _full((MLA_VT, 1)))]
    return pl.pallas_call(
        functools.partial(_kkv_body, transpose_v=seq is not None),
        grid=(T // tm,),
        in_specs=[row(KV_LORA), row(MLA_ROPE), _full((KV_LORA, MLA_QW)), wv_spec, _full((1, MLA_QW))]
        + [sp for _, sp in extra_in],
        out_specs=[row(MLA_QW), v_spec],
        out_shape=[jax.ShapeDtypeStruct((T, MLA_QW), BF16), jax.ShapeDtypeStruct(v_shape, BF16)],
        compiler_params=_params(("arbitrary",)),
        name="kkv_up",
    )(ckv2d, kr2d, lw['w_uk'], wv, lw['k_gain'], *[a for a, _ in extra_in])


def _mla_prompt_body(q_ref, k_ref, vt_ref, o_ref, m_sc, acc_sc, *, tq):
    qi = pl.program_id(1)
    m_sc[...] = jnp.full(m_sc.shape, NEG_INF, F32)
    acc_sc[...] = jnp.zeros(acc_sc.shape, F32)

    def tile(kb, masked):
        start = pl.multiple_of(kb * tq, tq)
        if masked:
            kc = lax.broadcasted_iota(jnp.int32, (tq, tq), 0) // CHUNK
            qc = lax.broadcasted_iota(jnp.int32, (tq, tq), 1) // CHUNK
            visible = kc <= qc

        def scores(h):
            q = q_ref[:, h * HEAD_PAD:(h + 1) * HEAD_PAD]
            k = k_ref[pl.ds(start, tq), h * HEAD_PAD:(h + 1) * HEAD_PAD]
            st = _dot_nt(k, q)
            return jnp.where(visible, st, NEG_INF) if masked else st

        def softmax(h, st):
            m_prev = m_sc[h:h + 1, :]
            m_new = jnp.maximum(m_prev, jnp.max(st, axis=0, keepdims=True))
            alpha = jnp.exp2(m_prev - m_new)
            p = jnp.exp2(st - m_new)
            m_sc[h:h + 1, :] = m_new
            return p.astype(BF16), alpha

        def values(h, p, alpha):
            vt = vt_ref[h * MLA_VROWS:(h + 1) * MLA_VROWS, pl.ds(start, tq)]
            acc_sc[h] = alpha * acc_sc[h] + _dot(vt, p)

        st = {0: scores(0), 1: scores(1)}
        pa = {}
        for h in range(MLA_HEADS):
            pa[h] = softmax(h, st.pop(h))
            if h >= 1:
                values(h - 1, *pa.pop(h - 1))
            if h + 2 < MLA_HEADS:
                st[h + 2] = scores(h + 2)
        values(MLA_HEADS - 1, *pa.pop(MLA_HEADS - 1))

    def body(kb, carry):
        tile(kb, False)
        return carry

    lax.fori_loop(0, qi, body, 0)
    tile(qi, True)
    for j in range(MLA_HEADS // 2):
        pair = jnp.concatenate([acc_sc[h, :MLA_V, :] * (1.0 / acc_sc[h, MLA_V:MLA_V + 1, :])
                                for h in (2 * j, 2 * j + 1)], axis=0)
        o_ref[:, j * LANES:(j + 1) * LANES] = pair.T.astype(BF16)


def _mla_prompt(q, k, vt, *, tq):
    B, S, _ = q.shape
    once = pl.Buffered(1)
    return pl.pallas_call(
        functools.partial(_mla_prompt_body, tq=tq),
        grid=(B, S // tq),
        in_specs=[pl.BlockSpec((None, tq, MLA_QW), lambda b, i: (b, i, 0)),
                  pl.BlockSpec((None, S, MLA_QW), lambda b, i: (b, 0, 0), pipeline_mode=once),
                  pl.BlockSpec((None, MLA_VT, S), lambda b, i: (b, 0, 0), pipeline_mode=once)],
        out_specs=pl.BlockSpec((None, tq, MLA_W), lambda b, i: (b, i, 0)),
        out_shape=jax.ShapeDtypeStruct((B, S, MLA_W), BF16),
        scratch_shapes=[pltpu.VMEM((MLA_HEADS, tq), F32), pltpu.VMEM((MLA_HEADS, MLA_VROWS, tq), F32)],
        compiler_params=_params(("arbitrary", "arbitrary")),
        name="mla_prompt",
    )(q, k, vt)


def _mla_sample_body(q_ref, kp_ref, vp_ref, kn_ref, vn_ref, o_ref):
    lane = lax.broadcasted_iota(jnp.int32, (1, LANES), 1)
    outs = []
    scores = []
    for h in range(MLA_HEADS):
        hs = slice(h * HEAD_PAD, (h + 1) * HEAD_PAD)
        q = q_ref[:, hs]
        scores.append((_dot_nt(q, kp_ref[:, hs]), _dot_nt(q, kn_ref[:, hs])))
    for h in range(MLA_HEADS):
        ps = slice((h // 2) * LANES, (h // 2 + 1) * LANES)
        v_lanes = (lane // MLA_V) == (h % 2)
        s1, s2 = scores[h]
        m = jnp.maximum(jnp.max(s1, axis=-1, keepdims=True), jnp.max(s2, axis=-1, keepdims=True))
        p1 = jnp.exp2(s1 - m)
        p2 = jnp.exp2(s2 - m)
        l = jnp.sum(p1, axis=-1, keepdims=True) + jnp.sum(p2, axis=-1, keepdims=True)
        v1 = vp_ref[:, ps]
        v2 = vn_ref[:, ps]
        acc = (_dot(p1.astype(BF16), jnp.where(v_lanes, v1, jnp.zeros_like(v1)))
               + _dot(p2.astype(BF16), jnp.where(v_lanes, v2, jnp.zeros_like(v2))))
        outs.append(acc * (1.0 / l))
    for j in range(MLA_HEADS // 2):
        o_ref[:, j * LANES:(j + 1) * LANES] = (outs[2 * j] + outs[2 * j + 1]).astype(BF16)


def _band_heads(q, pieces, biases, out_dtype):
    lane = lax.broadcasted_iota(jnp.int32, (1, BAND_W), 1)
    out = jnp.zeros((q.shape[0], BAND_W), F32)
    raw = []
    for h in range(BAND_HEADS):
        qh = jnp.where((lane // BAND_DH) == h, q, jnp.zeros_like(q))
        raw.append([_dot_nt(qh, k) for k, _, _ in pieces])
    for h in range(BAND_HEADS):
        sel = (lane // BAND_DH) == h
        ss = []
        for s, (_, _, gate), b_ref in zip(raw[h], pieces, biases):
            s = s + b_ref[h]
            if gate is not None:
                s = jnp.where(gate, s, NEG_INF)
            ss.append(s)
        m = functools.reduce(jnp.maximum, [jnp.max(s, axis=-1, keepdims=True) for s in ss])
        ps = [jnp.exp(s - m) for s in ss]
        l = functools.reduce(jnp.add, [jnp.sum(p, axis=-1, keepdims=True) for p in ps])
        acc = functools.reduce(jnp.add, [
            _dot(p.astype(BF16), jnp.where(sel, v, jnp.zeros_like(v))) for p, (_, v, _) in zip(ps, pieces)])
        out = out + acc * (1.0 / l)
    return out.astype(out_dtype)


def _band_prompt_body(q_ref, kp_ref, kc_ref, vp_ref, vc_ref, bp_ref, bc_ref, o_ref):
    has_prev = pl.program_id(1) > 0
    o_ref[...] = _band_heads(q_ref[...],
                             [(kp_ref[...], vp_ref[...], has_prev), (kc_ref[...], vc_ref[...], None)],
                             [bp_ref, bc_ref], BF16)


def _band_prompt(q, k, v, bias_prev, bias_cur, *, tq):
    B, S, _ = q.shape
    cur = pl.BlockSpec((None, tq, BAND_W), lambda b, i: (b, i, 0))
    prev = pl.BlockSpec((None, tq, BAND_W), lambda b, i: (b, jnp.maximum(i - 1, 0), 0))
    bias = _full((BAND_HEADS, tq, tq))
    return pl.pallas_call(
        _band_prompt_body,
        grid=(B, S // tq),
        in_specs=[cur, prev, cur, prev, cur, bias, bias],
        out_specs=cur,
        out_shape=jax.ShapeDtypeStruct((B, S, BAND_W), BF16),
        compiler_params=_params(("arbitrary", "arbitrary")),
        name="band_prompt",
    )(q, k, k, v, v, bias_prev, bias_cur)


def _band_sample_body(q_ref, kp_ref, vp_ref, kn_ref, vn_ref, bp_ref, bn_ref, o_ref):
    o_ref[...] = _band_heads(q_ref[...],
                             [(kp_ref[...].astype(BF16), vp_ref[...].astype(BF16), None),
                              (kn_ref[...], vn_ref[...], None)],
                             [bp_ref, bn_ref], BF16)


def _ret_chunk(q, k, v, state, d_ref, xi_ref, zeta_ref, gam_ref, bd_ref, gain_ref):
    lane = lax.broadcasted_iota(jnp.int32, (1, RET_W), 1)
    o = _dot(q, state.astype(BF16)) * xi_ref[...]
    for h in range(RET_HEADS):
        sel = (lane // RET_DV) == h
        a = _dot_nt(jnp.where(sel, q, jnp.zeros_like(q)), k) * d_ref[h]
        o = o + _dot(a.astype(BF16), jnp.where(sel, v, jnp.zeros_like(v)))
    kz = (k.astype(F32) * zeta_ref[...]).astype(BF16)
    s_new = gam_ref[...] * state + bd_ref[...] * _dot_tn(kz, v)
    return (_group_rms_normed(o, RET_DV, RET_DV, RET_DV) * gain_ref[...]).astype(BF16), s_new


def _ret_body(q_ref, k_ref, v_ref, s0_ref, d_ref, xi_ref, zeta_ref, gam_ref, bd_ref, gain_ref,
              o_ref, sfin_ref, s_sc):
    @pl.when(pl.program_id(1) == 0)
    def _():
        s_sc[...] = s0_ref[...]

    o, s_new = _ret_chunk(q_ref[...], k_ref[...], v_ref[...], s_sc[...],
                          d_ref, xi_ref, zeta_ref, gam_ref, bd_ref, gain_ref)
    s_sc[...] = s_new
    sfin_ref[...] = s_new
    o_ref[...] = o


def _retention(q, k, v, s0, rt, consts, gain, *, c):
    B, S, _ = q.shape
    blk = pl.BlockSpec((None, c, RET_W), lambda b, t: (b, t, 0))
    st = pl.BlockSpec((None, RET_W, RET_W), lambda b, t: (b, 0, 0))
    return pl.pallas_call(
        _ret_body,
        grid=(B, S // c),
        in_specs=[blk, blk, blk, st, _full((RET_HEADS, c, c)), _full((c, RET_W)), _full((c, RET_W)),
                  _full((RET_W, RET_W)), _full((RET_W, RET_W)), _full((1, RET_W))],
        out_specs=[blk, st],
        out_shape=[jax.ShapeDtypeStruct((B, S, RET_W), BF16), jax.ShapeDtypeStruct((B, RET_W, RET_W), F32)],
        scratch_shapes=[pltpu.VMEM((RET_W, RET_W), F32)],
        compiler_params=_params(("arbitrary", "arbitrary")),
        name="retention",
    )(q, k, v, s0, rt['decay'], rt['xi'], rt['zeta'], rt['gamma'], consts['bd_mask'], gain)


def _sample_mixers_body(mq_ref, mkp_ref, mvp_ref, mkn_ref, mvn_ref,
                        bq_ref, bkp_ref, bvp_ref, bkn_ref, bvn_ref, bp_ref, bn_ref,
                        rq_ref, rk_ref, rv_ref, s0_ref, d_ref, xi_ref, zeta_ref, gam_ref, bd_ref, gain_ref,
                        mo_ref, bo_ref, ro_ref, sfin_ref):
    _mla_sample_body(mq_ref, mkp_ref, mvp_ref, mkn_ref, mvn_ref, mo_ref)
    _band_sample_body(bq_ref, bkp_ref, bvp_ref, bkn_ref, bvn_ref, bp_ref, bn_ref, bo_ref)
    o, s_new = _ret_chunk(rq_ref[...], rk_ref[...], rv_ref[...], s0_ref[...],
                          d_ref, xi_ref, zeta_ref, gam_ref, bd_ref, gain_ref)
    ro_ref[...] = o
    sfin_ref[...] = s_new


def _sample_mixers(mq, mkp, mvp, mkn, mvn, bq, bkp, bvp, bkn, bvn, bias_past, bias_new,
                   rq, rk, rv, s0, rt, consts, gain):
    B, L, _ = mq.shape
    P, PB = mkp.shape[1], bkp.shape[1]
    blk = lambda n, w: pl.BlockSpec((None, n, w), lambda b: (b, 0, 0))
    return pl.pallas_call(
        _sample_mixers_body,
        grid=(B,),
        in_specs=[blk(L, MLA_QW), blk(P, MLA_QW), blk(P, MLA_W), blk(L, MLA_QW), blk(L, MLA_W),
                  blk(L, BAND_W), blk(PB, BAND_W), blk(PB, BAND_W), blk(L, BAND_W), blk(L, BAND_W),
                  _full((BAND_HEADS, L, PB)), _full((BAND_HEADS, L, L)),
                  blk(L, RET_W), blk(L, RET_W), blk(L, RET_W), blk(RET_W, RET_W),
                  _full((RET_HEADS, L, L)), _full((L, RET_W)), _full((L, RET_W)),
                  _full((RET_W, RET_W)), _full((RET_W, RET_W)), _full((1, RET_W))],
        out_specs=[blk(L, MLA_W), blk(L, BAND_W), blk(L, RET_W), blk(RET_W, RET_W)],
        out_shape=[jax.ShapeDtypeStruct((B, L, MLA_W), BF16), jax.ShapeDtypeStruct((B, L, BAND_W), BF16),
                   jax.ShapeDtypeStruct((B, L, RET_W), BF16), jax.ShapeDtypeStruct((B, RET_W, RET_W), F32)],
        compiler_params=_params(("arbitrary",)),
        name="sample_mixers",
    )(mq, mkp, mvp, mkn, mvn, bq, bkp, bvp, bkn, bvn, bias_past, bias_new,
      rq, rk, rv, s0, rt['decay'], rt['xi'], rt['zeta'], rt['gamma'], consts['bd_mask'], gain)


def _kout_body(x_ref, ro_ref, mo_ref, bo_ref, g_ref, w_ref, y_ref):
    g = g_ref[...]
    y = x_ref[...]
    y = y + _dot(ro_ref[...] * g[:, :RET_W], w_ref[:RET_W, :])
    y = y + _dot(mo_ref[...] * g[:, RET_W:RET_W + MLA_W], w_ref[RET_W:RET_W + MLA_W, :])
    y = y + _dot(bo_ref[...] * g[:, RET_W + MLA_W:], w_ref[RET_W + MLA_W:, :])
    y_ref[...] = y


def _kout(x2d, ro, mo, bo, gate, w_out, *, tm):
    T = x2d.shape[0]
    row = lambda w: pl.BlockSpec((tm, w), lambda i: (i, 0))
    return pl.pallas_call(
        _kout_body,
        grid=(T // tm,),
        in_specs=[row(D_MODEL), row(RET_W), row(MLA_W), row(BAND_W), row(D_MIX), _full((D_MIX, D_MODEL))],
        out_specs=row(D_MODEL),
        out_shape=jax.ShapeDtypeStruct((T, D_MODEL), F32),
        compiler_params=_params(("arbitrary",)),
        name="kout_proj",
    )(x2d, ro, mo, bo, gate, w_out)


def _constants():
    bd = np.kron(np.eye(RET_HEADS, dtype=np.float32), np.ones((RET_DK, RET_DV), np.float32))
    return {'bd_mask': jnp.asarray(bd)}


def _rope_tables(pos, reps=1):
    pos = np.asarray(pos, np.float64)

    def cs(half):
        inv = ROPE_BASE ** (-np.arange(half, dtype=np.float64) / half)
        ang = pos[:, None] * inv[None, :]
        c, s = np.cos(ang), np.sin(ang)
        return np.concatenate([c, c], axis=-1), np.concatenate([-s, s], axis=-1)

    T = pos.shape[0]
    c32, s32 = cs(RET_DK // 2)
    c16, s16 = cs(MLA_ROPE // 2)
    ones = lambda w: np.ones((T, w))
    zeros = lambda w: np.zeros((T, w))
    pad = HEAD_PAD - MLA_QK
    tabs = {
        'cret': np.tile(c32, (1, RET_HEADS)), 'sret': np.tile(s32, (1, RET_HEADS)),
        'cq': np.concatenate([ones(MLA_NOPE), c16, ones(pad)], axis=-1),
        'sq': np.concatenate([zeros(MLA_NOPE), s16, zeros(pad)], axis=-1),
        'ckr': np.concatenate([c16, ones(LANES - MLA_ROPE)], axis=-1),
        'skr': np.concatenate([s16, zeros(LANES - MLA_ROPE)], axis=-1),
    }
    return {k: jnp.asarray(np.tile(v, (reps, 1)), F32) for k, v in tabs.items()}


def _retention_tables(c):
    lg = np.log1p(-np.exp2(-5.0 - np.arange(RET_HEADS, dtype=np.float64)))
    idx = np.arange(c, dtype=np.float64)
    diff = idx[:, None] - idx[None, :]
    decay = np.where(diff >= 0, np.exp(lg[:, None, None] * np.maximum(diff, 0.0)), 0.0)
    per_lane = lambda t: np.repeat(t, RET_DV, axis=-1)
    xi = per_lane(np.exp(lg[None, :] * (idx[:, None] + 1.0)))
    zeta = per_lane(np.exp(lg[None, :] * (c - 1.0 - idx)[:, None]))
    gamma = np.broadcast_to(per_lane(np.exp(lg * c)[None, :]).T, (RET_W, RET_W))
    return {k: jnp.asarray(v, F32) for k, v in
            {'decay': decay, 'xi': xi, 'zeta': zeta, 'gamma': gamma}.items()}


def _band_tables_body(rp_ref, rc_ref, prev_ref, cur_ref, past_ref, new_ref, *, tq, n_new):
    w = 2 * tq
    tp = pltpu.roll(jnp.broadcast_to(rp_ref[...], (tq, w)), 0, 1, stride=1, stride_axis=0)[:, :tq]
    tc = pltpu.roll(jnp.broadcast_to(rc_ref[...], (tq, w)), 0, 1, stride=1, stride_axis=0)[:, :tq]
    qc = lax.broadcasted_iota(jnp.int32, (tq, tq), 0) // CHUNK
    kc = lax.broadcasted_iota(jnp.int32, (tq, tq), 1) // CHUNK
    prev_ref[...] = jnp.where(kc >= qc, tp, NEG_INF)
    cur_ref[...] = jnp.where(kc <= qc, tc, NEG_INF)
    past_ref[...] = tp[:n_new, :]
    new_ref[...] = tc[:n_new, :n_new]


def _band_tables(band_bias, *, tq, n_new):
    H = band_bias.shape[0]
    lo, mid, hi = band_bias[:, :1], band_bias[:, MAX_REL + 1:2 * MAX_REL], band_bias[:, 2 * MAX_REL:]
    rep = lambda col, n: jnp.broadcast_to(col, (H, n))
    r_cur = jnp.concatenate([band_bias[:, MAX_REL::-1], rep(lo, tq - MAX_REL - 1), rep(hi, tq - MAX_REL),
                             band_bias[:, :MAX_REL:-1]], axis=1)
    r_prev = jnp.concatenate([rep(hi, tq - MAX_REL + 1), mid[:, ::-1], rep(hi, tq)], axis=1)
    row = pl.BlockSpec((None, 1, 2 * tq), lambda h: (h, 0, 0))
    tile = lambda n, m: pl.BlockSpec((None, n, m), lambda h: (h, 0, 0))
    return pl.pallas_call(
        functools.partial(_band_tables_body, tq=tq, n_new=n_new),
        grid=(H,),
        in_specs=[row, row],
        out_specs=[tile(tq, tq), tile(tq, tq), tile(n_new, tq), tile(n_new, n_new)],
        out_shape=[jax.ShapeDtypeStruct((H, tq, tq), F32), jax.ShapeDtypeStruct((H, tq, tq), F32),
                   jax.ShapeDtypeStruct((H, n_new, tq), F32), jax.ShapeDtypeStruct((H, n_new, n_new), F32)],
        compiler_params=_params(("arbitrary",)),
        name="band_tables",
    )(r_prev[:, None, :], r_cur[:, None, :])


def _layer_weights(l, norm_g, w_in, ret_gn_g, mla_qa_g, mla_w_uq, mla_qn_g, mla_qr_g, mla_kva_g, mla_kr_g,
                   mla_w_ukv, mla_kn_g, band_qn_g, band_kn_g, w_out):
    cuts = np.cumsum(SEG)[:-1].tolist()
    a_q, a_k, a_v, a_g, b_cq, b_ckv, b_kr, b_g, c_q, c_k, c_v, c_g = jnp.split(w_in[l], cuts, axis=-1)
    b_kr = jnp.pad(b_kr, ((0, 0), (0, LANES - MLA_ROPE)))
    w_in_p = jnp.concatenate([a_q, a_k * (RET_DK ** -0.5), a_v, b_cq, b_ckv, b_kr, c_q, c_k, c_v, a_g, b_g, c_g],
                             axis=-1).astype(BF16)
    pad = HEAD_PAD - MLA_QK
    w_uq = jnp.pad(mla_w_uq[l].reshape(Q_LORA, MLA_HEADS, MLA_QK), ((0, 0), (0, 0), (0, pad)))
    ukv = mla_w_ukv[l].reshape(KV_LORA, MLA_HEADS, MLA_NOPE + MLA_V)
    w_uk = jnp.pad(ukv[:, :, :MLA_NOPE], ((0, 0), (0, 0), (0, HEAD_PAD - MLA_NOPE)))
    zpad = jnp.zeros((pad,), F32)
    q_gain = jnp.tile(jnp.concatenate([mla_qn_g[l], mla_qr_g[l], zpad]) * (MLA_QK ** -0.5 * LOG2E), MLA_HEADS)
    k_gain = jnp.tile(jnp.concatenate([mla_kn_g[l], jnp.zeros((HEAD_PAD - MLA_NOPE,), F32)]), MLA_HEADS)
    return {
        'norm_g': norm_g[l][None], 'w_in': w_in_p,
        'w_uq': w_uq.reshape(Q_LORA, MLA_QW).astype(BF16), 'qa_g': mla_qa_g[l][None],
        'kva_g': mla_kva_g[l][None],
        'kr_g': jnp.concatenate([mla_kr_g[l], jnp.zeros((LANES - MLA_ROPE,), F32)])[None],
        'q_gain': q_gain[None],
        'bq_gain': (jnp.tile(band_qn_g[l], BAND_HEADS) * (BAND_DH ** -0.5))[None],
        'bk_gain': jnp.tile(band_kn_g[l], BAND_HEADS)[None],
        'w_uk': w_uk.reshape(KV_LORA, MLA_QW).astype(BF16),
        'w_uv': ukv[:, :, MLA_NOPE:].reshape(KV_LORA, MLA_W).astype(BF16),
        'w_uv_t': jnp.pad(ukv[:, :, MLA_NOPE:].transpose(1, 2, 0), ((0, 0), (0, MLA_VROWS - MLA_V), (0, 0))
                          ).reshape(MLA_VT, KV_LORA).astype(BF16),
        'k_gain': k_gain[None], 'ret_gain': ret_gn_g[l][None], 'w_out': w_out[l].astype(BF16),
    }


def _block_diag_state(s):
    B = s.shape[0]
    eye = jnp.eye(RET_HEADS, dtype=s.dtype)
    return jnp.einsum('bhde,hg->bhdge', s, eye).reshape(B, RET_W, RET_W)


def _diag_blocks(s_bd):
    B = s_bd.shape[0]
    s = s_bd.reshape(B, RET_HEADS, RET_DK, RET_HEADS, RET_DV)
    return jnp.stack([s[:, h, :, h, :] for h in range(RET_HEADS)], axis=1)


def kernel(x_prompt, x_sample, state_ret, cache_mla_ckv, cache_mla_krope, cache_band_k, cache_band_v, norm_g, w_in, ret_gn_g, mla_qa_g, mla_w_uq, mla_qn_g, mla_qr_g, mla_kva_g, mla_kr_g, mla_w_ukv, mla_kn_g, band_qn_g, band_kn_g, band_bias, w_out):
    B, S, _ = x_prompt.shape
    DB, L, _ = x_sample.shape
    past_len = cache_mla_ckv.shape[2]
    n_band_past = cache_band_k.shape[2]
    n_keep_p = min(BAND_PAST, S)

    TM_P, TM_S, TQ, RET_C = 512, 256, 512, 256
    consts = _constants()
    tabs_p = _rope_tables(np.arange(S))
    tabs_s = _rope_tables(past_len + np.arange(L), reps=DB)
    rt_p = _retention_tables(RET_C)
    rt_s = _retention_tables(L)

    xp = x_prompt.reshape(B * S, D_MODEL)
    xs = x_sample.reshape(DB * L, D_MODEL)
    zeros_state = jnp.zeros((B, RET_W, RET_W), F32)
    p_st, s_st = [], []
    for l in range(DEPTH):
        lw = _layer_weights(l, norm_g, w_in, ret_gn_g, mla_qa_g, mla_w_uq, mla_qn_g, mla_qr_g, mla_kva_g,
                            mla_kr_g, mla_w_ukv, mla_kn_g, band_qn_g, band_kn_g, w_out)

        rq, rk, rv, gate, qm, ckv, kr, bq, bk, bv, bks, bvs = _k1(
            xp, lw, tabs_p, consts, tm=TM_P, rows_per_seq=S, n_keep=n_keep_p)
        km, vt = _kkv(ckv, kr, lw, tm=TM_P, seq=S)
        mla_o = _mla_prompt(qm.reshape(B, S, MLA_QW), km.reshape(B, S, MLA_QW), vt, tq=TQ)
        assert TQ == BAND_PAST == n_band_past
        bias_prev, bias_cur, bias_past, bias_new = _band_tables(band_bias[l], tq=TQ, n_new=L)
        band_o = _band_prompt(bq.reshape(B, S, BAND_W), bk.reshape(B, S, BAND_W), bv.reshape(B, S, BAND_W),
                              bias_prev, bias_cur, tq=TQ)
        ret_o, ret_s = _retention(rq.reshape(B, S, RET_W), rk.reshape(B, S, RET_W), rv.reshape(B, S, RET_W),
                                  zeros_state, rt_p, consts, lw['ret_gain'], c=RET_C)
        xp = _kout(xp, ret_o.reshape(B * S, RET_W), mla_o.reshape(B * S, MLA_W), band_o.reshape(B * S, BAND_W),
                   gate, lw['w_out'], tm=TM_P)
        p_st.append((_diag_blocks(ret_s), ckv.reshape(B, S, KV_LORA), kr.reshape(B, S, MLA_ROPE),
                     bks.reshape(B, n_keep_p, BAND_HEADS, BAND_DH), bvs.reshape(B, n_keep_p, BAND_HEADS, BAND_DH)))

        rq, rk, rv, gate, qm, ckv, kr, bq, bk, bv, bks, bvs = _k1(
            xs, lw, tabs_s, consts, tm=TM_S, rows_per_seq=DB * L, n_keep=DB * L)
        kn, vn = _kkv(ckv, kr, lw, tm=TM_S)
        kp, vp = _kkv(cache_mla_ckv[l].reshape(DB * past_len, KV_LORA),
                      cache_mla_krope[l].reshape(DB * past_len, MLA_ROPE), lw, tm=TM_P)
        mla_o, band_o, ret_o, ret_s = _sample_mixers(
            qm.reshape(DB, L, MLA_QW), kp.reshape(DB, past_len, MLA_QW), vp.reshape(DB, past_len, MLA_W),
            kn.reshape(DB, L, MLA_QW), vn.reshape(DB, L, MLA_W),
            bq.reshape(DB, L, BAND_W), cache_band_k[l].reshape(DB, n_band_past, BAND_W),
            cache_band_v[l].reshape(DB, n_band_past, BAND_W), bk.reshape(DB, L, BAND_W), bv.reshape(DB, L, BAND_W),
            bias_past, bias_new,
            rq.reshape(DB, L, RET_W), rk.reshape(DB, L, RET_W), rv.reshape(DB, L, RET_W),
            _block_diag_state(state_ret[l]), rt_s, consts, lw['ret_gain'])
        xs = _kout(xs, ret_o.reshape(DB * L, RET_W), mla_o.reshape(DB * L, MLA_W), band_o.reshape(DB * L, BAND_W),
                   gate, lw['w_out'], tm=TM_S)
        s_st.append((_diag_blocks(ret_s), ckv.reshape(DB, L, KV_LORA), kr.reshape(DB, L, MLA_ROPE),
                     bks.reshape(DB, L, BAND_HEADS, BAND_DH), bvs.reshape(DB, L, BAND_HEADS, BAND_DH)))

    stack = lambda sts, i: jnp.stack([s[i] for s in sts])
    return (xp.reshape(B, S, D_MODEL), xs.reshape(DB, L, D_MODEL),
            stack(p_st, 0), stack(p_st, 1), stack(p_st, 2), stack(p_st, 3), stack(p_st, 4),
            stack(s_st, 0), stack(s_st, 1), stack(s_st, 2), stack(s_st, 3), stack(s_st, 4))
```

```python
import functools

import jax
import jax.numpy as jnp
import numpy as np
from jax import lax
from jax.experimental import pallas as pl
from jax.experimental.pallas import tpu as pltpu

F32 = jnp.float32
BF16 = jnp.bfloat16

D_MODEL = 1024
DEPTH = 2
CHUNK = 64
EPS = 1e-6
NEG_INF = -1e30
ROPE_BASE = 10000.0
LOG2E = 1.4426950408889634
RET_HEADS, RET_DK, RET_DV = 4, 64, 64
RET_W = RET_HEADS * RET_DV
MLA_HEADS, MLA_NOPE, MLA_ROPE, MLA_V = 8, 64, 32, 64
MLA_QK = MLA_NOPE + MLA_ROPE
Q_LORA, KV_LORA = 256, 128
MLA_W = MLA_HEADS * MLA_V
BAND_HEADS, BAND_DH = 4, 64
BAND_W = BAND_HEADS * BAND_DH
BAND_PREV_CHUNKS = 8
BAND_PAST = BAND_PREV_CHUNKS * CHUNK
MAX_REL = 128
D_MIX = RET_W + MLA_W + BAND_W
SEG = (RET_HEADS * RET_DK, RET_HEADS * RET_DK, RET_W, RET_W, Q_LORA, KV_LORA, MLA_ROPE, MLA_W,
       BAND_W, BAND_W, BAND_W, BAND_W)

LANES = 128
HEAD_PAD = LANES
MLA_QW = MLA_HEADS * HEAD_PAD
MLA_VROWS = MLA_V + 16
MLA_VT = MLA_HEADS * MLA_VROWS
C_AQ, C_AK, C_AV, C_CQ, C_CKV, C_KR, C_BQ, C_BK, C_BV, C_G, C_END = (
    0, 256, 512, 768, 1024, 1152, 1280, 1536, 1792, 2048, 3072)
VMEM_LIMIT = 56 * 1024 * 1024
K1_SUB = 256


def _dot(a, b):
    return jnp.dot(a, b, preferred_element_type=F32)


def _dot_nt(a, b):
    return lax.dot_general(a, b, (((1,), (1,)), ((), ())), preferred_element_type=F32)


def _dot_tn(a, b):
    return lax.dot_general(a, b, (((0,), (0,)), ((), ())), preferred_element_type=F32)


def _group_sums(x, split, two_groups):
    lane = lax.broadcasted_iota(jnp.int32, (1, LANES), 1)
    lo = lane < split
    sums = []
    for b in range(x.shape[-1] // LANES):
        xb = x[:, b * LANES:(b + 1) * LANES]
        sq = xb * xb
        if two_groups:
            sums.append((jnp.sum(jnp.where(lo, sq, 0.0), axis=-1, keepdims=True),
                         jnp.sum(jnp.where(lo, 0.0, sq), axis=-1, keepdims=True)))
        else:
            sums.append((jnp.sum(sq, axis=-1, keepdims=True), None))
    return sums


def _group_normalize(x, sums, split, n_lo, n_hi):
    lane = lax.broadcasted_iota(jnp.int32, (1, LANES), 1)
    lo = lane < split
    outs = []
    for b, (s_lo, s_hi) in enumerate(sums):
        inv = lax.rsqrt(s_lo * (1.0 / n_lo) + EPS)
        if n_hi:
            inv = jnp.where(lo, inv, lax.rsqrt(s_hi * (1.0 / n_hi) + EPS))
        outs.append(x[:, b * LANES:(b + 1) * LANES] * inv)
    return jnp.concatenate(outs, axis=1)


def _group_rms_normed(x, split, n_lo, n_hi):
    return _group_normalize(x, _group_sums(x, split, n_hi > 0), split, n_lo, n_hi)


def _group64_inv_rms(x, bd, inv_n):
    sq = (x * x).astype(BF16)
    w = bd.shape[0]
    sums = jnp.concatenate([_dot(sq[:, c * w:(c + 1) * w], bd) for c in range(x.shape[-1] // w)], axis=1)
    return lax.rsqrt(sums * inv_n + EPS)


def _swap_halves(x, half):
    w = x.shape[-1]
    lane = lax.broadcasted_iota(jnp.int32, (1, w), 1)
    nxt = pltpu.roll(x, w - half, axis=1)
    prv = pltpu.roll(x, half, axis=1)
    return jnp.where((lane & half) == 0, nxt, prv)


def _rope(x, cos, sin_signed, half):
    return x * cos + _swap_halves(x, half) * sin_signed


def _silu(g):
    return g * (1.0 / (1.0 + jnp.exp(-g)))


def _full(shape):
    nd = len(shape)
    return pl.BlockSpec(shape, lambda *_: (0,) * nd)


def _params(sem):
    return pltpu.CompilerParams(dimension_semantics=sem, vmem_limit_bytes=VMEM_LIMIT)


def _k1_body(x_ref, ng_ref, win_ref, wuq_ref, qag_ref, kvag_ref, krg_ref, qgain_ref, bqg_ref, bkg_ref,
             cret_ref, sret_ref, cq_ref, sq_ref, ckr_ref, skr_ref, wuqr_ref, qgainr_ref, bd_ref, invnq_ref,
             rq_ref, rk_ref, rv_ref, gate_ref, qm_ref, ckv_ref, kr_ref, bq_ref, bk_ref, bv_ref,
             bks_ref, bvs_ref, *, sub):
    n_sub = x_ref.shape[0] // sub
    bd = bd_ref[...]

    def stage_a(r):
        x = x_ref[r * sub:(r + 1) * sub, :]
        h = x * lax.rsqrt(jnp.mean(x * x, axis=-1, keepdims=True) + EPS) * ng_ref[...]
        hb = h.astype(BF16)

        def seg(lo, hi):
            return _dot(hb, win_ref[:, lo:hi])

        z = {'cq': seg(C_CQ, C_CKV), 'bq': seg(C_BQ, C_BK), 'bk': seg(C_BK, C_BV)}
        cq = z['cq']
        cq = cq * lax.rsqrt(jnp.mean(cq * cq, axis=-1, keepdims=True) + EPS) * qag_ref[...]
        z['aq'], z['ak'] = seg(C_AQ, C_AK), seg(C_AK, C_AV)
        cqb = cq.astype(BF16)
        z['qf'] = _dot(cqb, wuq_ref[...])
        z['qf_sw'] = _dot(cqb, wuqr_ref[...])
        z['bq_inv'] = _group64_inv_rms(z['bq'], bd, 1.0 / BAND_DH)
        z['bk_inv'] = _group64_inv_rms(z['bk'], bd, 1.0 / BAND_DH)
        z['g'] = seg(C_G, C_END)
        z['q_inv'] = _group64_inv_rms(z['qf'], bd, invnq_ref[...])
        z['av'], z['ckv'] = seg(C_AV, C_CQ), seg(C_CKV, C_KR)
        z['kr'], z['bv'] = seg(C_KR, C_BQ), seg(C_BV, C_G)
        return z

    def stage_b(r, z):
        rows = slice(r * sub, (r + 1) * sub)
        bq_ref[rows, :] = (z['bq'] * z['bq_inv'] * bqg_ref[...]).astype(BF16)
        bk = z['bk'] * z['bk_inv'] * bkg_ref[...]
        bk_ref[rows, :] = bk.astype(BF16)
        bks_ref[rows, :] = bk
        bv_ref[rows, :] = z['bv'].astype(BF16)
        bvs_ref[rows, :] = z['bv']
        cret, sret = cret_ref[rows, :], sret_ref[rows, :]
        rq_ref[rows, :] = _rope(z['aq'], cret, sret, RET_DK // 2).astype(BF16)
        rk_ref[rows, :] = _rope(z['ak'], cret, sret, RET_DK // 2).astype(BF16)
        rv_ref[rows, :] = z['av'].astype(BF16)
        gate_ref[rows, :] = _silu(z['g']).astype(BF16)
        cq_t = jnp.concatenate([cq_ref[rows, :]] * MLA_HEADS, axis=1)
        sq_t = jnp.concatenate([sq_ref[rows, :]] * MLA_HEADS, axis=1)
        qm_ref[rows, :] = (z['qf'] * (z['q_inv'] * qgain_ref[...]) * cq_t
                           + z['qf_sw'] * (z['q_inv'] * qgainr_ref[...]) * sq_t).astype(BF16)
        ckv, kr = z['ckv'], z['kr']
        ckv_ref[rows, :] = ckv * lax.rsqrt(jnp.mean(ckv * ckv, axis=-1, keepdims=True) + EPS) * kvag_ref[...]
        kr = kr * lax.rsqrt(jnp.sum(kr * kr, axis=-1, keepdims=True) * (1.0 / MLA_ROPE) + EPS) * krg_ref[...]
        kr = _rope(kr, ckr_ref[rows, :], skr_ref[rows, :], MLA_ROPE // 2)
        kr_ref[rows, :] = kr[:, :MLA_ROPE]

    z = stage_a(0)
    for r in range(n_sub):
        z_next = stage_a(r + 1) if r + 1 < n_sub else None
        stage_b(r, z)
        z = z_next


def _k1(x2d, lw, tabs, consts, *, tm, rows_per_seq, n_keep):
    T = x2d.shape[0]
    nb = rows_per_seq // tm
    nkb = n_keep // tm
    n_seq = T // rows_per_seq

    row = lambda w: pl.BlockSpec((tm, w), lambda i: (i, 0))
    tab = lambda w: pl.BlockSpec((tm, w), lambda i: (i % nb, 0))
    keep = pl.BlockSpec((tm, BAND_W), lambda i: ((i // nb) * nkb + jnp.maximum(i % nb - (nb - nkb), 0), 0))

    ins = [
        (x2d, row(D_MODEL)), (lw['norm_g'], _full((1, D_MODEL))), (lw['w_in'], _full((D_MODEL, C_END))),
        (lw['w_uq'], _full((Q_LORA, MLA_QW))), (lw['qa_g'], _full((1, Q_LORA))),
        (lw['kva_g'], _full((1, KV_LORA))), (lw['kr_g'], _full((1, LANES))),
        (lw['q_gain'], _full((1, MLA_QW))), (lw['bq_gain'], _full((1, BAND_W))),
        (lw['bk_gain'], _full((1, BAND_W))),
        (tabs['cret'], tab(RET_W)), (tabs['sret'], tab(RET_W)), (tabs['cq'], tab(LANES)),
        (tabs['sq'], tab(LANES)), (tabs['ckr'], tab(LANES)), (tabs['skr'], tab(LANES)),
        (lw['w_uq_sw'], _full((Q_LORA, MLA_QW))), (lw['q_gain_sw'], _full((1, MLA_QW))),
        (consts['bd64'], _full((RET_W, RET_W))), (consts['invn_q'], _full((1, MLA_QW))),
    ]
    outs = [
        ((T, RET_W), BF16, row(RET_W)), ((T, RET_W), BF16, row(RET_W)), ((T, RET_W), BF16, row(RET_W)),
        ((T, D_MIX), BF16, row(D_MIX)), ((T, MLA_QW), BF16, row(MLA_QW)),
        ((T, KV_LORA), F32, row(KV_LORA)), ((T, MLA_ROPE), F32, row(MLA_ROPE)),
        ((T, BAND_W), BF16, row(BAND_W)), ((T, BAND_W), BF16, row(BAND_W)), ((T, BAND_W), BF16, row(BAND_W)),
        ((n_seq * n_keep, BAND_W), F32, keep), ((n_seq * n_keep, BAND_W), F32, keep),
    ]
    return pl.pallas_call(
        functools.partial(_k1_body, sub=min(tm, K1_SUB)),
        grid=(T // tm,),
        in_specs=[s for _, s in ins],
        out_specs=[s for _, _, s in outs],
        out_shape=[jax.ShapeDtypeStruct(sh, dt) for sh, dt, _ in outs],
        compiler_params=_params(("arbitrary",)),
        name="k1_proj",
    )(*[a for a, _ in ins])


def _kkv_body(ckv_ref, kr_ref, wk_ref, wv_ref, kgain_ref, *rest, transpose_v):
    if transpose_v:
        ones_ref, k_ref, v_ref = rest
    else:
        k_ref, v_ref = rest
    c = ckv_ref[...].astype(BF16)
    kn = _dot(c, wk_ref[...])
    kn = _group_rms_normed(kn, LANES, MLA_NOPE, 0) * kgain_ref[...]
    kr = kr_ref[...]
    rows = kr.shape[0]
    kr_block = jnp.concatenate([jnp.zeros((rows, MLA_NOPE), F32), kr,
                                jnp.zeros((rows, HEAD_PAD - MLA_QK), F32)], axis=1)
    k_ref[...] = (kn + jnp.concatenate([kr_block] * MLA_HEADS, axis=1)).astype(BF16)
    if transpose_v:
        v_ref[...] = (_dot_nt(wv_ref[...], c) + ones_ref[...]).astype(BF16)
    else:
        v_ref[...] = _dot(c, wv_ref[...]).astype(BF16)


def _kkv(ckv2d, kr2d, lw, *, tm, seq=None):
    T = ckv2d.shape[0]
    row = lambda w: pl.BlockSpec((tm, w), lambda i: (i, 0))
    extra_in = []
    if seq is None:
        wv, wv_spec = lw['w_uv'], _full((KV_LORA, MLA_W))
        v_spec, v_shape = row(MLA_W), (T, MLA_W)
    else:
        nb = seq // tm
        wv, wv_spec = lw['w_uv_t'], _full((MLA_VT, KV_LORA))
        v_spec = pl.BlockSpec((None, MLA_VT, tm), lambda i: (i // nb, 0, i % nb))
        v_shape = (T // seq, MLA_VT, seq)
        ones = np.zeros((MLA_HEADS, MLA_VROWS, 1), np.float32)
        ones[:, MLA_V] = 1.0
        extra_in = [(jnp.asarray(ones.reshape(MLA_VT, 1)), _full((MLA_VT, 1)))]
    return pl.pallas_call(
        functools.partial(_kkv_body, transpose_v=seq is not None),
        grid=(T // tm,),
        in_specs=[row(KV_LORA), row(MLA_ROPE), _full((KV_LORA, MLA_QW)), wv_spec, _full((1, MLA_QW))]
        + [sp for _, sp in extra_in],
        out_specs=[row(MLA_QW), v_spec],
        out_shape=[jax.ShapeDtypeStruct((T, MLA_QW), BF16), jax.ShapeDtypeStruct(v_shape, BF16)],
        compiler_params=_params(("arbitrary",)),
        name="kkv_up",
    )(ckv2d, kr2d, lw['w_uk'], wv, lw['k_gain'], *[a for a, _ in extra_in])


def _mla_prompt_body(q_ref, k_ref, vt_ref, o_ref, m_sc, acc_sc, st_sc, *, tq):
    qi = pl.program_id(1)
    m_sc[...] = jnp.full(m_sc.shape, NEG_INF, F32)
    acc_sc[...] = jnp.zeros(acc_sc.shape, F32)
    lookahead = st_sc.shape[0]

    def scores(start, h):
        q = q_ref[:, h * HEAD_PAD:(h + 1) * HEAD_PAD]
        k = k_ref[pl.ds(start, tq), h * HEAD_PAD:(h + 1) * HEAD_PAD]
        return _dot_nt(k, q)

    for h in range(lookahead):
        st_sc[h] = scores(0, h)

    def tile(kb, masked):
        start = pl.multiple_of(kb * tq, tq)
        if masked:
            kc = lax.broadcasted_iota(jnp.int32, (tq, tq), 0) // CHUNK
            qc = lax.broadcasted_iota(jnp.int32, (tq, tq), 1) // CHUNK
            visible = kc <= qc

        def softmax(h, st):
            if masked:
                st = jnp.where(visible, st, NEG_INF)
            m_prev = m_sc[h:h + 1, :]
            m_new = jnp.maximum(m_prev, jnp.max(st, axis=0, keepdims=True))
            alpha = jnp.exp2(m_prev - m_new)
            p = jnp.exp2(st - m_new)
            m_sc[h:h + 1, :] = m_new
            return p.astype(BF16), alpha

        def values(h, p, alpha):
            vt = vt_ref[h * MLA_VROWS:(h + 1) * MLA_VROWS, pl.ds(start, tq)]
            acc_sc[h] = alpha * acc_sc[h] + _dot(vt, p)

        st = {h: st_sc[h] for h in range(lookahead)}
        pa = {}
        for h in range(MLA_HEADS):
            pa[h] = softmax(h, st.pop(h))
            if h >= 1:
                values(h - 1, *pa.pop(h - 1))
            if h + lookahead < MLA_HEADS:
                st[h + lookahead] = scores(start, h + lookahead)
            elif not masked:
                st_sc[h + lookahead - MLA_HEADS] = scores(pl.multiple_of(start + tq, tq),
                                                          h + lookahead - MLA_HEADS)
        values(MLA_HEADS - 1, *pa.pop(MLA_HEADS - 1))

    def body(kb, carry):
        tile(kb, False)
        return carry

    lax.fori_loop(0, qi, body, 0)
    tile(qi, True)
    for j in range(MLA_HEADS // 2):
        pair = jnp.concatenate([acc_sc[h, :MLA_V, :] * (1.0 / acc_sc[h, MLA_V:MLA_V + 1, :])
                                for h in (2 * j, 2 * j + 1)], axis=0)
        o_ref[:, j * LANES:(j + 1) * LANES] = pair.T.astype(BF16)


def _mla_prompt(q, k, vt, *, tq):
    B, S, _ = q.shape
    once = pl.Buffered(1)
    return pl.pallas_call(
        functools.partial(_mla_prompt_body, tq=tq),
        grid=(B, S // tq),
        in_specs=[pl.BlockSpec((None, tq, MLA_QW), lambda b, i: (b, i, 0)),
                  pl.BlockSpec((None, S, MLA_QW), lambda b, i: (b, 0, 0), pipeline_mode=once),
                  pl.BlockSpec((None, MLA_VT, S), lambda b, i: (b, 0, 0), pipeline_mode=once)],
        out_specs=pl.BlockSpec((None, tq, MLA_W), lambda b, i: (b, i, 0)),
        out_shape=jax.ShapeDtypeStruct((B, S, MLA_W), BF16),
        scratch_shapes=[pltpu.VMEM((MLA_HEADS, tq), F32), pltpu.VMEM((MLA_HEADS, MLA_VROWS, tq), F32),
                        pltpu.VMEM((2, tq, tq), F32)],
        compiler_params=_params(("arbitrary", "arbitrary")),
        name="mla_prompt",
    )(q, k, vt)


def _mla_sample_body(q_ref, kp_ref, vp_ref, kn_ref, vn_ref, o_ref):
    lane = lax.broadcasted_iota(jnp.int32, (1, LANES), 1)
    outs = []
    scores = []
    for h in range(MLA_HEADS):
        hs = slice(h * HEAD_PAD, (h + 1) * HEAD_PAD)
        q = q_ref[:, hs]
        scores.append((_dot_nt(q, kp_ref[:, hs]), _dot_nt(q, kn_ref[:, hs])))
    for h in range(MLA_HEADS):
        ps = slice((h // 2) * LANES, (h // 2 + 1) * LANES)
        v_lanes = (lane // MLA_V) == (h % 2)
        s1, s2 = scores[h]
        m = jnp.maximum(jnp.max(s1, axis=-1, keepdims=True), jnp.max(s2, axis=-1, keepdims=True))
        p1 = jnp.exp2(s1 - m)
        p2 = jnp.exp2(s2 - m)
        l = jnp.sum(p1, axis=-1, keepdims=True) + jnp.sum(p2, axis=-1, keepdims=True)
        v1 = vp_ref[:, ps]
        v2 = vn_ref[:, ps]
        acc = (_dot(p1.astype(BF16), jnp.where(v_lanes, v1, jnp.zeros_like(v1)))
               + _dot(p2.astype(BF16), jnp.where(v_lanes, v2, jnp.zeros_like(v2))))
        outs.append(acc * (1.0 / l))
    for j in range(MLA_HEADS // 2):
        o_ref[:, j * LANES:(j + 1) * LANES] = (outs[2 * j] + outs[2 * j + 1]).astype(BF16)


def _band_heads(q, pieces, biases, out_dtype):
    lane = lax.broadcasted_iota(jnp.int32, (1, BAND_W), 1)
    out = jnp.zeros((q.shape[0], BAND_W), F32)
    raw = []
    for h in range(BAND_HEADS):
        qh = jnp.where((lane // BAND_DH) == h, q, jnp.zeros_like(q))
        raw.append([_dot_nt(qh, k) for k, _, _ in pieces])
    for h in range(BAND_HEADS):
        sel = (lane // BAND_DH) == h
        ss = []
        for s, (_, _, gate), b_ref in zip(raw[h], pieces, biases):
            s = s + b_ref[h]
            if gate is not None:
                s = jnp.where(gate, s, NEG_INF)
            ss.append(s)
        m = functools.reduce(jnp.maximum, [jnp.max(s, axis=-1, keepdims=True) for s in ss])
        ps = [jnp.exp(s - m) for s in ss]
        l = functools.reduce(jnp.add, [jnp.sum(p, axis=-1, keepdims=True) for p in ps])
        acc = functools.reduce(jnp.add, [
            _dot(p.astype(BF16), jnp.where(sel, v, jnp.zeros_like(v))) for p, (_, v, _) in zip(ps, pieces)])
        out = out + acc * (1.0 / l)
    return out.astype(out_dtype)


def _band_prompt_body(q_ref, kp_ref, kc_ref, vp_ref, vc_ref, bp_ref, bc_ref, o_ref):
    has_prev = pl.program_id(1) > 0
    o_ref[...] = _band_heads(q_ref[...],
                             [(kp_ref[...], vp_ref[...], has_prev), (kc_ref[...], vc_ref[...], None)],
                             [bp_ref, bc_ref], BF16)


def _band_prompt(q, k, v, bias_prev, bias_cur, *, tq):
    B, S, _ = q.shape
    cur = pl.BlockSpec((None, tq, BAND_W), lambda b, i: (b, i, 0))
    prev = pl.BlockSpec((None, tq, BAND_W), lambda b, i: (b, jnp.maximum(i - 1, 0), 0))
    bias = _full((BAND_HEADS, tq, tq))
    return pl.pallas_call(
        _band_prompt_body,
        grid=(B, S // tq),
        in_specs=[cur, prev, cur, prev, cur, bias, bias],
        out_specs=cur,
        out_shape=jax.ShapeDtypeStruct((B, S, BAND_W), BF16),
        compiler_params=_params(("arbitrary", "arbitrary")),
        name="band_prompt",
    )(q, k, k, v, v, bias_prev, bias_cur)


def _band_sample_body(q_ref, kp_ref, vp_ref, kn_ref, vn_ref, bp_ref, bn_ref, o_ref):
    o_ref[...] = _band_heads(q_ref[...],
                             [(kp_ref[...].astype(BF16), vp_ref[...].astype(BF16), None),
                              (kn_ref[...], vn_ref[...], None)],
                             [bp_ref, bn_ref], BF16)


def _ret_chunk(q, k, v, state, d_ref, xi_ref, zeta_ref, gam_ref, bd_ref, gain_ref):
    lane = lax.broadcasted_iota(jnp.int32, (1, RET_W), 1)
    o = _dot(q, state.astype(BF16)) * xi_ref[...]
    for h in range(RET_HEADS):
        sel = (lane // RET_DV) == h
        a = _dot_nt(jnp.where(sel, q, jnp.zeros_like(q)), k) * d_ref[h]
        o = o + _dot(a.astype(BF16), jnp.where(sel, v, jnp.zeros_like(v)))
    kz = (k.astype(F32) * zeta_ref[...]).astype(BF16)
    s_new = gam_ref[...] * state + bd_ref[...] * _dot_tn(kz, v)
    inv = _group64_inv_rms(o, bd_ref[...].astype(BF16), 1.0 / RET_DV)
    return (o * inv * gain_ref[...]).astype(BF16), s_new


def _ret_body(q_ref, k_ref, v_ref, s0_ref, d_ref, xi_ref, zeta_ref, gam_ref, bd_ref, gain_ref,
              o_ref, sfin_ref, s_sc):
    @pl.when(pl.program_id(1) == 0)
    def _():
        s_sc[...] = s0_ref[...]

    o, s_new = _ret_chunk(q_ref[...], k_ref[...], v_ref[...], s_sc[...],
                          d_ref, xi_ref, zeta_ref, gam_ref, bd_ref, gain_ref)
    s_sc[...] = s_new
    sfin_ref[...] = s_new
    o_ref[...] = o


def _retention(q, k, v, s0, rt, consts, gain, *, c):
    B, S, _ = q.shape
    blk = pl.BlockSpec((None, c, RET_W), lambda b, t: (b, t, 0))
    st = pl.BlockSpec((None, RET_W, RET_W), lambda b, t: (b, 0, 0))
    return pl.pallas_call(
        _ret_body,
        grid=(B, S // c),
        in_specs=[blk, blk, blk, st, _full((RET_HEADS, c, c)), _full((c, RET_W)), _full((c, RET_W)),
                  _full((RET_W, RET_W)), _full((RET_W, RET_W)), _full((1, RET_W))],
        out_specs=[blk, st],
        out_shape=[jax.ShapeDtypeStruct((B, S, RET_W), BF16), jax.ShapeDtypeStruct((B, RET_W, RET_W), F32)],
        scratch_shapes=[pltpu.VMEM((RET_W, RET_W), F32)],
        compiler_params=_params(("arbitrary", "arbitrary")),
        name="retention",
    )(q, k, v, s0, rt['decay'], rt['xi'], rt['zeta'], rt['gamma'], consts['bd_mask'], gain)


def _sample_mixers_body(mq_ref, mkp_ref, mvp_ref, mkn_ref, mvn_ref,
                        bq_ref, bkp_ref, bvp_ref, bkn_ref, bvn_ref, bp_ref, bn_ref,
                        rq_ref, rk_ref, rv_ref, s0_ref, d_ref, xi_ref, zeta_ref, gam_ref, bd_ref, gain_ref,
                        mo_ref, bo_ref, ro_ref, sfin_ref):
    _mla_sample_body(mq_ref, mkp_ref, mvp_ref, mkn_ref, mvn_ref, mo_ref)
    _band_sample_body(bq_ref, bkp_ref, bvp_ref, bkn_ref, bvn_ref, bp_ref, bn_ref, bo_ref)
    o, s_new = _ret_chunk(rq_ref[...], rk_ref[...], rv_ref[...], s0_ref[...],
                          d_ref, xi_ref, zeta_ref, gam_ref, bd_ref, gain_ref)
    ro_ref[...] = o
    sfin_ref[...] = s_new


def _sample_mixers(mq, mkp, mvp, mkn, mvn, bq, bkp, bvp, bkn, bvn, bias_past, bias_new,
                   rq, rk, rv, s0, rt, consts, gain):
    B, L, _ = mq.shape
    P, PB = mkp.shape[1], bkp.shape[1]
    blk = lambda n, w: pl.BlockSpec((None, n, w), lambda b: (b, 0, 0))
    return pl.pallas_call(
        _sample_mixers_body,
        grid=(B,),
        in_specs=[blk(L, MLA_QW), blk(P, MLA_QW), blk(P, MLA_W), blk(L, MLA_QW), blk(L, MLA_W),
                  blk(L, BAND_W), blk(PB, BAND_W), blk(PB, BAND_W), blk(L, BAND_W), blk(L, BAND_W),
                  _full((BAND_HEADS, L, PB)), _full((BAND_HEADS, L, L)),
                  blk(L, RET_W), blk(L, RET_W), blk(L, RET_W), blk(RET_W, RET_W),
                  _full((RET_HEADS, L, L)), _full((L, RET_W)), _full((L, RET_W)),
                  _full((RET_W, RET_W)), _full((RET_W, RET_W)), _full((1, RET_W))],
        out_specs=[blk(L, MLA_W), blk(L, BAND_W), blk(L, RET_W), blk(RET_W, RET_W)],
        out_shape=[jax.ShapeDtypeStruct((B, L, MLA_W), BF16), jax.ShapeDtypeStruct((B, L, BAND_W), BF16),
                   jax.ShapeDtypeStruct((B, L, RET_W), BF16), jax.ShapeDtypeStruct((B, RET_W, RET_W), F32)],
        compiler_params=_params(("arbitrary",)),
        name="sample_mixers",
    )(mq, mkp, mvp, mkn, mvn, bq, bkp, bvp, bkn, bvn, bias_past, bias_new,
      rq, rk, rv, s0, rt['decay'], rt['xi'], rt['zeta'], rt['gamma'], consts['bd_mask'], gain)


def _kout_body(x_ref, ro_ref, mo_ref, bo_ref, g_ref, w_ref, y_ref):
    g = g_ref[...]
    y = x_ref[...]
    y = y + _dot(ro_ref[...] * g[:, :RET_W], w_ref[:RET_W, :])
    y = y + _dot(mo_ref[...] * g[:, RET_W:RET_W + MLA_W], w_ref[RET_W:RET_W + MLA_W, :])
    y = y + _dot(bo_ref[...] * g[:, RET_W + MLA_W:], w_ref[RET_W + MLA_W:, :])
    y_ref[...] = y


def _kout(x2d, ro, mo, bo, gate, w_out, *, tm):
    T = x2d.shape[0]
    row = lambda w: pl.BlockSpec((tm, w), lambda i: (i, 0))
    return pl.pallas_call(
        _kout_body,
        grid=(T // tm,),
        in_specs=[row(D_MODEL), row(RET_W), row(MLA_W), row(BAND_W), row(D_MIX), _full((D_MIX, D_MODEL))],
        out_specs=row(D_MODEL),
        out_shape=jax.ShapeDtypeStruct((T, D_MODEL), F32),
        compiler_params=_params(("arbitrary",)),
        name="kout_proj",
    )(x2d, ro, mo, bo, gate, w_out)


def _constants():
    bd = np.kron(np.eye(RET_HEADS, dtype=np.float32), np.ones((RET_DK, RET_DV), np.float32))
    invn_q = np.tile(np.repeat([1.0 / MLA_NOPE, 1.0 / MLA_ROPE], HEAD_PAD // 2), MLA_HEADS)[None]
    return {'bd_mask': jnp.asarray(bd), 'bd64': jnp.asarray(bd, BF16), 'invn_q': jnp.asarray(invn_q, F32)}


def _rope_tables(pos, reps=1):
    pos = np.asarray(pos, np.float64)

    def cs(half):
        inv = ROPE_BASE ** (-np.arange(half, dtype=np.float64) / half)
        ang = pos[:, None] * inv[None, :]
        c, s = np.cos(ang), np.sin(ang)
        return np.concatenate([c, c], axis=-1), np.concatenate([-s, s], axis=-1)

    T = pos.shape[0]
    c32, s32 = cs(RET_DK // 2)
    c16, s16 = cs(MLA_ROPE // 2)
    ones = lambda w: np.ones((T, w))
    zeros = lambda w: np.zeros((T, w))
    pad = HEAD_PAD - MLA_QK
    tabs = {
        'cret': np.tile(c32, (1, RET_HEADS)), 'sret': np.tile(s32, (1, RET_HEADS)),
        'cq': np.concatenate([ones(MLA_NOPE), c16, ones(pad)], axis=-1),
        'sq': np.concatenate([zeros(MLA_NOPE), s16, zeros(pad)], axis=-1),
        'ckr': np.concatenate([c16, ones(LANES - MLA_ROPE)], axis=-1),
        'skr': np.concatenate([s16, zeros(LANES - MLA_ROPE)], axis=-1),
    }
    return {k: jnp.asarray(np.tile(v, (reps, 1)), F32) for k, v in tabs.items()}


def _retention_tables(c):
    lg = np.log1p(-np.exp2(-5.0 - np.arange(RET_HEADS, dtype=np.float64)))
    idx = np.arange(c, dtype=np.float64)
    diff = idx[:, None] - idx[None, :]
    decay = np.where(diff >= 0, np.exp(lg[:, None, None] * np.maximum(diff, 0.0)), 0.0)
    per_lane = lambda t: np.repeat(t, RET_DV, axis=-1)
    xi = per_lane(np.exp(lg[None, :] * (idx[:, None] + 1.0)))
    zeta = per_lane(np.exp(lg[None, :] * (c - 1.0 - idx)[:, None]))
    gamma = np.broadcast_to(per_lane(np.exp(lg * c)[None, :]).T, (RET_W, RET_W))
    return {k: jnp.asarray(v, F32) for k, v in
            {'decay': decay, 'xi': xi, 'zeta': zeta, 'gamma': gamma}.items()}


def _band_tables_body(rp_ref, rc_ref, prev_ref, cur_ref, past_ref, new_ref, *, tq, n_new):
    w = 2 * tq
    tp = pltpu.roll(jnp.broadcast_to(rp_ref[...], (tq, w)), 0, 1, stride=1, stride_axis=0)[:, :tq]
    tc = pltpu.roll(jnp.broadcast_to(rc_ref[...], (tq, w)), 0, 1, stride=1, stride_axis=0)[:, :tq]
    qc = lax.broadcasted_iota(jnp.int32, (tq, tq), 0) // CHUNK
    kc = lax.broadcasted_iota(jnp.int32, (tq, tq), 1) // CHUNK
    prev_ref[...] = jnp.where(kc >= qc, tp, NEG_INF)
    cur_ref[...] = jnp.where(kc <= qc, tc, NEG_INF)
    past_ref[...] = tp[:n_new, :]
    new_ref[...] = tc[:n_new, :n_new]


def _band_tables(band_bias, *, tq, n_new):
    H = band_bias.shape[0]
    lo, mid, hi = band_bias[:, :1], band_bias[:, MAX_REL + 1:2 * MAX_REL], band_bias[:, 2 * MAX_REL:]
    rep = lambda col, n: jnp.broadcast_to(col, (H, n))
    r_cur = jnp.concatenate([band_bias[:, MAX_REL::-1], rep(lo, tq - MAX_REL - 1), rep(hi, tq - MAX_REL),
                             band_bias[:, :MAX_REL:-1]], axis=1)
    r_prev = jnp.concatenate([rep(hi, tq - MAX_REL + 1), mid[:, ::-1], rep(hi, tq)], axis=1)
    row = pl.BlockSpec((None, 1, 2 * tq), lambda h: (h, 0, 0))
    tile = lambda n, m: pl.BlockSpec((None, n, m), lambda h: (h, 0, 0))
    return pl.pallas_call(
        functools.partial(_band_tables_body, tq=tq, n_new=n_new),
        grid=(H,),
        in_specs=[row, row],
        out_specs=[tile(tq, tq), tile(tq, tq), tile(n_new, tq), tile(n_new, n_new)],
        out_shape=[jax.ShapeDtypeStruct((H, tq, tq), F32), jax.ShapeDtypeStruct((H, tq, tq), F32),
                   jax.ShapeDtypeStruct((H, n_new, tq), F32), jax.ShapeDtypeStruct((H, n_new, n_new), F32)],
        compiler_params=_params(("arbitrary",)),
        name="band_tables",
    )(r_prev[:, None, :], r_cur[:, None, :])


def _layer_weights(l, norm_g, w_in, ret_gn_g, mla_qa_g, mla_w_uq, mla_qn_g, mla_qr_g, mla_kva_g, mla_kr_g,
                   mla_w_ukv, mla_kn_g, band_qn_g, band_kn_g, w_out):
    cuts = np.cumsum(SEG)[:-1].tolist()
    a_q, a_k, a_v, a_g, b_cq, b_ckv, b_kr, b_g, c_q, c_k, c_v, c_g = jnp.split(w_in[l], cuts, axis=-1)
    b_kr = jnp.pad(b_kr, ((0, 0), (0, LANES - MLA_ROPE)))
    w_in_p = jnp.concatenate([a_q, a_k * (RET_DK ** -0.5), a_v, b_cq, b_ckv, b_kr, c_q, c_k, c_v, a_g, b_g, c_g],
                             axis=-1).astype(BF16)
    pad = HEAD_PAD - MLA_QK
    w_uq = jnp.pad(mla_w_uq[l].reshape(Q_LORA, MLA_HEADS, MLA_QK), ((0, 0), (0, 0), (0, pad)))
    half = MLA_ROPE // 2

    def swap_rope(t):
        return jnp.concatenate([t[..., :MLA_NOPE], t[..., MLA_NOPE + half:MLA_QK], t[..., MLA_NOPE:MLA_NOPE + half],
                                t[..., MLA_QK:]], axis=-1)
    ukv = mla_w_ukv[l].reshape(KV_LORA, MLA_HEADS, MLA_NOPE + MLA_V)
    w_uk = jnp.pad(ukv[:, :, :MLA_NOPE], ((0, 0), (0, 0), (0, HEAD_PAD - MLA_NOPE)))
    zpad = jnp.zeros((pad,), F32)
    q_gain_head = jnp.concatenate([mla_qn_g[l], mla_qr_g[l], zpad]) * (MLA_QK ** -0.5 * LOG2E)
    q_gain = jnp.tile(q_gain_head, MLA_HEADS)
    q_gain_sw = jnp.tile(swap_rope(q_gain_head), MLA_HEADS)
    k_gain = jnp.tile(jnp.concatenate([mla_kn_g[l], jnp.zeros((HEAD_PAD - MLA_NOPE,), F32)]), MLA_HEADS)
    return {
        'norm_g': norm_g[l][None], 'w_in': w_in_p,
        'w_uq': w_uq.reshape(Q_LORA, MLA_QW).astype(BF16), 'qa_g': mla_qa_g[l][None],
        'w_uq_sw': swap_rope(w_uq).reshape(Q_LORA, MLA_QW).astype(BF16), 'q_gain_sw': q_gain_sw[None],
        'kva_g': mla_kva_g[l][None],
        'kr_g': jnp.concatenate([mla_kr_g[l], jnp.zeros((LANES - MLA_ROPE,), F32)])[None],
        'q_gain': q_gain[None],
        'bq_gain': (jnp.tile(band_qn_g[l], BAND_HEADS) * (BAND_DH ** -0.5))[None],
        'bk_gain': jnp.tile(band_kn_g[l], BAND_HEADS)[None],
        'w_uk': w_uk.reshape(KV_LORA, MLA_QW).astype(BF16),
        'w_uv': ukv[:, :, MLA_NOPE:].reshape(KV_LORA, MLA_W).astype(BF16),
        'w_uv_t': jnp.pad(ukv[:, :, MLA_NOPE:].transpose(1, 2, 0), ((0, 0), (0, MLA_VROWS - MLA_V), (0, 0))
                          ).reshape(MLA_VT, KV_LORA).astype(BF16),
        'k_gain': k_gain[None], 'ret_gain': ret_gn_g[l][None], 'w_out': w_out[l].astype(BF16),
    }


def _block_diag_state(s):
    B = s.shape[0]
    eye = jnp.eye(RET_HEADS, dtype=s.dtype)
    return jnp.einsum('bhde,hg->bhdge', s, eye).reshape(B, RET_W, RET_W)


def _diag_blocks(s_bd):
    B = s_bd.shape[0]
    s = s_bd.reshape(B, RET_HEADS, RET_DK, RET_HEADS, RET_DV)
    return jnp.stack([s[:, h, :, h, :] for h in range(RET_HEADS)], axis=1)


def kernel(x_prompt, x_sample, state_ret, cache_mla_ckv, cache_mla_krope, cache_band_k, cache_band_v, norm_g, w_in, ret_gn_g, mla_qa_g, mla_w_uq, mla_qn_g, mla_qr_g, mla_kva_g, mla_kr_g, mla_w_ukv, mla_kn_g, band_qn_g, band_kn_g, band_bias, w_out):
    B, S, _ = x_prompt.shape
    DB, L, _ = x_sample.shape
    past_len = cache_mla_ckv.shape[2]
    n_band_past = cache_band_k.shape[2]
    n_keep_p = min(BAND_PAST, S)

    TM_P, TM_S, TQ, RET_C = 512, 256, 512, 256
    consts = _constants()
    tabs_p = _rope_tables(np.arange(S))
    tabs_s = _rope_tables(past_len + np.arange(L), reps=DB)
    rt_p = _retention_tables(RET_C)
    rt_s = _retention_tables(L)

    xp = x_prompt.reshape(B * S, D_MODEL)
    xs = x_sample.reshape(DB * L, D_MODEL)
    zeros_state = jnp.zeros((B, RET_W, RET_W), F32)
    p_st, s_st = [], []
    for l in range(DEPTH):
        lw = _layer_weights(l, norm_g, w_in, ret_gn_g, mla_qa_g, mla_w_uq, mla_qn_g, mla_qr_g, mla_kva_g,
                            mla_kr_g, mla_w_ukv, mla_kn_g, band_qn_g, band_kn_g, w_out)

        rq, rk, rv, gate, qm, ckv, kr, bq, bk, bv, bks, bvs = _k1(
            xp, lw, tabs_p, consts, tm=TM_P, rows_per_seq=S, n_keep=n_keep_p)
        km, vt = _kkv(ckv, kr, lw, tm=TM_P, seq=S)
        mla_o = _mla_prompt(qm.reshape(B, S, MLA_QW), km.reshape(B, S, MLA_QW), vt, tq=TQ)
        assert TQ == BAND_PAST == n_band_past
        bias_prev, bias_cur, bias_past, bias_new = _band_tables(band_bias[l], tq=TQ, n_new=L)
        band_o = _band_prompt(bq.reshape(B, S, BAND_W), bk.reshape(B, S, BAND_W), bv.reshape(B, S, BAND_W),
                              bias_prev, bias_cur, tq=TQ)
        ret_o, ret_s = _retention(rq.reshape(B, S, RET_W), rk.reshape(B, S, RET_W), rv.reshape(B, S, RET_W),
                                  zeros_state, rt_p, consts, lw['ret_gain'], c=RET_C)
        xp = _kout(xp, ret_o.reshape(B * S, RET_W), mla_o.reshape(B * S, MLA_W), band_o.reshape(B * S, BAND_W),
                   gate, lw['w_out'], tm=TM_P)
        p_st.append((_diag_blocks(ret_s), ckv.reshape(B, S, KV_LORA), kr.reshape(B, S, MLA_ROPE),
                     bks.reshape(B, n_keep_p, BAND_HEADS, BAND_DH), bvs.reshape(B, n_keep_p, BAND_HEADS, BAND_DH)))

        rq, rk, rv, gate, qm, ckv, kr, bq, bk, bv, bks, bvs = _k1(
            xs, lw, tabs_s, consts, tm=TM_S, rows_per_seq=DB * L, n_keep=DB * L)
        kn, vn = _kkv(ckv, kr, lw, tm=TM_S)
        kp, vp = _kkv(cache_mla_ckv[l].reshape(DB * past_len, KV_LORA),
                      cache_mla_krope[l].reshape(DB * past_len, MLA_ROPE), lw, tm=TM_P)
        mla_o, band_o, ret_o, ret_s = _sample_mixers(
            qm.reshape(DB, L, MLA_QW), kp.reshape(DB, past_len, MLA_QW), vp.reshape(DB, past_len, MLA_W),
            kn.reshape(DB, L, MLA_QW), vn.reshape(DB, L, MLA_W),
            bq.reshape(DB, L, BAND_W), cache_band_k[l].reshape(DB, n_band_past, BAND_W),
            cache_band_v[l].reshape(DB, n_band_past, BAND_W), bk.reshape(DB, L, BAND_W), bv.reshape(DB, L, BAND_W),
            bias_past, bias_new,
            rq.reshape(DB, L, RET_W), rk.reshape(DB, L, RET_W), rv.reshape(DB, L, RET_W),
            _block_diag_state(state_ret[l]), rt_s, consts, lw['ret_gain'])
        xs = _kout(xs, ret_o.reshape(DB * L, RET_W), mla_o.reshape(DB * L, MLA_W), band_o.reshape(DB * L, BAND_W),
                   gate, lw['w_out'], tm=TM_S)
        s_st.append((_diag_blocks(ret_s), ckv.reshape(DB, L, KV_LORA), kr.reshape(DB, L, MLA_ROPE),
                     bks.reshape(DB, L, BAND_HEADS, BAND_DH), bvs.reshape(DB, L, BAND_HEADS, BAND_DH)))

    stack = lambda sts, i: jnp.stack([s[i] for s in sts])
    return (xp.reshape(B, S, D_MODEL), xs.reshape(DB, L, D_MODEL),
            stack(p_st, 0), stack(p_st, 1), stack(p_st, 2), stack(p_st, 3), stack(p_st, 4),
            stack(s_st, 0), stack(s_st, 1), stack(s_st, 2), stack(s_st, 3), stack(s_st, 4))
```

```python
import functools

import jax
import jax.numpy as jnp
import numpy as np
from jax import lax
from jax.experimental import pallas as pl
from jax.experimental.pallas import tpu as pltpu

F32 = jnp.float32
BF16 = jnp.bfloat16

D_MODEL = 1024
DEPTH = 2
CHUNK = 64
EPS = 1e-6
NEG_INF = -1e30
ROPE_BASE = 10000.0
LOG2E = 1.4426950408889634
RET_HEADS, RET_DK, RET_DV = 4, 64, 64
RET_W = RET_HEADS * RET_DV
MLA_HEADS, MLA_NOPE, MLA_ROPE, MLA_V = 8, 64, 32, 64
MLA_QK = MLA_NOPE + MLA_ROPE
Q_LORA, KV_LORA = 256, 128
MLA_W = MLA_HEADS * MLA_V
BAND_HEADS, BAND_DH = 4, 64
BAND_W = BAND_HEADS * BAND_DH
BAND_PREV_CHUNKS = 8
BAND_PAST = BAND_PREV_CHUNKS * CHUNK
MAX_REL = 128
D_MIX = RET_W + MLA_W + BAND_W
SEG = (RET_HEADS * RET_DK, RET_HEADS * RET_DK, RET_W, RET_W, Q_LORA, KV_LORA, MLA_ROPE, MLA_W,
       BAND_W, BAND_W, BAND_W, BAND_W)

LANES = 128
HEAD_PAD = LANES
MLA_QW = MLA_HEADS * HEAD_PAD
MLA_VROWS = MLA_V + 16
MLA_VT = MLA_HEADS * MLA_VROWS
C_AQ, C_AK, C_AV, C_CQ, C_CKV, C_KR, C_BQ, C_BK, C_BV, C_G, C_END = (
    0, 256, 512, 768, 1024, 1152, 1280, 1536, 1792, 2048, 3072)
VMEM_LIMIT = 56 * 1024 * 1024
K1_SUB = 256


def _dot(a, b):
    return jnp.dot(a, b, preferred_element_type=F32)


def _dot_nt(a, b):
    return lax.dot_general(a, b, (((1,), (1,)), ((), ())), preferred_element_type=F32)


def _dot_tn(a, b):
    return lax.dot_general(a, b, (((0,), (0,)), ((), ())), preferred_element_type=F32)


def _group_sums(x, split, two_groups):
    lane = lax.broadcasted_iota(jnp.int32, (1, LANES), 1)
    lo = lane < split
    sums = []
    for b in range(x.shape[-1] // LANES):
        xb = x[:, b * LANES:(b + 1) * LANES]
        sq = xb * xb
        if two_groups:
            sums.append((jnp.sum(jnp.where(lo, sq, 0.0), axis=-1, keepdims=True),
                         jnp.sum(jnp.where(lo, 0.0, sq), axis=-1, keepdims=True)))
        else:
            sums.append((jnp.sum(sq, axis=-1, keepdims=True), None))
    return sums


def _group_normalize(x, sums, split, n_lo, n_hi):
    lane = lax.broadcasted_iota(jnp.int32, (1, LANES), 1)
    lo = lane < split
    outs = []
    for b, (s_lo, s_hi) in enumerate(sums):
        inv = lax.rsqrt(s_lo * (1.0 / n_lo) + EPS)
        if n_hi:
            inv = jnp.where(lo, inv, lax.rsqrt(s_hi * (1.0 / n_hi) + EPS))
        outs.append(x[:, b * LANES:(b + 1) * LANES] * inv)
    return jnp.concatenate(outs, axis=1)


def _group_rms_normed(x, split, n_lo, n_hi):
    return _group_normalize(x, _group_sums(x, split, n_hi > 0), split, n_lo, n_hi)


def _group64_inv_rms(x, bd, inv_n):
    sq = (x * x).astype(BF16)
    w = bd.shape[0]
    sums = jnp.concatenate([_dot(sq[:, c * w:(c + 1) * w], bd) for c in range(x.shape[-1] // w)], axis=1)
    return lax.rsqrt(sums * inv_n + EPS)


def _swap_halves(x, half):
    w = x.shape[-1]
    lane = lax.broadcasted_iota(jnp.int32, (1, w), 1)
    nxt = pltpu.roll(x, w - half, axis=1)
    prv = pltpu.roll(x, half, axis=1)
    return jnp.where((lane & half) == 0, nxt, prv)


def _rope(x, cos, sin_signed, half):
    return x * cos + _swap_halves(x, half) * sin_signed


def _silu(g):
    return g * (1.0 / (1.0 + jnp.exp(-g)))


def _full(shape):
    nd = len(shape)
    return pl.BlockSpec(shape, lambda *_: (0,) * nd)


def _params(sem):
    return pltpu.CompilerParams(dimension_semantics=sem, vmem_limit_bytes=VMEM_LIMIT)


def _k1_body(x_ref, ng_ref, win_ref, wuq_ref, qag_ref, kvag_ref, krg_ref, qgain_ref, bqg_ref, bkg_ref,
             cret_ref, sret_ref, cq_ref, sq_ref, ckr_ref, skr_ref, wuqr_ref, qgainr_ref, bd_ref, invnq_ref,
             rq_ref, rk_ref, rv_ref, gate_ref, qm_ref, ckv_ref, kr_ref, bq_ref, bk_ref, bv_ref,
             bks_ref, bvs_ref, *, sub):
    n_sub = x_ref.shape[0] // sub
    bd = bd_ref[...]

    def stage_a(r):
        x = x_ref[r * sub:(r + 1) * sub, :]
        h = x * lax.rsqrt(jnp.mean(x * x, axis=-1, keepdims=True) + EPS) * ng_ref[...]
        hb = h.astype(BF16)

        def seg(lo, hi):
            return _dot(hb, win_ref[:, lo:hi])

        z = {'cq': seg(C_CQ, C_CKV), 'bq': seg(C_BQ, C_BK), 'bk': seg(C_BK, C_BV)}
        cq = z['cq']
        cq = cq * lax.rsqrt(jnp.mean(cq * cq, axis=-1, keepdims=True) + EPS) * qag_ref[...]
        z['aq'], z['ak'] = seg(C_AQ, C_AK), seg(C_AK, C_AV)
        cqb = cq.astype(BF16)
        z['qf'] = _dot(cqb, wuq_ref[...])
        z['qf_sw'] = _dot(cqb, wuqr_ref[...])
        z['bq_inv'] = _group64_inv_rms(z['bq'], bd, 1.0 / BAND_DH)
        z['bk_inv'] = _group64_inv_rms(z['bk'], bd, 1.0 / BAND_DH)
        z['g'] = seg(C_G, C_END)
        z['q_inv'] = _group64_inv_rms(z['qf'], bd, invnq_ref[...])
        z['av'], z['ckv'] = seg(C_AV, C_CQ), seg(C_CKV, C_KR)
        z['kr'], z['bv'] = seg(C_KR, C_BQ), seg(C_BV, C_G)
        return z

    def stage_b(r, z):
        rows = slice(r * sub, (r + 1) * sub)
        bq_ref[rows, :] = (z['bq'] * z['bq_inv'] * bqg_ref[...]).astype(BF16)
        bk = z['bk'] * z['bk_inv'] * bkg_ref[...]
        bk_ref[rows, :] = bk.astype(BF16)
        bks_ref[rows, :] = bk
        bv_ref[rows, :] = z['bv'].astype(BF16)
        bvs_ref[rows, :] = z['bv']
        cret, sret = cret_ref[rows, :], sret_ref[rows, :]
        rq_ref[rows, :] = _rope(z['aq'], cret, sret, RET_DK // 2).astype(BF16)
        rk_ref[rows, :] = _rope(z['ak'], cret, sret, RET_DK // 2).astype(BF16)
        rv_ref[rows, :] = z['av'].astype(BF16)
        gate_ref[rows, :] = _silu(z['g']).astype(BF16)
        cq_t = jnp.concatenate([cq_ref[rows, :]] * MLA_HEADS, axis=1)
        sq_t = jnp.concatenate([sq_ref[rows, :]] * MLA_HEADS, axis=1)
        qm_ref[rows, :] = (z['qf'] * (z['q_inv'] * qgain_ref[...]) * cq_t
                           + z['qf_sw'] * (z['q_inv'] * qgainr_ref[...]) * sq_t).astype(BF16)
        ckv, kr = z['ckv'], z['kr']
        ckv_ref[rows, :] = ckv * lax.rsqrt(jnp.mean(ckv * ckv, axis=-1, keepdims=True) + EPS) * kvag_ref[...]
        kr = kr * lax.rsqrt(jnp.sum(kr * kr, axis=-1, keepdims=True) * (1.0 / MLA_ROPE) + EPS) * krg_ref[...]
        kr = _rope(kr, ckr_ref[rows, :], skr_ref[rows, :], MLA_ROPE // 2)
        kr_ref[rows, :] = kr[:, :MLA_ROPE]

    z = stage_a(0)
    for r in range(n_sub):
        z_next = stage_a(r + 1) if r + 1 < n_sub else None
        stage_b(r, z)
        z = z_next


def _k1(x2d, lw, tabs, consts, *, tm, rows_per_seq, n_keep):
    T = x2d.shape[0]
    nb = rows_per_seq // tm
    nkb = n_keep // tm
    n_seq = T // rows_per_seq

    row = lambda w: pl.BlockSpec((tm, w), lambda i: (i, 0))
    tab = lambda w: pl.BlockSpec((tm, w), lambda i: (i % nb, 0))
    keep = pl.BlockSpec((tm, BAND_W), lambda i: ((i // nb) * nkb + jnp.maximum(i % nb - (nb - nkb), 0), 0))

    ins = [
        (x2d, row(D_MODEL)), (lw['norm_g'], _full((1, D_MODEL))), (lw['w_in'], _full((D_MODEL, C_END))),
        (lw['w_uq'], _full((Q_LORA, MLA_QW))), (lw['qa_g'], _full((1, Q_LORA))),
        (lw['kva_g'], _full((1, KV_LORA))), (lw['kr_g'], _full((1, LANES))),
        (lw['q_gain'], _full((1, MLA_QW))), (lw['bq_gain'], _full((1, BAND_W))),
        (lw['bk_gain'], _full((1, BAND_W))),
        (tabs['cret'], tab(RET_W)), (tabs['sret'], tab(RET_W)), (tabs['cq'], tab(LANES)),
        (tabs['sq'], tab(LANES)), (tabs['ckr'], tab(LANES)), (tabs['skr'], tab(LANES)),
        (lw['w_uq_sw'], _full((Q_LORA, MLA_QW))), (lw['q_gain_sw'], _full((1, MLA_QW))),
        (consts['bd64'], _full((RET_W, RET_W))), (consts['invn_q'], _full((1, MLA_QW))),
    ]
    outs = [
        ((T, RET_W), BF16, row(RET_W)), ((T, RET_W), BF16, row(RET_W)), ((T, RET_W), BF16, row(RET_W)),
        ((T, D_MIX), BF16, row(D_MIX)), ((T, MLA_QW), BF16, row(MLA_QW)),
        ((T, KV_LORA), F32, row(KV_LORA)), ((T, MLA_ROPE), F32, row(MLA_ROPE)),
        ((T, BAND_W), BF16, row(BAND_W)), ((T, BAND_W), BF16, row(BAND_W)), ((T, BAND_W), BF16, row(BAND_W)),
        ((n_seq * n_keep, BAND_W), F32, keep), ((n_seq * n_keep, BAND_W), F32, keep),
    ]
    return pl.pallas_call(
        functools.partial(_k1_body, sub=min(tm, K1_SUB)),
        grid=(T // tm,),
        in_specs=[s for _, s in ins],
        out_specs=[s for _, _, s in outs],
        out_shape=[jax.ShapeDtypeStruct(sh, dt) for sh, dt, _ in outs],
        compiler_params=_params(("arbitrary",)),
        name="k1_proj",
    )(*[a for a, _ in ins])


def _to_heads_body(k_ref, v_ref, ko_ref, vo_ref):
    for hd in range(BAND_HEADS):
        lanes = slice(hd * BAND_DH, (hd + 1) * BAND_DH)
        ko_ref[:, hd, :] = k_ref[:, lanes]
        vo_ref[:, hd, :] = v_ref[:, lanes]


def _to_heads(k2d, v2d, *, tm):
    n = k2d.shape[0]
    row = pl.BlockSpec((tm, BAND_W), lambda i: (i, 0))
    head = pl.BlockSpec((tm, BAND_HEADS, BAND_DH), lambda i: (i, 0, 0))
    shape = jax.ShapeDtypeStruct((n, BAND_HEADS, BAND_DH), F32)
    return pl.pallas_call(
        _to_heads_body, grid=(n // tm,), in_specs=[row, row], out_specs=[head, head],
        out_shape=[shape, shape], compiler_params=_params(("arbitrary",)), name="band_state_heads",
    )(k2d, v2d)


def _kkv_body(ckv_ref, kr_ref, wk_ref, wv_ref, kgain_ref, *rest, transpose_v):
    if transpose_v:
        ones_ref, k_ref, v_ref = rest
    else:
        k_ref, v_ref = rest
    c = ckv_ref[...].astype(BF16)
    kn = _dot(c, wk_ref[...])
    kn = _group_rms_normed(kn, LANES, MLA_NOPE, 0) * kgain_ref[...]
    kr = kr_ref[...]
    rows = kr.shape[0]
    kr_block = jnp.concatenate([jnp.zeros((rows, MLA_NOPE), F32), kr,
                                jnp.zeros((rows, HEAD_PAD - MLA_QK), F32)], axis=1)
    k_ref[...] = (kn + jnp.concatenate([kr_block] * MLA_HEADS, axis=1)).astype(BF16)
    if transpose_v:
        v_ref[...] = (_dot_nt(wv_ref[...], c) + ones_ref[...]).astype(BF16)
    else:
        v_ref[...] = _dot(c, wv_ref[...]).astype(BF16)


def _kkv(ckv2d, kr2d, lw, *, tm, seq=None):
    T = ckv2d.shape[0]
    row = lambda w: pl.BlockSpec((tm, w), lambda i: (i, 0))
    extra_in = []
    if seq is None:
        wv, wv_spec = lw['w_uv'], _full((KV_LORA, MLA_W))
        v_spec, v_shape = row(MLA_W), (T, MLA_W)
    else:
        nb = seq // tm
        wv, wv_spec = lw['w_uv_t'], _full((MLA_VT, KV_LORA))
        v_spec = pl.BlockSpec((None, MLA_VT, tm), lambda i: (i // nb, 0, i % nb))
        v_shape = (T // seq, MLA_VT, seq)
        ones = np.zeros((MLA_HEADS, MLA_VROWS, 1), np.float32)
        ones[:, MLA_V] = 1.0
        extra_in = [(jnp.asarray(ones.reshape(MLA_VT, 1)), _full((MLA_VT, 1)))]
    return pl.pallas_call(
        functools.partial(_kkv_body, transpose_v=seq is not None),
        grid=(T // tm,),
        in_specs=[row(KV_LORA), row(MLA_ROPE), _full((KV_LORA, MLA_QW)), wv_spec, _full((1, MLA_QW))]
        + [sp for _, sp in extra_in],
        out_specs=[row(MLA_QW), v_spec],
        out_shape=[jax.ShapeDtypeStruct((T, MLA_QW), BF16), jax.ShapeDtypeStruct(v_shape, BF16)],
        compiler_params=_params(("arbitrary",)),
        name="kkv_up",
    )(ckv2d, kr2d, lw['w_uk'], wv, lw['k_gain'], *[a for a, _ in extra_in])


def _mla_prompt_body(q_ref, k_ref, vt_ref, o_ref, m_sc, acc_sc, *, tq):
    qi = pl.program_id(1)
    m_sc[...] = jnp.full(m_sc.shape, NEG_INF, F32)
    acc_sc[...] = jnp.zeros(acc_sc.shape, F32)

    def tile(kb, masked):
        start = pl.multiple_of(kb * tq, tq)
        if masked:
            kc = lax.broadcasted_iota(jnp.int32, (tq, tq), 0) // CHUNK
            qc = lax.broadcasted_iota(jnp.int32, (tq, tq), 1) // CHUNK
            visible = kc <= qc

        def scores(h):
            q = q_ref[:, h * HEAD_PAD:(h + 1) * HEAD_PAD]
            k = k_ref[pl.ds(start, tq), h * HEAD_PAD:(h + 1) * HEAD_PAD]
            st = _dot_nt(k, q)
            return jnp.where(visible, st, NEG_INF) if masked else st

        def softmax(h, st):
            m_prev = m_sc[h:h + 1, :]
            m_new = jnp.maximum(m_prev, jnp.max(st, axis=0, keepdims=True))
            alpha = jnp.exp2(m_prev - m_new)
            p = jnp.exp2(st - m_new)
            m_sc[h:h + 1, :] = m_new
            return p.astype(BF16), alpha

        def values(h, p, alpha):
            vt = vt_ref[h * MLA_VROWS:(h + 1) * MLA_VROWS, pl.ds(start, tq)]
            acc_sc[h] = alpha * acc_sc[h] + _dot(vt, p)

        st = {0: scores(0), 1: scores(1)}
        pa = {}
        for h in range(MLA_HEADS):
            pa[h] = softmax(h, st.pop(h))
            if h >= 1:
                values(h - 1, *pa.pop(h - 1))
            if h + 2 < MLA_HEADS:
                st[h + 2] = scores(h + 2)
        values(MLA_HEADS - 1, *pa.pop(MLA_HEADS - 1))

    def body(kb, carry):
        tile(kb, False)
        return carry

    lax.fori_loop(0, qi, body, 0)
    tile(qi, True)
    for j in range(MLA_HEADS // 2):
        pair = jnp.concatenate([acc_sc[h, :MLA_V, :] * (1.0 / acc_sc[h, MLA_V:MLA_V + 1, :])
                                for h in (2 * j, 2 * j + 1)], axis=0)
        o_ref[:, j * LANES:(j + 1) * LANES] = pair.T.astype(BF16)


def _mla_prompt(q, k, vt, *, tq):
    B, S, _ = q.shape
    once = pl.Buffered(1)
    return pl.pallas_call(
        functools.partial(_mla_prompt_body, tq=tq),
        grid=(B, S // tq),
        in_specs=[pl.BlockSpec((None, tq, MLA_QW), lambda b, i: (b, i, 0)),
                  pl.BlockSpec((None, S, MLA_QW), lambda b, i: (b, 0, 0), pipeline_mode=once),
                  pl.BlockSpec((None, MLA_VT, S), lambda b, i: (b, 0, 0), pipeline_mode=once)],
        out_specs=pl.BlockSpec((None, tq, MLA_W), lambda b, i: (b, i, 0)),
        out_shape=jax.ShapeDtypeStruct((B, S, MLA_W), BF16),
        scratch_shapes=[pltpu.VMEM((MLA_HEADS, tq), F32), pltpu.VMEM((MLA_HEADS, MLA_VROWS, tq), F32)],
        compiler_params=_params(("arbitrary", "arbitrary")),
        name="mla_prompt",
    )(q, k, vt)


def _mla_sample_body(q_ref, kp_ref, vp_ref, kn_ref, vn_ref, o_ref):
    lane = lax.broadcasted_iota(jnp.int32, (1, LANES), 1)
    outs = []
    scores = []
    for h in range(MLA_HEADS):
        hs = slice(h * HEAD_PAD, (h + 1) * HEAD_PAD)
        q = q_ref[:, hs]
        scores.append((_dot_nt(q, kp_ref[:, hs]), _dot_nt(q, kn_ref[:, hs])))
    for h in range(MLA_HEADS):
        ps = slice((h // 2) * LANES, (h // 2 + 1) * LANES)
        v_lanes = (lane // MLA_V) == (h % 2)
        s1, s2 = scores[h]
        m = jnp.maximum(jnp.max(s1, axis=-1, keepdims=True), jnp.max(s2, axis=-1, keepdims=True))
        p1 = jnp.exp2(s1 - m)
        p2 = jnp.exp2(s2 - m)
        l = jnp.sum(p1, axis=-1, keepdims=True) + jnp.sum(p2, axis=-1, keepdims=True)
        v1 = vp_ref[:, ps]
        v2 = vn_ref[:, ps]
        acc = (_dot(p1.astype(BF16), jnp.where(v_lanes, v1, jnp.zeros_like(v1)))
               + _dot(p2.astype(BF16), jnp.where(v_lanes, v2, jnp.zeros_like(v2))))
        outs.append(acc * (1.0 / l))
    for j in range(MLA_HEADS // 2):
        o_ref[:, j * LANES:(j + 1) * LANES] = (outs[2 * j] + outs[2 * j + 1]).astype(BF16)


def _band_heads(q, pieces, biases, out_dtype):
    lane = lax.broadcasted_iota(jnp.int32, (1, BAND_W), 1)
    out = jnp.zeros((q.shape[0], BAND_W), F32)
    raw = []
    for h in range(BAND_HEADS):
        qh = jnp.where((lane // BAND_DH) == h, q, jnp.zeros_like(q))
        raw.append([_dot_nt(qh, k) for k, _ in pieces])
    for h in range(BAND_HEADS):
        sel = (lane // BAND_DH) == h
        ss = [s + b_ref[h] for s, b_ref in zip(raw[h], biases)]
        m = functools.reduce(jnp.maximum, [jnp.max(s, axis=-1, keepdims=True) for s in ss])
        ps = [jnp.exp2(s - m) for s in ss]
        l = functools.reduce(jnp.add, [jnp.sum(p, axis=-1, keepdims=True) for p in ps])
        acc = functools.reduce(jnp.add, [
            _dot(p.astype(BF16), jnp.where(sel, v, jnp.zeros_like(v))) for p, (_, v) in zip(ps, pieces)])
        out = out + acc * (1.0 / l)
    return out.astype(out_dtype)


def _band_prompt_body(q_ref, kp_ref, kc_ref, vp_ref, vc_ref, bp_ref, bc_ref, o_ref):
    o_ref[...] = _band_heads(q_ref[...], [(kp_ref[...], vp_ref[...]), (kc_ref[...], vc_ref[...])],
                             [bp_ref, bc_ref], BF16)


def _band_prompt(q, k, v, bias_prev, bias_cur, *, tq):
    B, S, _ = q.shape
    cur = pl.BlockSpec((None, tq, BAND_W), lambda b, i: (b, i, 0))
    prev = pl.BlockSpec((None, tq, BAND_W), lambda b, i: (b, jnp.maximum(i - 1, 0), 0))
    bias = _full((BAND_HEADS, tq, tq))
    bias_prev_spec = pl.BlockSpec((None, BAND_HEADS, tq, tq), lambda b, i: (jnp.minimum(i, 1), 0, 0, 0))
    return pl.pallas_call(
        _band_prompt_body,
        grid=(B, S // tq),
        in_specs=[cur, prev, cur, prev, cur, bias_prev_spec, bias],
        out_specs=cur,
        out_shape=jax.ShapeDtypeStruct((B, S, BAND_W), BF16),
        compiler_params=_params(("arbitrary", "arbitrary")),
        name="band_prompt",
    )(q, k, k, v, v, bias_prev, bias_cur)


def _band_sample_body(q_ref, kp_ref, vp_ref, kn_ref, vn_ref, bp_ref, bn_ref, o_ref):
    o_ref[...] = _band_heads(q_ref[...],
                             [(kp_ref[...].astype(BF16), vp_ref[...].astype(BF16)), (kn_ref[...], vn_ref[...])],
                             [bp_ref, bn_ref], BF16)


def _ret_chunk(q, k, v, state, d_ref, xi_ref, zeta_ref, gam_ref, bd_ref, gain_ref):
    lane = lax.broadcasted_iota(jnp.int32, (1, RET_W), 1)
    o = _dot(q, state.astype(BF16)) * xi_ref[...]
    for h in range(RET_HEADS):
        sel = (lane // RET_DV) == h
        a = _dot_nt(jnp.where(sel, q, jnp.zeros_like(q)), k) * d_ref[h]
        o = o + _dot(a.astype(BF16), jnp.where(sel, v, jnp.zeros_like(v)))
    kz = (k.astype(F32) * zeta_ref[...]).astype(BF16)
    s_new = gam_ref[...] * state + bd_ref[...] * _dot_tn(kz, v)
    inv = _group64_inv_rms(o, bd_ref[...].astype(BF16), 1.0 / RET_DV)
    return (o * inv * gain_ref[...]).astype(BF16), s_new


def _ret_body(q_ref, k_ref, v_ref, s0_ref, d_ref, xi_ref, zeta_ref, gam_ref, bd_ref, gain_ref,
              o_ref, sfin_ref, s_sc):
    @pl.when(pl.program_id(1) == 0)
    def _():
        s_sc[...] = s0_ref[...]

    o, s_new = _ret_chunk(q_ref[...], k_ref[...], v_ref[...], s_sc[...],
                          d_ref, xi_ref, zeta_ref, gam_ref, bd_ref, gain_ref)
    s_sc[...] = s_new
    sfin_ref[...] = s_new
    o_ref[...] = o


def _retention(q, k, v, s0, rt, consts, gain, *, c):
    B, S, _ = q.shape
    blk = pl.BlockSpec((None, c, RET_W), lambda b, t: (b, t, 0))
    st = pl.BlockSpec((None, RET_W, RET_W), lambda b, t: (b, 0, 0))
    return pl.pallas_call(
        _ret_body,
        grid=(B, S // c),
        in_specs=[blk, blk, blk, st, _full((RET_HEADS, c, c)), _full((c, RET_W)), _full((c, RET_W)),
                  _full((RET_W, RET_W)), _full((RET_W, RET_W)), _full((1, RET_W))],
        out_specs=[blk, st],
        out_shape=[jax.ShapeDtypeStruct((B, S, RET_W), BF16), jax.ShapeDtypeStruct((B, RET_W, RET_W), F32)],
        scratch_shapes=[pltpu.VMEM((RET_W, RET_W), F32)],
        compiler_params=_params(("arbitrary", "arbitrary")),
        name="retention",
    )(q, k, v, s0, rt['decay'], rt['xi'], rt['zeta'], rt['gamma'], consts['bd_mask'], gain)


def _sample_mixers_body(mq_ref, mkp_ref, mvp_ref, mkn_ref, mvn_ref,
                        bq_ref, bkp_ref, bvp_ref, bkn_ref, bvn_ref, bp_ref, bn_ref,
                        rq_ref, rk_ref, rv_ref, s0_ref, d_ref, xi_ref, zeta_ref, gam_ref, bd_ref, gain_ref,
                        mo_ref, bo_ref, ro_ref, sfin_ref):
    _mla_sample_body(mq_ref, mkp_ref, mvp_ref, mkn_ref, mvn_ref, mo_ref)
    _band_sample_body(bq_ref, bkp_ref, bvp_ref, bkn_ref, bvn_ref, bp_ref, bn_ref, bo_ref)
    o, s_new = _ret_chunk(rq_ref[...], rk_ref[...], rv_ref[...], s0_ref[...],
                          d_ref, xi_ref, zeta_ref, gam_ref, bd_ref, gain_ref)
    ro_ref[...] = o
    sfin_ref[...] = s_new


def _sample_mixers(mq, mkp, mvp, mkn, mvn, bq, bkp, bvp, bkn, bvn, bias_past, bias_new,
                   rq, rk, rv, s0, rt, consts, gain):
    B, L, _ = mq.shape
    P, PB = mkp.shape[1], bkp.shape[1]
    blk = lambda n, w: pl.BlockSpec((None, n, w), lambda b: (b, 0, 0))
    return pl.pallas_call(
        _sample_mixers_body,
        grid=(B,),
        in_specs=[blk(L, MLA_QW), blk(P, MLA_QW), blk(P, MLA_W), blk(L, MLA_QW), blk(L, MLA_W),
                  blk(L, BAND_W), blk(PB, BAND_W), blk(PB, BAND_W), blk(L, BAND_W), blk(L, BAND_W),
                  _full((BAND_HEADS, L, PB)), _full((BAND_HEADS, L, L)),
                  blk(L, RET_W), blk(L, RET_W), blk(L, RET_W), blk(RET_W, RET_W),
                  _full((RET_HEADS, L, L)), _full((L, RET_W)), _full((L, RET_W)),
                  _full((RET_W, RET_W)), _full((RET_W, RET_W)), _full((1, RET_W))],
        out_specs=[blk(L, MLA_W), blk(L, BAND_W), blk(L, RET_W), blk(RET_W, RET_W)],
        out_shape=[jax.ShapeDtypeStruct((B, L, MLA_W), BF16), jax.ShapeDtypeStruct((B, L, BAND_W), BF16),
                   jax.ShapeDtypeStruct((B, L, RET_W), BF16), jax.ShapeDtypeStruct((B, RET_W, RET_W), F32)],
        compiler_params=_params(("arbitrary",)),
        name="sample_mixers",
    )(mq, mkp, mvp, mkn, mvn, bq, bkp, bvp, bkn, bvn, bias_past, bias_new,
      rq, rk, rv, s0, rt['decay'], rt['xi'], rt['zeta'], rt['gamma'], consts['bd_mask'], gain)


def _kout_body(x_ref, ro_ref, mo_ref, bo_ref, g_ref, w_ref, y_ref):
    g = g_ref[...]
    y = x_ref[...]
    y = y + _dot(ro_ref[...] * g[:, :RET_W], w_ref[:RET_W, :])
    y = y + _dot(mo_ref[...] * g[:, RET_W:RET_W + MLA_W], w_ref[RET_W:RET_W + MLA_W, :])
    y = y + _dot(bo_ref[...] * g[:, RET_W + MLA_W:], w_ref[RET_W + MLA_W:, :])
    y_ref[...] = y


def _kout(x2d, ro, mo, bo, gate, w_out, *, tm):
    T = x2d.shape[0]
    row = lambda w: pl.BlockSpec((tm, w), lambda i: (i, 0))
    return pl.pallas_call(
        _kout_body,
        grid=(T // tm,),
        in_specs=[row(D_MODEL), row(RET_W), row(MLA_W), row(BAND_W), row(D_MIX), _full((D_MIX, D_MODEL))],
        out_specs=row(D_MODEL),
        out_shape=jax.ShapeDtypeStruct((T, D_MODEL), F32),
        compiler_params=_params(("arbitrary",)),
        name="kout_proj",
    )(x2d, ro, mo, bo, gate, w_out)


def _constants():
    bd = np.kron(np.eye(RET_HEADS, dtype=np.float32), np.ones((RET_DK, RET_DV), np.float32))
    invn_q = np.tile(np.repeat([1.0 / MLA_NOPE, 1.0 / MLA_ROPE], HEAD_PAD // 2), MLA_HEADS)[None]
    return {'bd_mask': jnp.asarray(bd), 'bd64': jnp.asarray(bd, BF16), 'invn_q': jnp.asarray(invn_q, F32)}


def _rope_tables(pos, reps=1):
    pos = np.asarray(pos, np.float64)

    def cs(half):
        inv = ROPE_BASE ** (-np.arange(half, dtype=np.float64) / half)
        ang = pos[:, None] * inv[None, :]
        c, s = np.cos(ang), np.sin(ang)
        return np.concatenate([c, c], axis=-1), np.concatenate([-s, s], axis=-1)

    T = pos.shape[0]
    c32, s32 = cs(RET_DK // 2)
    c16, s16 = cs(MLA_ROPE // 2)
    ones = lambda w: np.ones((T, w))
    zeros = lambda w: np.zeros((T, w))
    pad = HEAD_PAD - MLA_QK
    tabs = {
        'cret': np.tile(c32, (1, RET_HEADS)), 'sret': np.tile(s32, (1, RET_HEADS)),
        'cq': np.concatenate([ones(MLA_NOPE), c16, ones(pad)], axis=-1),
        'sq': np.concatenate([zeros(MLA_NOPE), s16, zeros(pad)], axis=-1),
        'ckr': np.concatenate([c16, ones(LANES - MLA_ROPE)], axis=-1),
        'skr': np.concatenate([s16, zeros(LANES - MLA_ROPE)], axis=-1),
    }
    return {k: jnp.asarray(np.tile(v, (reps, 1)), F32) for k, v in tabs.items()}


def _retention_tables(c):
    lg = np.log1p(-np.exp2(-5.0 - np.arange(RET_HEADS, dtype=np.float64)))
    idx = np.arange(c, dtype=np.float64)
    diff = idx[:, None] - idx[None, :]
    decay = np.where(diff >= 0, np.exp(lg[:, None, None] * np.maximum(diff, 0.0)), 0.0)
    per_lane = lambda t: np.repeat(t, RET_DV, axis=-1)
    xi = per_lane(np.exp(lg[None, :] * (idx[:, None] + 1.0)))
    zeta = per_lane(np.exp(lg[None, :] * (c - 1.0 - idx)[:, None]))
    gamma = np.broadcast_to(per_lane(np.exp(lg * c)[None, :]).T, (RET_W, RET_W))
    return {k: jnp.asarray(v, F32) for k, v in
            {'decay': decay, 'xi': xi, 'zeta': zeta, 'gamma': gamma}.items()}


def _band_tables_body(rp_ref, rc_ref, prev_ref, cur_ref, past_ref, new_ref, *, tq, n_new):
    w = 2 * tq
    tp = pltpu.roll(jnp.broadcast_to(rp_ref[...], (tq, w)), 0, 1, stride=1, stride_axis=0)[:, :tq]
    tc = pltpu.roll(jnp.broadcast_to(rc_ref[...], (tq, w)), 0, 1, stride=1, stride_axis=0)[:, :tq]
    qc = lax.broadcasted_iota(jnp.int32, (tq, tq), 0) // CHUNK
    kc = lax.broadcasted_iota(jnp.int32, (tq, tq), 1) // CHUNK
    prev_ref[0] = jnp.full((tq, tq), NEG_INF, F32)
    prev_ref[1] = jnp.where(kc >= qc, tp, NEG_INF)
    cur_ref[...] = jnp.where(kc <= qc, tc, NEG_INF)
    past_ref[...] = tp[:n_new, :]
    new_ref[...] = tc[:n_new, :n_new]


def _band_tables(band_bias, *, tq, n_new):
    H = band_bias.shape[0]
    band_bias = band_bias * LOG2E
    lo, mid, hi = band_bias[:, :1], band_bias[:, MAX_REL + 1:2 * MAX_REL], band_bias[:, 2 * MAX_REL:]
    rep = lambda col, n: jnp.broadcast_to(col, (H, n))
    r_cur = jnp.concatenate([band_bias[:, MAX_REL::-1], rep(lo, tq - MAX_REL - 1), rep(hi, tq - MAX_REL),
                             band_bias[:, :MAX_REL:-1]], axis=1)
    r_prev = jnp.concatenate([rep(hi, tq - MAX_REL + 1), mid[:, ::-1], rep(hi, tq)], axis=1)
    row = pl.BlockSpec((None, 1, 2 * tq), lambda h: (h, 0, 0))
    tile = lambda n, m: pl.BlockSpec((None, n, m), lambda h: (h, 0, 0))
    return pl.pallas_call(
        functools.partial(_band_tables_body, tq=tq, n_new=n_new),
        grid=(H,),
        in_specs=[row, row],
        out_specs=[pl.BlockSpec((2, None, tq, tq), lambda h: (0, h, 0, 0)), tile(tq, tq), tile(n_new, tq),
                   tile(n_new, n_new)],
        out_shape=[jax.ShapeDtypeStruct((2, H, tq, tq), F32), jax.ShapeDtypeStruct((H, tq, tq), F32),
                   jax.ShapeDtypeStruct((H, n_new, tq), F32), jax.ShapeDtypeStruct((H, n_new, n_new), F32)],
        compiler_params=_params(("arbitrary",)),
        name="band_tables",
    )(r_prev[:, None, :], r_cur[:, None, :])


def _layer_weights(l, norm_g, w_in, ret_gn_g, mla_qa_g, mla_w_uq, mla_qn_g, mla_qr_g, mla_kva_g, mla_kr_g,
                   mla_w_ukv, mla_kn_g, band_qn_g, band_kn_g, w_out):
    cuts = np.cumsum(SEG)[:-1].tolist()
    a_q, a_k, a_v, a_g, b_cq, b_ckv, b_kr, b_g, c_q, c_k, c_v, c_g = jnp.split(w_in[l], cuts, axis=-1)
    b_kr = jnp.pad(b_kr, ((0, 0), (0, LANES - MLA_ROPE)))
    w_in_p = jnp.concatenate([a_q, a_k * (RET_DK ** -0.5), a_v, b_cq, b_ckv, b_kr, c_q, c_k, c_v, a_g, b_g, c_g],
                             axis=-1).astype(BF16)
    pad = HEAD_PAD - MLA_QK
    w_uq = jnp.pad(mla_w_uq[l].reshape(Q_LORA, MLA_HEADS, MLA_QK), ((0, 0), (0, 0), (0, pad)))
    half = MLA_ROPE // 2

    def swap_rope(t):
        return jnp.concatenate([t[..., :MLA_NOPE], t[..., MLA_NOPE + half:MLA_QK], t[..., MLA_NOPE:MLA_NOPE + half],
                                t[..., MLA_QK:]], axis=-1)
    ukv = mla_w_ukv[l].reshape(KV_LORA, MLA_HEADS, MLA_NOPE + MLA_V)
    w_uk = jnp.pad(ukv[:, :, :MLA_NOPE], ((0, 0), (0, 0), (0, HEAD_PAD - MLA_NOPE)))
    zpad = jnp.zeros((pad,), F32)
    q_gain_head = jnp.concatenate([mla_qn_g[l], mla_qr_g[l], zpad]) * (MLA_QK ** -0.5 * LOG2E)
    q_gain = jnp.tile(q_gain_head, MLA_HEADS)
    q_gain_sw = jnp.tile(swap_rope(q_gain_head), MLA_HEADS)
    k_gain = jnp.tile(jnp.concatenate([mla_kn_g[l], jnp.zeros((HEAD_PAD - MLA_NOPE,), F32)]), MLA_HEADS)
    return {
        'norm_g': norm_g[l][None], 'w_in': w_in_p,
        'w_uq': w_uq.reshape(Q_LORA, MLA_QW).astype(BF16), 'qa_g': mla_qa_g[l][None],
        'w_uq_sw': swap_rope(w_uq).reshape(Q_LORA, MLA_QW).astype(BF16), 'q_gain_sw': q_gain_sw[None],
        'kva_g': mla_kva_g[l][None],
        'kr_g': jnp.concatenate([mla_kr_g[l], jnp.zeros((LANES - MLA_ROPE,), F32)])[None],
        'q_gain': q_gain[None],
        'bq_gain': (jnp.tile(band_qn_g[l], BAND_HEADS) * (BAND_DH ** -0.5 * LOG2E))[None],
        'bk_gain': jnp.tile(band_kn_g[l], BAND_HEADS)[None],
        'w_uk': w_uk.reshape(KV_LORA, MLA_QW).astype(BF16),
        'w_uv': ukv[:, :, MLA_NOPE:].reshape(KV_LORA, MLA_W).astype(BF16),
        'w_uv_t': jnp.pad(ukv[:, :, MLA_NOPE:].transpose(1, 2, 0), ((0, 0), (0, MLA_VROWS - MLA_V), (0, 0))
                          ).reshape(MLA_VT, KV_LORA).astype(BF16),
        'k_gain': k_gain[None], 'ret_gain': ret_gn_g[l][None], 'w_out': w_out[l].astype(BF16),
    }


def _block_diag_state(s):
    B = s.shape[0]
    eye = jnp.eye(RET_HEADS, dtype=s.dtype)
    return jnp.einsum('bhde,hg->bhdge', s, eye).reshape(B, RET_W, RET_W)


def _diag_blocks(s_bd):
    B = s_bd.shape[0]
    s = s_bd.reshape(B, RET_HEADS, RET_DK, RET_HEADS, RET_DV)
    return jnp.stack([s[:, h, :, h, :] for h in range(RET_HEADS)], axis=1)


def kernel(x_prompt, x_sample, state_ret, cache_mla_ckv, cache_mla_krope, cache_band_k, cache_band_v, norm_g, w_in, ret_gn_g, mla_qa_g, mla_w_uq, mla_qn_g, mla_qr_g, mla_kva_g, mla_kr_g, mla_w_ukv, mla_kn_g, band_qn_g, band_kn_g, band_bias, w_out):
    B, S, _ = x_prompt.shape
    DB, L, _ = x_sample.shape
    past_len = cache_mla_ckv.shape[2]
    n_band_past = cache_band_k.shape[2]
    n_keep_p = min(BAND_PAST, S)

    TM_P, TM_S, TQ, RET_C = 512, 256, 512, 256
    consts = _constants()
    tabs_p = _rope_tables(np.arange(S))
    tabs_s = _rope_tables(past_len + np.arange(L), reps=DB)
    rt_p = _retention_tables(RET_C)
    rt_s = _retention_tables(L)

    xp = x_prompt.reshape(B * S, D_MODEL)
    xs = x_sample.reshape(DB * L, D_MODEL)
    zeros_state = jnp.zeros((B, RET_W, RET_W), F32)
    p_st, s_st = [], []
    for l in range(DEPTH):
        lw = _layer_weights(l, norm_g, w_in, ret_gn_g, mla_qa_g, mla_w_uq, mla_qn_g, mla_qr_g, mla_kva_g,
                            mla_kr_g, mla_w_ukv, mla_kn_g, band_qn_g, band_kn_g, w_out)

        rq, rk, rv, gate, qm, ckv, kr, bq, bk, bv, bks, bvs = _k1(
            xp, lw, tabs_p, consts, tm=TM_P, rows_per_seq=S, n_keep=n_keep_p)
        km, vt = _kkv(ckv, kr, lw, tm=TM_P, seq=S)
        mla_o = _mla_prompt(qm.reshape(B, S, MLA_QW), km.reshape(B, S, MLA_QW), vt, tq=TQ)
        assert TQ == BAND_PAST == n_band_past
        bias_prev, bias_cur, bias_past, bias_new = _band_tables(band_bias[l], tq=TQ, n_new=L)
        band_o = _band_prompt(bq.reshape(B, S, BAND_W), bk.reshape(B, S, BAND_W), bv.reshape(B, S, BAND_W),
                              bias_prev, bias_cur, tq=TQ)
        ret_o, ret_s = _retention(rq.reshape(B, S, RET_W), rk.reshape(B, S, RET_W), rv.reshape(B, S, RET_W),
                                  zeros_state, rt_p, consts, lw['ret_gain'], c=RET_C)
        xp = _kout(xp, ret_o.reshape(B * S, RET_W), mla_o.reshape(B * S, MLA_W), band_o.reshape(B * S, BAND_W),
                   gate, lw['w_out'], tm=TM_P)
        bks, bvs = _to_heads(bks, bvs, tm=TM_S)
        p_st.append((_diag_blocks(ret_s), ckv.reshape(B, S, KV_LORA), kr.reshape(B, S, MLA_ROPE),
                     bks.reshape(B, n_keep_p, BAND_HEADS, BAND_DH), bvs.reshape(B, n_keep_p, BAND_HEADS, BAND_DH)))

        rq, rk, rv, gate, qm, ckv, kr, bq, bk, bv, bks, bvs = _k1(
            xs, lw, tabs_s, consts, tm=TM_S, rows_per_seq=DB * L, n_keep=DB * L)
        kn, vn = _kkv(ckv, kr, lw, tm=TM_S)
        kp, vp = _kkv(cache_mla_ckv[l].reshape(DB * past_len, KV_LORA),
                      cache_mla_krope[l].reshape(DB * past_len, MLA_ROPE), lw, tm=TM_P)
        mla_o, band_o, ret_o, ret_s = _sample_mixers(
            qm.reshape(DB, L, MLA_QW), kp.reshape(DB, past_len, MLA_QW), vp.reshape(DB, past_len, MLA_W),
            kn.reshape(DB, L, MLA_QW), vn.reshape(DB, L, MLA_W),
            bq.reshape(DB, L, BAND_W), cache_band_k[l].reshape(DB, n_band_past, BAND_W),
            cache_band_v[l].reshape(DB, n_band_past, BAND_W), bk.reshape(DB, L, BAND_W), bv.reshape(DB, L, BAND_W),
            bias_past, bias_new,
            rq.reshape(DB, L, RET_W), rk.reshape(DB, L, RET_W), rv.reshape(DB, L, RET_W),
            _block_diag_state(state_ret[l]), rt_s, consts, lw['ret_gain'])
        xs = _kout(xs, ret_o.reshape(DB * L, RET_W), mla_o.reshape(DB * L, MLA_W), band_o.reshape(DB * L, BAND_W),
                   gate, lw['w_out'], tm=TM_S)
        bks, bvs = _to_heads(bks, bvs, tm=TM_S)
        s_st.append((_diag_blocks(ret_s), ckv.reshape(DB, L, KV_LORA), kr.reshape(DB, L, MLA_ROPE),
                     bks.reshape(DB, L, BAND_HEADS, BAND_DH), bvs.reshape(DB, L, BAND_HEADS, BAND_DH)))

    stack = lambda sts, i: jnp.stack([s[i] for s in sts])
    return (xp.reshape(B, S, D_MODEL), xs.reshape(DB, L, D_MODEL),
            stack(p_st, 0), stack(p_st, 1), stack(p_st, 2), stack(p_st, 3), stack(p_st, 4),
            stack(s_st, 0), stack(s_st, 1), stack(s_st, 2), stack(s_st, 3), stack(s_st, 4))
```

```python
import functools

import jax
import jax.numpy as jnp
import numpy as np
from jax import lax
from jax.experimental import pallas as pl
from jax.experimental.pallas import tpu as pltpu

F32 = jnp.float32
BF16 = jnp.bfloat16

D_MODEL = 1024
DEPTH = 2
CHUNK = 64
EPS = 1e-6
NEG_INF = -1e30
ROPE_BASE = 10000.0
LOG2E = 1.4426950408889634
RET_HEADS, RET_DK, RET_DV = 4, 64, 64
RET_W = RET_HEADS * RET_DV
MLA_HEADS, MLA_NOPE, MLA_ROPE, MLA_V = 8, 64, 32, 64
MLA_QK = MLA_NOPE + MLA_ROPE
Q_LORA, KV_LORA = 256, 128
MLA_W = MLA_HEADS * MLA_V
BAND_HEADS, BAND_DH = 4, 64
BAND_W = BAND_HEADS * BAND_DH
BAND_PREV_CHUNKS = 8
BAND_PAST = BAND_PREV_CHUNKS * CHUNK
MAX_REL = 128
D_MIX = RET_W + MLA_W + BAND_W
SEG = (RET_HEADS * RET_DK, RET_HEADS * RET_DK, RET_W, RET_W, Q_LORA, KV_LORA, MLA_ROPE, MLA_W,
       BAND_W, BAND_W, BAND_W, BAND_W)

LANES = 128
HEAD_PAD = LANES
MLA_QW = MLA_HEADS * HEAD_PAD
MLA_VROWS = MLA_V + 16
MLA_VT = MLA_HEADS * MLA_VROWS
C_AQ, C_AK, C_AV, C_CQ, C_CKV, C_KR, C_BQ, C_BK, C_BV, C_G, C_END = (
    0, 256, 512, 768, 1024, 1152, 1280, 1536, 1792, 2048, 3072)
VMEM_LIMIT = 56 * 1024 * 1024
K1_SUB = 256


def _dot(a, b):
    return jnp.dot(a, b, preferred_element_type=F32)


def _dot_nt(a, b):
    return lax.dot_general(a, b, (((1,), (1,)), ((), ())), preferred_element_type=F32)


def _dot_tn(a, b):
    return lax.dot_general(a, b, (((0,), (0,)), ((), ())), preferred_element_type=F32)


def _group_sums(x, split, two_groups):
    lane = lax.broadcasted_iota(jnp.int32, (1, LANES), 1)
    lo = lane < split
    sums = []
    for b in range(x.shape[-1] // LANES):
        xb = x[:, b * LANES:(b + 1) * LANES]
        sq = xb * xb
        if two_groups:
            sums.append((jnp.sum(jnp.where(lo, sq, 0.0), axis=-1, keepdims=True),
                         jnp.sum(jnp.where(lo, 0.0, sq), axis=-1, keepdims=True)))
        else:
            sums.append((jnp.sum(sq, axis=-1, keepdims=True), None))
    return sums


def _group_normalize(x, sums, split, n_lo, n_hi):
    lane = lax.broadcasted_iota(jnp.int32, (1, LANES), 1)
    lo = lane < split
    outs = []
    for b, (s_lo, s_hi) in enumerate(sums):
        inv = lax.rsqrt(s_lo * (1.0 / n_lo) + EPS)
        if n_hi:
            inv = jnp.where(lo, inv, lax.rsqrt(s_hi * (1.0 / n_hi) + EPS))
        outs.append(x[:, b * LANES:(b + 1) * LANES] * inv)
    return jnp.concatenate(outs, axis=1)


def _group_rms_normed(x, split, n_lo, n_hi):
    return _group_normalize(x, _group_sums(x, split, n_hi > 0), split, n_lo, n_hi)


def _group64_inv_rms(x, bd, inv_n):
    sq = (x * x).astype(BF16)
    w = bd.shape[0]
    sums = jnp.concatenate([_dot(sq[:, c * w:(c + 1) * w], bd) for c in range(x.shape[-1] // w)], axis=1)
    return lax.rsqrt(sums * inv_n + EPS)


def _swap_halves(x, half):
    w = x.shape[-1]
    lane = lax.broadcasted_iota(jnp.int32, (1, w), 1)
    nxt = pltpu.roll(x, w - half, axis=1)
    prv = pltpu.roll(x, half, axis=1)
    return jnp.where((lane & half) == 0, nxt, prv)


def _rope(x, cos, sin_signed, half):
    return x * cos + _swap_halves(x, half) * sin_signed


def _silu(g):
    return g * (1.0 / (1.0 + jnp.exp(-g)))


def _full(shape):
    nd = len(shape)
    return pl.BlockSpec(shape, lambda *_: (0,) * nd)


def _params(sem):
    return pltpu.CompilerParams(dimension_semantics=sem, vmem_limit_bytes=VMEM_LIMIT)


def _k1_body(x_ref, ng_ref, win_ref, wuq_ref, qag_ref, kvag_ref, krg_ref, qgain_ref, bqg_ref, bkg_ref,
             cret_ref, sret_ref, cq_ref, sq_ref, ckr_ref, skr_ref, wuqr_ref, qgainr_ref, bd_ref, invnq_ref,
             rq_ref, rk_ref, rv_ref, gate_ref, qm_ref, ckv_ref, kr_ref, bq_ref, bk_ref, bv_ref,
             bks_ref, bvs_ref, *, sub):
    n_sub = x_ref.shape[0] // sub
    bd = bd_ref[...]

    def stage_a(r):
        x = x_ref[r * sub:(r + 1) * sub, :]
        h = x * lax.rsqrt(jnp.mean(x * x, axis=-1, keepdims=True) + EPS) * ng_ref[...]
        hb = h.astype(BF16)

        def seg(lo, hi):
            return _dot(hb, win_ref[:, lo:hi])

        z = {'cq': seg(C_CQ, C_CKV), 'bq': seg(C_BQ, C_BK), 'bk': seg(C_BK, C_BV)}
        cq = z['cq']
        cq = cq * lax.rsqrt(jnp.mean(cq * cq, axis=-1, keepdims=True) + EPS) * qag_ref[...]
        z['aq'], z['ak'] = seg(C_AQ, C_AK), seg(C_AK, C_AV)
        cqb = cq.astype(BF16)
        z['qf'] = _dot(cqb, wuq_ref[...])
        z['qf_sw'] = _dot(cqb, wuqr_ref[...])
        z['bq_inv'] = _group64_inv_rms(z['bq'], bd, 1.0 / BAND_DH)
        z['bk_inv'] = _group64_inv_rms(z['bk'], bd, 1.0 / BAND_DH)
        z['g'] = seg(C_G, C_END)
        z['q_inv'] = _group64_inv_rms(z['qf'], bd, invnq_ref[...])
        z['av'], z['ckv'] = seg(C_AV, C_CQ), seg(C_CKV, C_KR)
        z['kr'], z['bv'] = seg(C_KR, C_BQ), seg(C_BV, C_G)
        return z

    def stage_b(r, z):
        rows = slice(r * sub, (r + 1) * sub)
        bq_ref[rows, :] = (z['bq'] * z['bq_inv'] * bqg_ref[...]).astype(BF16)
        bk = z['bk'] * z['bk_inv'] * bkg_ref[...]
        bk_ref[rows, :] = bk.astype(BF16)
        bks_ref[rows, :] = bk
        bv_ref[rows, :] = z['bv'].astype(BF16)
        bvs_ref[rows, :] = z['bv']
        cret, sret = cret_ref[rows, :], sret_ref[rows, :]
        rq_ref[rows, :] = _rope(z['aq'], cret, sret, RET_DK // 2).astype(BF16)
        rk_ref[rows, :] = _rope(z['ak'], cret, sret, RET_DK // 2).astype(BF16)
        rv_ref[rows, :] = z['av'].astype(BF16)
        gate_ref[rows, :] = _silu(z['g']).astype(BF16)
        cq_t = jnp.concatenate([cq_ref[rows, :]] * MLA_HEADS, axis=1)
        sq_t = jnp.concatenate([sq_ref[rows, :]] * MLA_HEADS, axis=1)
        qm_ref[rows, :] = (z['qf'] * (z['q_inv'] * qgain_ref[...]) * cq_t
                           + z['qf_sw'] * (z['q_inv'] * qgainr_ref[...]) * sq_t).astype(BF16)
        ckv, kr = z['ckv'], z['kr']
        ckv_ref[rows, :] = ckv * lax.rsqrt(jnp.mean(ckv * ckv, axis=-1, keepdims=True) + EPS) * kvag_ref[...]
        kr = kr * lax.rsqrt(jnp.sum(kr * kr, axis=-1, keepdims=True) * (1.0 / MLA_ROPE) + EPS) * krg_ref[...]
        kr = _rope(kr, ckr_ref[rows, :], skr_ref[rows, :], MLA_ROPE // 2)
        kr_ref[rows, :] = kr[:, :MLA_ROPE]

    z = stage_a(0)
    for r in range(n_sub):
        z_next = stage_a(r + 1) if r + 1 < n_sub else None
        stage_b(r, z)
        z = z_next


def _k1(x2d, lw, tabs, consts, *, tm, rows_per_seq, n_keep):
    T = x2d.shape[0]
    nb = rows_per_seq // tm
    nkb = n_keep // tm
    n_seq = T // rows_per_seq

    row = lambda w: pl.BlockSpec((tm, w), lambda i: (i, 0))
    tab = lambda w: pl.BlockSpec((tm, w), lambda i: (i % nb, 0))
    keep = pl.BlockSpec((tm, BAND_W), lambda i: ((i // nb) * nkb + jnp.maximum(i % nb - (nb - nkb), 0), 0))

    ins = [
        (x2d, row(D_MODEL)), (lw['norm_g'], _full((1, D_MODEL))), (lw['w_in'], _full((D_MODEL, C_END))),
        (lw['w_uq'], _full((Q_LORA, MLA_QW))), (lw['qa_g'], _full((1, Q_LORA))),
        (lw['kva_g'], _full((1, KV_LORA))), (lw['kr_g'], _full((1, LANES))),
        (lw['q_gain'], _full((1, MLA_QW))), (lw['bq_gain'], _full((1, BAND_W))),
        (lw['bk_gain'], _full((1, BAND_W))),
        (tabs['cret'], tab(RET_W)), (tabs['sret'], tab(RET_W)), (tabs['cq'], tab(LANES)),
        (tabs['sq'], tab(LANES)), (tabs['ckr'], tab(LANES)), (tabs['skr'], tab(LANES)),
        (lw['w_uq_sw'], _full((Q_LORA, MLA_QW))), (lw['q_gain_sw'], _full((1, MLA_QW))),
        (consts['bd64'], _full((RET_W, RET_W))), (consts['invn_q'], _full((1, MLA_QW))),
    ]
    outs = [
        ((T, RET_W), BF16, row(RET_W)), ((T, RET_W), BF16, row(RET_W)), ((T, RET_W), BF16, row(RET_W)),
        ((T, D_MIX), BF16, row(D_MIX)), ((T, MLA_QW), BF16, row(MLA_QW)),
        ((T, KV_LORA), F32, row(KV_LORA)), ((T, MLA_ROPE), F32, row(MLA_ROPE)),
        ((T, BAND_W), BF16, row(BAND_W)), ((T, BAND_W), BF16, row(BAND_W)), ((T, BAND_W), BF16, row(BAND_W)),
        ((n_seq * n_keep, BAND_W), F32, keep), ((n_seq * n_keep, BAND_W), F32, keep),
    ]
    return pl.pallas_call(
        functools.partial(_k1_body, sub=min(tm, K1_SUB)),
        grid=(T // tm,),
        in_specs=[s for _, s in ins],
        out_specs=[s for _, _, s in outs],
        out_shape=[jax.ShapeDtypeStruct(sh, dt) for sh, dt, _ in outs],
        compiler_params=_params(("arbitrary",)),
        name="k1_proj",
    )(*[a for a, _ in ins])


def _to_heads_body(k_ref, v_ref, ko_ref, vo_ref):
    for hd in range(BAND_HEADS):
        lanes = slice(hd * BAND_DH, (hd + 1) * BAND_DH)
        ko_ref[:, hd, :] = k_ref[:, lanes]
        vo_ref[:, hd, :] = v_ref[:, lanes]


def _to_heads(k2d, v2d, *, tm):
    n = k2d.shape[0]
    row = pl.BlockSpec((tm, BAND_W), lambda i: (i, 0))
    head = pl.BlockSpec((tm, BAND_HEADS, BAND_DH), lambda i: (i, 0, 0))
    shape = jax.ShapeDtypeStruct((n, BAND_HEADS, BAND_DH), F32)
    return pl.pallas_call(
        _to_heads_body, grid=(n // tm,), in_specs=[row, row], out_specs=[head, head],
        out_shape=[shape, shape], compiler_params=_params(("arbitrary",)), name="band_state_heads",
    )(k2d, v2d)


def _kkv_body(ckv_ref, kr_ref, wk_ref, wv_ref, kgain_ref, *rest, transpose_v):
    if transpose_v:
        ones_ref, k_ref, v_ref = rest
    else:
        k_ref, v_ref = rest
    c = ckv_ref[...].astype(BF16)
    kn = _dot(c, wk_ref[...])
    kn = _group_rms_normed(kn, LANES, MLA_NOPE, 0) * kgain_ref[...]
    kr = kr_ref[...]
    rows = kr.shape[0]
    kr_block = jnp.concatenate([jnp.zeros((rows, MLA_NOPE), F32), kr,
                                jnp.zeros((rows, HEAD_PAD - MLA_QK), F32)], axis=1)
    k_ref[...] = (kn + jnp.concatenate([kr_block] * MLA_HEADS, axis=1)).astype(BF16)
    if transpose_v:
        v_ref[...] = (_dot_nt(wv_ref[...], c) + ones_ref[...]).astype(BF16)
    else:
        v_ref[...] = _dot(c, wv_ref[...]).astype(BF16)


def _kkv(ckv2d, kr2d, lw, *, tm, seq=None):
    T = ckv2d.shape[0]
    row = lambda w: pl.BlockSpec((tm, w), lambda i: (i, 0))
    extra_in = []
    if seq is None:
        wv, wv_spec = lw['w_uv'], _full((KV_LORA, MLA_W))
        v_spec, v_shape = row(MLA_W), (T, MLA_W)
    else:
        nb = seq // tm
        wv, wv_spec = lw['w_uv_t'], _full((MLA_VT, KV_LORA))
        v_spec = pl.BlockSpec((None, MLA_VT, tm), lambda i: (i // nb, 0, i % nb))
        v_shape = (T // seq, MLA_VT, seq)
        ones = np.zeros((MLA_HEADS, MLA_VROWS, 1), np.float32)
        ones[:, MLA_V] = 1.0
        extra_in = [(jnp.asarray(ones.reshape(MLA_VT, 1)), _full((MLA_VT, 1)))]
    return pl.pallas_call(
        functools.partial(_kkv_body, transpose_v=seq is not None),
        grid=(T // tm,),
        in_specs=[row(KV_LORA), row(MLA_ROPE), _full((KV_LORA, MLA_QW)), wv_spec, _full((1, MLA_QW))]
        + [sp for _, sp in extra_in],
        out_specs=[row(MLA_QW), v_spec],
        out_shape=[jax.ShapeDtypeStruct((T, MLA_QW), BF16), jax.ShapeDtypeStruct(v_shape, BF16)],
        compiler_params=_params(("arbitrary",)),
        name="kkv_up",
    )(ckv2d, kr2d, lw['w_uk'], wv, lw['k_gain'], *[a for a, _ in extra_in])


def _mla_prompt_body(q_ref, k_ref, vt_ref, o_ref, m_sc, acc_sc, *, tq):
    qi = pl.program_id(1)
    m_sc[...] = jnp.full(m_sc.shape, NEG_INF, F32)
    acc_sc[...] = jnp.zeros(acc_sc.shape, F32)

    def tile(kb, masked):
        start = pl.multiple_of(kb * tq, tq)
        if masked:
            kc = lax.broadcasted_iota(jnp.int32, (tq, tq), 0) // CHUNK
            qc = lax.broadcasted_iota(jnp.int32, (tq, tq), 1) // CHUNK
            visible = kc <= qc

        def scores(h):
            q = q_ref[:, h * HEAD_PAD:(h + 1) * HEAD_PAD]
            k = k_ref[pl.ds(start, tq), h * HEAD_PAD:(h + 1) * HEAD_PAD]
            st = _dot_nt(k, q)
            return jnp.where(visible, st, NEG_INF) if masked else st

        def softmax(h, st):
            m_prev = m_sc[h:h + 1, :]
            m_new = jnp.maximum(m_prev, jnp.max(st, axis=0, keepdims=True))
            alpha = jnp.exp2(m_prev - m_new)
            p = jnp.exp2(st - m_new)
            m_sc[h:h + 1, :] = m_new
            return p.astype(BF16), alpha

        def values(h, p, alpha):
            vt = vt_ref[h * MLA_VROWS:(h + 1) * MLA_VROWS, pl.ds(start, tq)]
            acc_sc[h] = alpha * acc_sc[h] + _dot(vt, p)

        st = {0: scores(0), 1: scores(1)}
        pa = {}
        for h in range(MLA_HEADS):
            pa[h] = softmax(h, st.pop(h))
            if h >= 1:
                values(h - 1, *pa.pop(h - 1))
            if h + 2 < MLA_HEADS:
                st[h + 2] = scores(h + 2)
        values(MLA_HEADS - 1, *pa.pop(MLA_HEADS - 1))

    def body(kb, carry):
        tile(kb, False)
        return carry

    lax.fori_loop(0, qi, body, 0)
    tile(qi, True)
    for j in range(MLA_HEADS // 2):
        pair = jnp.concatenate([acc_sc[h, :MLA_V, :] * (1.0 / acc_sc[h, MLA_V:MLA_V + 1, :])
                                for h in (2 * j, 2 * j + 1)], axis=0)
        o_ref[:, j * LANES:(j + 1) * LANES] = pair.T.astype(BF16)


def _mla_prompt(q, k, vt, *, tq):
    B, S, _ = q.shape
    once = pl.Buffered(1)
    return pl.pallas_call(
        functools.partial(_mla_prompt_body, tq=tq),
        grid=(B, S // tq),
        in_specs=[pl.BlockSpec((None, tq, MLA_QW), lambda b, i: (b, i, 0)),
                  pl.BlockSpec((None, S, MLA_QW), lambda b, i: (b, 0, 0), pipeline_mode=once),
                  pl.BlockSpec((None, MLA_VT, S), lambda b, i: (b, 0, 0), pipeline_mode=once)],
        out_specs=pl.BlockSpec((None, tq, MLA_W), lambda b, i: (b, i, 0)),
        out_shape=jax.ShapeDtypeStruct((B, S, MLA_W), BF16),
        scratch_shapes=[pltpu.VMEM((MLA_HEADS, tq), F32), pltpu.VMEM((MLA_HEADS, MLA_VROWS, tq), F32)],
        compiler_params=_params(("arbitrary", "arbitrary")),
        name="mla_prompt",
    )(q, k, vt)


def _mla_sample_body(q_ref, kp_ref, vp_ref, kn_ref, vn_ref, o_ref):
    lane = lax.broadcasted_iota(jnp.int32, (1, LANES), 1)
    outs = []
    scores = []
    for h in range(MLA_HEADS):
        hs = slice(h * HEAD_PAD, (h + 1) * HEAD_PAD)
        q = q_ref[:, hs]
        scores.append((_dot_nt(q, kp_ref[:, hs]), _dot_nt(q, kn_ref[:, hs])))
    for h in range(MLA_HEADS):
        ps = slice((h // 2) * LANES, (h // 2 + 1) * LANES)
        v_lanes = (lane // MLA_V) == (h % 2)
        s1, s2 = scores[h]
        m = jnp.maximum(jnp.max(s1, axis=-1, keepdims=True), jnp.max(s2, axis=-1, keepdims=True))
        p1 = jnp.exp2(s1 - m)
        p2 = jnp.exp2(s2 - m)
        l = jnp.sum(p1, axis=-1, keepdims=True) + jnp.sum(p2, axis=-1, keepdims=True)
        v1 = vp_ref[:, ps]
        v2 = vn_ref[:, ps]
        acc = (_dot(p1.astype(BF16), jnp.where(v_lanes, v1, jnp.zeros_like(v1)))
               + _dot(p2.astype(BF16), jnp.where(v_lanes, v2, jnp.zeros_like(v2))))
        outs.append(acc * (1.0 / l))
    for j in range(MLA_HEADS // 2):
        o_ref[:, j * LANES:(j + 1) * LANES] = (outs[2 * j] + outs[2 * j + 1]).astype(BF16)


def _band_heads(q, pieces, biases, out_dtype):
    lane = lax.broadcasted_iota(jnp.int32, (1, BAND_W), 1)
    out = jnp.zeros((q.shape[0], BAND_W), F32)
    raw = []
    for h in range(BAND_HEADS):
        qh = jnp.where((lane // BAND_DH) == h, q, jnp.zeros_like(q))
        raw.append([_dot_nt(qh, k) for k, _ in pieces])
    for h in range(BAND_HEADS):
        sel = (lane // BAND_DH) == h
        ss = [s + b_ref[h] for s, b_ref in zip(raw[h], biases)]
        m = functools.reduce(jnp.maximum, [jnp.max(s, axis=-1, keepdims=True) for s in ss])
        ps = [jnp.exp2(s - m) for s in ss]
        l = functools.reduce(jnp.add, [jnp.sum(p, axis=-1, keepdims=True) for p in ps])
        acc = functools.reduce(jnp.add, [
            _dot(p.astype(BF16), jnp.where(sel, v, jnp.zeros_like(v))) for p, (_, v) in zip(ps, pieces)])
        out = out + acc * (1.0 / l)
    return out.astype(out_dtype)


def _band_prompt_body(q_ref, kp_ref, kc_ref, vp_ref, vc_ref, bp_ref, bc_ref, o_ref):
    o_ref[...] = _band_heads(q_ref[...], [(kp_ref[...], vp_ref[...]), (kc_ref[...], vc_ref[...])],
                             [bp_ref, bc_ref], BF16)


def _band_prompt(q, k, v, bias_prev, bias_cur, *, tq):
    B, S, _ = q.shape
    cur = pl.BlockSpec((None, tq, BAND_W), lambda b, i: (b, i, 0))
    prev = pl.BlockSpec((None, tq, BAND_W), lambda b, i: (b, jnp.maximum(i - 1, 0), 0))
    bias = _full((BAND_HEADS, tq, tq))
    bias_prev_spec = pl.BlockSpec((None, BAND_HEADS, tq, tq), lambda b, i: (jnp.minimum(i, 1), 0, 0, 0))
    return pl.pallas_call(
        _band_prompt_body,
        grid=(B, S // tq),
        in_specs=[cur, prev, cur, prev, cur, bias_prev_spec, bias],
        out_specs=cur,
        out_shape=jax.ShapeDtypeStruct((B, S, BAND_W), BF16),
        compiler_params=_params(("arbitrary", "arbitrary")),
        name="band_prompt",
    )(q, k, k, v, v, bias_prev, bias_cur)


def _band_sample_body(q_ref, kp_ref, vp_ref, kn_ref, vn_ref, bp_ref, bn_ref, o_ref):
    q, kn, vn = q_ref[...], kn_ref[...], vn_ref[...]
    heads = [slice(h * BAND_DH, (h + 1) * BAND_DH) for h in range(BAND_HEADS)]
    kps = [kp_ref[:, h, :].astype(BF16) for h in range(BAND_HEADS)]
    raw = [(_dot_nt(q[:, hs], kps[h]), _dot_nt(q[:, hs], kn[:, hs])) for h, hs in enumerate(heads)]
    outs = []
    for h, hs in enumerate(heads):
        s1, s2 = raw[h][0] + bp_ref[h], raw[h][1] + bn_ref[h]
        m = jnp.maximum(jnp.max(s1, axis=-1, keepdims=True), jnp.max(s2, axis=-1, keepdims=True))
        p1, p2 = jnp.exp2(s1 - m), jnp.exp2(s2 - m)
        l = jnp.sum(p1, axis=-1, keepdims=True) + jnp.sum(p2, axis=-1, keepdims=True)
        acc = _dot(p1.astype(BF16), vp_ref[:, h, :].astype(BF16)) + _dot(p2.astype(BF16), vn[:, hs])
        outs.append(acc * (1.0 / l))
    o_ref[...] = jnp.concatenate(outs, axis=1).astype(BF16)


def _ret_chunk(q, k, v, state, d_ref, xi_ref, zeta_ref, gam_ref, bd_ref, gain_ref):
    lane = lax.broadcasted_iota(jnp.int32, (1, RET_W), 1)
    o = _dot(q, state.astype(BF16)) * xi_ref[...]
    for h in range(RET_HEADS):
        sel = (lane // RET_DV) == h
        a = _dot_nt(jnp.where(sel, q, jnp.zeros_like(q)), k) * d_ref[h]
        o = o + _dot(a.astype(BF16), jnp.where(sel, v, jnp.zeros_like(v)))
    kz = (k.astype(F32) * zeta_ref[...]).astype(BF16)
    s_new = gam_ref[...] * state + bd_ref[...] * _dot_tn(kz, v)
    inv = _group64_inv_rms(o, bd_ref[...].astype(BF16), 1.0 / RET_DV)
    return (o * inv * gain_ref[...]).astype(BF16), s_new


def _ret_body(q_ref, k_ref, v_ref, s0_ref, d_ref, xi_ref, zeta_ref, gam_ref, bd_ref, gain_ref,
              o_ref, sfin_ref, s_sc):
    @pl.when(pl.program_id(1) == 0)
    def _():
        s_sc[...] = s0_ref[...]

    o, s_new = _ret_chunk(q_ref[...], k_ref[...], v_ref[...], s_sc[...],
                          d_ref, xi_ref, zeta_ref, gam_ref, bd_ref, gain_ref)
    s_sc[...] = s_new
    sfin_ref[...] = s_new
    o_ref[...] = o


def _retention(q, k, v, s0, rt, consts, gain, *, c):
    B, S, _ = q.shape
    blk = pl.BlockSpec((None, c, RET_W), lambda b, t: (b, t, 0))
    st = pl.BlockSpec((None, RET_W, RET_W), lambda b, t: (b, 0, 0))
    return pl.pallas_call(
        _ret_body,
        grid=(B, S // c),
        in_specs=[blk, blk, blk, st, _full((RET_HEADS, c, c)), _full((c, RET_W)), _full((c, RET_W)),
                  _full((RET_W, RET_W)), _full((RET_W, RET_W)), _full((1, RET_W))],
        out_specs=[blk, st],
        out_shape=[jax.ShapeDtypeStruct((B, S, RET_W), BF16), jax.ShapeDtypeStruct((B, RET_W, RET_W), F32)],
        scratch_shapes=[pltpu.VMEM((RET_W, RET_W), F32)],
        compiler_params=_params(("arbitrary", "arbitrary")),
        name="retention",
    )(q, k, v, s0, rt['decay'], rt['xi'], rt['zeta'], rt['gamma'], consts['bd_mask'], gain)


def _sample_mixers_body(mq_ref, mkp_ref, mvp_ref, mkn_ref, mvn_ref,
                        bq_ref, bkp_ref, bvp_ref, bkn_ref, bvn_ref, bp_ref, bn_ref,
                        rq_ref, rk_ref, rv_ref, s0_ref, d_ref, xi_ref, zeta_ref, gam_ref, bd_ref, gain_ref,
                        mo_ref, bo_ref, ro_ref, sfin_ref):
    _mla_sample_body(mq_ref, mkp_ref, mvp_ref, mkn_ref, mvn_ref, mo_ref)
    _band_sample_body(bq_ref, bkp_ref, bvp_ref, bkn_ref, bvn_ref, bp_ref, bn_ref, bo_ref)
    s0 = s0_ref[...]
    state = jnp.concatenate([
        jnp.concatenate([s0[h] if g == h else jnp.zeros((RET_DK, RET_DV), F32) for g in range(RET_HEADS)], axis=1)
        for h in range(RET_HEADS)], axis=0)
    o, s_new = _ret_chunk(rq_ref[...], rk_ref[...], rv_ref[...], state,
                          d_ref, xi_ref, zeta_ref, gam_ref, bd_ref, gain_ref)
    ro_ref[...] = o
    for h in range(RET_HEADS):
        sfin_ref[h] = s_new[h * RET_DK:(h + 1) * RET_DK, h * RET_DV:(h + 1) * RET_DV]


def _sample_mixers(mq, mkp, mvp, mkn, mvn, bq, bkp, bvp, bkn, bvn, bias_past, bias_new,
                   rq, rk, rv, s0, rt, consts, gain, *, layer):
    B, L, _ = mq.shape
    P, PB = mkp.shape[1], bkp.shape[2]
    blk = lambda n, w: pl.BlockSpec((None, n, w), lambda b: (b, 0, 0))
    cache = pl.BlockSpec((None, None, PB, BAND_HEADS, BAND_DH), lambda b: (layer, b, 0, 0, 0))
    return pl.pallas_call(
        _sample_mixers_body,
        grid=(B,),
        in_specs=[blk(L, MLA_QW), blk(P, MLA_QW), blk(P, MLA_W), blk(L, MLA_QW), blk(L, MLA_W),
                  blk(L, BAND_W), cache, cache, blk(L, BAND_W), blk(L, BAND_W),
                  _full((BAND_HEADS, L, PB)), _full((BAND_HEADS, L, L)),
                  blk(L, RET_W), blk(L, RET_W), blk(L, RET_W),
                  pl.BlockSpec((None, None, RET_HEADS, RET_DK, RET_DV), lambda b: (layer, b, 0, 0, 0)),
                  _full((RET_HEADS, L, L)), _full((L, RET_W)), _full((L, RET_W)),
                  _full((RET_W, RET_W)), _full((RET_W, RET_W)), _full((1, RET_W))],
        out_specs=[blk(L, MLA_W), blk(L, BAND_W), blk(L, RET_W),
                   pl.BlockSpec((None, RET_HEADS, RET_DK, RET_DV), lambda b: (b, 0, 0, 0))],
        out_shape=[jax.ShapeDtypeStruct((B, L, MLA_W), BF16), jax.ShapeDtypeStruct((B, L, BAND_W), BF16),
                   jax.ShapeDtypeStruct((B, L, RET_W), BF16),
                   jax.ShapeDtypeStruct((B, RET_HEADS, RET_DK, RET_DV), F32)],
        compiler_params=_params(("arbitrary",)),
        name="sample_mixers",
    )(mq, mkp, mvp, mkn, mvn, bq, bkp, bvp, bkn, bvn, bias_past, bias_new,
      rq, rk, rv, s0, rt['decay'], rt['xi'], rt['zeta'], rt['gamma'], consts['bd_mask'], gain)


def _kout_body(x_ref, ro_ref, mo_ref, bo_ref, g_ref, w_ref, y_ref):
    g = g_ref[...]
    y = x_ref[...]
    y = y + _dot(ro_ref[...] * g[:, :RET_W], w_ref[:RET_W, :])
    y = y + _dot(mo_ref[...] * g[:, RET_W:RET_W + MLA_W], w_ref[RET_W:RET_W + MLA_W, :])
    y = y + _dot(bo_ref[...] * g[:, RET_W + MLA_W:], w_ref[RET_W + MLA_W:, :])
    y_ref[...] = y


def _kout(x2d, ro, mo, bo, gate, w_out, *, tm):
    T = x2d.shape[0]
    row = lambda w: pl.BlockSpec((tm, w), lambda i: (i, 0))
    return pl.pallas_call(
        _kout_body,
        grid=(T // tm,),
        in_specs=[row(D_MODEL), row(RET_W), row(MLA_W), row(BAND_W), row(D_MIX), _full((D_MIX, D_MODEL))],
        out_specs=row(D_MODEL),
        out_shape=jax.ShapeDtypeStruct((T, D_MODEL), F32),
        compiler_params=_params(("arbitrary",)),
        name="kout_proj",
    )(x2d, ro, mo, bo, gate, w_out)


def _constants():
    bd = np.kron(np.eye(RET_HEADS, dtype=np.float32), np.ones((RET_DK, RET_DV), np.float32))
    invn_q = np.tile(np.repeat([1.0 / MLA_NOPE, 1.0 / MLA_ROPE], HEAD_PAD // 2), MLA_HEADS)[None]
    return {'bd_mask': jnp.asarray(bd), 'bd64': jnp.asarray(bd, BF16), 'invn_q': jnp.asarray(invn_q, F32)}


def _rope_tables(pos, reps=1):
    pos = np.asarray(pos, np.float64)

    def cs(half):
        inv = ROPE_BASE ** (-np.arange(half, dtype=np.float64) / half)
        ang = pos[:, None] * inv[None, :]
        c, s = np.cos(ang), np.sin(ang)
        return np.concatenate([c, c], axis=-1), np.concatenate([-s, s], axis=-1)

    T = pos.shape[0]
    c32, s32 = cs(RET_DK // 2)
    c16, s16 = cs(MLA_ROPE // 2)
    ones = lambda w: np.ones((T, w))
    zeros = lambda w: np.zeros((T, w))
    pad = HEAD_PAD - MLA_QK
    tabs = {
        'cret': np.tile(c32, (1, RET_HEADS)), 'sret': np.tile(s32, (1, RET_HEADS)),
        'cq': np.concatenate([ones(MLA_NOPE), c16, ones(pad)], axis=-1),
        'sq': np.concatenate([zeros(MLA_NOPE), s16, zeros(pad)], axis=-1),
        'ckr': np.concatenate([c16, ones(LANES - MLA_ROPE)], axis=-1),
        'skr': np.concatenate([s16, zeros(LANES - MLA_ROPE)], axis=-1),
    }
    return {k: jnp.asarray(np.tile(v, (reps, 1)), F32) for k, v in tabs.items()}


def _retention_tables(c):
    lg = np.log1p(-np.exp2(-5.0 - np.arange(RET_HEADS, dtype=np.float64)))
    idx = np.arange(c, dtype=np.float64)
    diff = idx[:, None] - idx[None, :]
    decay = np.where(diff >= 0, np.exp(lg[:, None, None] * np.maximum(diff, 0.0)), 0.0)
    per_lane = lambda t: np.repeat(t, RET_DV, axis=-1)
    xi = per_lane(np.exp(lg[None, :] * (idx[:, None] + 1.0)))
    zeta = per_lane(np.exp(lg[None, :] * (c - 1.0 - idx)[:, None]))
    gamma = np.broadcast_to(per_lane(np.exp(lg * c)[None, :]).T, (RET_W, RET_W))
    return {k: jnp.asarray(v, F32) for k, v in
            {'decay': decay, 'xi': xi, 'zeta': zeta, 'gamma': gamma}.items()}


def _band_tables_body(rp_ref, rc_ref, prev_ref, cur_ref, past_ref, new_ref, *, tq, n_new):
    w = 2 * tq
    tp = pltpu.roll(jnp.broadcast_to(rp_ref[...], (tq, w)), 0, 1, stride=1, stride_axis=0)[:, :tq]
    tc = pltpu.roll(jnp.broadcast_to(rc_ref[...], (tq, w)), 0, 1, stride=1, stride_axis=0)[:, :tq]
    qc = lax.broadcasted_iota(jnp.int32, (tq, tq), 0) // CHUNK
    kc = lax.broadcasted_iota(jnp.int32, (tq, tq), 1) // CHUNK
    prev_ref[0] = jnp.full((tq, tq), NEG_INF, F32)
    prev_ref[1] = jnp.where(kc >= qc, tp, NEG_INF)
    cur_ref[...] = jnp.where(kc <= qc, tc, NEG_INF)
    past_ref[...] = tp[:n_new, :]
    new_ref[...] = tc[:n_new, :n_new]


def _band_tables(band_bias, *, tq, n_new):
    H = band_bias.shape[0]
    band_bias = band_bias * LOG2E
    lo, mid, hi = band_bias[:, :1], band_bias[:, MAX_REL + 1:2 * MAX_REL], band_bias[:, 2 * MAX_REL:]
    rep = lambda col, n: jnp.broadcast_to(col, (H, n))
    r_cur = jnp.concatenate([band_bias[:, MAX_REL::-1], rep(lo, tq - MAX_REL - 1), rep(hi, tq - MAX_REL),
                             band_bias[:, :MAX_REL:-1]], axis=1)
    r_prev = jnp.concatenate([rep(hi, tq - MAX_REL + 1), mid[:, ::-1], rep(hi, tq)], axis=1)
    row = pl.BlockSpec((None, 1, 2 * tq), lambda h: (h, 0, 0))
    tile = lambda n, m: pl.BlockSpec((None, n, m), lambda h: (h, 0, 0))
    return pl.pallas_call(
        functools.partial(_band_tables_body, tq=tq, n_new=n_new),
        grid=(H,),
        in_specs=[row, row],
        out_specs=[pl.BlockSpec((2, None, tq, tq), lambda h: (0, h, 0, 0)), tile(tq, tq), tile(n_new, tq),
                   tile(n_new, n_new)],
        out_shape=[jax.ShapeDtypeStruct((2, H, tq, tq), F32), jax.ShapeDtypeStruct((H, tq, tq), F32),
                   jax.ShapeDtypeStruct((H, n_new, tq), F32), jax.ShapeDtypeStruct((H, n_new, n_new), F32)],
        compiler_params=_params(("arbitrary",)),
        name="band_tables",
    )(r_prev[:, None, :], r_cur[:, None, :])


def _layer_weights(l, norm_g, w_in, ret_gn_g, mla_qa_g, mla_w_uq, mla_qn_g, mla_qr_g, mla_kva_g, mla_kr_g,
                   mla_w_ukv, mla_kn_g, band_qn_g, band_kn_g, w_out):
    cuts = np.cumsum(SEG)[:-1].tolist()
    a_q, a_k, a_v, a_g, b_cq, b_ckv, b_kr, b_g, c_q, c_k, c_v, c_g = jnp.split(w_in[l], cuts, axis=-1)
    b_kr = jnp.pad(b_kr, ((0, 0), (0, LANES - MLA_ROPE)))
    w_in_p = jnp.concatenate([a_q, a_k * (RET_DK ** -0.5), a_v, b_cq, b_ckv, b_kr, c_q, c_k, c_v, a_g, b_g, c_g],
                             axis=-1).astype(BF16)
    pad = HEAD_PAD - MLA_QK
    w_uq = jnp.pad(mla_w_uq[l].reshape(Q_LORA, MLA_HEADS, MLA_QK), ((0, 0), (0, 0), (0, pad)))
    half = MLA_ROPE // 2

    def swap_rope(t):
        return jnp.concatenate([t[..., :MLA_NOPE], t[..., MLA_NOPE + half:MLA_QK], t[..., MLA_NOPE:MLA_NOPE + half],
                                t[..., MLA_QK:]], axis=-1)
    ukv = mla_w_ukv[l].reshape(KV_LORA, MLA_HEADS, MLA_NOPE + MLA_V)
    w_uk = jnp.pad(ukv[:, :, :MLA_NOPE], ((0, 0), (0, 0), (0, HEAD_PAD - MLA_NOPE)))
    zpad = jnp.zeros((pad,), F32)
    q_gain_head = jnp.concatenate([mla_qn_g[l], mla_qr_g[l], zpad]) * (MLA_QK ** -0.5 * LOG2E)
    q_gain = jnp.tile(q_gain_head, MLA_HEADS)
    q_gain_sw = jnp.tile(swap_rope(q_gain_head), MLA_HEADS)
    k_gain = jnp.tile(jnp.concatenate([mla_kn_g[l], jnp.zeros((HEAD_PAD - MLA_NOPE,), F32)]), MLA_HEADS)
    return {
        'norm_g': norm_g[l][None], 'w_in': w_in_p,
        'w_uq': w_uq.reshape(Q_LORA, MLA_QW).astype(BF16), 'qa_g': mla_qa_g[l][None],
        'w_uq_sw': swap_rope(w_uq).reshape(Q_LORA, MLA_QW).astype(BF16), 'q_gain_sw': q_gain_sw[None],
        'kva_g': mla_kva_g[l][None],
        'kr_g': jnp.concatenate([mla_kr_g[l], jnp.zeros((LANES - MLA_ROPE,), F32)])[None],
        'q_gain': q_gain[None],
        'bq_gain': (jnp.tile(band_qn_g[l], BAND_HEADS) * (BAND_DH ** -0.5 * LOG2E))[None],
        'bk_gain': jnp.tile(band_kn_g[l], BAND_HEADS)[None],
        'w_uk': w_uk.reshape(KV_LORA, MLA_QW).astype(BF16),
        'w_uv': ukv[:, :, MLA_NOPE:].reshape(KV_LORA, MLA_W).astype(BF16),
        'w_uv_t': jnp.pad(ukv[:, :, MLA_NOPE:].transpose(1, 2, 0), ((0, 0), (0, MLA_VROWS - MLA_V), (0, 0))
                          ).reshape(MLA_VT, KV_LORA).astype(BF16),
        'k_gain': k_gain[None], 'ret_gain': ret_gn_g[l][None], 'w_out': w_out[l].astype(BF16),
    }


def _diag_blocks(s_bd):
    B = s_bd.shape[0]
    s = s_bd.reshape(B, RET_HEADS, RET_DK, RET_HEADS, RET_DV)
    return jnp.stack([s[:, h, :, h, :] for h in range(RET_HEADS)], axis=1)


def kernel(x_prompt, x_sample, state_ret, cache_mla_ckv, cache_mla_krope, cache_band_k, cache_band_v, norm_g, w_in, ret_gn_g, mla_qa_g, mla_w_uq, mla_qn_g, mla_qr_g, mla_kva_g, mla_kr_g, mla_w_ukv, mla_kn_g, band_qn_g, band_kn_g, band_bias, w_out):
    B, S, _ = x_prompt.shape
    DB, L, _ = x_sample.shape
    past_len = cache_mla_ckv.shape[2]
    n_band_past = cache_band_k.shape[2]
    n_keep_p = min(BAND_PAST, S)

    TM_P, TM_S, TQ, RET_C = 512, 256, 512, 256
    consts = _constants()
    tabs_p = _rope_tables(np.arange(S))
    tabs_s = _rope_tables(past_len + np.arange(L), reps=DB)
    rt_p = _retention_tables(RET_C)
    rt_s = _retention_tables(L)

    xp = x_prompt.reshape(B * S, D_MODEL)
    xs = x_sample.reshape(DB * L, D_MODEL)
    zeros_state = jnp.zeros((B, RET_W, RET_W), F32)
    p_st, s_st = [], []
    for l in range(DEPTH):
        lw = _layer_weights(l, norm_g, w_in, ret_gn_g, mla_qa_g, mla_w_uq, mla_qn_g, mla_qr_g, mla_kva_g,
                            mla_kr_g, mla_w_ukv, mla_kn_g, band_qn_g, band_kn_g, w_out)

        rq, rk, rv, gate, qm, ckv, kr, bq, bk, bv, bks, bvs = _k1(
            xp, lw, tabs_p, consts, tm=TM_P, rows_per_seq=S, n_keep=n_keep_p)
        km, vt = _kkv(ckv, kr, lw, tm=TM_P, seq=S)
        mla_o = _mla_prompt(qm.reshape(B, S, MLA_QW), km.reshape(B, S, MLA_QW), vt, tq=TQ)
        assert TQ == BAND_PAST == n_band_past
        bias_prev, bias_cur, bias_past, bias_new = _band_tables(band_bias[l], tq=TQ, n_new=L)
        band_o = _band_prompt(bq.reshape(B, S, BAND_W), bk.reshape(B, S, BAND_W), bv.reshape(B, S, BAND_W),
                              bias_prev, bias_cur, tq=TQ)
        ret_o, ret_s = _retention(rq.reshape(B, S, RET_W), rk.reshape(B, S, RET_W), rv.reshape(B, S, RET_W),
                                  zeros_state, rt_p, consts, lw['ret_gain'], c=RET_C)
        xp = _kout(xp, ret_o.reshape(B * S, RET_W), mla_o.reshape(B * S, MLA_W), band_o.reshape(B * S, BAND_W),
                   gate, lw['w_out'], tm=TM_P)
        bks, bvs = _to_heads(bks, bvs, tm=TM_S)
        p_st.append((_diag_blocks(ret_s), ckv.reshape(B, S, KV_LORA), kr.reshape(B, S, MLA_ROPE),
                     bks.reshape(B, n_keep_p, BAND_HEADS, BAND_DH), bvs.reshape(B, n_keep_p, BAND_HEADS, BAND_DH)))

        rq, rk, rv, gate, qm, ckv, kr, bq, bk, bv, bks, bvs = _k1(
            xs, lw, tabs_s, consts, tm=TM_S, rows_per_seq=DB * L, n_keep=DB * L)
        kn, vn = _kkv(ckv, kr, lw, tm=TM_S)
        kp, vp = _kkv(cache_mla_ckv[l].reshape(DB * past_len, KV_LORA),
                      cache_mla_krope[l].reshape(DB * past_len, MLA_ROPE), lw, tm=2 * TM_P)
        mla_o, band_o, ret_o, ret_s = _sample_mixers(
            qm.reshape(DB, L, MLA_QW), kp.reshape(DB, past_len, MLA_QW), vp.reshape(DB, past_len, MLA_W),
            kn.reshape(DB, L, MLA_QW), vn.reshape(DB, L, MLA_W),
            bq.reshape(DB, L, BAND_W), cache_band_k, cache_band_v, bk.reshape(DB, L, BAND_W), bv.reshape(DB, L, BAND_W),
            bias_past, bias_new,
            rq.reshape(DB, L, RET_W), rk.reshape(DB, L, RET_W), rv.reshape(DB, L, RET_W),
            state_ret, rt_s, consts, lw['ret_gain'], layer=l)
        xs = _kout(xs, ret_o.reshape(DB * L, RET_W), mla_o.reshape(DB * L, MLA_W), band_o.reshape(DB * L, BAND_W),
                   gate, lw['w_out'], tm=TM_S)
        bks, bvs = _to_heads(bks, bvs, tm=TM_S)
        s_st.append((ret_s, ckv.reshape(DB, L, KV_LORA), kr.reshape(DB, L, MLA_ROPE),
                     bks.reshape(DB, L, BAND_HEADS, BAND_DH), bvs.reshape(DB, L, BAND_HEADS, BAND_DH)))

    stack = lambda sts, i: jnp.stack([s[i] for s in sts])
    return (xp.reshape(B, S, D_MODEL), xs.reshape(DB, L, D_MODEL),
            stack(p_st, 0), stack(p_st, 1), stack(p_st, 2), stack(p_st, 3), stack(p_st, 4),
            stack(s_st, 0), stack(s_st, 1), stack(s_st, 2), stack(s_st, 3), stack(s_st, 4))
```

```python
import functools

import jax
import jax.numpy as jnp
import numpy as np
from jax import lax
from jax.experimental import pallas as pl
from jax.experimental.pallas import tpu as pltpu

F32 = jnp.float32
BF16 = jnp.bfloat16

D_MODEL = 1024
DEPTH = 2
CHUNK = 64
EPS = 1e-6
NEG_INF = -1e30
ROPE_BASE = 10000.0
LOG2E = 1.4426950408889634
RET_HEADS, RET_DK, RET_DV = 4, 64, 64
RET_W = RET_HEADS * RET_DV
MLA_HEADS, MLA_NOPE, MLA_ROPE, MLA_V = 8, 64, 32, 64
MLA_QK = MLA_NOPE + MLA_ROPE
Q_LORA, KV_LORA = 256, 128
MLA_W = MLA_HEADS * MLA_V
BAND_HEADS, BAND_DH = 4, 64
BAND_W = BAND_HEADS * BAND_DH
BAND_PREV_CHUNKS = 8
BAND_PAST = BAND_PREV_CHUNKS * CHUNK
MAX_REL = 128
D_MIX = RET_W + MLA_W + BAND_W
SEG = (RET_HEADS * RET_DK, RET_HEADS * RET_DK, RET_W, RET_W, Q_LORA, KV_LORA, MLA_ROPE, MLA_W,
       BAND_W, BAND_W, BAND_W, BAND_W)

LANES = 128
HEAD_PAD = LANES
MLA_QW = MLA_HEADS * HEAD_PAD
MLA_VROWS = MLA_V + 16
MLA_VT = MLA_HEADS * MLA_VROWS
C_AQ, C_AK, C_AV, C_CQ, C_CKV, C_KR, C_BQ, C_BK, C_BV, C_G, C_END = (
    0, 256, 512, 768, 1024, 1152, 1280, 1536, 1792, 2048, 3072)
VMEM_LIMIT = 56 * 1024 * 1024
K1_SUB = 256


def _dot(a, b):
    return jnp.dot(a, b, preferred_element_type=F32)


def _dot_nt(a, b):
    return lax.dot_general(a, b, (((1,), (1,)), ((), ())), preferred_element_type=F32)


def _dot_tn(a, b):
    return lax.dot_general(a, b, (((0,), (0,)), ((), ())), preferred_element_type=F32)


def _group_sums(x, split, two_groups):
    lane = lax.broadcasted_iota(jnp.int32, (1, LANES), 1)
    lo = lane < split
    sums = []
    for b in range(x.shape[-1] // LANES):
        xb = x[:, b * LANES:(b + 1) * LANES]
        sq = xb * xb
        if two_groups:
            sums.append((jnp.sum(jnp.where(lo, sq, 0.0), axis=-1, keepdims=True),
                         jnp.sum(jnp.where(lo, 0.0, sq), axis=-1, keepdims=True)))
        else:
            sums.append((jnp.sum(sq, axis=-1, keepdims=True), None))
    return sums


def _group_normalize(x, sums, split, n_lo, n_hi):
    lane = lax.broadcasted_iota(jnp.int32, (1, LANES), 1)
    lo = lane < split
    outs = []
    for b, (s_lo, s_hi) in enumerate(sums):
        inv = lax.rsqrt(s_lo * (1.0 / n_lo) + EPS)
        if n_hi:
            inv = jnp.where(lo, inv, lax.rsqrt(s_hi * (1.0 / n_hi) + EPS))
        outs.append(x[:, b * LANES:(b + 1) * LANES] * inv)
    return jnp.concatenate(outs, axis=1)


def _group_rms_normed(x, split, n_lo, n_hi):
    return _group_normalize(x, _group_sums(x, split, n_hi > 0), split, n_lo, n_hi)


def _group64_inv_rms(x, bd, inv_n):
    sq = (x * x).astype(BF16)
    w = bd.shape[0]
    sums = jnp.concatenate([_dot(sq[:, c * w:(c + 1) * w], bd) for c in range(x.shape[-1] // w)], axis=1)
    return lax.rsqrt(sums * inv_n + EPS)


def _swap_halves(x, half):
    w = x.shape[-1]
    lane = lax.broadcasted_iota(jnp.int32, (1, w), 1)
    nxt = pltpu.roll(x, w - half, axis=1)
    prv = pltpu.roll(x, half, axis=1)
    return jnp.where((lane & half) == 0, nxt, prv)


def _rope(x, cos, sin_signed, half):
    return x * cos + _swap_halves(x, half) * sin_signed


def _silu(g):
    return g * (1.0 / (1.0 + jnp.exp(-g)))


def _full(shape):
    nd = len(shape)
    return pl.BlockSpec(shape, lambda *_: (0,) * nd)


def _params(sem):
    return pltpu.CompilerParams(dimension_semantics=sem, vmem_limit_bytes=VMEM_LIMIT)


def _k1_body(x_ref, ng_ref, win_ref, wuq_ref, qag_ref, kvag_ref, krg_ref, qgain_ref, bqg_ref, bkg_ref,
             cret_ref, sret_ref, cq_ref, sq_ref, ckr_ref, skr_ref, wuqr_ref, qgainr_ref, bd_ref, invnq_ref,
             rq_ref, rk_ref, rv_ref, gate_ref, qm_ref, ckv_ref, kr_ref, bq_ref, bk_ref, bv_ref,
             bks_ref, bvs_ref, *, sub):
    n_sub = x_ref.shape[0] // sub
    bd = bd_ref[...]

    def stage_a(r):
        x = x_ref[r * sub:(r + 1) * sub, :]
        h = x * lax.rsqrt(jnp.mean(x * x, axis=-1, keepdims=True) + EPS) * ng_ref[...]
        hb = h.astype(BF16)

        def seg(lo, hi):
            return _dot(hb, win_ref[:, lo:hi])

        z = {'cq': seg(C_CQ, C_CKV), 'bq': seg(C_BQ, C_BK), 'bk': seg(C_BK, C_BV)}
        cq = z['cq']
        cq = cq * lax.rsqrt(jnp.mean(cq * cq, axis=-1, keepdims=True) + EPS) * qag_ref[...]
        z['aq'], z['ak'] = seg(C_AQ, C_AK), seg(C_AK, C_AV)
        cqb = cq.astype(BF16)
        z['qf'] = _dot(cqb, wuq_ref[...])
        z['qf_sw'] = _dot(cqb, wuqr_ref[...])
        z['bq_inv'] = _group64_inv_rms(z['bq'], bd, 1.0 / BAND_DH)
        z['bk_inv'] = _group64_inv_rms(z['bk'], bd, 1.0 / BAND_DH)
        z['g'] = seg(C_G, C_END)
        z['q_inv'] = _group64_inv_rms(z['qf'], bd, invnq_ref[...])
        z['av'], z['ckv'] = seg(C_AV, C_CQ), seg(C_CKV, C_KR)
        z['kr'], z['bv'] = seg(C_KR, C_BQ), seg(C_BV, C_G)
        return z

    def stage_b(r, z):
        rows = slice(r * sub, (r + 1) * sub)
        bq_ref[rows, :] = (z['bq'] * z['bq_inv'] * bqg_ref[...]).astype(BF16)
        bk = z['bk'] * z['bk_inv'] * bkg_ref[...]
        bk_ref[rows, :] = bk.astype(BF16)
        bks_ref[rows, :] = bk
        bv_ref[rows, :] = z['bv'].astype(BF16)
        bvs_ref[rows, :] = z['bv']
        cret, sret = cret_ref[rows, :], sret_ref[rows, :]
        rq_ref[rows, :] = _rope(z['aq'], cret, sret, RET_DK // 2).astype(BF16)
        rk_ref[rows, :] = _rope(z['ak'], cret, sret, RET_DK // 2).astype(BF16)
        rv_ref[rows, :] = z['av'].astype(BF16)
        gate_ref[rows, :] = _silu(z['g']).astype(BF16)
        cq_t = jnp.concatenate([cq_ref[rows, :]] * MLA_HEADS, axis=1)
        sq_t = jnp.concatenate([sq_ref[rows, :]] * MLA_HEADS, axis=1)
        qm_ref[rows, :] = (z['qf'] * (z['q_inv'] * qgain_ref[...]) * cq_t
                           + z['qf_sw'] * (z['q_inv'] * qgainr_ref[...]) * sq_t).astype(BF16)
        ckv, kr = z['ckv'], z['kr']
        ckv_ref[rows, :] = ckv * lax.rsqrt(jnp.mean(ckv * ckv, axis=-1, keepdims=True) + EPS) * kvag_ref[...]
        kr = kr * lax.rsqrt(jnp.sum(kr * kr, axis=-1, keepdims=True) * (1.0 / MLA_ROPE) + EPS) * krg_ref[...]
        kr = _rope(kr, ckr_ref[rows, :], skr_ref[rows, :], MLA_ROPE // 2)
        kr_ref[rows, :] = kr[:, :MLA_ROPE]

    z = stage_a(0)
    for r in range(n_sub):
        z_next = stage_a(r + 1) if r + 1 < n_sub else None
        stage_b(r, z)
        z = z_next


def _k1(x2d, lw, tabs, consts, *, tm, rows_per_seq, n_keep):
    T = x2d.shape[0]
    nb = rows_per_seq // tm
    nkb = n_keep // tm
    n_seq = T // rows_per_seq

    row = lambda w: pl.BlockSpec((tm, w), lambda i: (i, 0))
    tab = lambda w: pl.BlockSpec((tm, w), lambda i: (i % nb, 0))
    keep = pl.BlockSpec((tm, BAND_W), lambda i: ((i // nb) * nkb + jnp.maximum(i % nb - (nb - nkb), 0), 0))

    ins = [
        (x2d, row(D_MODEL)), (lw['norm_g'], _full((1, D_MODEL))), (lw['w_in'], _full((D_MODEL, C_END))),
        (lw['w_uq'], _full((Q_LORA, MLA_QW))), (lw['qa_g'], _full((1, Q_LORA))),
        (lw['kva_g'], _full((1, KV_LORA))), (lw['kr_g'], _full((1, LANES))),
        (lw['q_gain'], _full((1, MLA_QW))), (lw['bq_gain'], _full((1, BAND_W))),
        (lw['bk_gain'], _full((1, BAND_W))),
        (tabs['cret'], tab(RET_W)), (tabs['sret'], tab(RET_W)), (tabs['cq'], tab(LANES)),
        (tabs['sq'], tab(LANES)), (tabs['ckr'], tab(LANES)), (tabs['skr'], tab(LANES)),
        (lw['w_uq_sw'], _full((Q_LORA, MLA_QW))), (lw['q_gain_sw'], _full((1, MLA_QW))),
        (consts['bd64'], _full((RET_W, RET_W))), (consts['invn_q'], _full((1, MLA_QW))),
    ]
    outs = [
        ((T, RET_W), BF16, row(RET_W)), ((T, RET_W), BF16, row(RET_W)), ((T, RET_W), BF16, row(RET_W)),
        ((T, D_MIX), BF16, row(D_MIX)), ((T, MLA_QW), BF16, row(MLA_QW)),
        ((T, KV_LORA), F32, row(KV_LORA)), ((T, MLA_ROPE), F32, row(MLA_ROPE)),
        ((T, BAND_W), BF16, row(BAND_W)), ((T, BAND_W), BF16, row(BAND_W)), ((T, BAND_W), BF16, row(BAND_W)),
        ((n_seq * n_keep, BAND_W), F32, keep), ((n_seq * n_keep, BAND_W), F32, keep),
    ]
    return pl.pallas_call(
        functools.partial(_k1_body, sub=min(tm, K1_SUB)),
        grid=(T // tm,),
        in_specs=[s for _, s in ins],
        out_specs=[s for _, _, s in outs],
        out_shape=[jax.ShapeDtypeStruct(sh, dt) for sh, dt, _ in outs],
        compiler_params=_params(("arbitrary",)),
        name="k1_proj",
    )(*[a for a, _ in ins])


def _to_heads_body(k_ref, v_ref, ko_ref, vo_ref):
    for hd in range(BAND_HEADS):
        lanes = slice(hd * BAND_DH, (hd + 1) * BAND_DH)
        ko_ref[:, hd, :] = k_ref[:, lanes]
        vo_ref[:, hd, :] = v_ref[:, lanes]


def _to_heads(k2d, v2d, *, tm):
    n = k2d.shape[0]
    row = pl.BlockSpec((tm, BAND_W), lambda i: (i, 0))
    head = pl.BlockSpec((tm, BAND_HEADS, BAND_DH), lambda i: (i, 0, 0))
    shape = jax.ShapeDtypeStruct((n, BAND_HEADS, BAND_DH), F32)
    return pl.pallas_call(
        _to_heads_body, grid=(n // tm,), in_specs=[row, row], out_specs=[head, head],
        out_shape=[shape, shape], compiler_params=_params(("arbitrary",)), name="band_state_heads",
    )(k2d, v2d)


def _kkv_body(ckv_ref, kr_ref, wk_ref, wv_ref, kgain_ref, *rest, transpose_v):
    if transpose_v:
        ones_ref, k_ref, v_ref = rest
    else:
        k_ref, v_ref = rest
    c = ckv_ref[...].astype(BF16)
    kn = _dot(c, wk_ref[...])
    kn = _group_rms_normed(kn, LANES, MLA_NOPE, 0) * kgain_ref[...]
    kr = kr_ref[...]
    rows = kr.shape[0]
    kr_block = jnp.concatenate([jnp.zeros((rows, MLA_NOPE), F32), kr,
                                jnp.zeros((rows, HEAD_PAD - MLA_QK), F32)], axis=1)
    k_ref[...] = (kn + jnp.concatenate([kr_block] * MLA_HEADS, axis=1)).astype(BF16)
    if transpose_v:
        v_ref[...] = (_dot_nt(wv_ref[...], c) + ones_ref[...]).astype(BF16)
    else:
        v_ref[...] = _dot(c, wv_ref[...]).astype(BF16)


def _kkv(ckv2d, kr2d, lw, *, tm, seq=None):
    T = ckv2d.shape[0]
    row = lambda w: pl.BlockSpec((tm, w), lambda i: (i, 0))
    extra_in = []
    if seq is None:
        wv, wv_spec = lw['w_uv'], _full((KV_LORA, MLA_W))
        v_spec, v_shape = row(MLA_W), (T, MLA_W)
    else:
        nb = seq // tm
        wv, wv_spec = lw['w_uv_t'], _full((MLA_VT, KV_LORA))
        v_spec = pl.BlockSpec((None, MLA_VT, tm), lambda i: (i // nb, 0, i % nb))
        v_shape = (T // seq, MLA_VT, seq)
        ones = np.zeros((MLA_HEADS, MLA_VROWS, 1), np.float32)
        ones[:, MLA_V] = 1.0
        extra_in = [(jnp.asarray(ones.reshape(MLA_VT, 1)), _full((MLA_VT, 1)))]
    return pl.pallas_call(
        functools.partial(_kkv_body, transpose_v=seq is not None),
        grid=(T // tm,),
        in_specs=[row(KV_LORA), row(MLA_ROPE), _full((KV_LORA, MLA_QW)), wv_spec, _full((1, MLA_QW))]
        + [sp for _, sp in extra_in],
        out_specs=[row(MLA_QW), v_spec],
        out_shape=[jax.ShapeDtypeStruct((T, MLA_QW), BF16), jax.ShapeDtypeStruct(v_shape, BF16)],
        compiler_params=_params(("arbitrary",)),
        name="kkv_up",
    )(ckv2d, kr2d, lw['w_uk'], wv, lw['k_gain'], *[a for a, _ in extra_in])


def _mla_prompt_body(q_ref, k_ref, vt_ref, o_ref, m_sc, acc_sc, *, tq):
    qi = pl.program_id(1)
    m_sc[...] = jnp.full(m_sc.shape, NEG_INF, F32)
    acc_sc[...] = jnp.zeros(acc_sc.shape, F32)

    def tile(kb, masked):
        start = pl.multiple_of(kb * tq, tq)
        if masked:
            kc = lax.broadcasted_iota(jnp.int32, (tq, tq), 0) // CHUNK
            qc = lax.broadcasted_iota(jnp.int32, (tq, tq), 1) // CHUNK
            visible = kc <= qc

        def scores(h):
            q = q_ref[:, h * HEAD_PAD:(h + 1) * HEAD_PAD]
            k = k_ref[pl.ds(start, tq), h * HEAD_PAD:(h + 1) * HEAD_PAD]
            st = _dot_nt(k, q)
            return jnp.where(visible, st, NEG_INF) if masked else st

        def softmax(h, st):
            m_prev = m_sc[h:h + 1, :]
            m_new = jnp.maximum(m_prev, jnp.max(st, axis=0, keepdims=True))
            alpha = jnp.exp2(m_prev - m_new)
            p = jnp.exp2(st - m_new)
            m_sc[h:h + 1, :] = m_new
            return p.astype(BF16), alpha

        def values(h, p, alpha):
            vt = vt_ref[h * MLA_VROWS:(h + 1) * MLA_VROWS, pl.ds(start, tq)]
            acc_sc[h] = alpha * acc_sc[h] + _dot(vt, p)

        st = {0: scores(0), 1: scores(1)}
        pa = {}
        for h in range(MLA_HEADS):
            pa[h] = softmax(h, st.pop(h))
            if h >= 1:
                values(h - 1, *pa.pop(h - 1))
            if h + 2 < MLA_HEADS:
                st[h + 2] = scores(h + 2)
        values(MLA_HEADS - 1, *pa.pop(MLA_HEADS - 1))

    def body(kb, carry):
        tile(kb, False)
        return carry

    lax.fori_loop(0, qi, body, 0)
    tile(qi, True)
    for j in range(MLA_HEADS // 2):
        pair = jnp.concatenate([acc_sc[h, :MLA_V, :] * (1.0 / acc_sc[h, MLA_V:MLA_V + 1, :])
                                for h in (2 * j, 2 * j + 1)], axis=0)
        o_ref[:, j * LANES:(j + 1) * LANES] = pair.T.astype(BF16)


def _mla_prompt(q, k, vt, *, tq):
    B, S, _ = q.shape
    once = pl.Buffered(1)
    return pl.pallas_call(
        functools.partial(_mla_prompt_body, tq=tq),
        grid=(B, S // tq),
        in_specs=[pl.BlockSpec((None, tq, MLA_QW), lambda b, i: (b, i, 0)),
                  pl.BlockSpec((None, S, MLA_QW), lambda b, i: (b, 0, 0), pipeline_mode=once),
                  pl.BlockSpec((None, MLA_VT, S), lambda b, i: (b, 0, 0), pipeline_mode=once)],
        out_specs=pl.BlockSpec((None, tq, MLA_W), lambda b, i: (b, i, 0)),
        out_shape=jax.ShapeDtypeStruct((B, S, MLA_W), BF16),
        scratch_shapes=[pltpu.VMEM((MLA_HEADS, tq), F32), pltpu.VMEM((MLA_HEADS, MLA_VROWS, tq), F32)],
        compiler_params=_params(("arbitrary", "arbitrary")),
        name="mla_prompt",
    )(q, k, vt)


def _mla_sample_body(q_ref, kp_ref, vp_ref, kn_ref, vn_ref, o_ref):
    lane = lax.broadcasted_iota(jnp.int32, (1, LANES), 1)
    outs = []
    scores = []
    for h in range(MLA_HEADS):
        hs = slice(h * HEAD_PAD, (h + 1) * HEAD_PAD)
        q = q_ref[:, hs]
        scores.append((_dot_nt(q, kp_ref[:, hs]), _dot_nt(q, kn_ref[:, hs])))
    for h in range(MLA_HEADS):
        ps = slice((h // 2) * LANES, (h // 2 + 1) * LANES)
        v_lanes = (lane // MLA_V) == (h % 2)
        s1, s2 = scores[h]
        m = jnp.maximum(jnp.max(s1, axis=-1, keepdims=True), jnp.max(s2, axis=-1, keepdims=True))
        p1 = jnp.exp2(s1 - m)
        p2 = jnp.exp2(s2 - m)
        l = jnp.sum(p1, axis=-1, keepdims=True) + jnp.sum(p2, axis=-1, keepdims=True)
        v1 = vp_ref[:, ps]
        v2 = vn_ref[:, ps]
        acc = (_dot(p1.astype(BF16), jnp.where(v_lanes, v1, jnp.zeros_like(v1)))
               + _dot(p2.astype(BF16), jnp.where(v_lanes, v2, jnp.zeros_like(v2))))
        outs.append(acc * (1.0 / l))
    for j in range(MLA_HEADS // 2):
        o_ref[:, j * LANES:(j + 1) * LANES] = (outs[2 * j] + outs[2 * j + 1]).astype(BF16)


def _band_scores(q, pieces):
    lane = lax.broadcasted_iota(jnp.int32, (1, BAND_W), 1)
    raw = []
    for h in range(BAND_HEADS):
        qh = jnp.where((lane // BAND_DH) == h, q, jnp.zeros_like(q))
        raw.append([_dot_nt(qh, k) for k, _ in pieces])
    return raw


def _band_head_out(h, raw_h, pieces, biases):
    lane = lax.broadcasted_iota(jnp.int32, (1, BAND_W), 1)
    sel = (lane // BAND_DH) == h
    ss = [s + b_ref[h] for s, b_ref in zip(raw_h, biases)]
    m = functools.reduce(jnp.maximum, [jnp.max(s, axis=-1, keepdims=True) for s in ss])
    ps = [jnp.exp2(s - m) for s in ss]
    l = functools.reduce(jnp.add, [jnp.sum(p, axis=-1, keepdims=True) for p in ps])
    acc = functools.reduce(jnp.add, [
        _dot(p.astype(BF16), jnp.where(sel, v, jnp.zeros_like(v))) for p, (_, v) in zip(ps, pieces)])
    return acc * (1.0 / l)


def _band_heads(q, pieces, biases, out_dtype):
    raw = _band_scores(q, pieces)
    out = functools.reduce(jnp.add, [_band_head_out(h, raw[h], pieces, biases) for h in range(BAND_HEADS)])
    return out.astype(out_dtype)


def _band_ret_prompt_body(q_ref, kp_ref, kc_ref, vp_ref, vc_ref, bp_ref, bc_ref,
                          rq_ref, rk_ref, rv_ref, d_ref, xi_ref, zeta_ref, gam_ref, bd_ref, gain_ref,
                          o_ref, ro_ref, sfin_ref, s_sc, *, c):
    @pl.when(pl.program_id(1) == 0)
    def _():
        s_sc[...] = jnp.zeros(s_sc.shape, F32)

    pieces = [(kp_ref[...], vp_ref[...]), (kc_ref[...], vc_ref[...])]
    biases = [bp_ref, bc_ref]
    raw = _band_scores(q_ref[...], pieces)
    n_chunks = q_ref.shape[0] // c
    heads_per_chunk = BAND_HEADS // n_chunks
    state = s_sc[...]
    out = jnp.zeros(o_ref.shape, F32)
    for j in range(n_chunks):
        rows = slice(j * c, (j + 1) * c)
        o, state = _ret_chunk(rq_ref[rows, :], rk_ref[rows, :], rv_ref[rows, :], state,
                              d_ref, xi_ref, zeta_ref, gam_ref, bd_ref, gain_ref)
        ro_ref[rows, :] = o
        for h in range(j * heads_per_chunk, (j + 1) * heads_per_chunk):
            out = out + _band_head_out(h, raw[h], pieces, biases)
    o_ref[...] = out.astype(o_ref.dtype)
    s_sc[...] = state
    sfin_ref[...] = state


def _band_ret_prompt(q, k, v, bias_prev, bias_cur, rq, rk, rv, rt, consts, gain, *, tq, c):
    B, S, _ = q.shape
    cur = pl.BlockSpec((None, tq, BAND_W), lambda b, i: (b, i, 0))
    prev = pl.BlockSpec((None, tq, BAND_W), lambda b, i: (b, jnp.maximum(i - 1, 0), 0))
    bias = _full((BAND_HEADS, tq, tq))
    bias_prev_spec = pl.BlockSpec((None, BAND_HEADS, tq, tq), lambda b, i: (jnp.minimum(i, 1), 0, 0, 0))
    st = pl.BlockSpec((None, RET_W, RET_W), lambda b, i: (b, 0, 0))
    return pl.pallas_call(
        functools.partial(_band_ret_prompt_body, c=c),
        grid=(B, S // tq),
        in_specs=[cur, prev, cur, prev, cur, bias_prev_spec, bias, cur, cur, cur,
                  _full((RET_HEADS, c, c)), _full((c, RET_W)), _full((c, RET_W)),
                  _full((RET_W, RET_W)), _full((RET_W, RET_W)), _full((1, RET_W))],
        out_specs=[cur, cur, st],
        out_shape=[jax.ShapeDtypeStruct((B, S, BAND_W), BF16), jax.ShapeDtypeStruct((B, S, RET_W), BF16),
                   jax.ShapeDtypeStruct((B, RET_W, RET_W), F32)],
        scratch_shapes=[pltpu.VMEM((RET_W, RET_W), F32)],
        compiler_params=_params(("arbitrary", "arbitrary")),
        name="band_ret_prompt",
    )(q, k, k, v, v, bias_prev, bias_cur, rq, rk, rv,
      rt['decay'], rt['xi'], rt['zeta'], rt['gamma'], consts['bd_mask'], gain)


def _band_sample_body(q_ref, kp_ref, vp_ref, kn_ref, vn_ref, bp_ref, bn_ref, o_ref):
    o_ref[...] = _band_heads(q_ref[...],
                             [(kp_ref[...].astype(BF16), vp_ref[...].astype(BF16)), (kn_ref[...], vn_ref[...])],
                             [bp_ref, bn_ref], BF16)


def _ret_chunk(q, k, v, state, d_ref, xi_ref, zeta_ref, gam_ref, bd_ref, gain_ref):
    lane = lax.broadcasted_iota(jnp.int32, (1, RET_W), 1)
    o = _dot(q, state.astype(BF16)) * xi_ref[...]
    for h in range(RET_HEADS):
        sel = (lane // RET_DV) == h
        a = _dot_nt(jnp.where(sel, q, jnp.zeros_like(q)), k) * d_ref[h]
        o = o + _dot(a.astype(BF16), jnp.where(sel, v, jnp.zeros_like(v)))
    kz = (k.astype(F32) * zeta_ref[...]).astype(BF16)
    s_new = gam_ref[...] * state + bd_ref[...] * _dot_tn(kz, v)
    inv = _group64_inv_rms(o, bd_ref[...].astype(BF16), 1.0 / RET_DV)
    return (o * inv * gain_ref[...]).astype(BF16), s_new


def _sample_mixers_body(mq_ref, mkp_ref, mvp_ref, mkn_ref, mvn_ref,
                        bq_ref, bkp_ref, bvp_ref, bkn_ref, bvn_ref, bp_ref, bn_ref,
                        rq_ref, rk_ref, rv_ref, s0_ref, d_ref, xi_ref, zeta_ref, gam_ref, bd_ref, gain_ref,
                        mo_ref, bo_ref, ro_ref, sfin_ref):
    _mla_sample_body(mq_ref, mkp_ref, mvp_ref, mkn_ref, mvn_ref, mo_ref)
    _band_sample_body(bq_ref, bkp_ref, bvp_ref, bkn_ref, bvn_ref, bp_ref, bn_ref, bo_ref)
    s0 = s0_ref[...]
    state = jnp.concatenate([
        jnp.concatenate([s0[h] if g == h else jnp.zeros((RET_DK, RET_DV), F32) for g in range(RET_HEADS)], axis=1)
        for h in range(RET_HEADS)], axis=0)
    o, s_new = _ret_chunk(rq_ref[...], rk_ref[...], rv_ref[...], state,
                          d_ref, xi_ref, zeta_ref, gam_ref, bd_ref, gain_ref)
    ro_ref[...] = o
    for h in range(RET_HEADS):
        sfin_ref[h] = s_new[h * RET_DK:(h + 1) * RET_DK, h * RET_DV:(h + 1) * RET_DV]


def _sample_mixers(mq, mkp, mvp, mkn, mvn, bq, bkp, bvp, bkn, bvn, bias_past, bias_new,
                   rq, rk, rv, s0, rt, consts, gain, *, layer):
    B, L, _ = mq.shape
    P, PB = mkp.shape[1], bkp.shape[1]
    blk = lambda n, w: pl.BlockSpec((None, n, w), lambda b: (b, 0, 0))
    return pl.pallas_call(
        _sample_mixers_body,
        grid=(B,),
        in_specs=[blk(L, MLA_QW), blk(P, MLA_QW), blk(P, MLA_W), blk(L, MLA_QW), blk(L, MLA_W),
                  blk(L, BAND_W), blk(PB, BAND_W), blk(PB, BAND_W), blk(L, BAND_W), blk(L, BAND_W),
                  _full((BAND_HEADS, L, PB)), _full((BAND_HEADS, L, L)),
                  blk(L, RET_W), blk(L, RET_W), blk(L, RET_W),
                  pl.BlockSpec((None, None, RET_HEADS, RET_DK, RET_DV), lambda b: (layer, b, 0, 0, 0)),
                  _full((RET_HEADS, L, L)), _full((L, RET_W)), _full((L, RET_W)),
                  _full((RET_W, RET_W)), _full((RET_W, RET_W)), _full((1, RET_W))],
        out_specs=[blk(L, MLA_W), blk(L, BAND_W), blk(L, RET_W),
                   pl.BlockSpec((None, RET_HEADS, RET_DK, RET_DV), lambda b: (b, 0, 0, 0))],
        out_shape=[jax.ShapeDtypeStruct((B, L, MLA_W), BF16), jax.ShapeDtypeStruct((B, L, BAND_W), BF16),
                   jax.ShapeDtypeStruct((B, L, RET_W), BF16),
                   jax.ShapeDtypeStruct((B, RET_HEADS, RET_DK, RET_DV), F32)],
        compiler_params=_params(("arbitrary",)),
        name="sample_mixers",
    )(mq, mkp, mvp, mkn, mvn, bq, bkp, bvp, bkn, bvn, bias_past, bias_new,
      rq, rk, rv, s0, rt['decay'], rt['xi'], rt['zeta'], rt['gamma'], consts['bd_mask'], gain)


def _kout_body(x_ref, ro_ref, mo_ref, bo_ref, g_ref, w_ref, y_ref):
    g = g_ref[...]
    y = x_ref[...]
    y = y + _dot(ro_ref[...] * g[:, :RET_W], w_ref[:RET_W, :])
    y = y + _dot(mo_ref[...] * g[:, RET_W:RET_W + MLA_W], w_ref[RET_W:RET_W + MLA_W, :])
    y = y + _dot(bo_ref[...] * g[:, RET_W + MLA_W:], w_ref[RET_W + MLA_W:, :])
    y_ref[...] = y


def _kout(x2d, ro, mo, bo, gate, w_out, *, tm):
    T = x2d.shape[0]
    row = lambda w: pl.BlockSpec((tm, w), lambda i: (i, 0))
    return pl.pallas_call(
        _kout_body,
        grid=(T // tm,),
        in_specs=[row(D_MODEL), row(RET_W), row(MLA_W), row(BAND_W), row(D_MIX), _full((D_MIX, D_MODEL))],
        out_specs=row(D_MODEL),
        out_shape=jax.ShapeDtypeStruct((T, D_MODEL), F32),
        compiler_params=_params(("arbitrary",)),
        name="kout_proj",
    )(x2d, ro, mo, bo, gate, w_out)


def _constants():
    bd = np.kron(np.eye(RET_HEADS, dtype=np.float32), np.ones((RET_DK, RET_DV), np.float32))
    invn_q = np.tile(np.repeat([1.0 / MLA_NOPE, 1.0 / MLA_ROPE], HEAD_PAD // 2), MLA_HEADS)[None]
    return {'bd_mask': jnp.asarray(bd), 'bd64': jnp.asarray(bd, BF16), 'invn_q': jnp.asarray(invn_q, F32)}


def _rope_tables(pos, reps=1):
    pos = np.asarray(pos, np.float64)

    def cs(half):
        inv = ROPE_BASE ** (-np.arange(half, dtype=np.float64) / half)
        ang = pos[:, None] * inv[None, :]
        c, s = np.cos(ang), np.sin(ang)
        return np.concatenate([c, c], axis=-1), np.concatenate([-s, s], axis=-1)

    T = pos.shape[0]
    c32, s32 = cs(RET_DK // 2)
    c16, s16 = cs(MLA_ROPE // 2)
    ones = lambda w: np.ones((T, w))
    zeros = lambda w: np.zeros((T, w))
    pad = HEAD_PAD - MLA_QK
    tabs = {
        'cret': np.tile(c32, (1, RET_HEADS)), 'sret': np.tile(s32, (1, RET_HEADS)),
        'cq': np.concatenate([ones(MLA_NOPE), c16, ones(pad)], axis=-1),
        'sq': np.concatenate([zeros(MLA_NOPE), s16, zeros(pad)], axis=-1),
        'ckr': np.concatenate([c16, ones(LANES - MLA_ROPE)], axis=-1),
        'skr': np.concatenate([s16, zeros(LANES - MLA_ROPE)], axis=-1),
    }
    return {k: jnp.asarray(np.tile(v, (reps, 1)), F32) for k, v in tabs.items()}


def _retention_tables(c):
    lg = np.log1p(-np.exp2(-5.0 - np.arange(RET_HEADS, dtype=np.float64)))
    idx = np.arange(c, dtype=np.float64)
    diff = idx[:, None] - idx[None, :]
    decay = np.where(diff >= 0, np.exp(lg[:, None, None] * np.maximum(diff, 0.0)), 0.0)
    per_lane = lambda t: np.repeat(t, RET_DV, axis=-1)
    xi = per_lane(np.exp(lg[None, :] * (idx[:, None] + 1.0)))
    zeta = per_lane(np.exp(lg[None, :] * (c - 1.0 - idx)[:, None]))
    gamma = np.broadcast_to(per_lane(np.exp(lg * c)[None, :]).T, (RET_W, RET_W))
    return {k: jnp.asarray(v, F32) for k, v in
            {'decay': decay, 'xi': xi, 'zeta': zeta, 'gamma': gamma}.items()}


def _band_tables_body(rp_ref, rc_ref, prev_ref, cur_ref, past_ref, new_ref, *, tq, n_new):
    w = 2 * tq
    tp = pltpu.roll(jnp.broadcast_to(rp_ref[...], (tq, w)), 0, 1, stride=1, stride_axis=0)[:, :tq]
    tc = pltpu.roll(jnp.broadcast_to(rc_ref[...], (tq, w)), 0, 1, stride=1, stride_axis=0)[:, :tq]
    qc = lax.broadcasted_iota(jnp.int32, (tq, tq), 0) // CHUNK
    kc = lax.broadcasted_iota(jnp.int32, (tq, tq), 1) // CHUNK
    prev_ref[0] = jnp.full((tq, tq), NEG_INF, F32)
    prev_ref[1] = jnp.where(kc >= qc, tp, NEG_INF)
    cur_ref[...] = jnp.where(kc <= qc, tc, NEG_INF)
    past_ref[...] = tp[:n_new, :]
    new_ref[...] = tc[:n_new, :n_new]


def _band_tables(band_bias, *, tq, n_new):
    H = band_bias.shape[0]
    band_bias = band_bias * LOG2E
    lo, mid, hi = band_bias[:, :1], band_bias[:, MAX_REL + 1:2 * MAX_REL], band_bias[:, 2 * MAX_REL:]
    rep = lambda col, n: jnp.broadcast_to(col, (H, n))
    r_cur = jnp.concatenate([band_bias[:, MAX_REL::-1], rep(lo, tq - MAX_REL - 1), rep(hi, tq - MAX_REL),
                             band_bias[:, :MAX_REL:-1]], axis=1)
    r_prev = jnp.concatenate([rep(hi, tq - MAX_REL + 1), mid[:, ::-1], rep(hi, tq)], axis=1)
    row = pl.BlockSpec((None, 1, 2 * tq), lambda h: (h, 0, 0))
    tile = lambda n, m: pl.BlockSpec((None, n, m), lambda h: (h, 0, 0))
    return pl.pallas_call(
        functools.partial(_band_tables_body, tq=tq, n_new=n_new),
        grid=(H,),
        in_specs=[row, row],
        out_specs=[pl.BlockSpec((2, None, tq, tq), lambda h: (0, h, 0, 0)), tile(tq, tq), tile(n_new, tq),
                   tile(n_new, n_new)],
        out_shape=[jax.ShapeDtypeStruct((2, H, tq, tq), F32), jax.ShapeDtypeStruct((H, tq, tq), F32),
                   jax.ShapeDtypeStruct((H, n_new, tq), F32), jax.ShapeDtypeStruct((H, n_new, n_new), F32)],
        compiler_params=_params(("arbitrary",)),
        name="band_tables",
    )(r_prev[:, None, :], r_cur[:, None, :])


def _layer_weights(l, norm_g, w_in, ret_gn_g, mla_qa_g, mla_w_uq, mla_qn_g, mla_qr_g, mla_kva_g, mla_kr_g,
                   mla_w_ukv, mla_kn_g, band_qn_g, band_kn_g, w_out):
    cuts = np.cumsum(SEG)[:-1].tolist()
    a_q, a_k, a_v, a_g, b_cq, b_ckv, b_kr, b_g, c_q, c_k, c_v, c_g = jnp.split(w_in[l], cuts, axis=-1)
    b_kr = jnp.pad(b_kr, ((0, 0), (0, LANES - MLA_ROPE)))
    w_in_p = jnp.concatenate([a_q, a_k * (RET_DK ** -0.5), a_v, b_cq, b_ckv, b_kr, c_q, c_k, c_v, a_g, b_g, c_g],
                             axis=-1).astype(BF16)
    pad = HEAD_PAD - MLA_QK
    w_uq = jnp.pad(mla_w_uq[l].reshape(Q_LORA, MLA_HEADS, MLA_QK), ((0, 0), (0, 0), (0, pad)))
    half = MLA_ROPE // 2

    def swap_rope(t):
        return jnp.concatenate([t[..., :MLA_NOPE], t[..., MLA_NOPE + half:MLA_QK], t[..., MLA_NOPE:MLA_NOPE + half],
                                t[..., MLA_QK:]], axis=-1)
    ukv = mla_w_ukv[l].reshape(KV_LORA, MLA_HEADS, MLA_NOPE + MLA_V)
    w_uk = jnp.pad(ukv[:, :, :MLA_NOPE], ((0, 0), (0, 0), (0, HEAD_PAD - MLA_NOPE)))
    zpad = jnp.zeros((pad,), F32)
    q_gain_head = jnp.concatenate([mla_qn_g[l], mla_qr_g[l], zpad]) * (MLA_QK ** -0.5 * LOG2E)
    q_gain = jnp.tile(q_gain_head, MLA_HEADS)
    q_gain_sw = jnp.tile(swap_rope(q_gain_head), MLA_HEADS)
    k_gain = jnp.tile(jnp.concatenate([mla_kn_g[l], jnp.zeros((HEAD_PAD - MLA_NOPE,), F32)]), MLA_HEADS)
    return {
        'norm_g': norm_g[l][None], 'w_in': w_in_p,
        'w_uq': w_uq.reshape(Q_LORA, MLA_QW).astype(BF16), 'qa_g': mla_qa_g[l][None],
        'w_uq_sw': swap_rope(w_uq).reshape(Q_LORA, MLA_QW).astype(BF16), 'q_gain_sw': q_gain_sw[None],
        'kva_g': mla_kva_g[l][None],
        'kr_g': jnp.concatenate([mla_kr_g[l], jnp.zeros((LANES - MLA_ROPE,), F32)])[None],
        'q_gain': q_gain[None],
        'bq_gain': (jnp.tile(band_qn_g[l], BAND_HEADS) * (BAND_DH ** -0.5 * LOG2E))[None],
        'bk_gain': jnp.tile(band_kn_g[l], BAND_HEADS)[None],
        'w_uk': w_uk.reshape(KV_LORA, MLA_QW).astype(BF16),
        'w_uv': ukv[:, :, MLA_NOPE:].reshape(KV_LORA, MLA_W).astype(BF16),
        'w_uv_t': jnp.pad(ukv[:, :, MLA_NOPE:].transpose(1, 2, 0), ((0, 0), (0, MLA_VROWS - MLA_V), (0, 0))
                          ).reshape(MLA_VT, KV_LORA).astype(BF16),
        'k_gain': k_gain[None], 'ret_gain': ret_gn_g[l][None], 'w_out': w_out[l].astype(BF16),
    }


def _diag_blocks(s_bd):
    B = s_bd.shape[0]
    s = s_bd.reshape(B, RET_HEADS, RET_DK, RET_HEADS, RET_DV)
    return jnp.stack([s[:, h, :, h, :] for h in range(RET_HEADS)], axis=1)


def kernel(x_prompt, x_sample, state_ret, cache_mla_ckv, cache_mla_krope, cache_band_k, cache_band_v, norm_g, w_in, ret_gn_g, mla_qa_g, mla_w_uq, mla_qn_g, mla_qr_g, mla_kva_g, mla_kr_g, mla_w_ukv, mla_kn_g, band_qn_g, band_kn_g, band_bias, w_out):
    B, S, _ = x_prompt.shape
    DB, L, _ = x_sample.shape
    past_len = cache_mla_ckv.shape[2]
    n_band_past = cache_band_k.shape[2]
    n_keep_p = min(BAND_PAST, S)

    TM_P, TM_S, TM_KV, TQ, RET_C = 512, 256, 1024, 512, 256
    consts = _constants()
    tabs_p = _rope_tables(np.arange(S))
    tabs_s = _rope_tables(past_len + np.arange(L), reps=DB)
    rt_p = _retention_tables(RET_C)
    rt_s = _retention_tables(L)

    xp = x_prompt.reshape(B * S, D_MODEL)
    xs = x_sample.reshape(DB * L, D_MODEL)
    p_st, s_st = [], []
    for l in range(DEPTH):
        lw = _layer_weights(l, norm_g, w_in, ret_gn_g, mla_qa_g, mla_w_uq, mla_qn_g, mla_qr_g, mla_kva_g,
                            mla_kr_g, mla_w_ukv, mla_kn_g, band_qn_g, band_kn_g, w_out)

        rq, rk, rv, gate, qm, ckv, kr, bq, bk, bv, bks, bvs = _k1(
            xp, lw, tabs_p, consts, tm=TM_P, rows_per_seq=S, n_keep=n_keep_p)
        km, vt = _kkv(ckv, kr, lw, tm=TM_KV, seq=S)
        mla_o = _mla_prompt(qm.reshape(B, S, MLA_QW), km.reshape(B, S, MLA_QW), vt, tq=TQ)
        assert TQ == BAND_PAST == n_band_past
        bias_prev, bias_cur, bias_past, bias_new = _band_tables(band_bias[l], tq=TQ, n_new=L)
        band_o, ret_o, ret_s = _band_ret_prompt(
            bq.reshape(B, S, BAND_W), bk.reshape(B, S, BAND_W), bv.reshape(B, S, BAND_W), bias_prev, bias_cur,
            rq.reshape(B, S, RET_W), rk.reshape(B, S, RET_W), rv.reshape(B, S, RET_W),
            rt_p, consts, lw['ret_gain'], tq=TQ, c=RET_C)
        xp = _kout(xp, ret_o.reshape(B * S, RET_W), mla_o.reshape(B * S, MLA_W), band_o.reshape(B * S, BAND_W),
                   gate, lw['w_out'], tm=TM_P)
        bks, bvs = _to_heads(bks, bvs, tm=TM_S)
        p_st.append((_diag_blocks(ret_s), ckv.reshape(B, S, KV_LORA), kr.reshape(B, S, MLA_ROPE),
                     bks.reshape(B, n_keep_p, BAND_HEADS, BAND_DH), bvs.reshape(B, n_keep_p, BAND_HEADS, BAND_DH)))

        rq, rk, rv, gate, qm, ckv, kr, bq, bk, bv, bks, bvs = _k1(
            xs, lw, tabs_s, consts, tm=TM_S, rows_per_seq=DB * L, n_keep=DB * L)
        kn, vn = _kkv(ckv, kr, lw, tm=TM_S)
        kp, vp = _kkv(cache_mla_ckv[l].reshape(DB * past_len, KV_LORA),
                      cache_mla_krope[l].reshape(DB * past_len, MLA_ROPE), lw, tm=TM_KV)
        mla_o, band_o, ret_o, ret_s = _sample_mixers(
            qm.reshape(DB, L, MLA_QW), kp.reshape(DB, past_len, MLA_QW), vp.reshape(DB, past_len, MLA_W),
            kn.reshape(DB, L, MLA_QW), vn.reshape(DB, L, MLA_W),
            bq.reshape(DB, L, BAND_W), cache_band_k[l].reshape(DB, n_band_past, BAND_W),
            cache_band_v[l].reshape(DB, n_band_past, BAND_W), bk.reshape(DB, L, BAND_W), bv.reshape(DB, L, BAND_W),
            bias_past, bias_new,
            rq.reshape(DB, L, RET_W), rk.reshape(DB, L, RET_W), rv.reshape(DB, L, RET_W),
            state_ret, rt_s, consts, lw['ret_gain'], layer=l)
        xs = _kout(xs, ret_o.reshape(DB * L, RET_W), mla_o.reshape(DB * L, MLA_W), band_o.reshape(DB * L, BAND_W),
                   gate, lw['w_out'], tm=TM_S)
        bks, bvs = _to_heads(bks, bvs, tm=TM_S)
        s_st.append((ret_s, ckv.reshape(DB, L, KV_LORA), kr.reshape(DB, L, MLA_ROPE),
                     bks.reshape(DB, L, BAND_HEADS, BAND_DH), bvs.reshape(DB, L, BAND_HEADS, BAND_DH)))

    stack = lambda sts, i: jnp.stack([s[i] for s in sts])
    return (xp.reshape(B, S, D_MODEL), xs.reshape(DB, L, D_MODEL),
            stack(p_st, 0), stack(p_st, 1), stack(p_st, 2), stack(p_st, 3), stack(p_st, 4),
            stack(s_st, 0), stack(s_st, 1), stack(s_st, 2), stack(s_st, 3), stack(s_st, 4))
```

```python
import functools

import jax
import jax.numpy as jnp
import numpy as np
from jax import lax
from jax.experimental import pallas as pl
from jax.experimental.pallas import tpu as pltpu

F32 = jnp.float32
BF16 = jnp.bfloat16

D_MODEL = 1024
DEPTH = 2
CHUNK = 64
EPS = 1e-6
NEG_INF = -1e30
ROPE_BASE = 10000.0
LOG2E = 1.4426950408889634
RET_HEADS, RET_DK, RET_DV = 4, 64, 64
RET_W = RET_HEADS * RET_DV
MLA_HEADS, MLA_NOPE, MLA_ROPE, MLA_V = 8, 64, 32, 64
MLA_QK = MLA_NOPE + MLA_ROPE
Q_LORA, KV_LORA = 256, 128
MLA_W = MLA_HEADS * MLA_V
BAND_HEADS, BAND_DH = 4, 64
BAND_W = BAND_HEADS * BAND_DH
BAND_PREV_CHUNKS = 8
BAND_PAST = BAND_PREV_CHUNKS * CHUNK
MAX_REL = 128
D_MIX = RET_W + MLA_W + BAND_W
SEG = (RET_HEADS * RET_DK, RET_HEADS * RET_DK, RET_W, RET_W, Q_LORA, KV_LORA, MLA_ROPE, MLA_W,
       BAND_W, BAND_W, BAND_W, BAND_W)

LANES = 128
HEAD_PAD = LANES
MLA_QW = MLA_HEADS * HEAD_PAD
MLA_VROWS = MLA_V + 16
MLA_VT = MLA_HEADS * MLA_VROWS
C_AQ, C_AK, C_AV, C_CQ, C_CKV, C_KR, C_BQ, C_BK, C_BV, C_G, C_END = (
    0, 256, 512, 768, 1024, 1152, 1280, 1536, 1792, 2048, 3072)
VMEM_LIMIT = 56 * 1024 * 1024
K1_SUB = 256


def _dot(a, b):
    return jnp.dot(a, b, preferred_element_type=F32)


def _dot_nt(a, b):
    return lax.dot_general(a, b, (((1,), (1,)), ((), ())), preferred_element_type=F32)


def _dot_tn(a, b):
    return lax.dot_general(a, b, (((0,), (0,)), ((), ())), preferred_element_type=F32)


def _group_sums(x, split, two_groups):
    lane = lax.broadcasted_iota(jnp.int32, (1, LANES), 1)
    lo = lane < split
    sums = []
    for b in range(x.shape[-1] // LANES):
        xb = x[:, b * LANES:(b + 1) * LANES]
        sq = xb * xb
        if two_groups:
            sums.append((jnp.sum(jnp.where(lo, sq, 0.0), axis=-1, keepdims=True),
                         jnp.sum(jnp.where(lo, 0.0, sq), axis=-1, keepdims=True)))
        else:
            sums.append((jnp.sum(sq, axis=-1, keepdims=True), None))
    return sums


def _group_normalize(x, sums, split, n_lo, n_hi):
    lane = lax.broadcasted_iota(jnp.int32, (1, LANES), 1)
    lo = lane < split
    outs = []
    for b, (s_lo, s_hi) in enumerate(sums):
        inv = lax.rsqrt(s_lo * (1.0 / n_lo) + EPS)
        if n_hi:
            inv = jnp.where(lo, inv, lax.rsqrt(s_hi * (1.0 / n_hi) + EPS))
        outs.append(x[:, b * LANES:(b + 1) * LANES] * inv)
    return jnp.concatenate(outs, axis=1)


def _group_rms_normed(x, split, n_lo, n_hi):
    return _group_normalize(x, _group_sums(x, split, n_hi > 0), split, n_lo, n_hi)


def _group64_inv_rms(x, bd, inv_n):
    sq = (x * x).astype(BF16)
    w = bd.shape[0]
    sums = jnp.concatenate([_dot(sq[:, c * w:(c + 1) * w], bd) for c in range(x.shape[-1] // w)], axis=1)
    return lax.rsqrt(sums * inv_n + EPS)


def _swap_halves(x, half):
    w = x.shape[-1]
    lane = lax.broadcasted_iota(jnp.int32, (1, w), 1)
    nxt = pltpu.roll(x, w - half, axis=1)
    prv = pltpu.roll(x, half, axis=1)
    return jnp.where((lane & half) == 0, nxt, prv)


def _rope(x, cos, sin_signed, half):
    return x * cos + _swap_halves(x, half) * sin_signed


def _silu(g):
    return g * (1.0 / (1.0 + jnp.exp(-g)))


def _full(shape):
    nd = len(shape)
    return pl.BlockSpec(shape, lambda *_: (0,) * nd)


def _params(sem):
    return pltpu.CompilerParams(dimension_semantics=sem, vmem_limit_bytes=VMEM_LIMIT)


def _k1_body(x_ref, ng_ref, win_ref, wuq_ref, qag_ref, kvag_ref, krg_ref, qgain_ref, bqg_ref, bkg_ref,
             cret_ref, sret_ref, cq_ref, sq_ref, ckr_ref, skr_ref, wuqr_ref, qgainr_ref, bd_ref, invnq_ref,
             rq_ref, rk_ref, rv_ref, gate_ref, qm_ref, ckv_ref, kr_ref, bq_ref, bk_ref, bv_ref,
             bks_ref, bvs_ref, *, sub, q_transposed):
    n_sub = x_ref.shape[0] // sub
    bd = bd_ref[...]
    lane_reps = sub // LANES

    def stage_a(r):
        x = x_ref[r * sub:(r + 1) * sub, :]
        h = x * lax.rsqrt(jnp.mean(x * x, axis=-1, keepdims=True) + EPS) * ng_ref[...]
        hb = h.astype(BF16)

        def seg(lo, hi):
            return _dot(hb, win_ref[:, lo:hi])

        z = {'cq': seg(C_CQ, C_CKV), 'bq': seg(C_BQ, C_BK), 'bk': seg(C_BK, C_BV)}
        cq = z['cq']
        cq = cq * lax.rsqrt(jnp.mean(cq * cq, axis=-1, keepdims=True) + EPS) * qag_ref[...]
        z['aq'], z['ak'] = seg(C_AQ, C_AK), seg(C_AK, C_AV)
        cqb = cq.astype(BF16)
        if q_transposed:
            z['qf'] = _dot_nt(wuq_ref[...], cqb)
            z['qf_sw'] = _dot_nt(wuqr_ref[...], cqb)
        else:
            z['qf'] = _dot(cqb, wuq_ref[...])
            z['qf_sw'] = _dot(cqb, wuqr_ref[...])
        z['bq_inv'] = _group64_inv_rms(z['bq'], bd, 1.0 / BAND_DH)
        z['bk_inv'] = _group64_inv_rms(z['bk'], bd, 1.0 / BAND_DH)
        z['g'] = seg(C_G, C_END)
        if q_transposed:
            sq = (z['qf'] * z['qf']).astype(BF16)
            sums = jnp.concatenate([_dot(bd, sq[c * RET_W:(c + 1) * RET_W, :]) for c in range(MLA_QW // RET_W)],
                                   axis=0)
            z['q_inv'] = lax.rsqrt(sums * jnp.concatenate([invnq_ref[...]] * lane_reps, axis=1) + EPS)
        else:
            z['q_inv'] = _group64_inv_rms(z['qf'], bd, invnq_ref[...])
        z['av'], z['ckv'] = seg(C_AV, C_CQ), seg(C_CKV, C_KR)
        z['kr'], z['bv'] = seg(C_KR, C_BQ), seg(C_BV, C_G)
        return z

    def stage_b(r, z):
        rows = slice(r * sub, (r + 1) * sub)
        bq_ref[rows, :] = (z['bq'] * z['bq_inv'] * bqg_ref[...]).astype(BF16)
        bk = z['bk'] * z['bk_inv'] * bkg_ref[...]
        bk_ref[rows, :] = bk.astype(BF16)
        bks_ref[rows, :] = bk
        bv_ref[rows, :] = z['bv'].astype(BF16)
        bvs_ref[rows, :] = z['bv']
        cret, sret = cret_ref[rows, :], sret_ref[rows, :]
        rq_ref[rows, :] = _rope(z['aq'], cret, sret, RET_DK // 2).astype(BF16)
        rk_ref[rows, :] = _rope(z['ak'], cret, sret, RET_DK // 2).astype(BF16)
        rv_ref[rows, :] = z['av'].astype(BF16)
        gate_ref[rows, :] = _silu(z['g']).astype(BF16)
        if q_transposed:
            cq_t = jnp.concatenate([cq_ref[:, rows]] * MLA_HEADS, axis=0)
            sq_t = jnp.concatenate([sq_ref[:, rows]] * MLA_HEADS, axis=0)
            gain = jnp.concatenate([qgain_ref[...]] * lane_reps, axis=1)
            gain_sw = jnp.concatenate([qgainr_ref[...]] * lane_reps, axis=1)
            qm_ref[:, rows] = (z['qf'] * (z['q_inv'] * gain) * cq_t
                               + z['qf_sw'] * (z['q_inv'] * gain_sw) * sq_t).astype(BF16)
        else:
            cq_t = jnp.concatenate([cq_ref[rows, :]] * MLA_HEADS, axis=1)
            sq_t = jnp.concatenate([sq_ref[rows, :]] * MLA_HEADS, axis=1)
            qm_ref[rows, :] = (z['qf'] * (z['q_inv'] * qgain_ref[...]) * cq_t
                               + z['qf_sw'] * (z['q_inv'] * qgainr_ref[...]) * sq_t).astype(BF16)
        ckv, kr = z['ckv'], z['kr']
        ckv_ref[rows, :] = ckv * lax.rsqrt(jnp.mean(ckv * ckv, axis=-1, keepdims=True) + EPS) * kvag_ref[...]
        kr = kr * lax.rsqrt(jnp.sum(kr * kr, axis=-1, keepdims=True) * (1.0 / MLA_ROPE) + EPS) * krg_ref[...]
        kr = _rope(kr, ckr_ref[rows, :], skr_ref[rows, :], MLA_ROPE // 2)
        kr_ref[rows, :] = kr[:, :MLA_ROPE]

    z = stage_a(0)
    for r in range(n_sub):
        z_next = stage_a(r + 1) if r + 1 < n_sub else None
        stage_b(r, z)
        z = z_next


def _k1(x2d, lw, tabs, consts, *, tm, rows_per_seq, n_keep, q_transposed=False):
    T = x2d.shape[0]
    nb = rows_per_seq // tm
    nkb = n_keep // tm
    n_seq = T // rows_per_seq

    row = lambda w: pl.BlockSpec((tm, w), lambda i: (i, 0))
    tab = lambda w: pl.BlockSpec((tm, w), lambda i: (i % nb, 0))
    keep = pl.BlockSpec((tm, BAND_W), lambda i: ((i // nb) * nkb + jnp.maximum(i % nb - (nb - nkb), 0), 0))

    if q_transposed:
        tab_t = pl.BlockSpec((LANES, tm), lambda i: (0, i % nb))
        col = _full((MLA_QW, LANES))
        q_ins = {'w_uq': (lw['w_uq_t'], _full((MLA_QW, Q_LORA))), 'w_uq_sw': (lw['w_uq_sw_t'], _full((MLA_QW, Q_LORA))),
                 'q_gain': (lw['q_gain_col'], col), 'q_gain_sw': (lw['q_gain_sw_col'], col),
                 'invn_q': (consts['invn_q_col'], col), 'cq': (tabs['cq_t'], tab_t), 'sq': (tabs['sq_t'], tab_t)}
        q_out = ((n_seq, MLA_QW, rows_per_seq), BF16,
                 pl.BlockSpec((None, MLA_QW, tm), lambda i: (i // nb, 0, i % nb)))
    else:
        q_ins = {'w_uq': (lw['w_uq'], _full((Q_LORA, MLA_QW))), 'w_uq_sw': (lw['w_uq_sw'], _full((Q_LORA, MLA_QW))),
                 'q_gain': (lw['q_gain'], _full((1, MLA_QW))), 'q_gain_sw': (lw['q_gain_sw'], _full((1, MLA_QW))),
                 'invn_q': (consts['invn_q'], _full((1, MLA_QW))), 'cq': (tabs['cq'], tab(LANES)),
                 'sq': (tabs['sq'], tab(LANES))}
        q_out = ((T, MLA_QW), BF16, row(MLA_QW))
    ins = [
        (x2d, row(D_MODEL)), (lw['norm_g'], _full((1, D_MODEL))), (lw['w_in'], _full((D_MODEL, C_END))),
        q_ins['w_uq'], (lw['qa_g'], _full((1, Q_LORA))),
        (lw['kva_g'], _full((1, KV_LORA))), (lw['kr_g'], _full((1, LANES))),
        q_ins['q_gain'], (lw['bq_gain'], _full((1, BAND_W))),
        (lw['bk_gain'], _full((1, BAND_W))),
        (tabs['cret'], tab(RET_W)), (tabs['sret'], tab(RET_W)), q_ins['cq'],
        q_ins['sq'], (tabs['ckr'], tab(LANES)), (tabs['skr'], tab(LANES)),
        q_ins['w_uq_sw'], q_ins['q_gain_sw'],
        (consts['bd64'], _full((RET_W, RET_W))), q_ins['invn_q'],
    ]
    outs = [
        ((T, RET_W), BF16, row(RET_W)), ((T, RET_W), BF16, row(RET_W)), ((T, RET_W), BF16, row(RET_W)),
        ((T, D_MIX), BF16, row(D_MIX)), q_out,
        ((T, KV_LORA), F32, row(KV_LORA)), ((T, MLA_ROPE), F32, row(MLA_ROPE)),
        ((T, BAND_W), BF16, row(BAND_W)), ((T, BAND_W), BF16, row(BAND_W)), ((T, BAND_W), BF16, row(BAND_W)),
        ((n_seq * n_keep, BAND_W), F32, keep), ((n_seq * n_keep, BAND_W), F32, keep),
    ]
    return pl.pallas_call(
        functools.partial(_k1_body, sub=min(tm, K1_SUB), q_transposed=q_transposed),
        grid=(T // tm,),
        in_specs=[s for _, s in ins],
        out_specs=[s for _, _, s in outs],
        out_shape=[jax.ShapeDtypeStruct(sh, dt) for sh, dt, _ in outs],
        compiler_params=_params(("arbitrary",)),
        name="k1_proj",
    )(*[a for a, _ in ins])


def _to_heads_body(k_ref, v_ref, ko_ref, vo_ref):
    for hd in range(BAND_HEADS):
        lanes = slice(hd * BAND_DH, (hd + 1) * BAND_DH)
        ko_ref[:, hd, :] = k_ref[:, lanes]
        vo_ref[:, hd, :] = v_ref[:, lanes]


def _to_heads(k2d, v2d, *, tm):
    n = k2d.shape[0]
    row = pl.BlockSpec((tm, BAND_W), lambda i: (i, 0))
    head = pl.BlockSpec((tm, BAND_HEADS, BAND_DH), lambda i: (i, 0, 0))
    shape = jax.ShapeDtypeStruct((n, BAND_HEADS, BAND_DH), F32)
    return pl.pallas_call(
        _to_heads_body, grid=(n // tm,), in_specs=[row, row], out_specs=[head, head],
        out_shape=[shape, shape], compiler_params=_params(("arbitrary",)), name="band_state_heads",
    )(k2d, v2d)


def _kkv_body(ckv_ref, kr_ref, wk_ref, wv_ref, kgain_ref, *rest, transpose_v):
    if transpose_v:
        ones_ref, k_ref, v_ref = rest
    else:
        k_ref, v_ref = rest
    c = ckv_ref[...].astype(BF16)
    kn = _dot(c, wk_ref[...])
    kn = _group_rms_normed(kn, LANES, MLA_NOPE, 0) * kgain_ref[...]
    kr = kr_ref[...]
    rows = kr.shape[0]
    kr_block = jnp.concatenate([jnp.zeros((rows, MLA_NOPE), F32), kr,
                                jnp.zeros((rows, HEAD_PAD - MLA_QK), F32)], axis=1)
    k_ref[...] = (kn + jnp.concatenate([kr_block] * MLA_HEADS, axis=1)).astype(BF16)
    if transpose_v:
        v_ref[...] = (_dot_nt(wv_ref[...], c) + ones_ref[...]).astype(BF16)
    else:
        v_ref[...] = _dot(c, wv_ref[...]).astype(BF16)


def _kkv(ckv2d, kr2d, lw, *, tm, seq=None):
    T = ckv2d.shape[0]
    row = lambda w: pl.BlockSpec((tm, w), lambda i: (i, 0))
    extra_in = []
    if seq is None:
        wv, wv_spec = lw['w_uv'], _full((KV_LORA, MLA_W))
        v_spec, v_shape = row(MLA_W), (T, MLA_W)
    else:
        nb = seq // tm
        wv, wv_spec = lw['w_uv_t'], _full((MLA_VT, KV_LORA))
        v_spec = pl.BlockSpec((None, MLA_VT, tm), lambda i: (i // nb, 0, i % nb))
        v_shape = (T // seq, MLA_VT, seq)
        ones = np.zeros((MLA_HEADS, MLA_VROWS, 1), np.float32)
        ones[:, MLA_V] = 1.0
        extra_in = [(jnp.asarray(ones.reshape(MLA_VT, 1)), _full((MLA_VT, 1)))]
    return pl.pallas_call(
        functools.partial(_kkv_body, transpose_v=seq is not None),
        grid=(T // tm,),
        in_specs=[row(KV_LORA), row(MLA_ROPE), _full((KV_LORA, MLA_QW)), wv_spec, _full((1, MLA_QW))]
        + [sp for _, sp in extra_in],
        out_specs=[row(MLA_QW), v_spec],
        out_shape=[jax.ShapeDtypeStruct((T, MLA_QW), BF16), jax.ShapeDtypeStruct(v_shape, BF16)],
        compiler_params=_params(("arbitrary",)),
        name="kkv_up",
    )(ckv2d, kr2d, lw['w_uk'], wv, lw['k_gain'], *[a for a, _ in extra_in])


def _mla_prompt_body(q_ref, k_ref, vt_ref, o_ref, m_sc, acc_sc, *, tq):
    qi = pl.program_id(1)
    m_sc[...] = jnp.full(m_sc.shape, NEG_INF, F32)
    acc_sc[...] = jnp.zeros(acc_sc.shape, F32)

    def tile(kb, masked):
        start = pl.multiple_of(kb * tq, tq)
        if masked:
            kc = lax.broadcasted_iota(jnp.int32, (tq, tq), 0) // CHUNK
            qc = lax.broadcasted_iota(jnp.int32, (tq, tq), 1) // CHUNK
            visible = kc <= qc

        def scores(h):
            qt = q_ref[h * HEAD_PAD:(h + 1) * HEAD_PAD, :]
            k = k_ref[pl.ds(start, tq), h * HEAD_PAD:(h + 1) * HEAD_PAD]
            st = _dot(k, qt)
            return jnp.where(visible, st, NEG_INF) if masked else st

        def softmax(h, st):
            m_prev = m_sc[h:h + 1, :]
            m_new = jnp.maximum(m_prev, jnp.max(st, axis=0, keepdims=True))
            alpha = jnp.exp2(m_prev - m_new)
            p = jnp.exp2(st - m_new)
            m_sc[h:h + 1, :] = m_new
            return p.astype(BF16), alpha

        def values(h, p, alpha):
            vt = vt_ref[h * MLA_VROWS:(h + 1) * MLA_VROWS, pl.ds(start, tq)]
            acc_sc[h] = alpha * acc_sc[h] + _dot(vt, p)

        st = {0: scores(0), 1: scores(1)}
        pa = {}
        for h in range(MLA_HEADS):
            pa[h] = softmax(h, st.pop(h))
            if h >= 1:
                values(h - 1, *pa.pop(h - 1))
            if h + 2 < MLA_HEADS:
                st[h + 2] = scores(h + 2)
        values(MLA_HEADS - 1, *pa.pop(MLA_HEADS - 1))

    def body(kb, carry):
        tile(kb, False)
        return carry

    lax.fori_loop(0, qi, body, 0)
    tile(qi, True)
    for j in range(MLA_HEADS // 2):
        pair = jnp.concatenate([acc_sc[h, :MLA_V, :] * (1.0 / acc_sc[h, MLA_V:MLA_V + 1, :])
                                for h in (2 * j, 2 * j + 1)], axis=0)
        o_ref[:, j * LANES:(j + 1) * LANES] = pair.T.astype(BF16)


def _mla_prompt(qt, k, vt, *, tq):
    B, S, _ = k.shape
    once = pl.Buffered(1)
    return pl.pallas_call(
        functools.partial(_mla_prompt_body, tq=tq),
        grid=(B, S // tq),
        in_specs=[pl.BlockSpec((None, MLA_QW, tq), lambda b, i: (b, 0, i)),
                  pl.BlockSpec((None, S, MLA_QW), lambda b, i: (b, 0, 0), pipeline_mode=once),
                  pl.BlockSpec((None, MLA_VT, S), lambda b, i: (b, 0, 0), pipeline_mode=once)],
        out_specs=pl.BlockSpec((None, tq, MLA_W), lambda b, i: (b, i, 0)),
        out_shape=jax.ShapeDtypeStruct((B, S, MLA_W), BF16),
        scratch_shapes=[pltpu.VMEM((MLA_HEADS, tq), F32), pltpu.VMEM((MLA_HEADS, MLA_VROWS, tq), F32)],
        compiler_params=_params(("arbitrary", "arbitrary")),
        name="mla_prompt",
    )(qt, k, vt)


def _mla_sample_body(q_ref, kp_ref, vp_ref, kn_ref, vn_ref, o_ref):
    lane = lax.broadcasted_iota(jnp.int32, (1, LANES), 1)
    outs = []
    scores = []
    for h in range(MLA_HEADS):
        hs = slice(h * HEAD_PAD, (h + 1) * HEAD_PAD)
        q = q_ref[:, hs]
        scores.append((_dot_nt(q, kp_ref[:, hs]), _dot_nt(q, kn_ref[:, hs])))
    for h in range(MLA_HEADS):
        ps = slice((h // 2) * LANES, (h // 2 + 1) * LANES)
        v_lanes = (lane // MLA_V) == (h % 2)
        s1, s2 = scores[h]
        m = jnp.maximum(jnp.max(s1, axis=-1, keepdims=True), jnp.max(s2, axis=-1, keepdims=True))
        p1 = jnp.exp2(s1 - m)
        p2 = jnp.exp2(s2 - m)
        l = jnp.sum(p1, axis=-1, keepdims=True) + jnp.sum(p2, axis=-1, keepdims=True)
        v1 = vp_ref[:, ps]
        v2 = vn_ref[:, ps]
        acc = (_dot(p1.astype(BF16), jnp.where(v_lanes, v1, jnp.zeros_like(v1)))
               + _dot(p2.astype(BF16), jnp.where(v_lanes, v2, jnp.zeros_like(v2))))
        outs.append(acc * (1.0 / l))
    for j in range(MLA_HEADS // 2):
        o_ref[:, j * LANES:(j + 1) * LANES] = (outs[2 * j] + outs[2 * j + 1]).astype(BF16)


def _band_scores(q, pieces):
    lane = lax.broadcasted_iota(jnp.int32, (1, BAND_W), 1)
    raw = []
    for h in range(BAND_HEADS):
        qh = jnp.where((lane // BAND_DH) == h, q, jnp.zeros_like(q))
        raw.append([_dot_nt(qh, k) for k, _ in pieces])
    return raw


def _band_head_out(h, raw_h, pieces, biases):
    lane = lax.broadcasted_iota(jnp.int32, (1, BAND_W), 1)
    sel = (lane // BAND_DH) == h
    ss = [s + b_ref[h] for s, b_ref in zip(raw_h, biases)]
    m = functools.reduce(jnp.maximum, [jnp.max(s, axis=-1, keepdims=True) for s in ss])
    ps = [jnp.exp2(s - m) for s in ss]
    l = functools.reduce(jnp.add, [jnp.sum(p, axis=-1, keepdims=True) for p in ps])
    acc = functools.reduce(jnp.add, [
        _dot(p.astype(BF16), jnp.where(sel, v, jnp.zeros_like(v))) for p, (_, v) in zip(ps, pieces)])
    return acc * (1.0 / l)


def _band_heads(q, pieces, biases, out_dtype):
    raw = _band_scores(q, pieces)
    out = functools.reduce(jnp.add, [_band_head_out(h, raw[h], pieces, biases) for h in range(BAND_HEADS)])
    return out.astype(out_dtype)


def _band_ret_prompt_body(q_ref, kp_ref, kc_ref, vp_ref, vc_ref, bp_ref, bc_ref,
                          rq_ref, rk_ref, rv_ref, d_ref, xi_ref, zeta_ref, gam_ref, bd_ref, gain_ref,
                          o_ref, ro_ref, sfin_ref, s_sc, *, c):
    @pl.when(pl.program_id(1) == 0)
    def _():
        s_sc[...] = jnp.zeros(s_sc.shape, F32)

    pieces = [(kp_ref[...], vp_ref[...]), (kc_ref[...], vc_ref[...])]
    biases = [bp_ref, bc_ref]
    raw = _band_scores(q_ref[...], pieces)
    n_chunks = q_ref.shape[0] // c
    heads_per_chunk = BAND_HEADS // n_chunks
    state = s_sc[...]
    out = jnp.zeros(o_ref.shape, F32)
    for j in range(n_chunks):
        rows = slice(j * c, (j + 1) * c)
        o, state = _ret_chunk(rq_ref[rows, :], rk_ref[rows, :], rv_ref[rows, :], state,
                              d_ref, xi_ref, zeta_ref, gam_ref, bd_ref, gain_ref)
        ro_ref[rows, :] = o
        for h in range(j * heads_per_chunk, (j + 1) * heads_per_chunk):
            out = out + _band_head_out(h, raw[h], pieces, biases)
    o_ref[...] = out.astype(o_ref.dtype)
    s_sc[...] = state
    sfin_ref[...] = state


def _band_ret_prompt(q, k, v, bias_prev, bias_cur, rq, rk, rv, rt, consts, gain, *, tq, c):
    B, S, _ = q.shape
    cur = pl.BlockSpec((None, tq, BAND_W), lambda b, i: (b, i, 0))
    prev = pl.BlockSpec((None, tq, BAND_W), lambda b, i: (b, jnp.maximum(i - 1, 0), 0))
    bias = _full((BAND_HEADS, tq, tq))
    bias_prev_spec = pl.BlockSpec((None, BAND_HEADS, tq, tq), lambda b, i: (jnp.minimum(i, 1), 0, 0, 0))
    st = pl.BlockSpec((None, RET_W, RET_W), lambda b, i: (b, 0, 0))
    return pl.pallas_call(
        functools.partial(_band_ret_prompt_body, c=c),
        grid=(B, S // tq),
        in_specs=[cur, prev, cur, prev, cur, bias_prev_spec, bias, cur, cur, cur,
                  _full((RET_HEADS, c, c)), _full((c, RET_W)), _full((c, RET_W)),
                  _full((RET_W, RET_W)), _full((RET_W, RET_W)), _full((1, RET_W))],
        out_specs=[cur, cur, st],
        out_shape=[jax.ShapeDtypeStruct((B, S, BAND_W), BF16), jax.ShapeDtypeStruct((B, S, RET_W), BF16),
                   jax.ShapeDtypeStruct((B, RET_W, RET_W), F32)],
        scratch_shapes=[pltpu.VMEM((RET_W, RET_W), F32)],
        compiler_params=_params(("arbitrary", "arbitrary")),
        name="band_ret_prompt",
    )(q, k, k, v, v, bias_prev, bias_cur, rq, rk, rv,
      rt['decay'], rt['xi'], rt['zeta'], rt['gamma'], consts['bd_mask'], gain)


def _band_sample_body(q_ref, kp_ref, vp_ref, kn_ref, vn_ref, bp_ref, bn_ref, o_ref):
    o_ref[...] = _band_heads(q_ref[...],
                             [(kp_ref[...].astype(BF16), vp_ref[...].astype(BF16)), (kn_ref[...], vn_ref[...])],
                             [bp_ref, bn_ref], BF16)


def _ret_chunk(q, k, v, state, d_ref, xi_ref, zeta_ref, gam_ref, bd_ref, gain_ref):
    lane = lax.broadcasted_iota(jnp.int32, (1, RET_W), 1)
    o = _dot(q, state.astype(BF16)) * xi_ref[...]
    for h in range(RET_HEADS):
        sel = (lane // RET_DV) == h
        a = _dot_nt(jnp.where(sel, q, jnp.zeros_like(q)), k) * d_ref[h]
        o = o + _dot(a.astype(BF16), jnp.where(sel, v, jnp.zeros_like(v)))
    kz = (k.astype(F32) * zeta_ref[...]).astype(BF16)
    s_new = gam_ref[...] * state + bd_ref[...] * _dot_tn(kz, v)
    inv = _group64_inv_rms(o, bd_ref[...].astype(BF16), 1.0 / RET_DV)
    return (o * inv * gain_ref[...]).astype(BF16), s_new


def _sample_mixers_body(mq_ref, mkp_ref, mvp_ref, mkn_ref, mvn_ref,
                        bq_ref, bkp_ref, bvp_ref, bkn_ref, bvn_ref, bp_ref, bn_ref,
                        rq_ref, rk_ref, rv_ref, s0_ref, d_ref, xi_ref, zeta_ref, gam_ref, bd_ref, gain_ref,
                        mo_ref, bo_ref, ro_ref, sfin_ref):
    _mla_sample_body(mq_ref, mkp_ref, mvp_ref, mkn_ref, mvn_ref, mo_ref)
    _band_sample_body(bq_ref, bkp_ref, bvp_ref, bkn_ref, bvn_ref, bp_ref, bn_ref, bo_ref)
    s0 = s0_ref[...]
    state = jnp.concatenate([
        jnp.concatenate([s0[h] if g == h else jnp.zeros((RET_DK, RET_DV), F32) for g in range(RET_HEADS)], axis=1)
        for h in range(RET_HEADS)], axis=0)
    o, s_new = _ret_chunk(rq_ref[...], rk_ref[...], rv_ref[...], state,
                          d_ref, xi_ref, zeta_ref, gam_ref, bd_ref, gain_ref)
    ro_ref[...] = o
    for h in range(RET_HEADS):
        sfin_ref[h] = s_new[h * RET_DK:(h + 1) * RET_DK, h * RET_DV:(h + 1) * RET_DV]


def _sample_mixers(mq, mkp, mvp, mkn, mvn, bq, bkp, bvp, bkn, bvn, bias_past, bias_new,
                   rq, rk, rv, s0, rt, consts, gain, *, layer):
    B, L, _ = mq.shape
    P, PB = mkp.shape[1], bkp.shape[1]
    blk = lambda n, w: pl.BlockSpec((None, n, w), lambda b: (b, 0, 0))
    return pl.pallas_call(
        _sample_mixers_body,
        grid=(B,),
        in_specs=[blk(L, MLA_QW), blk(P, MLA_QW), blk(P, MLA_W), blk(L, MLA_QW), blk(L, MLA_W),
                  blk(L, BAND_W), blk(PB, BAND_W), blk(PB, BAND_W), blk(L, BAND_W), blk(L, BAND_W),
                  _full((BAND_HEADS, L, PB)), _full((BAND_HEADS, L, L)),
                  blk(L, RET_W), blk(L, RET_W), blk(L, RET_W),
                  pl.BlockSpec((None, None, RET_HEADS, RET_DK, RET_DV), lambda b: (layer, b, 0, 0, 0)),
                  _full((RET_HEADS, L, L)), _full((L, RET_W)), _full((L, RET_W)),
                  _full((RET_W, RET_W)), _full((RET_W, RET_W)), _full((1, RET_W))],
        out_specs=[blk(L, MLA_W), blk(L, BAND_W), blk(L, RET_W),
                   pl.BlockSpec((None, RET_HEADS, RET_DK, RET_DV), lambda b: (b, 0, 0, 0))],
        out_shape=[jax.ShapeDtypeStruct((B, L, MLA_W), BF16), jax.ShapeDtypeStruct((B, L, BAND_W), BF16),
                   jax.ShapeDtypeStruct((B, L, RET_W), BF16),
                   jax.ShapeDtypeStruct((B, RET_HEADS, RET_DK, RET_DV), F32)],
        compiler_params=_params(("arbitrary",)),
        name="sample_mixers",
    )(mq, mkp, mvp, mkn, mvn, bq, bkp, bvp, bkn, bvn, bias_past, bias_new,
      rq, rk, rv, s0, rt['decay'], rt['xi'], rt['zeta'], rt['gamma'], consts['bd_mask'], gain)


def _kout_body(x_ref, ro_ref, mo_ref, bo_ref, g_ref, w_ref, y_ref):
    g = g_ref[...]
    y = x_ref[...]
    y = y + _dot(ro_ref[...] * g[:, :RET_W], w_ref[:RET_W, :])
    y = y + _dot(mo_ref[...] * g[:, RET_W:RET_W + MLA_W], w_ref[RET_W:RET_W + MLA_W, :])
    y = y + _dot(bo_ref[...] * g[:, RET_W + MLA_W:], w_ref[RET_W + MLA_W:, :])
    y_ref[...] = y


def _kout(x2d, ro, mo, bo, gate, w_out, *, tm):
    T = x2d.shape[0]
    row = lambda w: pl.BlockSpec((tm, w), lambda i: (i, 0))
    return pl.pallas_call(
        _kout_body,
        grid=(T // tm,),
        in_specs=[row(D_MODEL), row(RET_W), row(MLA_W), row(BAND_W), row(D_MIX), _full((D_MIX, D_MODEL))],
        out_specs=row(D_MODEL),
        out_shape=jax.ShapeDtypeStruct((T, D_MODEL), F32),
        compiler_params=_params(("arbitrary",)),
        name="kout_proj",
    )(x2d, ro, mo, bo, gate, w_out)


def _constants():
    bd = np.kron(np.eye(RET_HEADS, dtype=np.float32), np.ones((RET_DK, RET_DV), np.float32))
    invn_q = np.tile(np.repeat([1.0 / MLA_NOPE, 1.0 / MLA_ROPE], HEAD_PAD // 2), MLA_HEADS)[None]
    invn_q_col = np.broadcast_to(invn_q.T, (MLA_QW, LANES))
    return {'bd_mask': jnp.asarray(bd), 'bd64': jnp.asarray(bd, BF16), 'invn_q': jnp.asarray(invn_q, F32),
            'invn_q_col': jnp.asarray(invn_q_col, F32)}


def _rope_tables(pos, reps=1):
    pos = np.asarray(pos, np.float64)

    def cs(half):
        inv = ROPE_BASE ** (-np.arange(half, dtype=np.float64) / half)
        ang = pos[:, None] * inv[None, :]
        c, s = np.cos(ang), np.sin(ang)
        return np.concatenate([c, c], axis=-1), np.concatenate([-s, s], axis=-1)

    T = pos.shape[0]
    c32, s32 = cs(RET_DK // 2)
    c16, s16 = cs(MLA_ROPE // 2)
    ones = lambda w: np.ones((T, w))
    zeros = lambda w: np.zeros((T, w))
    pad = HEAD_PAD - MLA_QK
    tabs = {
        'cret': np.tile(c32, (1, RET_HEADS)), 'sret': np.tile(s32, (1, RET_HEADS)),
        'cq': np.concatenate([ones(MLA_NOPE), c16, ones(pad)], axis=-1),
        'sq': np.concatenate([zeros(MLA_NOPE), s16, zeros(pad)], axis=-1),
        'ckr': np.concatenate([c16, ones(LANES - MLA_ROPE)], axis=-1),
        'skr': np.concatenate([s16, zeros(LANES - MLA_ROPE)], axis=-1),
    }
    out = {k: jnp.asarray(np.tile(v, (reps, 1)), F32) for k, v in tabs.items()}
    if reps == 1:
        out['cq_t'], out['sq_t'] = jnp.asarray(tabs['cq'].T, F32), jnp.asarray(tabs['sq'].T, F32)
    return out


def _retention_tables(c):
    lg = np.log1p(-np.exp2(-5.0 - np.arange(RET_HEADS, dtype=np.float64)))
    idx = np.arange(c, dtype=np.float64)
    diff = idx[:, None] - idx[None, :]
    decay = np.where(diff >= 0, np.exp(lg[:, None, None] * np.maximum(diff, 0.0)), 0.0)
    per_lane = lambda t: np.repeat(t, RET_DV, axis=-1)
    xi = per_lane(np.exp(lg[None, :] * (idx[:, None] + 1.0)))
    zeta = per_lane(np.exp(lg[None, :] * (c - 1.0 - idx)[:, None]))
    gamma = np.broadcast_to(per_lane(np.exp(lg * c)[None, :]).T, (RET_W, RET_W))
    return {k: jnp.asarray(v, F32) for k, v in
            {'decay': decay, 'xi': xi, 'zeta': zeta, 'gamma': gamma}.items()}


def _band_tables_body(rp_ref, rc_ref, prev_ref, cur_ref, past_ref, new_ref, *, tq, n_new):
    w = 2 * tq
    tp = pltpu.roll(jnp.broadcast_to(rp_ref[...], (tq, w)), 0, 1, stride=1, stride_axis=0)[:, :tq]
    tc = pltpu.roll(jnp.broadcast_to(rc_ref[...], (tq, w)), 0, 1, stride=1, stride_axis=0)[:, :tq]
    qc = lax.broadcasted_iota(jnp.int32, (tq, tq), 0) // CHUNK
    kc = lax.broadcasted_iota(jnp.int32, (tq, tq), 1) // CHUNK
    prev_ref[0] = jnp.full((tq, tq), NEG_INF, F32)
    prev_ref[1] = jnp.where(kc >= qc, tp, NEG_INF)
    cur_ref[...] = jnp.where(kc <= qc, tc, NEG_INF)
    past_ref[...] = tp[:n_new, :]
    new_ref[...] = tc[:n_new, :n_new]


def _band_tables(band_bias, *, tq, n_new):
    H = band_bias.shape[0]
    band_bias = band_bias * LOG2E
    lo, mid, hi = band_bias[:, :1], band_bias[:, MAX_REL + 1:2 * MAX_REL], band_bias[:, 2 * MAX_REL:]
    rep = lambda col, n: jnp.broadcast_to(col, (H, n))
    r_cur = jnp.concatenate([band_bias[:, MAX_REL::-1], rep(lo, tq - MAX_REL - 1), rep(hi, tq - MAX_REL),
                             band_bias[:, :MAX_REL:-1]], axis=1)
    r_prev = jnp.concatenate([rep(hi, tq - MAX_REL + 1), mid[:, ::-1], rep(hi, tq)], axis=1)
    row = pl.BlockSpec((None, 1, 2 * tq), lambda h: (h, 0, 0))
    tile = lambda n, m: pl.BlockSpec((None, n, m), lambda h: (h, 0, 0))
    return pl.pallas_call(
        functools.partial(_band_tables_body, tq=tq, n_new=n_new),
        grid=(H,),
        in_specs=[row, row],
        out_specs=[pl.BlockSpec((2, None, tq, tq), lambda h: (0, h, 0, 0)), tile(tq, tq), tile(n_new, tq),
                   tile(n_new, n_new)],
        out_shape=[jax.ShapeDtypeStruct((2, H, tq, tq), F32), jax.ShapeDtypeStruct((H, tq, tq), F32),
                   jax.ShapeDtypeStruct((H, n_new, tq), F32), jax.ShapeDtypeStruct((H, n_new, n_new), F32)],
        compiler_params=_params(("arbitrary",)),
        name="band_tables",
    )(r_prev[:, None, :], r_cur[:, None, :])


def _layer_weights(l, norm_g, w_in, ret_gn_g, mla_qa_g, mla_w_uq, mla_qn_g, mla_qr_g, mla_kva_g, mla_kr_g,
                   mla_w_ukv, mla_kn_g, band_qn_g, band_kn_g, w_out):
    cuts = np.cumsum(SEG)[:-1].tolist()
    a_q, a_k, a_v, a_g, b_cq, b_ckv, b_kr, b_g, c_q, c_k, c_v, c_g = jnp.split(w_in[l], cuts, axis=-1)
    b_kr = jnp.pad(b_kr, ((0, 0), (0, LANES - MLA_ROPE)))
    w_in_p = jnp.concatenate([a_q, a_k * (RET_DK ** -0.5), a_v, b_cq, b_ckv, b_kr, c_q, c_k, c_v, a_g, b_g, c_g],
                             axis=-1).astype(BF16)
    pad = HEAD_PAD - MLA_QK
    w_uq = jnp.pad(mla_w_uq[l].reshape(Q_LORA, MLA_HEADS, MLA_QK), ((0, 0), (0, 0), (0, pad)))
    half = MLA_ROPE // 2

    def swap_rope(t):
        return jnp.concatenate([t[..., :MLA_NOPE], t[..., MLA_NOPE + half:MLA_QK], t[..., MLA_NOPE:MLA_NOPE + half],
                                t[..., MLA_QK:]], axis=-1)
    ukv = mla_w_ukv[l].reshape(KV_LORA, MLA_HEADS, MLA_NOPE + MLA_V)
    w_uk = jnp.pad(ukv[:, :, :MLA_NOPE], ((0, 0), (0, 0), (0, HEAD_PAD - MLA_NOPE)))
    zpad = jnp.zeros((pad,), F32)
    q_gain_head = jnp.concatenate([mla_qn_g[l], mla_qr_g[l], zpad]) * (MLA_QK ** -0.5 * LOG2E)
    q_gain = jnp.tile(q_gain_head, MLA_HEADS)
    q_gain_sw = jnp.tile(swap_rope(q_gain_head), MLA_HEADS)
    k_gain = jnp.tile(jnp.concatenate([mla_kn_g[l], jnp.zeros((HEAD_PAD - MLA_NOPE,), F32)]), MLA_HEADS)
    return {
        'norm_g': norm_g[l][None], 'w_in': w_in_p,
        'w_uq': w_uq.reshape(Q_LORA, MLA_QW).astype(BF16), 'qa_g': mla_qa_g[l][None],
        'w_uq_sw': swap_rope(w_uq).reshape(Q_LORA, MLA_QW).astype(BF16), 'q_gain_sw': q_gain_sw[None],
        'w_uq_t': w_uq.reshape(Q_LORA, MLA_QW).T.astype(BF16),
        'w_uq_sw_t': swap_rope(w_uq).reshape(Q_LORA, MLA_QW).T.astype(BF16),
        'q_gain_col': jnp.broadcast_to(q_gain[:, None], (MLA_QW, LANES)),
        'q_gain_sw_col': jnp.broadcast_to(q_gain_sw[:, None], (MLA_QW, LANES)),
        'kva_g': mla_kva_g[l][None],
        'kr_g': jnp.concatenate([mla_kr_g[l], jnp.zeros((LANES - MLA_ROPE,), F32)])[None],
        'q_gain': q_gain[None],
        'bq_gain': (jnp.tile(band_qn_g[l], BAND_HEADS) * (BAND_DH ** -0.5 * LOG2E))[None],
        'bk_gain': jnp.tile(band_kn_g[l], BAND_HEADS)[None],
        'w_uk': w_uk.reshape(KV_LORA, MLA_QW).astype(BF16),
        'w_uv': ukv[:, :, MLA_NOPE:].reshape(KV_LORA, MLA_W).astype(BF16),
        'w_uv_t': jnp.pad(ukv[:, :, MLA_NOPE:].transpose(1, 2, 0), ((0, 0), (0, MLA_VROWS - MLA_V), (0, 0))
                          ).reshape(MLA_VT, KV_LORA).astype(BF16),
        'k_gain': k_gain[None], 'ret_gain': ret_gn_g[l][None], 'w_out': w_out[l].astype(BF16),
    }


def _diag_blocks(s_bd):
    B = s_bd.shape[0]
    s = s_bd.reshape(B, RET_HEADS, RET_DK, RET_HEADS, RET_DV)
    return jnp.stack([s[:, h, :, h, :] for h in range(RET_HEADS)], axis=1)


def kernel(x_prompt, x_sample, state_ret, cache_mla_ckv, cache_mla_krope, cache_band_k, cache_band_v, norm_g, w_in, ret_gn_g, mla_qa_g, mla_w_uq, mla_qn_g, mla_qr_g, mla_kva_g, mla_kr_g, mla_w_ukv, mla_kn_g, band_qn_g, band_kn_g, band_bias, w_out):
    B, S, _ = x_prompt.shape
    DB, L, _ = x_sample.shape
    past_len = cache_mla_ckv.shape[2]
    n_band_past = cache_band_k.shape[2]
    n_keep_p = min(BAND_PAST, S)

    TM_P, TM_S, TM_KV, TQ, RET_C = 512, 256, 1024, 512, 256
    consts = _constants()
    tabs_p = _rope_tables(np.arange(S))
    tabs_s = _rope_tables(past_len + np.arange(L), reps=DB)
    rt_p = _retention_tables(RET_C)
    rt_s = _retention_tables(L)

    xp = x_prompt.reshape(B * S, D_MODEL)
    xs = x_sample.reshape(DB * L, D_MODEL)
    p_st, s_st = [], []
    for l in range(DEPTH):
        lw = _layer_weights(l, norm_g, w_in, ret_gn_g, mla_qa_g, mla_w_uq, mla_qn_g, mla_qr_g, mla_kva_g,
                            mla_kr_g, mla_w_ukv, mla_kn_g, band_qn_g, band_kn_g, w_out)

        rq, rk, rv, gate, qm, ckv, kr, bq, bk, bv, bks, bvs = _k1(
            xp, lw, tabs_p, consts, tm=TM_P, rows_per_seq=S, n_keep=n_keep_p, q_transposed=True)
        km, vt = _kkv(ckv, kr, lw, tm=TM_KV, seq=S)
        mla_o = _mla_prompt(qm, km.reshape(B, S, MLA_QW), vt, tq=TQ)
        assert TQ == BAND_PAST == n_band_past
        bias_prev, bias_cur, bias_past, bias_new = _band_tables(band_bias[l], tq=TQ, n_new=L)
        band_o, ret_o, ret_s = _band_ret_prompt(
            bq.reshape(B, S, BAND_W), bk.reshape(B, S, BAND_W), bv.reshape(B, S, BAND_W), bias_prev, bias_cur,
            rq.reshape(B, S, RET_W), rk.reshape(B, S, RET_W), rv.reshape(B, S, RET_W),
            rt_p, consts, lw['ret_gain'], tq=TQ, c=RET_C)
        xp = _kout(xp, ret_o.reshape(B * S, RET_W), mla_o.reshape(B * S, MLA_W), band_o.reshape(B * S, BAND_W),
                   gate, lw['w_out'], tm=TM_P)
        bks, bvs = _to_heads(bks, bvs, tm=TM_S)
        p_st.append((_diag_blocks(ret_s), ckv.reshape(B, S, KV_LORA), kr.reshape(B, S, MLA_ROPE),
                     bks.reshape(B, n_keep_p, BAND_HEADS, BAND_DH), bvs.reshape(B, n_keep_p, BAND_HEADS, BAND_DH)))

        rq, rk, rv, gate, qm, ckv, kr, bq, bk, bv, bks, bvs = _k1(
            xs, lw, tabs_s, consts, tm=TM_S, rows_per_seq=DB * L, n_keep=DB * L)
        kn, vn = _kkv(ckv, kr, lw, tm=TM_S)
        kp, vp = _kkv(cache_mla_ckv[l].reshape(DB * past_len, KV_LORA),
                      cache_mla_krope[l].reshape(DB * past_len, MLA_ROPE), lw, tm=TM_KV)
        mla_o, band_o, ret_o, ret_s = _sample_mixers(
            qm.reshape(DB, L, MLA_QW), kp.reshape(DB, past_len, MLA_QW), vp.reshape(DB, past_len, MLA_W),
            kn.reshape(DB, L, MLA_QW), vn.reshape(DB, L, MLA_W),
            bq.reshape(DB, L, BAND_W), cache_band_k[l].reshape(DB, n_band_past, BAND_W),
            cache_band_v[l].reshape(DB, n_band_past, BAND_W), bk.reshape(DB, L, BAND_W), bv.reshape(DB, L, BAND_W),
            bias_past, bias_new,
            rq.reshape(DB, L, RET_W), rk.reshape(DB, L, RET_W), rv.reshape(DB, L, RET_W),
            state_ret, rt_s, consts, lw['ret_gain'], layer=l)
        xs = _kout(xs, ret_o.reshape(DB * L, RET_W), mla_o.reshape(DB * L, MLA_W), band_o.reshape(DB * L, BAND_W),
                   gate, lw['w_out'], tm=TM_S)
        bks, bvs = _to_heads(bks, bvs, tm=TM_S)
        s_st.append((ret_s, ckv.reshape(DB, L, KV_LORA), kr.reshape(DB, L, MLA_ROPE),
                     bks.reshape(DB, L, BAND_HEADS, BAND_DH), bvs.reshape(DB, L, BAND_HEADS, BAND_DH)))

    stack = lambda sts, i: jnp.stack([s[i] for s in sts])
    return (xp.reshape(B, S, D_MODEL), xs.reshape(DB, L, D_MODEL),
            stack(p_st, 0), stack(p_st, 1), stack(p_st, 2), stack(p_st, 3), stack(p_st, 4),
            stack(s_st, 0), stack(s_st, 1), stack(s_st, 2), stack(s_st, 3), stack(s_st, 4))
```

```python
import functools

import jax
import jax.numpy as jnp
import numpy as np
from jax import lax
from jax.experimental import pallas as pl
from jax.experimental.pallas import tpu as pltpu

F32 = jnp.float32
BF16 = jnp.bfloat16

D_MODEL = 1024
DEPTH = 2
CHUNK = 64
EPS = 1e-6
NEG_INF = -1e30
ROPE_BASE = 10000.0
LOG2E = 1.4426950408889634
RET_HEADS, RET_DK, RET_DV = 4, 64, 64
RET_W = RET_HEADS * RET_DV
MLA_HEADS, MLA_NOPE, MLA_ROPE, MLA_V = 8, 64, 32, 64
MLA_QK = MLA_NOPE + MLA_ROPE
Q_LORA, KV_LORA = 256, 128
MLA_W = MLA_HEADS * MLA_V
BAND_HEADS, BAND_DH = 4, 64
BAND_W = BAND_HEADS * BAND_DH
BAND_PREV_CHUNKS = 8
BAND_PAST = BAND_PREV_CHUNKS * CHUNK
MAX_REL = 128
D_MIX = RET_W + MLA_W + BAND_W
SEG = (RET_HEADS * RET_DK, RET_HEADS * RET_DK, RET_W, RET_W, Q_LORA, KV_LORA, MLA_ROPE, MLA_W,
       BAND_W, BAND_W, BAND_W, BAND_W)

LANES = 128
HEAD_PAD = LANES
MLA_QW = MLA_HEADS * HEAD_PAD
MLA_VROWS = MLA_V + 16
MLA_VT = MLA_HEADS * MLA_VROWS
W_IN_KR_END = sum(SEG[:7])
W_IN_COLS = sum(SEG) + LANES - MLA_ROPE
_edges = np.cumsum((0,) + SEG[:6] + (LANES,) + SEG[7:]).tolist()
SEG_COLS = dict(zip(('aq', 'ak', 'av', 'ag', 'cq', 'ckv', 'kr', 'bg', 'bq', 'bk', 'bv', 'cg'),
                    zip(_edges[:-1], _edges[1:])))
VMEM_LIMIT = 56 * 1024 * 1024
K1_SUB = 256


def _dot(a, b):
    return jnp.dot(a, b, preferred_element_type=F32)


def _dot_nt(a, b):
    return lax.dot_general(a, b, (((1,), (1,)), ((), ())), preferred_element_type=F32)


def _dot_tn(a, b):
    return lax.dot_general(a, b, (((0,), (0,)), ((), ())), preferred_element_type=F32)


def _group_sums(x, split, two_groups):
    lane = lax.broadcasted_iota(jnp.int32, (1, LANES), 1)
    lo = lane < split
    sums = []
    for b in range(x.shape[-1] // LANES):
        xb = x[:, b * LANES:(b + 1) * LANES]
        sq = xb * xb
        if two_groups:
            sums.append((jnp.sum(jnp.where(lo, sq, 0.0), axis=-1, keepdims=True),
                         jnp.sum(jnp.where(lo, 0.0, sq), axis=-1, keepdims=True)))
        else:
            sums.append((jnp.sum(sq, axis=-1, keepdims=True), None))
    return sums


def _group_normalize(x, sums, split, n_lo, n_hi):
    lane = lax.broadcasted_iota(jnp.int32, (1, LANES), 1)
    lo = lane < split
    outs = []
    for b, (s_lo, s_hi) in enumerate(sums):
        inv = lax.rsqrt(s_lo * (1.0 / n_lo) + EPS)
        if n_hi:
            inv = jnp.where(lo, inv, lax.rsqrt(s_hi * (1.0 / n_hi) + EPS))
        outs.append(x[:, b * LANES:(b + 1) * LANES] * inv)
    return jnp.concatenate(outs, axis=1)


def _group_rms_normed(x, split, n_lo, n_hi):
    return _group_normalize(x, _group_sums(x, split, n_hi > 0), split, n_lo, n_hi)


def _group64_inv_rms(x, bd, inv_n):
    sq = (x * x).astype(BF16)
    w = bd.shape[0]
    sums = jnp.concatenate([_dot(sq[:, c * w:(c + 1) * w], bd) for c in range(x.shape[-1] // w)], axis=1)
    return lax.rsqrt(sums * inv_n + EPS)


def _swap_halves(x, half):
    w = x.shape[-1]
    lane = lax.broadcasted_iota(jnp.int32, (1, w), 1)
    nxt = pltpu.roll(x, w - half, axis=1)
    prv = pltpu.roll(x, half, axis=1)
    return jnp.where((lane & half) == 0, nxt, prv)


def _rope(x, cos, sin_signed, half):
    return x * cos + _swap_halves(x, half) * sin_signed


def _silu(g):
    return g * (1.0 / (1.0 + jnp.exp(-g)))


def _full(shape):
    nd = len(shape)
    return pl.BlockSpec(shape, lambda *_: (0,) * nd)


def _params(sem):
    return pltpu.CompilerParams(dimension_semantics=sem, vmem_limit_bytes=VMEM_LIMIT)


def _k1_body(x_ref, ng_ref, win_ref, wuq_ref, qag_ref, kvag_ref, krg_ref, qgain_ref, bqg_ref, bkg_ref,
             cret_ref, sret_ref, cq_ref, sq_ref, ckr_ref, skr_ref, wuqr_ref, qgainr_ref, bd_ref, invnq_ref,
             rq_ref, rk_ref, rv_ref, gate_ref, qm_ref, ckv_ref, kr_ref, bq_ref, bk_ref, bv_ref,
             bks_ref, bvs_ref, *, sub, q_transposed):
    n_sub = x_ref.shape[0] // sub
    bd = bd_ref[...]
    lane_reps = sub // LANES

    def stage_a(r):
        x = x_ref[r * sub:(r + 1) * sub, :]
        h = x * lax.rsqrt(jnp.mean(x * x, axis=-1, keepdims=True) + EPS) * ng_ref[...]
        hb = h.astype(BF16)

        def seg(name):
            lo, hi = SEG_COLS[name]
            return _dot(hb, win_ref[:, lo:hi])

        z = {'cq': seg('cq'), 'bq': seg('bq'), 'bk': seg('bk')}
        cq = z['cq']
        cq = cq * lax.rsqrt(jnp.mean(cq * cq, axis=-1, keepdims=True) + EPS) * qag_ref[...]
        z['aq'], z['ak'] = seg('aq'), seg('ak')
        cqb = cq.astype(BF16)
        if q_transposed:
            z['qf'] = _dot_nt(wuq_ref[...], cqb)
        else:
            z['qf'] = _dot(cqb, wuq_ref[...])
            z['qf_sw'] = _dot(cqb, wuqr_ref[...])
        z['bq_inv'] = _group64_inv_rms(z['bq'], bd, 1.0 / BAND_DH)
        z['bk_inv'] = _group64_inv_rms(z['bk'], bd, 1.0 / BAND_DH)
        z['g'] = [seg('ag'), seg('bg'), seg('cg')]
        if q_transposed:
            groups = z['qf'].reshape(2 * MLA_HEADS, HEAD_PAD // 2, sub)
            ss = jnp.sum(groups * groups, axis=1, keepdims=True)
            is_nope = lax.broadcasted_iota(jnp.int32, (2 * MLA_HEADS, 1, 1), 0) % 2 == 0
            inv = lax.rsqrt(ss * jnp.where(is_nope, 1.0 / MLA_NOPE, 1.0 / MLA_ROPE) + EPS)
            z['qn'] = (groups * inv).reshape(MLA_QW, sub)
        else:
            z['q_inv'] = _group64_inv_rms(z['qf'], bd, invnq_ref[...])
        z['av'], z['ckv'] = seg('av'), seg('ckv')
        z['kr'], z['bv'] = seg('kr'), seg('bv')
        return z

    def stage_b(r, z):
        rows = slice(r * sub, (r + 1) * sub)
        bq_ref[rows, :] = (z['bq'] * z['bq_inv'] * bqg_ref[...]).astype(BF16)
        bk = z['bk'] * z['bk_inv'] * bkg_ref[...]
        bk_ref[rows, :] = bk.astype(BF16)
        bks_ref[rows, :] = bk
        bv_ref[rows, :] = z['bv'].astype(BF16)
        bvs_ref[rows, :] = z['bv']
        cret, sret = cret_ref[rows, :], sret_ref[rows, :]
        rq_ref[rows, :] = _rope(z['aq'], cret, sret, RET_DK // 2).astype(BF16)
        rk_ref[rows, :] = _rope(z['ak'], cret, sret, RET_DK // 2).astype(BF16)
        rv_ref[rows, :] = z['av'].astype(BF16)
        col = 0
        for g in z['g']:
            gate_ref[rows, col:col + g.shape[1]] = _silu(g).astype(BF16)
            col += g.shape[1]
        if q_transposed:
            cq_t = jnp.concatenate([cq_ref[:, rows]] * MLA_HEADS, axis=0)
            sq_t = jnp.concatenate([sq_ref[:, rows]] * MLA_HEADS, axis=0)
            qg = z['qn'] * jnp.concatenate([qgain_ref[...]] * lane_reps, axis=1)
            half = MLA_ROPE // 2
            pieces = []
            for hd in range(MLA_HEADS):
                o = hd * HEAD_PAD
                pieces += [qg[o:o + MLA_NOPE], qg[o + MLA_NOPE + half:o + MLA_QK],
                           qg[o + MLA_NOPE:o + MLA_NOPE + half], qg[o + MLA_QK:o + HEAD_PAD]]
            qm_ref[:, rows] = (qg * cq_t + jnp.concatenate(pieces, axis=0) * sq_t).astype(BF16)
        else:
            cq_t = jnp.concatenate([cq_ref[rows, :]] * MLA_HEADS, axis=1)
            sq_t = jnp.concatenate([sq_ref[rows, :]] * MLA_HEADS, axis=1)
            qm_ref[rows, :] = (z['qf'] * (z['q_inv'] * qgain_ref[...]) * cq_t
                               + z['qf_sw'] * (z['q_inv'] * qgainr_ref[...]) * sq_t).astype(BF16)
        ckv, kr = z['ckv'], z['kr']
        ckv_ref[rows, :] = ckv * lax.rsqrt(jnp.mean(ckv * ckv, axis=-1, keepdims=True) + EPS) * kvag_ref[...]
        kr = kr * lax.rsqrt(jnp.sum(kr * kr, axis=-1, keepdims=True) * (1.0 / MLA_ROPE) + EPS) * krg_ref[...]
        kr = _rope(kr, ckr_ref[rows, :], skr_ref[rows, :], MLA_ROPE // 2)
        kr_ref[rows, :] = kr[:, :MLA_ROPE]

    z = stage_a(0)
    for r in range(n_sub):
        z_next = stage_a(r + 1) if r + 1 < n_sub else None
        stage_b(r, z)
        z = z_next


def _k1(x2d, lw, tabs, consts, *, tm, rows_per_seq, n_keep, q_transposed=False):
    T = x2d.shape[0]
    nb = rows_per_seq // tm
    nkb = n_keep // tm
    n_seq = T // rows_per_seq

    row = lambda w: pl.BlockSpec((tm, w), lambda i: (i, 0))
    tab = lambda w: pl.BlockSpec((tm, w), lambda i: (i % nb, 0))
    keep = pl.BlockSpec((tm, BAND_W), lambda i: ((i // nb) * nkb + jnp.maximum(i % nb - (nb - nkb), 0), 0))

    if q_transposed:
        tab_t = pl.BlockSpec((LANES, tm), lambda i: (0, i % nb))
        col = _full((MLA_QW, LANES))
        unused = (jnp.zeros((1, LANES), F32), _full((1, LANES)))
        q_ins = {'w_uq': (lw['w_uq_t'], _full((MLA_QW, Q_LORA))), 'w_uq_sw': unused,
                 'q_gain': (lw['q_gain_col'], col), 'q_gain_sw': unused,
                 'invn_q': unused, 'cq': (tabs['cq_t'], tab_t), 'sq': (tabs['sq_t'], tab_t)}
        q_out = ((n_seq, MLA_QW, rows_per_seq), BF16,
                 pl.BlockSpec((None, MLA_QW, tm), lambda i: (i // nb, 0, i % nb)))
    else:
        q_ins = {'w_uq': (lw['w_uq'], _full((Q_LORA, MLA_QW))), 'w_uq_sw': (lw['w_uq_sw'], _full((Q_LORA, MLA_QW))),
                 'q_gain': (lw['q_gain'], _full((1, MLA_QW))), 'q_gain_sw': (lw['q_gain_sw'], _full((1, MLA_QW))),
                 'invn_q': (consts['invn_q'], _full((1, MLA_QW))), 'cq': (tabs['cq'], tab(LANES)),
                 'sq': (tabs['sq'], tab(LANES))}
        q_out = ((T, MLA_QW), BF16, row(MLA_QW))
    ins = [
        (x2d, row(D_MODEL)), (lw['norm_g'], _full((1, D_MODEL))), (lw['w_in'], _full((D_MODEL, W_IN_COLS))),
        q_ins['w_uq'], (lw['qa_g'], _full((1, Q_LORA))),
        (lw['kva_g'], _full((1, KV_LORA))), (lw['kr_g'], _full((1, LANES))),
        q_ins['q_gain'], (lw['bq_gain'], _full((1, BAND_W))),
        (lw['bk_gain'], _full((1, BAND_W))),
        (tabs['cret'], tab(RET_W)), (tabs['sret'], tab(RET_W)), q_ins['cq'],
        q_ins['sq'], (tabs['ckr'], tab(LANES)), (tabs['skr'], tab(LANES)),
        q_ins['w_uq_sw'], q_ins['q_gain_sw'],
        (consts['bd64'], _full((RET_W, RET_W))), q_ins['invn_q'],
    ]
    outs = [
        ((T, RET_W), BF16, row(RET_W)), ((T, RET_W), BF16, row(RET_W)), ((T, RET_W), BF16, row(RET_W)),
        ((T, D_MIX), BF16, row(D_MIX)), q_out,
        ((T, KV_LORA), F32, row(KV_LORA)), ((T, MLA_ROPE), F32, row(MLA_ROPE)),
        ((T, BAND_W), BF16, row(BAND_W)), ((T, BAND_W), BF16, row(BAND_W)), ((T, BAND_W), BF16, row(BAND_W)),
        ((n_seq * n_keep, BAND_W), F32, keep), ((n_seq * n_keep, BAND_W), F32, keep),
    ]
    return pl.pallas_call(
        functools.partial(_k1_body, sub=min(tm, K1_SUB), q_transposed=q_transposed),
        grid=(T // tm,),
        in_specs=[s for _, s in ins],
        out_specs=[s for _, _, s in outs],
        out_shape=[jax.ShapeDtypeStruct(sh, dt) for sh, dt, _ in outs],
        compiler_params=_params(("arbitrary",)),
        name="k1_proj",
    )(*[a for a, _ in ins])


def _to_heads_body(k_ref, v_ref, ko_ref, vo_ref):
    for hd in range(BAND_HEADS):
        lanes = slice(hd * BAND_DH, (hd + 1) * BAND_DH)
        ko_ref[:, hd, :] = k_ref[:, lanes]
        vo_ref[:, hd, :] = v_ref[:, lanes]


def _to_heads(k2d, v2d, *, tm):
    n = k2d.shape[0]
    row = pl.BlockSpec((tm, BAND_W), lambda i: (i, 0))
    head = pl.BlockSpec((tm, BAND_HEADS, BAND_DH), lambda i: (i, 0, 0))
    shape = jax.ShapeDtypeStruct((n, BAND_HEADS, BAND_DH), F32)
    return pl.pallas_call(
        _to_heads_body, grid=(n // tm,), in_specs=[row, row], out_specs=[head, head],
        out_shape=[shape, shape], compiler_params=_params(("arbitrary",)), name="band_state_heads",
    )(k2d, v2d)


def _kkv_body(ckv_ref, kr_ref, wk_ref, wv_ref, kgain_ref, *rest, transpose_v):
    if transpose_v:
        ones_ref, k_ref, v_ref = rest
    else:
        k_ref, v_ref = rest
    c = ckv_ref[...].astype(BF16)
    kn = _dot(c, wk_ref[...])
    kn = _group_rms_normed(kn, LANES, MLA_NOPE, 0) * kgain_ref[...]
    kr = kr_ref[...]
    rows = kr.shape[0]
    kr_block = jnp.concatenate([jnp.zeros((rows, MLA_NOPE), F32), kr,
                                jnp.zeros((rows, HEAD_PAD - MLA_QK), F32)], axis=1)
    k_ref[...] = (kn + jnp.concatenate([kr_block] * MLA_HEADS, axis=1)).astype(BF16)
    if transpose_v:
        v_ref[...] = (_dot_nt(wv_ref[...], c) + ones_ref[...]).astype(BF16)
    else:
        v_ref[...] = _dot(c, wv_ref[...]).astype(BF16)


def _kkv(ckv2d, kr2d, lw, *, tm, seq=None):
    T = ckv2d.shape[0]
    row = lambda w: pl.BlockSpec((tm, w), lambda i: (i, 0))
    extra_in = []
    if seq is None:
        wv, wv_spec = lw['w_uv'], _full((KV_LORA, MLA_W))
        v_spec, v_shape = row(MLA_W), (T, MLA_W)
    else:
        nb = seq // tm
        wv, wv_spec = lw['w_uv_t'], _full((MLA_VT, KV_LORA))
        v_spec = pl.BlockSpec((None, MLA_VT, tm), lambda i: (i // nb, 0, i % nb))
        v_shape = (T // seq, MLA_VT, seq)
        ones = np.zeros((MLA_HEADS, MLA_VROWS, 1), np.float32)
        ones[:, MLA_V] = 1.0
        extra_in = [(jnp.asarray(ones.reshape(MLA_VT, 1)), _full((MLA_VT, 1)))]
    return pl.pallas_call(
        functools.partial(_kkv_body, transpose_v=seq is not None),
        grid=(T // tm,),
        in_specs=[row(KV_LORA), row(MLA_ROPE), _full((KV_LORA, MLA_QW)), wv_spec, _full((1, MLA_QW))]
        + [sp for _, sp in extra_in],
        out_specs=[row(MLA_QW), v_spec],
        out_shape=[jax.ShapeDtypeStruct((T, MLA_QW), BF16), jax.ShapeDtypeStruct(v_shape, BF16)],
        compiler_params=_params(("arbitrary",)),
        name="kkv_up",
    )(ckv2d, kr2d, lw['w_uk'], wv, lw['k_gain'], *[a for a, _ in extra_in])


def _mla_prompt_body(q_ref, k_ref, vt_ref, o_ref, m_sc, acc_sc, *, tq):
    qi = pl.program_id(1)
    m_sc[...] = jnp.full(m_sc.shape, NEG_INF, F32)
    acc_sc[...] = jnp.zeros(acc_sc.shape, F32)

    def tile(kb, masked):
        start = pl.multiple_of(kb * tq, tq)
        if masked:
            kc = lax.broadcasted_iota(jnp.int32, (tq, tq), 0) // CHUNK
            qc = lax.broadcasted_iota(jnp.int32, (tq, tq), 1) // CHUNK
            visible = kc <= qc

        def scores(h):
            qt = q_ref[h * HEAD_PAD:(h + 1) * HEAD_PAD, :]
            k = k_ref[pl.ds(start, tq), h * HEAD_PAD:(h + 1) * HEAD_PAD]
            st = _dot(k, qt)
            return jnp.where(visible, st, NEG_INF) if masked else st

        def softmax(h, st):
            m_prev = m_sc[h:h + 1, :]
            m_new = jnp.maximum(m_prev, jnp.max(st, axis=0, keepdims=True))
            alpha = jnp.exp2(m_prev - m_new)
            p = jnp.exp2(st - m_new)
            m_sc[h:h + 1, :] = m_new
            return p.astype(BF16), alpha

        def values(h, p, alpha):
            vt = vt_ref[h * MLA_VROWS:(h + 1) * MLA_VROWS, pl.ds(start, tq)]
            acc_sc[h] = alpha * acc_sc[h] + _dot(vt, p)

        st = {0: scores(0), 1: scores(1)}
        pa = {}
        for h in range(MLA_HEADS):
            pa[h] = softmax(h, st.pop(h))
            if h >= 1:
                values(h - 1, *pa.pop(h - 1))
            if h + 2 < MLA_HEADS:
                st[h + 2] = scores(h + 2)
        values(MLA_HEADS - 1, *pa.pop(MLA_HEADS - 1))

    def body(kb, carry):
        tile(kb, False)
        return carry

    lax.fori_loop(0, qi, body, 0)
    tile(qi, True)
    for j in range(MLA_HEADS // 2):
        pair = jnp.concatenate([acc_sc[h, :MLA_V, :] * (1.0 / acc_sc[h, MLA_V:MLA_V + 1, :])
                                for h in (2 * j, 2 * j + 1)], axis=0)
        o_ref[:, j * LANES:(j + 1) * LANES] = pair.T.astype(BF16)


def _mla_prompt(qt, k, vt, *, tq):
    B, S, _ = k.shape
    once = pl.Buffered(1)
    return pl.pallas_call(
        functools.partial(_mla_prompt_body, tq=tq),
        grid=(B, S // tq),
        in_specs=[pl.BlockSpec((None, MLA_QW, tq), lambda b, i: (b, 0, i)),
                  pl.BlockSpec((None, S, MLA_QW), lambda b, i: (b, 0, 0), pipeline_mode=once),
                  pl.BlockSpec((None, MLA_VT, S), lambda b, i: (b, 0, 0), pipeline_mode=once)],
        out_specs=pl.BlockSpec((None, tq, MLA_W), lambda b, i: (b, i, 0)),
        out_shape=jax.ShapeDtypeStruct((B, S, MLA_W), BF16),
        scratch_shapes=[pltpu.VMEM((MLA_HEADS, tq), F32), pltpu.VMEM((MLA_HEADS, MLA_VROWS, tq), F32)],
        compiler_params=_params(("arbitrary", "arbitrary")),
        name="mla_prompt",
    )(qt, k, vt)


def _mla_sample_body(q_ref, kp_ref, vp_ref, kn_ref, vn_ref, o_ref):
    lane = lax.broadcasted_iota(jnp.int32, (1, LANES), 1)
    outs = []
    scores = []
    for h in range(MLA_HEADS):
        hs = slice(h * HEAD_PAD, (h + 1) * HEAD_PAD)
        q = q_ref[:, hs]
        scores.append((_dot_nt(q, kp_ref[:, hs]), _dot_nt(q, kn_ref[:, hs])))
    for h in range(MLA_HEADS):
        ps = slice((h // 2) * LANES, (h // 2 + 1) * LANES)
        v_lanes = (lane // MLA_V) == (h % 2)
        s1, s2 = scores[h]
        m = jnp.maximum(jnp.max(s1, axis=-1, keepdims=True), jnp.max(s2, axis=-1, keepdims=True))
        p1 = jnp.exp2(s1 - m)
        p2 = jnp.exp2(s2 - m)
        l = jnp.sum(p1, axis=-1, keepdims=True) + jnp.sum(p2, axis=-1, keepdims=True)
        v1 = vp_ref[:, ps]
        v2 = vn_ref[:, ps]
        acc = (_dot(p1.astype(BF16), jnp.where(v_lanes, v1, jnp.zeros_like(v1)))
               + _dot(p2.astype(BF16), jnp.where(v_lanes, v2, jnp.zeros_like(v2))))
        outs.append(acc * (1.0 / l))
    for j in range(MLA_HEADS // 2):
        o_ref[:, j * LANES:(j + 1) * LANES] = (outs[2 * j] + outs[2 * j + 1]).astype(BF16)


def _band_scores(q, pieces):
    lane = lax.broadcasted_iota(jnp.int32, (1, BAND_W), 1)
    raw = []
    for h in range(BAND_HEADS):
        qh = jnp.where((lane // BAND_DH) == h, q, jnp.zeros_like(q))
        raw.append([_dot_nt(qh, k) for k, _ in pieces])
    return raw


def _band_head_out(h, raw_h, pieces, biases):
    lane = lax.broadcasted_iota(jnp.int32, (1, BAND_W), 1)
    sel = (lane // BAND_DH) == h
    ss = [s + b_ref[h] for s, b_ref in zip(raw_h, biases)]
    m = functools.reduce(jnp.maximum, [jnp.max(s, axis=-1, keepdims=True) for s in ss])
    ps = [jnp.exp2(s - m) for s in ss]
    l = functools.reduce(jnp.add, [jnp.sum(p, axis=-1, keepdims=True) for p in ps])
    acc = functools.reduce(jnp.add, [
        _dot(p.astype(BF16), jnp.where(sel, v, jnp.zeros_like(v))) for p, (_, v) in zip(ps, pieces)])
    return acc * (1.0 / l)


def _band_heads(q, pieces, biases, out_dtype):
    raw = _band_scores(q, pieces)
    out = functools.reduce(jnp.add, [_band_head_out(h, raw[h], pieces, biases) for h in range(BAND_HEADS)])
    return out.astype(out_dtype)


def _band_ret_prompt_body(q_ref, kp_ref, kc_ref, vp_ref, vc_ref, bp_ref, bc_ref,
                          rq_ref, rk_ref, rv_ref, d_ref, xi_ref, zeta_ref, gam_ref, bd_ref, gain_ref,
                          o_ref, ro_ref, sfin_ref, s_sc, *, c):
    @pl.when(pl.program_id(1) == 0)
    def _():
        s_sc[...] = jnp.zeros(s_sc.shape, F32)

    pieces = [(kp_ref[...], vp_ref[...]), (kc_ref[...], vc_ref[...])]
    biases = [bp_ref, bc_ref]
    raw = _band_scores(q_ref[...], pieces)
    n_chunks = q_ref.shape[0] // c
    heads_per_chunk = BAND_HEADS // n_chunks
    state = s_sc[...]
    out = jnp.zeros(o_ref.shape, F32)
    for j in range(n_chunks):
        rows = slice(j * c, (j + 1) * c)
        o, state = _ret_chunk(rq_ref[rows, :], rk_ref[rows, :], rv_ref[rows, :], state,
                              d_ref, xi_ref, zeta_ref, gam_ref, bd_ref, gain_ref)
        ro_ref[rows, :] = o
        for h in range(j * heads_per_chunk, (j + 1) * heads_per_chunk):
            out = out + _band_head_out(h, raw[h], pieces, biases)
    o_ref[...] = out.astype(o_ref.dtype)
    s_sc[...] = state
    sfin_ref[...] = state


def _band_ret_prompt(q, k, v, bias_prev, bias_cur, rq, rk, rv, rt, consts, gain, *, tq, c):
    B, S, _ = q.shape
    cur = pl.BlockSpec((None, tq, BAND_W), lambda b, i: (b, i, 0))
    prev = pl.BlockSpec((None, tq, BAND_W), lambda b, i: (b, jnp.maximum(i - 1, 0), 0))
    bias = _full((BAND_HEADS, tq, tq))
    bias_prev_spec = pl.BlockSpec((None, BAND_HEADS, tq, tq), lambda b, i: (jnp.minimum(i, 1), 0, 0, 0))
    st = pl.BlockSpec((None, RET_W, RET_W), lambda b, i: (b, 0, 0))
    return pl.pallas_call(
        functools.partial(_band_ret_prompt_body, c=c),
        grid=(B, S // tq),
        in_specs=[cur, prev, cur, prev, cur, bias_prev_spec, bias, cur, cur, cur,
                  _full((RET_HEADS, c, c)), _full((c, RET_W)), _full((c, RET_W)),
                  _full((RET_W, RET_W)), _full((RET_W, RET_W)), _full((1, RET_W))],
        out_specs=[cur, cur, st],
        out_shape=[jax.ShapeDtypeStruct((B, S, BAND_W), BF16), jax.ShapeDtypeStruct((B, S, RET_W), BF16),
                   jax.ShapeDtypeStruct((B, RET_W, RET_W), F32)],
        scratch_shapes=[pltpu.VMEM((RET_W, RET_W), F32)],
        compiler_params=_params(("arbitrary", "arbitrary")),
        name="band_ret_prompt",
    )(q, k, k, v, v, bias_prev, bias_cur, rq, rk, rv,
      rt['decay'], rt['xi'], rt['zeta'], rt['gamma'], consts['bd_mask'], gain)


def _band_sample_body(q_ref, kp_ref, vp_ref, kn_ref, vn_ref, bp_ref, bn_ref, o_ref):
    o_ref[...] = _band_heads(q_ref[...],
                             [(kp_ref[...].astype(BF16), vp_ref[...].astype(BF16)), (kn_ref[...], vn_ref[...])],
                             [bp_ref, bn_ref], BF16)


def _ret_chunk(q, k, v, state, d_ref, xi_ref, zeta_ref, gam_ref, bd_ref, gain_ref):
    lane = lax.broadcasted_iota(jnp.int32, (1, RET_W), 1)
    o = _dot(q, state.astype(BF16)) * xi_ref[...]
    for h in range(RET_HEADS):
        sel = (lane // RET_DV) == h
        a = _dot_nt(jnp.where(sel, q, jnp.zeros_like(q)), k) * d_ref[h]
        o = o + _dot(a.astype(BF16), jnp.where(sel, v, jnp.zeros_like(v)))
    kz = (k.astype(F32) * zeta_ref[...]).astype(BF16)
    s_new = gam_ref[...] * state + bd_ref[...] * _dot_tn(kz, v)
    inv = _group64_inv_rms(o, bd_ref[...].astype(BF16), 1.0 / RET_DV)
    return (o * inv * gain_ref[...]).astype(BF16), s_new


def _sample_mixers_body(mq_ref, mkp_ref, mvp_ref, mkn_ref, mvn_ref,
                        bq_ref, bkp_ref, bvp_ref, bkn_ref, bvn_ref, bp_ref, bn_ref,
                        rq_ref, rk_ref, rv_ref, s0_ref, d_ref, xi_ref, zeta_ref, gam_ref, bd_ref, gain_ref,
                        mo_ref, bo_ref, ro_ref, sfin_ref):
    _mla_sample_body(mq_ref, mkp_ref, mvp_ref, mkn_ref, mvn_ref, mo_ref)
    _band_sample_body(bq_ref, bkp_ref, bvp_ref, bkn_ref, bvn_ref, bp_ref, bn_ref, bo_ref)
    s0 = s0_ref[...]
    state = jnp.concatenate([
        jnp.concatenate([s0[h] if g == h else jnp.zeros((RET_DK, RET_DV), F32) for g in range(RET_HEADS)], axis=1)
        for h in range(RET_HEADS)], axis=0)
    o, s_new = _ret_chunk(rq_ref[...], rk_ref[...], rv_ref[...], state,
                          d_ref, xi_ref, zeta_ref, gam_ref, bd_ref, gain_ref)
    ro_ref[...] = o
    for h in range(RET_HEADS):
        sfin_ref[h] = s_new[h * RET_DK:(h + 1) * RET_DK, h * RET_DV:(h + 1) * RET_DV]


def _sample_mixers(mq, mkp, mvp, mkn, mvn, bq, bkp, bvp, bkn, bvn, bias_past, bias_new,
                   rq, rk, rv, s0, rt, consts, gain, *, layer):
    B, L, _ = mq.shape
    P, PB = mkp.shape[1], bkp.shape[1]
    blk = lambda n, w: pl.BlockSpec((None, n, w), lambda b: (b, 0, 0))
    return pl.pallas_call(
        _sample_mixers_body,
        grid=(B,),
        in_specs=[blk(L, MLA_QW), blk(P, MLA_QW), blk(P, MLA_W), blk(L, MLA_QW), blk(L, MLA_W),
                  blk(L, BAND_W), blk(PB, BAND_W), blk(PB, BAND_W), blk(L, BAND_W), blk(L, BAND_W),
                  _full((BAND_HEADS, L, PB)), _full((BAND_HEADS, L, L)),
                  blk(L, RET_W), blk(L, RET_W), blk(L, RET_W),
                  pl.BlockSpec((None, None, RET_HEADS, RET_DK, RET_DV), lambda b: (layer, b, 0, 0, 0)),
                  _full((RET_HEADS, L, L)), _full((L, RET_W)), _full((L, RET_W)),
                  _full((RET_W, RET_W)), _full((RET_W, RET_W)), _full((1, RET_W))],
        out_specs=[blk(L, MLA_W), blk(L, BAND_W), blk(L, RET_W),
                   pl.BlockSpec((None, RET_HEADS, RET_DK, RET_DV), lambda b: (b, 0, 0, 0))],
        out_shape=[jax.ShapeDtypeStruct((B, L, MLA_W), BF16), jax.ShapeDtypeStruct((B, L, BAND_W), BF16),
                   jax.ShapeDtypeStruct((B, L, RET_W), BF16),
                   jax.ShapeDtypeStruct((B, RET_HEADS, RET_DK, RET_DV), F32)],
        compiler_params=_params(("arbitrary",)),
        name="sample_mixers",
    )(mq, mkp, mvp, mkn, mvn, bq, bkp, bvp, bkn, bvn, bias_past, bias_new,
      rq, rk, rv, s0, rt['decay'], rt['xi'], rt['zeta'], rt['gamma'], consts['bd_mask'], gain)


def _kout_body(x_ref, ro_ref, mo_ref, bo_ref, g_ref, w_ref, y_ref):
    g = g_ref[...]
    y = x_ref[...]
    y = y + _dot(ro_ref[...] * g[:, :RET_W], w_ref[:RET_W, :])
    y = y + _dot(mo_ref[...] * g[:, RET_W:RET_W + MLA_W], w_ref[RET_W:RET_W + MLA_W, :])
    y = y + _dot(bo_ref[...] * g[:, RET_W + MLA_W:], w_ref[RET_W + MLA_W:, :])
    y_ref[...] = y


def _kout(x2d, ro, mo, bo, gate, w_out, *, tm):
    T = x2d.shape[0]
    row = lambda w: pl.BlockSpec((tm, w), lambda i: (i, 0))
    return pl.pallas_call(
        _kout_body,
        grid=(T // tm,),
        in_specs=[row(D_MODEL), row(RET_W), row(MLA_W), row(BAND_W), row(D_MIX), _full((D_MIX, D_MODEL))],
        out_specs=row(D_MODEL),
        out_shape=jax.ShapeDtypeStruct((T, D_MODEL), F32),
        compiler_params=_params(("arbitrary",)),
        name="kout_proj",
    )(x2d, ro, mo, bo, gate, w_out)


def _constants():
    bd = np.kron(np.eye(RET_HEADS, dtype=np.float32), np.ones((RET_DK, RET_DV), np.float32))
    invn_q = np.tile(np.repeat([1.0 / MLA_NOPE, 1.0 / MLA_ROPE], HEAD_PAD // 2), MLA_HEADS)[None]
    return {'bd_mask': jnp.asarray(bd), 'bd64': jnp.asarray(bd, BF16), 'invn_q': jnp.asarray(invn_q, F32)}


def _rope_tables(pos, reps=1):
    pos = np.asarray(pos, np.float64)

    def cs(half):
        inv = ROPE_BASE ** (-np.arange(half, dtype=np.float64) / half)
        ang = pos[:, None] * inv[None, :]
        c, s = np.cos(ang), np.sin(ang)
        return np.concatenate([c, c], axis=-1), np.concatenate([-s, s], axis=-1)

    T = pos.shape[0]
    c32, s32 = cs(RET_DK // 2)
    c16, s16 = cs(MLA_ROPE // 2)
    ones = lambda w: np.ones((T, w))
    zeros = lambda w: np.zeros((T, w))
    pad = HEAD_PAD - MLA_QK
    tabs = {
        'cret': np.tile(c32, (1, RET_HEADS)), 'sret': np.tile(s32, (1, RET_HEADS)),
        'cq': np.concatenate([ones(MLA_NOPE), c16, ones(pad)], axis=-1),
        'sq': np.concatenate([zeros(MLA_NOPE), s16, zeros(pad)], axis=-1),
        'ckr': np.concatenate([c16, ones(LANES - MLA_ROPE)], axis=-1),
        'skr': np.concatenate([s16, zeros(LANES - MLA_ROPE)], axis=-1),
    }
    out = {k: jnp.asarray(np.tile(v, (reps, 1)), F32) for k, v in tabs.items()}
    if reps == 1:
        out['cq_t'], out['sq_t'] = jnp.asarray(tabs['cq'].T, F32), jnp.asarray(tabs['sq'].T, F32)
    return out


def _retention_tables(c):
    lg = np.log1p(-np.exp2(-5.0 - np.arange(RET_HEADS, dtype=np.float64)))
    idx = np.arange(c, dtype=np.float64)
    diff = idx[:, None] - idx[None, :]
    decay = np.where(diff >= 0, np.exp(lg[:, None, None] * np.maximum(diff, 0.0)), 0.0)
    per_lane = lambda t: np.repeat(t, RET_DV, axis=-1)
    xi = per_lane(np.exp(lg[None, :] * (idx[:, None] + 1.0)))
    zeta = per_lane(np.exp(lg[None, :] * (c - 1.0 - idx)[:, None]))
    gamma = np.broadcast_to(per_lane(np.exp(lg * c)[None, :]).T, (RET_W, RET_W))
    return {k: jnp.asarray(v, F32) for k, v in
            {'decay': decay, 'xi': xi, 'zeta': zeta, 'gamma': gamma}.items()}


def _band_tables_body(rp_ref, rc_ref, prev_ref, cur_ref, past_ref, new_ref, *, tq, n_new):
    w = 2 * tq
    tp = pltpu.roll(jnp.broadcast_to(rp_ref[...], (tq, w)), 0, 1, stride=1, stride_axis=0)[:, :tq]
    tc = pltpu.roll(jnp.broadcast_to(rc_ref[...], (tq, w)), 0, 1, stride=1, stride_axis=0)[:, :tq]
    qc = lax.broadcasted_iota(jnp.int32, (tq, tq), 0) // CHUNK
    kc = lax.broadcasted_iota(jnp.int32, (tq, tq), 1) // CHUNK
    prev_ref[0] = jnp.full((tq, tq), NEG_INF, F32)
    prev_ref[1] = jnp.where(kc >= qc, tp, NEG_INF)
    cur_ref[...] = jnp.where(kc <= qc, tc, NEG_INF)
    past_ref[...] = tp[:n_new, :]
    new_ref[...] = tc[:n_new, :n_new]


def _band_tables(band_bias, *, tq, n_new):
    H = band_bias.shape[0]
    band_bias = band_bias * LOG2E
    lo, mid, hi = band_bias[:, :1], band_bias[:, MAX_REL + 1:2 * MAX_REL], band_bias[:, 2 * MAX_REL:]
    rep = lambda col, n: jnp.broadcast_to(col, (H, n))
    r_cur = jnp.concatenate([band_bias[:, MAX_REL::-1], rep(lo, tq - MAX_REL - 1), rep(hi, tq - MAX_REL),
                             band_bias[:, :MAX_REL:-1]], axis=1)
    r_prev = jnp.concatenate([rep(hi, tq - MAX_REL + 1), mid[:, ::-1], rep(hi, tq)], axis=1)
    row = pl.BlockSpec((None, 1, 2 * tq), lambda h: (h, 0, 0))
    tile = lambda n, m: pl.BlockSpec((None, n, m), lambda h: (h, 0, 0))
    return pl.pallas_call(
        functools.partial(_band_tables_body, tq=tq, n_new=n_new),
        grid=(H,),
        in_specs=[row, row],
        out_specs=[pl.BlockSpec((2, None, tq, tq), lambda h: (0, h, 0, 0)), tile(tq, tq), tile(n_new, tq),
                   tile(n_new, n_new)],
        out_shape=[jax.ShapeDtypeStruct((2, H, tq, tq), F32), jax.ShapeDtypeStruct((H, tq, tq), F32),
                   jax.ShapeDtypeStruct((H, n_new, tq), F32), jax.ShapeDtypeStruct((H, n_new, n_new), F32)],
        compiler_params=_params(("arbitrary",)),
        name="band_tables",
    )(r_prev[:, None, :], r_cur[:, None, :])


def _layer_weights(l, norm_g, w_in, ret_gn_g, mla_qa_g, mla_w_uq, mla_qn_g, mla_qr_g, mla_kva_g, mla_kr_g,
                   mla_w_ukv, mla_kn_g, band_qn_g, band_kn_g, w_out):
    col_scale = np.ones((1, W_IN_COLS), np.float32)
    col_scale[:, slice(*SEG_COLS['ak'])] = RET_DK ** -0.5
    w_in_p = (jnp.concatenate([w_in[l][:, :W_IN_KR_END], jnp.zeros((D_MODEL, LANES - MLA_ROPE), F32),
                               w_in[l][:, W_IN_KR_END:]], axis=-1) * col_scale).astype(BF16)
    pad = HEAD_PAD - MLA_QK
    w_uq = jnp.pad(mla_w_uq[l].reshape(Q_LORA, MLA_HEADS, MLA_QK), ((0, 0), (0, 0), (0, pad)))
    half = MLA_ROPE // 2

    def swap_rope(t):
        return jnp.concatenate([t[..., :MLA_NOPE], t[..., MLA_NOPE + half:MLA_QK], t[..., MLA_NOPE:MLA_NOPE + half],
                                t[..., MLA_QK:]], axis=-1)
    ukv = mla_w_ukv[l].reshape(KV_LORA, MLA_HEADS, MLA_NOPE + MLA_V)
    w_uk = jnp.pad(ukv[:, :, :MLA_NOPE], ((0, 0), (0, 0), (0, HEAD_PAD - MLA_NOPE)))
    zpad = jnp.zeros((pad,), F32)
    q_gain_head = jnp.concatenate([mla_qn_g[l], mla_qr_g[l], zpad]) * (MLA_QK ** -0.5 * LOG2E)
    q_gain = jnp.tile(q_gain_head, MLA_HEADS)
    q_gain_sw = jnp.tile(swap_rope(q_gain_head), MLA_HEADS)
    k_gain = jnp.tile(jnp.concatenate([mla_kn_g[l], jnp.zeros((HEAD_PAD - MLA_NOPE,), F32)]), MLA_HEADS)
    return {
        'norm_g': norm_g[l][None], 'w_in': w_in_p,
        'w_uq': w_uq.reshape(Q_LORA, MLA_QW).astype(BF16), 'qa_g': mla_qa_g[l][None],
        'w_uq_sw': swap_rope(w_uq).reshape(Q_LORA, MLA_QW).astype(BF16), 'q_gain_sw': q_gain_sw[None],
        'w_uq_t': w_uq.reshape(Q_LORA, MLA_QW).T.astype(BF16),
        'q_gain_col': jnp.broadcast_to(q_gain[:, None], (MLA_QW, LANES)),
        'kva_g': mla_kva_g[l][None],
        'kr_g': jnp.concatenate([mla_kr_g[l], jnp.zeros((LANES - MLA_ROPE,), F32)])[None],
        'q_gain': q_gain[None],
        'bq_gain': (jnp.tile(band_qn_g[l], BAND_HEADS) * (BAND_DH ** -0.5 * LOG2E))[None],
        'bk_gain': jnp.tile(band_kn_g[l], BAND_HEADS)[None],
        'w_uk': w_uk.reshape(KV_LORA, MLA_QW).astype(BF16),
        'w_uv': ukv[:, :, MLA_NOPE:].reshape(KV_LORA, MLA_W).astype(BF16),
        'w_uv_t': jnp.pad(ukv[:, :, MLA_NOPE:].transpose(1, 2, 0), ((0, 0), (0, MLA_VROWS - MLA_V), (0, 0))
                          ).reshape(MLA_VT, KV_LORA).astype(BF16),
        'k_gain': k_gain[None], 'ret_gain': ret_gn_g[l][None], 'w_out': w_out[l].astype(BF16),
    }


def _diag_blocks(s_bd):
    B = s_bd.shape[0]
    s = s_bd.reshape(B, RET_HEADS, RET_DK, RET_HEADS, RET_DV)
    return jnp.stack([s[:, h, :, h, :] for h in range(RET_HEADS)], axis=1)


def kernel(x_prompt, x_sample, state_ret, cache_mla_ckv, cache_mla_krope, cache_band_k, cache_band_v, norm_g, w_in, ret_gn_g, mla_qa_g, mla_w_uq, mla_qn_g, mla_qr_g, mla_kva_g, mla_kr_g, mla_w_ukv, mla_kn_g, band_qn_g, band_kn_g, band_bias, w_out):
    B, S, _ = x_prompt.shape
    DB, L, _ = x_sample.shape
    past_len = cache_mla_ckv.shape[2]
    n_band_past = cache_band_k.shape[2]
    n_keep_p = min(BAND_PAST, S)

    TM_P, TM_S, TM_KV, TQ, RET_C = 512, 256, 1024, 512, 256
    consts = _constants()
    tabs_p = _rope_tables(np.arange(S))
    tabs_s = _rope_tables(past_len + np.arange(L), reps=DB)
    rt_p = _retention_tables(RET_C)
    rt_s = _retention_tables(L)

    xp = x_prompt.reshape(B * S, D_MODEL)
    xs = x_sample.reshape(DB * L, D_MODEL)
    p_st, s_st = [], []
    for l in range(DEPTH):
        lw = _layer_weights(l, norm_g, w_in, ret_gn_g, mla_qa_g, mla_w_uq, mla_qn_g, mla_qr_g, mla_kva_g,
                            mla_kr_g, mla_w_ukv, mla_kn_g, band_qn_g, band_kn_g, w_out)

        rq, rk, rv, gate, qm, ckv, kr, bq, bk, bv, bks, bvs = _k1(
            xp, lw, tabs_p, consts, tm=TM_P, rows_per_seq=S, n_keep=n_keep_p, q_transposed=True)
        km, vt = _kkv(ckv, kr, lw, tm=TM_KV, seq=S)
        mla_o = _mla_prompt(qm, km.reshape(B, S, MLA_QW), vt, tq=TQ)
        assert TQ == BAND_PAST == n_band_past
        bias_prev, bias_cur, bias_past, bias_new = _band_tables(band_bias[l], tq=TQ, n_new=L)
        band_o, ret_o, ret_s = _band_ret_prompt(
            bq.reshape(B, S, BAND_W), bk.reshape(B, S, BAND_W), bv.reshape(B, S, BAND_W), bias_prev, bias_cur,
            rq.reshape(B, S, RET_W), rk.reshape(B, S, RET_W), rv.reshape(B, S, RET_W),
            rt_p, consts, lw['ret_gain'], tq=TQ, c=RET_C)
        xp = _kout(xp, ret_o.reshape(B * S, RET_W), mla_o.reshape(B * S, MLA_W), band_o.reshape(B * S, BAND_W),
                   gate, lw['w_out'], tm=TM_P)
        bks, bvs = _to_heads(bks, bvs, tm=TM_S)
        p_st.append((_diag_blocks(ret_s), ckv.reshape(B, S, KV_LORA), kr.reshape(B, S, MLA_ROPE),
                     bks.reshape(B, n_keep_p, BAND_HEADS, BAND_DH), bvs.reshape(B, n_keep_p, BAND_HEADS, BAND_DH)))

        rq, rk, rv, gate, qm, ckv, kr, bq, bk, bv, bks, bvs = _k1(
            xs, lw, tabs_s, consts, tm=TM_S, rows_per_seq=DB * L, n_keep=DB * L)
        kn, vn = _kkv(ckv, kr, lw, tm=TM_S)
        kp, vp = _kkv(cache_mla_ckv[l].reshape(DB * past_len, KV_LORA),
                      cache_mla_krope[l].reshape(DB * past_len, MLA_ROPE), lw, tm=TM_KV)
        mla_o, band_o, ret_o, ret_s = _sample_mixers(
            qm.reshape(DB, L, MLA_QW), kp.reshape(DB, past_len, MLA_QW), vp.reshape(DB, past_len, MLA_W),
            kn.reshape(DB, L, MLA_QW), vn.reshape(DB, L, MLA_W),
            bq.reshape(DB, L, BAND_W), cache_band_k[l].reshape(DB, n_band_past, BAND_W),
            cache_band_v[l].reshape(DB, n_band_past, BAND_W), bk.reshape(DB, L, BAND_W), bv.reshape(DB, L, BAND_W),
            bias_past, bias_new,
            rq.reshape(DB, L, RET_W), rk.reshape(DB, L, RET_W), rv.reshape(DB, L, RET_W),
            state_ret, rt_s, consts, lw['ret_gain'], layer=l)
        xs = _kout(xs, ret_o.reshape(DB * L, RET_W), mla_o.reshape(DB * L, MLA_W), band_o.reshape(DB * L, BAND_W),
                   gate, lw['w_out'], tm=TM_S)
        bks, bvs = _to_heads(bks, bvs, tm=TM_S)
        s_st.append((ret_s, ckv.reshape(DB, L, KV_LORA), kr.reshape(DB, L, MLA_ROPE),
                     bks.reshape(DB, L, BAND_HEADS, BAND_DH), bvs.reshape(DB, L, BAND_HEADS, BAND_DH)))

    stack = lambda sts, i: jnp.stack([s[i] for s in sts])
    return (xp.reshape(B, S, D_MODEL), xs.reshape(DB, L, D_MODEL),
            stack(p_st, 0), stack(p_st, 1), stack(p_st, 2), stack(p_st, 3), stack(p_st, 4),
            stack(s_st, 0), stack(s_st, 1), stack(s_st, 2), stack(s_st, 3), stack(s_st, 4))
```

```python
import functools

import jax
import jax.numpy as jnp
import numpy as np
from jax import lax
from jax.experimental import pallas as pl
from jax.experimental.pallas import tpu as pltpu

F32 = jnp.float32
BF16 = jnp.bfloat16

D_MODEL = 1024
DEPTH = 2
CHUNK = 64
EPS = 1e-6
NEG_INF = -1e30
ROPE_BASE = 10000.0
LOG2E = 1.4426950408889634
RET_HEADS, RET_DK, RET_DV = 4, 64, 64
RET_W = RET_HEADS * RET_DV
MLA_HEADS, MLA_NOPE, MLA_ROPE, MLA_V = 8, 64, 32, 64
MLA_QK = MLA_NOPE + MLA_ROPE
Q_LORA, KV_LORA = 256, 128
MLA_W = MLA_HEADS * MLA_V
BAND_HEADS, BAND_DH = 4, 64
BAND_W = BAND_HEADS * BAND_DH
BAND_PREV_CHUNKS = 8
BAND_PAST = BAND_PREV_CHUNKS * CHUNK
MAX_REL = 128
D_MIX = RET_W + MLA_W + BAND_W
SEG = (RET_HEADS * RET_DK, RET_HEADS * RET_DK, RET_W, RET_W, Q_LORA, KV_LORA, MLA_ROPE, MLA_W,
       BAND_W, BAND_W, BAND_W, BAND_W)

LANES = 128
HEAD_PAD = LANES
MLA_QW = MLA_HEADS * HEAD_PAD
MLA_VROWS = MLA_V + 16
MLA_VT = MLA_HEADS * MLA_VROWS
C_AQ, C_AK, C_AV, C_CQ, C_CKV, C_KR, C_BQ, C_BK, C_BV, C_G, C_END = (
    0, 256, 512, 768, 1024, 1152, 1280, 1536, 1792, 2048, 3072)
VMEM_LIMIT = 56 * 1024 * 1024
K1_SUB = 256


def _dot(a, b):
    return jnp.dot(a, b, preferred_element_type=F32)


def _dot_nt(a, b):
    return lax.dot_general(a, b, (((1,), (1,)), ((), ())), preferred_element_type=F32)


def _dot_tn(a, b):
    return lax.dot_general(a, b, (((0,), (0,)), ((), ())), preferred_element_type=F32)


def _group_sums(x, split, two_groups):
    lane = lax.broadcasted_iota(jnp.int32, (1, LANES), 1)
    lo = lane < split
    sums = []
    for b in range(x.shape[-1] // LANES):
        xb = x[:, b * LANES:(b + 1) * LANES]
        sq = xb * xb
        if two_groups:
            sums.append((jnp.sum(jnp.where(lo, sq, 0.0), axis=-1, keepdims=True),
                         jnp.sum(jnp.where(lo, 0.0, sq), axis=-1, keepdims=True)))
        else:
            sums.append((jnp.sum(sq, axis=-1, keepdims=True), None))
    return sums


def _group_normalize(x, sums, split, n_lo, n_hi):
    lane = lax.broadcasted_iota(jnp.int32, (1, LANES), 1)
    lo = lane < split
    outs = []
    for b, (s_lo, s_hi) in enumerate(sums):
        inv = lax.rsqrt(s_lo * (1.0 / n_lo) + EPS)
        if n_hi:
            inv = jnp.where(lo, inv, lax.rsqrt(s_hi * (1.0 / n_hi) + EPS))
        outs.append(x[:, b * LANES:(b + 1) * LANES] * inv)
    return jnp.concatenate(outs, axis=1)


def _group_rms_normed(x, split, n_lo, n_hi):
    return _group_normalize(x, _group_sums(x, split, n_hi > 0), split, n_lo, n_hi)


def _group64_inv_rms(x, bd, inv_n):
    sq = (x * x).astype(BF16)
    w = bd.shape[0]
    sums = jnp.concatenate([_dot(sq[:, c * w:(c + 1) * w], bd) for c in range(x.shape[-1] // w)], axis=1)
    return lax.rsqrt(sums * inv_n + EPS)


def _swap_halves(x, half):
    w = x.shape[-1]
    lane = lax.broadcasted_iota(jnp.int32, (1, w), 1)
    nxt = pltpu.roll(x, w - half, axis=1)
    prv = pltpu.roll(x, half, axis=1)
    return jnp.where((lane & half) == 0, nxt, prv)


def _rope(x, cos, sin_signed, half):
    return x * cos + _swap_halves(x, half) * sin_signed


def _silu(g):
    return g * (1.0 / (1.0 + jnp.exp(-g)))


def _full(shape):
    nd = len(shape)
    return pl.BlockSpec(shape, lambda *_: (0,) * nd)


def _params(sem):
    return pltpu.CompilerParams(dimension_semantics=sem, vmem_limit_bytes=VMEM_LIMIT)


def _k1_body(x_ref, ng_ref, win_ref, wuq_ref, qag_ref, kvag_ref, krg_ref, qgain_ref, bqg_ref, bkg_ref,
             cret_ref, sret_ref, cq_ref, sq_ref, ckr_ref, skr_ref, wuqr_ref, qgainr_ref, bd_ref, invnq_ref,
             rq_ref, rk_ref, rv_ref, gate_ref, qm_ref, ckv_ref, kr_ref, bq_ref, bk_ref, bv_ref,
             bks_ref, bvs_ref, *, sub, q_transposed):
    n_sub = x_ref.shape[0] // sub
    bd = bd_ref[...]
    lane_reps = sub // LANES

    def stage_a(r):
        x = x_ref[r * sub:(r + 1) * sub, :]
        h = x * lax.rsqrt(jnp.mean(x * x, axis=-1, keepdims=True) + EPS) * ng_ref[...]
        hb = h.astype(BF16)

        def seg(lo, hi):
            return _dot(hb, win_ref[:, lo:hi])

        z = {'cq': seg(C_CQ, C_CKV), 'bq': seg(C_BQ, C_BK), 'bk': seg(C_BK, C_BV)}
        cq = z['cq']
        cq = cq * lax.rsqrt(jnp.mean(cq * cq, axis=-1, keepdims=True) + EPS) * qag_ref[...]
        z['aq'], z['ak'] = seg(C_AQ, C_AK), seg(C_AK, C_AV)
        cqb = cq.astype(BF16)
        if q_transposed:
            z['qf'] = _dot_nt(wuq_ref[...], cqb)
        else:
            z['qf'] = _dot(cqb, wuq_ref[...])
            z['qf_sw'] = _dot(cqb, wuqr_ref[...])
        z['bq_inv'] = _group64_inv_rms(z['bq'], bd, 1.0 / BAND_DH)
        z['bk_inv'] = _group64_inv_rms(z['bk'], bd, 1.0 / BAND_DH)
        z['g'] = seg(C_G, C_END)
        if q_transposed:
            groups = z['qf'].reshape(2 * MLA_HEADS, HEAD_PAD // 2, sub)
            ss = jnp.sum(groups * groups, axis=1, keepdims=True)
            is_nope = lax.broadcasted_iota(jnp.int32, (2 * MLA_HEADS, 1, 1), 0) % 2 == 0
            inv = lax.rsqrt(ss * jnp.where(is_nope, 1.0 / MLA_NOPE, 1.0 / MLA_ROPE) + EPS)
            z['qn'] = (groups * inv).reshape(MLA_QW, sub)
        else:
            z['q_inv'] = _group64_inv_rms(z['qf'], bd, invnq_ref[...])
        z['av'], z['ckv'] = seg(C_AV, C_CQ), seg(C_CKV, C_KR)
        z['kr'], z['bv'] = seg(C_KR, C_BQ), seg(C_BV, C_G)
        return z

    def stage_b(r, z):
        rows = slice(r * sub, (r + 1) * sub)
        bq_ref[rows, :] = (z['bq'] * z['bq_inv'] * bqg_ref[...]).astype(BF16)
        bk = z['bk'] * z['bk_inv'] * bkg_ref[...]
        bk_ref[rows, :] = bk.astype(BF16)
        bks_ref[rows, :] = bk
        bv_ref[rows, :] = z['bv'].astype(BF16)
        bvs_ref[rows, :] = z['bv']
        cret, sret = cret_ref[rows, :], sret_ref[rows, :]
        rq_ref[rows, :] = _rope(z['aq'], cret, sret, RET_DK // 2).astype(BF16)
        rk_ref[rows, :] = _rope(z['ak'], cret, sret, RET_DK // 2).astype(BF16)
        rv_ref[rows, :] = z['av'].astype(BF16)
        gate_ref[rows, :] = _silu(z['g']).astype(BF16)
        if q_transposed:
            cq_t = jnp.concatenate([cq_ref[:, rows]] * MLA_HEADS, axis=0)
            sq_t = jnp.concatenate([sq_ref[:, rows]] * MLA_HEADS, axis=0)
            qg = z['qn'] * jnp.concatenate([qgain_ref[...]] * lane_reps, axis=1)
            half = MLA_ROPE // 2
            pieces = []
            for hd in range(MLA_HEADS):
                o = hd * HEAD_PAD
                pieces += [qg[o:o + MLA_NOPE], qg[o + MLA_NOPE + half:o + MLA_QK],
                           qg[o + MLA_NOPE:o + MLA_NOPE + half], qg[o + MLA_QK:o + HEAD_PAD]]
            qm_ref[:, rows] = (qg * cq_t + jnp.concatenate(pieces, axis=0) * sq_t).astype(BF16)
        else:
            cq_t = jnp.concatenate([cq_ref[rows, :]] * MLA_HEADS, axis=1)
            sq_t = jnp.concatenate([sq_ref[rows, :]] * MLA_HEADS, axis=1)
            qm_ref[rows, :] = (z['qf'] * (z['q_inv'] * qgain_ref[...]) * cq_t
                               + z['qf_sw'] * (z['q_inv'] * qgainr_ref[...]) * sq_t).astype(BF16)
        ckv, kr = z['ckv'], z['kr']
        ckv_ref[rows, :] = ckv * lax.rsqrt(jnp.mean(ckv * ckv, axis=-1, keepdims=True) + EPS) * kvag_ref[...]
        kr = kr * lax.rsqrt(jnp.sum(kr * kr, axis=-1, keepdims=True) * (1.0 / MLA_ROPE) + EPS) * krg_ref[...]
        kr = _rope(kr, ckr_ref[rows, :], skr_ref[rows, :], MLA_ROPE // 2)
        kr_ref[rows, :] = kr[:, :MLA_ROPE]

    z = stage_a(0)
    for r in range(n_sub):
        z_next = stage_a(r + 1) if r + 1 < n_sub else None
        stage_b(r, z)
        z = z_next


def _k1(x2d, lw, tabs, consts, *, tm, rows_per_seq, n_keep, q_transposed=False):
    T = x2d.shape[0]
    nb = rows_per_seq // tm
    nkb = n_keep // tm
    n_seq = T // rows_per_seq

    row = lambda w: pl.BlockSpec((tm, w), lambda i: (i, 0))
    tab = lambda w: pl.BlockSpec((tm, w), lambda i: (i % nb, 0))
    keep = pl.BlockSpec((tm, BAND_W), lambda i: ((i // nb) * nkb + jnp.maximum(i % nb - (nb - nkb), 0), 0))

    if q_transposed:
        tab_t = pl.BlockSpec((LANES, tm), lambda i: (0, i % nb))
        col = _full((MLA_QW, LANES))
        unused = (jnp.zeros((1, LANES), F32), _full((1, LANES)))
        q_ins = {'w_uq': (lw['w_uq_t'], _full((MLA_QW, Q_LORA))), 'w_uq_sw': unused,
                 'q_gain': (lw['q_gain_col'], col), 'q_gain_sw': unused,
                 'invn_q': unused, 'cq': (tabs['cq_t'], tab_t), 'sq': (tabs['sq_t'], tab_t)}
        q_out = ((n_seq, MLA_QW, rows_per_seq), BF16,
                 pl.BlockSpec((None, MLA_QW, tm), lambda i: (i // nb, 0, i % nb)))
    else:
        q_ins = {'w_uq': (lw['w_uq'], _full((Q_LORA, MLA_QW))), 'w_uq_sw': (lw['w_uq_sw'], _full((Q_LORA, MLA_QW))),
                 'q_gain': (lw['q_gain'], _full((1, MLA_QW))), 'q_gain_sw': (lw['q_gain_sw'], _full((1, MLA_QW))),
                 'invn_q': (consts['invn_q'], _full((1, MLA_QW))), 'cq': (tabs['cq'], tab(LANES)),
                 'sq': (tabs['sq'], tab(LANES))}
        q_out = ((T, MLA_QW), BF16, row(MLA_QW))
    ins = [
        (x2d, row(D_MODEL)), (lw['norm_g'], _full((1, D_MODEL))), (lw['w_in'], _full((D_MODEL, C_END))),
        q_ins['w_uq'], (lw['qa_g'], _full((1, Q_LORA))),
        (lw['kva_g'], _full((1, KV_LORA))), (lw['kr_g'], _full((1, LANES))),
        q_ins['q_gain'], (lw['bq_gain'], _full((1, BAND_W))),
        (lw['bk_gain'], _full((1, BAND_W))),
        (tabs['cret'], tab(RET_W)), (tabs['sret'], tab(RET_W)), q_ins['cq'],
        q_ins['sq'], (tabs['ckr'], tab(LANES)), (tabs['skr'], tab(LANES)),
        q_ins['w_uq_sw'], q_ins['q_gain_sw'],
        (consts['bd64'], _full((RET_W, RET_W))), q_ins['invn_q'],
    ]
    outs = [
        ((T, RET_W), BF16, row(RET_W)), ((T, RET_W), BF16, row(RET_W)), ((T, RET_W), BF16, row(RET_W)),
        ((T, D_MIX), BF16, row(D_MIX)), q_out,
        ((T, KV_LORA), F32, row(KV_LORA)), ((T, MLA_ROPE), F32, row(MLA_ROPE)),
        ((T, BAND_W), BF16, row(BAND_W)), ((T, BAND_W), BF16, row(BAND_W)), ((T, BAND_W), BF16, row(BAND_W)),
        ((n_seq * n_keep, BAND_W), F32, keep), ((n_seq * n_keep, BAND_W), F32, keep),
    ]
    return pl.pallas_call(
        functools.partial(_k1_body, sub=min(tm, K1_SUB), q_transposed=q_transposed),
        grid=(T // tm,),
        in_specs=[s for _, s in ins],
        out_specs=[s for _, _, s in outs],
        out_shape=[jax.ShapeDtypeStruct(sh, dt) for sh, dt, _ in outs],
        compiler_params=_params(("arbitrary",)),
        name="k1_proj",
    )(*[a for a, _ in ins])


def _to_heads_body(k_ref, v_ref, ko_ref, vo_ref):
    for hd in range(BAND_HEADS):
        lanes = slice(hd * BAND_DH, (hd + 1) * BAND_DH)
        ko_ref[:, hd, :] = k_ref[:, lanes]
        vo_ref[:, hd, :] = v_ref[:, lanes]


def _to_heads(k2d, v2d, *, tm):
    n = k2d.shape[0]
    row = pl.BlockSpec((tm, BAND_W), lambda i: (i, 0))
    head = pl.BlockSpec((tm, BAND_HEADS, BAND_DH), lambda i: (i, 0, 0))
    shape = jax.ShapeDtypeStruct((n, BAND_HEADS, BAND_DH), F32)
    return pl.pallas_call(
        _to_heads_body, grid=(n // tm,), in_specs=[row, row], out_specs=[head, head],
        out_shape=[shape, shape], compiler_params=_params(("arbitrary",)), name="band_state_heads",
    )(k2d, v2d)


def _kkv_body(ckv_ref, kr_ref, wk_ref, wv_ref, kgain_ref, *rest, transpose_v):
    if transpose_v:
        ones_ref, k_ref, v_ref = rest
    else:
        k_ref, v_ref = rest
    c = ckv_ref[...].astype(BF16)
    kn = _dot(c, wk_ref[...])
    kn = _group_rms_normed(kn, LANES, MLA_NOPE, 0) * kgain_ref[...]
    kr = kr_ref[...]
    rows = kr.shape[0]
    kr_block = jnp.concatenate([jnp.zeros((rows, MLA_NOPE), F32), kr,
                                jnp.zeros((rows, HEAD_PAD - MLA_QK), F32)], axis=1)
    k_ref[...] = (kn + jnp.concatenate([kr_block] * MLA_HEADS, axis=1)).astype(BF16)
    if transpose_v:
        v_ref[...] = (_dot_nt(wv_ref[...], c) + ones_ref[...]).astype(BF16)
    else:
        v_ref[...] = _dot(c, wv_ref[...]).astype(BF16)


def _kkv(ckv2d, kr2d, lw, *, tm, seq=None):
    T = ckv2d.shape[0]
    row = lambda w: pl.BlockSpec((tm, w), lambda i: (i, 0))
    extra_in = []
    if seq is None:
        wv, wv_spec = lw['w_uv'], _full((KV_LORA, MLA_W))
        v_spec, v_shape = row(MLA_W), (T, MLA_W)
    else:
        nb = seq // tm
        wv, wv_spec = lw['w_uv_t'], _full((MLA_VT, KV_LORA))
        v_spec = pl.BlockSpec((None, MLA_VT, tm), lambda i: (i // nb, 0, i % nb))
        v_shape = (T // seq, MLA_VT, seq)
        ones = np.zeros((MLA_HEADS, MLA_VROWS, 1), np.float32)
        ones[:, MLA_V] = 1.0
        extra_in = [(jnp.asarray(ones.reshape(MLA_VT, 1)), _full((MLA_VT, 1)))]
    return pl.pallas_call(
        functools.partial(_kkv_body, transpose_v=seq is not None),
        grid=(T // tm,),
        in_specs=[row(KV_LORA), row(MLA_ROPE), _full((KV_LORA, MLA_QW)), wv_spec, _full((1, MLA_QW))]
        + [sp for _, sp in extra_in],
        out_specs=[row(MLA_QW), v_spec],
        out_shape=[jax.ShapeDtypeStruct((T, MLA_QW), BF16), jax.ShapeDtypeStruct(v_shape, BF16)],
        compiler_params=_params(("arbitrary",)),
        name="kkv_up",
    )(ckv2d, kr2d, lw['w_uk'], wv, lw['k_gain'], *[a for a, _ in extra_in])


def _mla_prompt_body(q_ref, k_ref, vt_ref, o_ref, m_sc, acc_sc, *, tq):
    qi = pl.program_id(1)
    m_sc[...] = jnp.full(m_sc.shape, NEG_INF, F32)
    acc_sc[...] = jnp.zeros(acc_sc.shape, F32)

    def tile(kb, masked):
        start = pl.multiple_of(kb * tq, tq)
        if masked:
            kc = lax.broadcasted_iota(jnp.int32, (tq, tq), 0) // CHUNK
            qc = lax.broadcasted_iota(jnp.int32, (tq, tq), 1) // CHUNK
            visible = kc <= qc

        def scores(h):
            qt = q_ref[h * HEAD_PAD:(h + 1) * HEAD_PAD, :]
            k = k_ref[pl.ds(start, tq), h * HEAD_PAD:(h + 1) * HEAD_PAD]
            st = _dot(k, qt)
            return jnp.where(visible, st, NEG_INF) if masked else st

        def softmax(h, st):
            m_prev = m_sc[h:h + 1, :]
            m_new = jnp.maximum(m_prev, jnp.max(st, axis=0, keepdims=True))
            alpha = jnp.exp2(m_prev - m_new)
            p = jnp.exp2(st - m_new)
            m_sc[h:h + 1, :] = m_new
            return p.astype(BF16), alpha

        def values(h, p, alpha):
            vt = vt_ref[h * MLA_VROWS:(h + 1) * MLA_VROWS, pl.ds(start, tq)]
            acc_sc[h] = alpha * acc_sc[h] + _dot(vt, p)

        st = {0: scores(0), 1: scores(1)}
        pa = {}
        for h in range(MLA_HEADS):
            pa[h] = softmax(h, st.pop(h))
            if h >= 1:
                values(h - 1, *pa.pop(h - 1))
            if h + 2 < MLA_HEADS:
                st[h + 2] = scores(h + 2)
        values(MLA_HEADS - 1, *pa.pop(MLA_HEADS - 1))

    def body(kb, carry):
        tile(kb, False)
        return carry

    lax.fori_loop(0, qi, body, 0)
    tile(qi, True)
    for j in range(MLA_HEADS // 2):
        pair = jnp.concatenate([acc_sc[h, :MLA_V, :] * (1.0 / acc_sc[h, MLA_V:MLA_V + 1, :])
                                for h in (2 * j, 2 * j + 1)], axis=0)
        o_ref[:, j * LANES:(j + 1) * LANES] = pair.T.astype(BF16)


def _mla_prompt(qt, k, vt, *, tq):
    B, S, _ = k.shape
    once = pl.Buffered(1)
    return pl.pallas_call(
        functools.partial(_mla_prompt_body, tq=tq),
        grid=(B, S // tq),
        in_specs=[pl.BlockSpec((None, MLA_QW, tq), lambda b, i: (b, 0, i)),
                  pl.BlockSpec((None, S, MLA_QW), lambda b, i: (b, 0, 0), pipeline_mode=once),
                  pl.BlockSpec((None, MLA_VT, S), lambda b, i: (b, 0, 0), pipeline_mode=once)],
        out_specs=pl.BlockSpec((None, tq, MLA_W), lambda b, i: (b, i, 0)),
        out_shape=jax.ShapeDtypeStruct((B, S, MLA_W), BF16),
        scratch_shapes=[pltpu.VMEM((MLA_HEADS, tq), F32), pltpu.VMEM((MLA_HEADS, MLA_VROWS, tq), F32)],
        compiler_params=_params(("arbitrary", "arbitrary")),
        name="mla_prompt",
    )(qt, k, vt)


def _mla_sample_body(q_ref, kp_ref, vp_ref, kn_ref, vn_ref, o_ref):
    lane = lax.broadcasted_iota(jnp.int32, (1, LANES), 1)
    outs = []
    scores = []
    for h in range(MLA_HEADS):
        hs = slice(h * HEAD_PAD, (h + 1) * HEAD_PAD)
        q = q_ref[:, hs]
        scores.append((_dot_nt(q, kp_ref[:, hs]), _dot_nt(q, kn_ref[:, hs])))
    for h in range(MLA_HEADS):
        ps = slice((h // 2) * LANES, (h // 2 + 1) * LANES)
        v_lanes = (lane // MLA_V) == (h % 2)
        s1, s2 = scores[h]
        m = jnp.maximum(jnp.max(s1, axis=-1, keepdims=True), jnp.max(s2, axis=-1, keepdims=True))
        p1 = jnp.exp2(s1 - m)
        p2 = jnp.exp2(s2 - m)
        l = jnp.sum(p1, axis=-1, keepdims=True) + jnp.sum(p2, axis=-1, keepdims=True)
        v1 = vp_ref[:, ps]
        v2 = vn_ref[:, ps]
        acc = (_dot(p1.astype(BF16), jnp.where(v_lanes, v1, jnp.zeros_like(v1)))
               + _dot(p2.astype(BF16), jnp.where(v_lanes, v2, jnp.zeros_like(v2))))
        outs.append(acc * (1.0 / l))
    for j in range(MLA_HEADS // 2):
        o_ref[:, j * LANES:(j + 1) * LANES] = (outs[2 * j] + outs[2 * j + 1]).astype(BF16)


def _band_scores(q, pieces):
    lane = lax.broadcasted_iota(jnp.int32, (1, BAND_W), 1)
    raw = []
    for h in range(BAND_HEADS):
        qh = jnp.where((lane // BAND_DH) == h, q, jnp.zeros_like(q))
        raw.append([_dot_nt(qh, k) for k, _ in pieces])
    return raw


def _band_head_out(h, raw_h, pieces, biases):
    lane = lax.broadcasted_iota(jnp.int32, (1, BAND_W), 1)
    sel = (lane // BAND_DH) == h
    ss = [s + bias(h) for s, bias in zip(raw_h, biases)]
    m = functools.reduce(jnp.maximum, [jnp.max(s, axis=-1, keepdims=True) for s in ss])
    ps = [jnp.exp2(s - m) for s in ss]
    l = functools.reduce(jnp.add, [jnp.sum(p, axis=-1, keepdims=True) for p in ps])
    acc = functools.reduce(jnp.add, [
        _dot(p.astype(BF16), jnp.where(sel, v, jnp.zeros_like(v))) for p, (_, v) in zip(ps, pieces)])
    return acc * (1.0 / l)


def _band_heads(q, pieces, biases, out_dtype):
    raw = _band_scores(q, pieces)
    out = functools.reduce(jnp.add, [_band_head_out(h, raw[h], pieces, biases) for h in range(BAND_HEADS)])
    return out.astype(out_dtype)


def _band_ret_prompt_body(q_ref, kp_ref, kc_ref, vp_ref, vc_ref, bp_ref, bc_ref,
                          rq_ref, rk_ref, rv_ref, d_ref, xi_ref, zeta_ref, gam_ref, bd_ref, gain_ref,
                          o_ref, ro_ref, sfin_ref, s_sc, *, c):
    @pl.when(pl.program_id(1) == 0)
    def _():
        s_sc[...] = jnp.zeros(s_sc.shape, F32)

    tq = q_ref.shape[0]
    hq = tq // 2
    assert c == hq
    halves = []
    for j, (prev_keys, cur_keys) in enumerate([(slice(0, tq), slice(0, hq)), (slice(hq, tq), slice(0, tq))]):
        rows = slice(j * hq, (j + 1) * hq)
        pieces = [(kp_ref[prev_keys, :], vp_ref[prev_keys, :]), (kc_ref[cur_keys, :], vc_ref[cur_keys, :])]
        biases = [lambda h, r=rows, k=prev_keys: bp_ref[h, r, k], lambda h, r=rows, k=cur_keys: bc_ref[h, r, k]]
        halves.append((rows, pieces, biases, _band_scores(q_ref[rows, :], pieces)))
    state = s_sc[...]
    for rows, pieces, biases, raw in halves:
        o, state = _ret_chunk(rq_ref[rows, :], rk_ref[rows, :], rv_ref[rows, :], state,
                              d_ref, xi_ref, zeta_ref, gam_ref, bd_ref, gain_ref)
        ro_ref[rows, :] = o
        out = functools.reduce(jnp.add, [_band_head_out(h, raw[h], pieces, biases) for h in range(BAND_HEADS)])
        o_ref[rows, :] = out.astype(o_ref.dtype)
    s_sc[...] = state
    sfin_ref[...] = state


def _band_ret_prompt(q, k, v, bias_prev, bias_cur, rq, rk, rv, rt, consts, gain, *, tq, c):
    B, S, _ = q.shape
    cur = pl.BlockSpec((None, tq, BAND_W), lambda b, i: (b, i, 0))
    prev = pl.BlockSpec((None, tq, BAND_W), lambda b, i: (b, jnp.maximum(i - 1, 0), 0))
    bias = _full((BAND_HEADS, tq, tq))
    bias_prev_spec = pl.BlockSpec((None, BAND_HEADS, tq, tq), lambda b, i: (jnp.minimum(i, 1), 0, 0, 0))
    st = pl.BlockSpec((None, RET_W, RET_W), lambda b, i: (b, 0, 0))
    return pl.pallas_call(
        functools.partial(_band_ret_prompt_body, c=c),
        grid=(B, S // tq),
        in_specs=[cur, prev, cur, prev, cur, bias_prev_spec, bias, cur, cur, cur,
                  _full((RET_HEADS, c, c)), _full((c, RET_W)), _full((c, RET_W)),
                  _full((RET_W, RET_W)), _full((RET_W, RET_W)), _full((1, RET_W))],
        out_specs=[cur, cur, st],
        out_shape=[jax.ShapeDtypeStruct((B, S, BAND_W), BF16), jax.ShapeDtypeStruct((B, S, RET_W), BF16),
                   jax.ShapeDtypeStruct((B, RET_W, RET_W), F32)],
        scratch_shapes=[pltpu.VMEM((RET_W, RET_W), F32)],
        compiler_params=_params(("arbitrary", "arbitrary")),
        name="band_ret_prompt",
    )(q, k, k, v, v, bias_prev, bias_cur, rq, rk, rv,
      rt['decay'], rt['xi'], rt['zeta'], rt['gamma'], consts['bd_mask'], gain)


def _band_sample_body(q_ref, kp_ref, vp_ref, kn_ref, vn_ref, bp_ref, bn_ref, o_ref):
    o_ref[...] = _band_heads(q_ref[...],
                             [(kp_ref[...].astype(BF16), vp_ref[...].astype(BF16)), (kn_ref[...], vn_ref[...])],
                             [lambda h: bp_ref[h], lambda h: bn_ref[h]], BF16)


def _ret_chunk(q, k, v, state, d_ref, xi_ref, zeta_ref, gam_ref, bd_ref, gain_ref):
    lane = lax.broadcasted_iota(jnp.int32, (1, RET_W), 1)
    o = _dot(q, state.astype(BF16)) * xi_ref[...]
    for h in range(RET_HEADS):
        sel = (lane // RET_DV) == h
        a = _dot_nt(jnp.where(sel, q, jnp.zeros_like(q)), k) * d_ref[h]
        o = o + _dot(a.astype(BF16), jnp.where(sel, v, jnp.zeros_like(v)))
    kz = (k.astype(F32) * zeta_ref[...]).astype(BF16)
    s_new = gam_ref[...] * state + bd_ref[...] * _dot_tn(kz, v)
    inv = _group64_inv_rms(o, bd_ref[...].astype(BF16), 1.0 / RET_DV)
    return (o * inv * gain_ref[...]).astype(BF16), s_new


def _sample_mixers_body(mq_ref, mkp_ref, mvp_ref, mkn_ref, mvn_ref,
                        bq_ref, bkp_ref, bvp_ref, bkn_ref, bvn_ref, bp_ref, bn_ref,
                        rq_ref, rk_ref, rv_ref, s0_ref, d_ref, xi_ref, zeta_ref, gam_ref, bd_ref, gain_ref,
                        mo_ref, bo_ref, ro_ref, sfin_ref):
    _mla_sample_body(mq_ref, mkp_ref, mvp_ref, mkn_ref, mvn_ref, mo_ref)
    _band_sample_body(bq_ref, bkp_ref, bvp_ref, bkn_ref, bvn_ref, bp_ref, bn_ref, bo_ref)
    s0 = s0_ref[...]
    state = jnp.concatenate([
        jnp.concatenate([s0[h] if g == h else jnp.zeros((RET_DK, RET_DV), F32) for g in range(RET_HEADS)], axis=1)
        for h in range(RET_HEADS)], axis=0)
    o, s_new = _ret_chunk(rq_ref[...], rk_ref[...], rv_ref[...], state,
                          d_ref, xi_ref, zeta_ref, gam_ref, bd_ref, gain_ref)
    ro_ref[...] = o
    for h in range(RET_HEADS):
        sfin_ref[h] = s_new[h * RET_DK:(h + 1) * RET_DK, h * RET_DV:(h + 1) * RET_DV]


def _sample_mixers(mq, mkp, mvp, mkn, mvn, bq, bkp, bvp, bkn, bvn, bias_past, bias_new,
                   rq, rk, rv, s0, rt, consts, gain, *, layer):
    B, L, _ = mq.shape
    P, PB = mkp.shape[1], bkp.shape[1]
    blk = lambda n, w: pl.BlockSpec((None, n, w), lambda b: (b, 0, 0))
    return pl.pallas_call(
        _sample_mixers_body,
        grid=(B,),
        in_specs=[blk(L, MLA_QW), blk(P, MLA_QW), blk(P, MLA_W), blk(L, MLA_QW), blk(L, MLA_W),
                  blk(L, BAND_W), blk(PB, BAND_W), blk(PB, BAND_W), blk(L, BAND_W), blk(L, BAND_W),
                  _full((BAND_HEADS, L, PB)), _full((BAND_HEADS, L, L)),
                  blk(L, RET_W), blk(L, RET_W), blk(L, RET_W),
                  pl.BlockSpec((None, None, RET_HEADS, RET_DK, RET_DV), lambda b: (layer, b, 0, 0, 0)),
                  _full((RET_HEADS, L, L)), _full((L, RET_W)), _full((L, RET_W)),
                  _full((RET_W, RET_W)), _full((RET_W, RET_W)), _full((1, RET_W))],
        out_specs=[blk(L, MLA_W), blk(L, BAND_W), blk(L, RET_W),
                   pl.BlockSpec((None, RET_HEADS, RET_DK, RET_DV), lambda b: (b, 0, 0, 0))],
        out_shape=[jax.ShapeDtypeStruct((B, L, MLA_W), BF16), jax.ShapeDtypeStruct((B, L, BAND_W), BF16),
                   jax.ShapeDtypeStruct((B, L, RET_W), BF16),
                   jax.ShapeDtypeStruct((B, RET_HEADS, RET_DK, RET_DV), F32)],
        compiler_params=_params(("arbitrary",)),
        name="sample_mixers",
    )(mq, mkp, mvp, mkn, mvn, bq, bkp, bvp, bkn, bvn, bias_past, bias_new,
      rq, rk, rv, s0, rt['decay'], rt['xi'], rt['zeta'], rt['gamma'], consts['bd_mask'], gain)


def _kout_body(x_ref, ro_ref, mo_ref, bo_ref, g_ref, w_ref, y_ref):
    g = g_ref[...]
    y = x_ref[...]
    y = y + _dot(ro_ref[...] * g[:, :RET_W], w_ref[:RET_W, :])
    y = y + _dot(mo_ref[...] * g[:, RET_W:RET_W + MLA_W], w_ref[RET_W:RET_W + MLA_W, :])
    y = y + _dot(bo_ref[...] * g[:, RET_W + MLA_W:], w_ref[RET_W + MLA_W:, :])
    y_ref[...] = y


def _kout(x2d, ro, mo, bo, gate, w_out, *, tm):
    T = x2d.shape[0]
    row = lambda w: pl.BlockSpec((tm, w), lambda i: (i, 0))
    return pl.pallas_call(
        _kout_body,
        grid=(T // tm,),
        in_specs=[row(D_MODEL), row(RET_W), row(MLA_W), row(BAND_W), row(D_MIX), _full((D_MIX, D_MODEL))],
        out_specs=row(D_MODEL),
        out_shape=jax.ShapeDtypeStruct((T, D_MODEL), F32),
        compiler_params=_params(("arbitrary",)),
        name="kout_proj",
    )(x2d, ro, mo, bo, gate, w_out)


def _constants():
    bd = np.kron(np.eye(RET_HEADS, dtype=np.float32), np.ones((RET_DK, RET_DV), np.float32))
    invn_q = np.tile(np.repeat([1.0 / MLA_NOPE, 1.0 / MLA_ROPE], HEAD_PAD // 2), MLA_HEADS)[None]
    return {'bd_mask': jnp.asarray(bd), 'bd64': jnp.asarray(bd, BF16), 'invn_q': jnp.asarray(invn_q, F32)}


def _rope_tables(pos, reps=1):
    pos = np.asarray(pos, np.float64)

    def cs(half):
        inv = ROPE_BASE ** (-np.arange(half, dtype=np.float64) / half)
        ang = pos[:, None] * inv[None, :]
        c, s = np.cos(ang), np.sin(ang)
        return np.concatenate([c, c], axis=-1), np.concatenate([-s, s], axis=-1)

    T = pos.shape[0]
    c32, s32 = cs(RET_DK // 2)
    c16, s16 = cs(MLA_ROPE // 2)
    ones = lambda w: np.ones((T, w))
    zeros = lambda w: np.zeros((T, w))
    pad = HEAD_PAD - MLA_QK
    tabs = {
        'cret': np.tile(c32, (1, RET_HEADS)), 'sret': np.tile(s32, (1, RET_HEADS)),
        'cq': np.concatenate([ones(MLA_NOPE), c16, ones(pad)], axis=-1),
        'sq': np.concatenate([zeros(MLA_NOPE), s16, zeros(pad)], axis=-1),
        'ckr': np.concatenate([c16, ones(LANES - MLA_ROPE)], axis=-1),
        'skr': np.concatenate([s16, zeros(LANES - MLA_ROPE)], axis=-1),
    }
    out = {k: jnp.asarray(np.tile(v, (reps, 1)), F32) for k, v in tabs.items()}
    if reps == 1:
        out['cq_t'], out['sq_t'] = jnp.asarray(tabs['cq'].T, F32), jnp.asarray(tabs['sq'].T, F32)
    return out


def _retention_tables(c):
    lg = np.log1p(-np.exp2(-5.0 - np.arange(RET_HEADS, dtype=np.float64)))
    idx = np.arange(c, dtype=np.float64)
    diff = idx[:, None] - idx[None, :]
    decay = np.where(diff >= 0, np.exp(lg[:, None, None] * np.maximum(diff, 0.0)), 0.0)
    per_lane = lambda t: np.repeat(t, RET_DV, axis=-1)
    xi = per_lane(np.exp(lg[None, :] * (idx[:, None] + 1.0)))
    zeta = per_lane(np.exp(lg[None, :] * (c - 1.0 - idx)[:, None]))
    gamma = np.broadcast_to(per_lane(np.exp(lg * c)[None, :]).T, (RET_W, RET_W))
    return {k: jnp.asarray(v, F32) for k, v in
            {'decay': decay, 'xi': xi, 'zeta': zeta, 'gamma': gamma}.items()}


def _band_tables_body(rp_ref, rc_ref, prev_ref, cur_ref, past_ref, new_ref, *, tq, n_new):
    w = 2 * tq
    tp = pltpu.roll(jnp.broadcast_to(rp_ref[...], (tq, w)), 0, 1, stride=1, stride_axis=0)[:, :tq]
    tc = pltpu.roll(jnp.broadcast_to(rc_ref[...], (tq, w)), 0, 1, stride=1, stride_axis=0)[:, :tq]
    qc = lax.broadcasted_iota(jnp.int32, (tq, tq), 0) // CHUNK
    kc = lax.broadcasted_iota(jnp.int32, (tq, tq), 1) // CHUNK
    prev_ref[0] = jnp.full((tq, tq), NEG_INF, F32)
    prev_ref[1] = jnp.where(kc >= qc, tp, NEG_INF)
    cur_ref[...] = jnp.where(kc <= qc, tc, NEG_INF)
    past_ref[...] = tp[:n_new, :]
    new_ref[...] = tc[:n_new, :n_new]


def _band_tables(band_bias, *, tq, n_new):
    H = band_bias.shape[0]
    band_bias = band_bias * LOG2E
    lo, mid, hi = band_bias[:, :1], band_bias[:, MAX_REL + 1:2 * MAX_REL], band_bias[:, 2 * MAX_REL:]
    rep = lambda col, n: jnp.broadcast_to(col, (H, n))
    r_cur = jnp.concatenate([band_bias[:, MAX_REL::-1], rep(lo, tq - MAX_REL - 1), rep(hi, tq - MAX_REL),
                             band_bias[:, :MAX_REL:-1]], axis=1)
    r_prev = jnp.concatenate([rep(hi, tq - MAX_REL + 1), mid[:, ::-1], rep(hi, tq)], axis=1)
    row = pl.BlockSpec((None, 1, 2 * tq), lambda h: (h, 0, 0))
    tile = lambda n, m: pl.BlockSpec((None, n, m), lambda h: (h, 0, 0))
    return pl.pallas_call(
        functools.partial(_band_tables_body, tq=tq, n_new=n_new),
        grid=(H,),
        in_specs=[row, row],
        out_specs=[pl.BlockSpec((2, None, tq, tq), lambda h: (0, h, 0, 0)), tile(tq, tq), tile(n_new, tq),
                   tile(n_new, n_new)],
        out_shape=[jax.ShapeDtypeStruct((2, H, tq, tq), F32), jax.ShapeDtypeStruct((H, tq, tq), F32),
                   jax.ShapeDtypeStruct((H, n_new, tq), F32), jax.ShapeDtypeStruct((H, n_new, n_new), F32)],
        compiler_params=_params(("arbitrary",)),
        name="band_tables",
    )(r_prev[:, None, :], r_cur[:, None, :])


def _layer_weights(l, norm_g, w_in, ret_gn_g, mla_qa_g, mla_w_uq, mla_qn_g, mla_qr_g, mla_kva_g, mla_kr_g,
                   mla_w_ukv, mla_kn_g, band_qn_g, band_kn_g, w_out):
    cuts = np.cumsum(SEG)[:-1].tolist()
    a_q, a_k, a_v, a_g, b_cq, b_ckv, b_kr, b_g, c_q, c_k, c_v, c_g = jnp.split(w_in[l], cuts, axis=-1)
    b_kr = jnp.pad(b_kr, ((0, 0), (0, LANES - MLA_ROPE)))
    w_in_p = jnp.concatenate([a_q, a_k * (RET_DK ** -0.5), a_v, b_cq, b_ckv, b_kr, c_q, c_k, c_v, a_g, b_g, c_g],
                             axis=-1).astype(BF16)
    pad = HEAD_PAD - MLA_QK
    w_uq = jnp.pad(mla_w_uq[l].reshape(Q_LORA, MLA_HEADS, MLA_QK), ((0, 0), (0, 0), (0, pad)))
    half = MLA_ROPE // 2

    def swap_rope(t):
        return jnp.concatenate([t[..., :MLA_NOPE], t[..., MLA_NOPE + half:MLA_QK], t[..., MLA_NOPE:MLA_NOPE + half],
                                t[..., MLA_QK:]], axis=-1)
    ukv = mla_w_ukv[l].reshape(KV_LORA, MLA_HEADS, MLA_NOPE + MLA_V)
    w_uk = jnp.pad(ukv[:, :, :MLA_NOPE], ((0, 0), (0, 0), (0, HEAD_PAD - MLA_NOPE)))
    zpad = jnp.zeros((pad,), F32)
    q_gain_head = jnp.concatenate([mla_qn_g[l], mla_qr_g[l], zpad]) * (MLA_QK ** -0.5 * LOG2E)
    q_gain = jnp.tile(q_gain_head, MLA_HEADS)
    q_gain_sw = jnp.tile(swap_rope(q_gain_head), MLA_HEADS)
    k_gain = jnp.tile(jnp.concatenate([mla_kn_g[l], jnp.zeros((HEAD_PAD - MLA_NOPE,), F32)]), MLA_HEADS)
    return {
        'norm_g': norm_g[l][None], 'w_in': w_in_p,
        'w_uq': w_uq.reshape(Q_LORA, MLA_QW).astype(BF16), 'qa_g': mla_qa_g[l][None],
        'w_uq_sw': swap_rope(w_uq).reshape(Q_LORA, MLA_QW).astype(BF16), 'q_gain_sw': q_gain_sw[None],
        'w_uq_t': w_uq.reshape(Q_LORA, MLA_QW).T.astype(BF16),
        'q_gain_col': jnp.broadcast_to(q_gain[:, None], (MLA_QW, LANES)),
        'kva_g': mla_kva_g[l][None],
        'kr_g': jnp.concatenate([mla_kr_g[l], jnp.zeros((LANES - MLA_ROPE,), F32)])[None],
        'q_gain': q_gain[None],
        'bq_gain': (jnp.tile(band_qn_g[l], BAND_HEADS) * (BAND_DH ** -0.5 * LOG2E))[None],
        'bk_gain': jnp.tile(band_kn_g[l], BAND_HEADS)[None],
        'w_uk': w_uk.reshape(KV_LORA, MLA_QW).astype(BF16),
        'w_uv': ukv[:, :, MLA_NOPE:].reshape(KV_LORA, MLA_W).astype(BF16),
        'w_uv_t': jnp.pad(ukv[:, :, MLA_NOPE:].transpose(1, 2, 0), ((0, 0), (0, MLA_VROWS - MLA_V), (0, 0))
                          ).reshape(MLA_VT, KV_LORA).astype(BF16),
        'k_gain': k_gain[None], 'ret_gain': ret_gn_g[l][None], 'w_out': w_out[l].astype(BF16),
    }


def _diag_blocks(s_bd):
    B = s_bd.shape[0]
    s = s_bd.reshape(B, RET_HEADS, RET_DK, RET_HEADS, RET_DV)
    return jnp.stack([s[:, h, :, h, :] for h in range(RET_HEADS)], axis=1)


def kernel(x_prompt, x_sample, state_ret, cache_mla_ckv, cache_mla_krope, cache_band_k, cache_band_v, norm_g, w_in, ret_gn_g, mla_qa_g, mla_w_uq, mla_qn_g, mla_qr_g, mla_kva_g, mla_kr_g, mla_w_ukv, mla_kn_g, band_qn_g, band_kn_g, band_bias, w_out):
    B, S, _ = x_prompt.shape
    DB, L, _ = x_sample.shape
    past_len = cache_mla_ckv.shape[2]
    n_band_past = cache_band_k.shape[2]
    n_keep_p = min(BAND_PAST, S)

    TM_P, TM_S, TM_KV, TQ, RET_C = 512, 256, 1024, 512, 256
    consts = _constants()
    tabs_p = _rope_tables(np.arange(S))
    tabs_s = _rope_tables(past_len + np.arange(L), reps=DB)
    rt_p = _retention_tables(RET_C)
    rt_s = _retention_tables(L)

    xp = x_prompt.reshape(B * S, D_MODEL)
    xs = x_sample.reshape(DB * L, D_MODEL)
    p_st, s_st = [], []
    for l in range(DEPTH):
        lw = _layer_weights(l, norm_g, w_in, ret_gn_g, mla_qa_g, mla_w_uq, mla_qn_g, mla_qr_g, mla_kva_g,
                            mla_kr_g, mla_w_ukv, mla_kn_g, band_qn_g, band_kn_g, w_out)

        rq, rk, rv, gate, qm, ckv, kr, bq, bk, bv, bks, bvs = _k1(
            xp, lw, tabs_p, consts, tm=TM_P, rows_per_seq=S, n_keep=n_keep_p, q_transposed=True)
        km, vt = _kkv(ckv, kr, lw, tm=TM_KV, seq=S)
        mla_o = _mla_prompt(qm, km.reshape(B, S, MLA_QW), vt, tq=TQ)
        assert TQ == BAND_PAST == n_band_past
        bias_prev, bias_cur, bias_past, bias_new = _band_tables(band_bias[l], tq=TQ, n_new=L)
        band_o, ret_o, ret_s = _band_ret_prompt(
            bq.reshape(B, S, BAND_W), bk.reshape(B, S, BAND_W), bv.reshape(B, S, BAND_W), bias_prev, bias_cur,
            rq.reshape(B, S, RET_W), rk.reshape(B, S, RET_W), rv.reshape(B, S, RET_W),
            rt_p, consts, lw['ret_gain'], tq=TQ, c=RET_C)
        xp = _kout(xp, ret_o.reshape(B * S, RET_W), mla_o.reshape(B * S, MLA_W), band_o.reshape(B * S, BAND_W),
                   gate, lw['w_out'], tm=TM_P)
        bks, bvs = _to_heads(bks, bvs, tm=TM_S)
        p_st.append((_diag_blocks(ret_s), ckv.reshape(B, S, KV_LORA), kr.reshape(B, S, MLA_ROPE),
                     bks.reshape(B, n_keep_p, BAND_HEADS, BAND_DH), bvs.reshape(B, n_keep_p, BAND_HEADS, BAND_DH)))

        rq, rk, rv, gate, qm, ckv, kr, bq, bk, bv, bks, bvs = _k1(
            xs, lw, tabs_s, consts, tm=TM_S, rows_per_seq=DB * L, n_keep=DB * L)
        kn, vn = _kkv(ckv, kr, lw, tm=TM_S)
        kp, vp = _kkv(cache_mla_ckv[l].reshape(DB * past_len, KV_LORA),
                      cache_mla_krope[l].reshape(DB * past_len, MLA_ROPE), lw, tm=TM_KV)
        mla_o, band_o, ret_o, ret_s = _sample_mixers(
            qm.reshape(DB, L, MLA_QW), kp.reshape(DB, past_len, MLA_QW), vp.reshape(DB, past_len, MLA_W),
            kn.reshape(DB, L, MLA_QW), vn.reshape(DB, L, MLA_W),
            bq.reshape(DB, L, BAND_W), cache_band_k[l].reshape(DB, n_band_past, BAND_W),
            cache_band_v[l].reshape(DB, n_band_past, BAND_W), bk.reshape(DB, L, BAND_W), bv.reshape(DB, L, BAND_W),
            bias_past, bias_new,
            rq.reshape(DB, L, RET_W), rk.reshape(DB, L, RET_W), rv.reshape(DB, L, RET_W),
            state_ret, rt_s, consts, lw['ret_gain'], layer=l)
        xs = _kout(xs, ret_o.reshape(DB * L, RET_W), mla_o.reshape(DB * L, MLA_W), band_o.reshape(DB * L, BAND_W),
                   gate, lw['w_out'], tm=TM_S)
        bks, bvs = _to_heads(bks, bvs, tm=TM_S)
        s_st.append((ret_s, ckv.reshape(DB, L, KV_LORA), kr.reshape(DB, L, MLA_ROPE),
                     bks.reshape(DB, L, BAND_HEADS, BAND_DH), bvs.reshape(DB, L, BAND_HEADS, BAND_DH)))

    stack = lambda sts, i: jnp.stack([s[i] for s in sts])
    return (xp.reshape(B, S, D_MODEL), xs.reshape(DB, L, D_MODEL),
            stack(p_st, 0), stack(p_st, 1), stack(p_st, 2), stack(p_st, 3), stack(p_st, 4),
            stack(s_st, 0), stack(s_st, 1), stack(s_st, 2), stack(s_st, 3), stack(s_st, 4))
```

```python
import functools

import jax
import jax.numpy as jnp
import numpy as np
from jax import lax
from jax.experimental import pallas as pl
from jax.experimental.pallas import tpu as pltpu

F32 = jnp.float32
BF16 = jnp.bfloat16

D_MODEL = 1024
DEPTH = 2
CHUNK = 64
EPS = 1e-6
NEG_INF = -1e30
ROPE_BASE = 10000.0
LOG2E = 1.4426950408889634
RET_HEADS, RET_DK, RET_DV = 4, 64, 64
RET_W = RET_HEADS * RET_DV
MLA_HEADS, MLA_NOPE, MLA_ROPE, MLA_V = 8, 64, 32, 64
MLA_QK = MLA_NOPE + MLA_ROPE
Q_LORA, KV_LORA = 256, 128
MLA_W = MLA_HEADS * MLA_V
BAND_HEADS, BAND_DH = 4, 64
BAND_W = BAND_HEADS * BAND_DH
BAND_PREV_CHUNKS = 8
BAND_PAST = BAND_PREV_CHUNKS * CHUNK
MAX_REL = 128
D_MIX = RET_W + MLA_W + BAND_W
SEG = (RET_HEADS * RET_DK, RET_HEADS * RET_DK, RET_W, RET_W, Q_LORA, KV_LORA, MLA_ROPE, MLA_W,
       BAND_W, BAND_W, BAND_W, BAND_W)

LANES = 128
HEAD_PAD = LANES
MLA_QW = MLA_HEADS * HEAD_PAD
MLA_VROWS = MLA_V + 16
MLA_VT = MLA_HEADS * MLA_VROWS
C_AQ, C_AK, C_AV, C_CQ, C_CKV, C_KR, C_BQ, C_BK, C_BV, C_G, C_END = (
    0, 256, 512, 768, 1024, 1152, 1280, 1536, 1792, 2048, 3072)
VMEM_LIMIT = 56 * 1024 * 1024
K1_SUB = 256


def _dot(a, b):
    return jnp.dot(a, b, preferred_element_type=F32)


def _dot_nt(a, b):
    return lax.dot_general(a, b, (((1,), (1,)), ((), ())), preferred_element_type=F32)


def _dot_tn(a, b):
    return lax.dot_general(a, b, (((0,), (0,)), ((), ())), preferred_element_type=F32)


def _group_sums(x, split, two_groups):
    lane = lax.broadcasted_iota(jnp.int32, (1, LANES), 1)
    lo = lane < split
    sums = []
    for b in range(x.shape[-1] // LANES):
        xb = x[:, b * LANES:(b + 1) * LANES]
        sq = xb * xb
        if two_groups:
            sums.append((jnp.sum(jnp.where(lo, sq, 0.0), axis=-1, keepdims=True),
                         jnp.sum(jnp.where(lo, 0.0, sq), axis=-1, keepdims=True)))
        else:
            sums.append((jnp.sum(sq, axis=-1, keepdims=True), None))
    return sums


def _group_normalize(x, sums, split, n_lo, n_hi):
    lane = lax.broadcasted_iota(jnp.int32, (1, LANES), 1)
    lo = lane < split
    outs = []
    for b, (s_lo, s_hi) in enumerate(sums):
        inv = lax.rsqrt(s_lo * (1.0 / n_lo) + EPS)
        if n_hi:
            inv = jnp.where(lo, inv, lax.rsqrt(s_hi * (1.0 / n_hi) + EPS))
        outs.append(x[:, b * LANES:(b + 1) * LANES] * inv)
    return jnp.concatenate(outs, axis=1)


def _group_rms_normed(x, split, n_lo, n_hi):
    return _group_normalize(x, _group_sums(x, split, n_hi > 0), split, n_lo, n_hi)


def _group64_inv_rms(x, bd, inv_n):
    sq = (x * x).astype(BF16)
    w = bd.shape[0]
    sums = jnp.concatenate([_dot(sq[:, c * w:(c + 1) * w], bd) for c in range(x.shape[-1] // w)], axis=1)
    return lax.rsqrt(sums * inv_n + EPS)


def _swap_halves(x, half):
    w = x.shape[-1]
    lane = lax.broadcasted_iota(jnp.int32, (1, w), 1)
    nxt = pltpu.roll(x, w - half, axis=1)
    prv = pltpu.roll(x, half, axis=1)
    return jnp.where((lane & half) == 0, nxt, prv)


def _rope(x, cos, sin_signed, half):
    return x * cos + _swap_halves(x, half) * sin_signed


def _silu(g):
    return g * (1.0 / (1.0 + jnp.exp(-g)))


def _full(shape):
    nd = len(shape)
    return pl.BlockSpec(shape, lambda *_: (0,) * nd)


def _params(sem):
    return pltpu.CompilerParams(dimension_semantics=sem, vmem_limit_bytes=VMEM_LIMIT)


def _k1_body(x_ref, ng_ref, win_ref, wuq_ref, qag_ref, kvag_ref, krg_ref, qgain_ref, bqg_ref, bkg_ref,
             cret_ref, sret_ref, cq_ref, sq_ref, ckr_ref, skr_ref, wuqr_ref, qgainr_ref, bd_ref, invnq_ref,
             rq_ref, rk_ref, rv_ref, gate_ref, qm_ref, ckv_ref, kr_ref, bq_ref, bk_ref, bv_ref,
             bks_ref, bvs_ref, *, sub, q_transposed):
    n_sub = x_ref.shape[0] // sub
    bd = bd_ref[...]
    lane_reps = sub // LANES

    def stage_a(r):
        x = x_ref[r * sub:(r + 1) * sub, :]
        h = x * lax.rsqrt(jnp.mean(x * x, axis=-1, keepdims=True) + EPS) * ng_ref[...]
        hb = h.astype(BF16)

        def seg(lo, hi):
            return _dot(hb, win_ref[:, lo:hi])

        z = {'cq': seg(C_CQ, C_CKV), 'bq': seg(C_BQ, C_BK), 'bk': seg(C_BK, C_BV)}
        cq = z['cq']
        cq = cq * lax.rsqrt(jnp.mean(cq * cq, axis=-1, keepdims=True) + EPS) * qag_ref[...]
        z['aq'], z['ak'] = seg(C_AQ, C_AK), seg(C_AK, C_AV)
        cqb = cq.astype(BF16)
        if q_transposed:
            z['qf'] = _dot_nt(wuq_ref[...], cqb)
        else:
            z['qf'] = _dot(cqb, wuq_ref[...])
            z['qf_sw'] = _dot(cqb, wuqr_ref[...])
        z['bq_inv'] = _group64_inv_rms(z['bq'], bd, 1.0 / BAND_DH)
        z['bk_inv'] = _group64_inv_rms(z['bk'], bd, 1.0 / BAND_DH)
        z['g'] = seg(C_G, C_END)
        if q_transposed:
            groups = z['qf'].reshape(2 * MLA_HEADS, HEAD_PAD // 2, sub)
            ss = jnp.sum(groups * groups, axis=1, keepdims=True)
            is_nope = lax.broadcasted_iota(jnp.int32, (2 * MLA_HEADS, 1, 1), 0) % 2 == 0
            inv = lax.rsqrt(ss * jnp.where(is_nope, 1.0 / MLA_NOPE, 1.0 / MLA_ROPE) + EPS)
            z['qn'] = (groups * inv).reshape(MLA_QW, sub)
        else:
            z['q_inv'] = _group64_inv_rms(z['qf'], bd, invnq_ref[...])
        z['av'], z['ckv'] = seg(C_AV, C_CQ), seg(C_CKV, C_KR)
        z['kr'], z['bv'] = seg(C_KR, C_BQ), seg(C_BV, C_G)
        return z

    def stage_b(r, z):
        rows = slice(r * sub, (r + 1) * sub)
        bq_ref[rows, :] = (z['bq'] * z['bq_inv'] * bqg_ref[...]).astype(BF16)
        bk = z['bk'] * z['bk_inv'] * bkg_ref[...]
        bk_ref[rows, :] = bk.astype(BF16)
        bks_ref[rows, :] = bk
        bv_ref[rows, :] = z['bv'].astype(BF16)
        bvs_ref[rows, :] = z['bv']
        cret, sret = cret_ref[rows, :], sret_ref[rows, :]
        rq_ref[rows, :] = _rope(z['aq'], cret, sret, RET_DK // 2).astype(BF16)
        rk_ref[rows, :] = _rope(z['ak'], cret, sret, RET_DK // 2).astype(BF16)
        rv_ref[rows, :] = z['av'].astype(BF16)
        gate_ref[rows, :] = _silu(z['g']).astype(BF16)
        if q_transposed:
            cq_t = jnp.concatenate([cq_ref[:, rows]] * MLA_HEADS, axis=0)
            sq_t = jnp.concatenate([sq_ref[:, rows]] * MLA_HEADS, axis=0)
            qg = z['qn'] * jnp.concatenate([qgain_ref[...]] * lane_reps, axis=1)
            half = MLA_ROPE // 2
            pieces = []
            for hd in range(MLA_HEADS):
                o = hd * HEAD_PAD
                pieces += [qg[o:o + MLA_NOPE], qg[o + MLA_NOPE + half:o + MLA_QK],
                           qg[o + MLA_NOPE:o + MLA_NOPE + half], qg[o + MLA_QK:o + HEAD_PAD]]
            qm_ref[:, rows] = (qg * cq_t + jnp.concatenate(pieces, axis=0) * sq_t).astype(BF16)
        else:
            cq_t = jnp.concatenate([cq_ref[rows, :]] * MLA_HEADS, axis=1)
            sq_t = jnp.concatenate([sq_ref[rows, :]] * MLA_HEADS, axis=1)
            qm_ref[rows, :] = (z['qf'] * (z['q_inv'] * qgain_ref[...]) * cq_t
                               + z['qf_sw'] * (z['q_inv'] * qgainr_ref[...]) * sq_t).astype(BF16)
        ckv, kr = z['ckv'], z['kr']
        ckv_ref[rows, :] = ckv * lax.rsqrt(jnp.mean(ckv * ckv, axis=-1, keepdims=True) + EPS) * kvag_ref[...]
        kr = kr * lax.rsqrt(jnp.sum(kr * kr, axis=-1, keepdims=True) * (1.0 / MLA_ROPE) + EPS) * krg_ref[...]
        kr = _rope(kr, ckr_ref[rows, :], skr_ref[rows, :], MLA_ROPE // 2)
        kr_ref[rows, :] = kr[:, :MLA_ROPE]

    z = stage_a(0)
    for r in range(n_sub):
        z_next = stage_a(r + 1) if r + 1 < n_sub else None
        stage_b(r, z)
        z = z_next


def _k1(x2d, lw, tabs, consts, *, tm, rows_per_seq, n_keep, q_transposed=False):
    T = x2d.shape[0]
    nb = rows_per_seq // tm
    nkb = n_keep // tm
    n_seq = T // rows_per_seq

    row = lambda w: pl.BlockSpec((tm, w), lambda i: (i, 0))
    tab = lambda w: pl.BlockSpec((tm, w), lambda i: (i % nb, 0))
    keep = pl.BlockSpec((tm, BAND_W), lambda i: ((i // nb) * nkb + jnp.maximum(i % nb - (nb - nkb), 0), 0))

    if q_transposed:
        tab_t = pl.BlockSpec((LANES, tm), lambda i: (0, i % nb))
        col = _full((MLA_QW, LANES))
        unused = (jnp.zeros((1, LANES), F32), _full((1, LANES)))
        q_ins = {'w_uq': (lw['w_uq_t'], _full((MLA_QW, Q_LORA))), 'w_uq_sw': unused,
                 'q_gain': (lw['q_gain_col'], col), 'q_gain_sw': unused,
                 'invn_q': unused, 'cq': (tabs['cq_t'], tab_t), 'sq': (tabs['sq_t'], tab_t)}
        q_out = ((n_seq, MLA_QW, rows_per_seq), BF16,
                 pl.BlockSpec((None, MLA_QW, tm), lambda i: (i // nb, 0, i % nb)))
    else:
        q_ins = {'w_uq': (lw['w_uq'], _full((Q_LORA, MLA_QW))), 'w_uq_sw': (lw['w_uq_sw'], _full((Q_LORA, MLA_QW))),
                 'q_gain': (lw['q_gain'], _full((1, MLA_QW))), 'q_gain_sw': (lw['q_gain_sw'], _full((1, MLA_QW))),
                 'invn_q': (consts['invn_q'], _full((1, MLA_QW))), 'cq': (tabs['cq'], tab(LANES)),
                 'sq': (tabs['sq'], tab(LANES))}
        q_out = ((T, MLA_QW), BF16, row(MLA_QW))
    ins = [
        (x2d, row(D_MODEL)), (lw['norm_g'], _full((1, D_MODEL))), (lw['w_in'], _full((D_MODEL, C_END))),
        q_ins['w_uq'], (lw['qa_g'], _full((1, Q_LORA))),
        (lw['kva_g'], _full((1, KV_LORA))), (lw['kr_g'], _full((1, LANES))),
        q_ins['q_gain'], (lw['bq_gain'], _full((1, BAND_W))),
        (lw['bk_gain'], _full((1, BAND_W))),
        (tabs['cret'], tab(RET_W)), (tabs['sret'], tab(RET_W)), q_ins['cq'],
        q_ins['sq'], (tabs['ckr'], tab(LANES)), (tabs['skr'], tab(LANES)),
        q_ins['w_uq_sw'], q_ins['q_gain_sw'],
        (consts['bd64'], _full((RET_W, RET_W))), q_ins['invn_q'],
    ]
    outs = [
        ((T, RET_W), BF16, row(RET_W)), ((T, RET_W), BF16, row(RET_W)), ((T, RET_W), BF16, row(RET_W)),
        ((T, D_MIX), BF16, row(D_MIX)), q_out,
        ((T, KV_LORA), F32, row(KV_LORA)), ((T, MLA_ROPE), F32, row(MLA_ROPE)),
        ((T, BAND_W), BF16, row(BAND_W)), ((T, BAND_W), BF16, row(BAND_W)), ((T, BAND_W), BF16, row(BAND_W)),
        ((n_seq * n_keep, BAND_W), F32, keep), ((n_seq * n_keep, BAND_W), F32, keep),
    ]
    return pl.pallas_call(
        functools.partial(_k1_body, sub=min(tm, K1_SUB), q_transposed=q_transposed),
        grid=(T // tm,),
        in_specs=[s for _, s in ins],
        out_specs=[s for _, _, s in outs],
        out_shape=[jax.ShapeDtypeStruct(sh, dt) for sh, dt, _ in outs],
        compiler_params=_params(("arbitrary",)),
        name="k1_proj",
    )(*[a for a, _ in ins])


def _to_heads_body(k_ref, v_ref, ko_ref, vo_ref):
    for hd in range(BAND_HEADS):
        lanes = slice(hd * BAND_DH, (hd + 1) * BAND_DH)
        ko_ref[:, hd, :] = k_ref[:, lanes]
        vo_ref[:, hd, :] = v_ref[:, lanes]


def _to_heads(k2d, v2d, *, tm):
    n = k2d.shape[0]
    row = pl.BlockSpec((tm, BAND_W), lambda i: (i, 0))
    head = pl.BlockSpec((tm, BAND_HEADS, BAND_DH), lambda i: (i, 0, 0))
    shape = jax.ShapeDtypeStruct((n, BAND_HEADS, BAND_DH), F32)
    return pl.pallas_call(
        _to_heads_body, grid=(n // tm,), in_specs=[row, row], out_specs=[head, head],
        out_shape=[shape, shape], compiler_params=_params(("arbitrary",)), name="band_state_heads",
    )(k2d, v2d)


def _kkv_body(ckv_ref, kr_ref, wk_ref, wv_ref, kgain_ref, *rest, transpose_v):
    if transpose_v:
        ones_ref, k_ref, v_ref = rest
    else:
        k_ref, v_ref = rest
    c = ckv_ref[...].astype(BF16)
    kn = _dot(c, wk_ref[...])
    kn = _group_rms_normed(kn, LANES, MLA_NOPE, 0) * kgain_ref[...]
    kr = kr_ref[...]
    rows = kr.shape[0]
    kr_block = jnp.concatenate([jnp.zeros((rows, MLA_NOPE), F32), kr,
                                jnp.zeros((rows, HEAD_PAD - MLA_QK), F32)], axis=1)
    k_ref[...] = (kn + jnp.concatenate([kr_block] * MLA_HEADS, axis=1)).astype(BF16)
    if transpose_v:
        v_ref[...] = (_dot_nt(wv_ref[...], c) + ones_ref[...]).astype(BF16)
    else:
        v_ref[...] = _dot(c, wv_ref[...]).astype(BF16)


def _kkv(ckv2d, kr2d, lw, *, tm, seq=None, rows=None, row_start=0):
    T = ckv2d.shape[0] if rows is None else rows
    first = row_start // tm
    row = lambda w: pl.BlockSpec((tm, w), lambda i: (i, 0))
    row_in = lambda w: pl.BlockSpec((tm, w), lambda i: (i + first, 0))
    extra_in = []
    if seq is None:
        wv, wv_spec = lw['w_uv'], _full((KV_LORA, MLA_W))
        v_spec, v_shape = row(MLA_W), (T, MLA_W)
    else:
        nb = seq // tm
        wv, wv_spec = lw['w_uv_t'], _full((MLA_VT, KV_LORA))
        v_spec = pl.BlockSpec((None, MLA_VT, tm), lambda i: (i // nb, 0, i % nb))
        v_shape = (T // seq, MLA_VT, seq)
        ones = np.zeros((MLA_HEADS, MLA_VROWS, 1), np.float32)
        ones[:, MLA_V] = 1.0
        extra_in = [(jnp.asarray(ones.reshape(MLA_VT, 1)), _full((MLA_VT, 1)))]
    return pl.pallas_call(
        functools.partial(_kkv_body, transpose_v=seq is not None),
        grid=(T // tm,),
        in_specs=[row_in(KV_LORA), row_in(MLA_ROPE), _full((KV_LORA, MLA_QW)), wv_spec, _full((1, MLA_QW))]
        + [sp for _, sp in extra_in],
        out_specs=[row(MLA_QW), v_spec],
        out_shape=[jax.ShapeDtypeStruct((T, MLA_QW), BF16), jax.ShapeDtypeStruct(v_shape, BF16)],
        compiler_params=_params(("arbitrary",)),
        name="kkv_up",
    )(ckv2d, kr2d, lw['w_uk'], wv, lw['k_gain'], *[a for a, _ in extra_in])


def _mla_prompt_body(q_ref, k_ref, vt_ref, o_ref, m_sc, acc_sc, *, tq):
    qi = pl.program_id(1)
    m_sc[...] = jnp.full(m_sc.shape, NEG_INF, F32)
    acc_sc[...] = jnp.zeros(acc_sc.shape, F32)

    def tile(kb, masked):
        start = pl.multiple_of(kb * tq, tq)
        if masked:
            kc = lax.broadcasted_iota(jnp.int32, (tq, tq), 0) // CHUNK
            qc = lax.broadcasted_iota(jnp.int32, (tq, tq), 1) // CHUNK
            visible = kc <= qc

        def scores(h):
            qt = q_ref[h * HEAD_PAD:(h + 1) * HEAD_PAD, :]
            k = k_ref[pl.ds(start, tq), h * HEAD_PAD:(h + 1) * HEAD_PAD]
            st = _dot(k, qt)
            return jnp.where(visible, st, NEG_INF) if masked else st

        def softmax(h, st):
            m_prev = m_sc[h:h + 1, :]
            m_new = jnp.maximum(m_prev, jnp.max(st, axis=0, keepdims=True))
            alpha = jnp.exp2(m_prev - m_new)
            p = jnp.exp2(st - m_new)
            m_sc[h:h + 1, :] = m_new
            return p.astype(BF16), alpha

        def values(h, p, alpha):
            vt = vt_ref[h * MLA_VROWS:(h + 1) * MLA_VROWS, pl.ds(start, tq)]
            acc_sc[h] = alpha * acc_sc[h] + _dot(vt, p)

        st = {0: scores(0), 1: scores(1)}
        pa = {}
        for h in range(MLA_HEADS):
            pa[h] = softmax(h, st.pop(h))
            if h >= 1:
                values(h - 1, *pa.pop(h - 1))
            if h + 2 < MLA_HEADS:
                st[h + 2] = scores(h + 2)
        values(MLA_HEADS - 1, *pa.pop(MLA_HEADS - 1))

    def body(kb, carry):
        tile(kb, False)
        return carry

    lax.fori_loop(0, qi, body, 0)
    tile(qi, True)
    for j in range(MLA_HEADS // 2):
        pair = jnp.concatenate([acc_sc[h, :MLA_V, :] * (1.0 / acc_sc[h, MLA_V:MLA_V + 1, :])
                                for h in (2 * j, 2 * j + 1)], axis=0)
        o_ref[:, j * LANES:(j + 1) * LANES] = pair.T.astype(BF16)


def _mla_prompt(qt, k, vt, *, tq):
    B, S, _ = k.shape
    once = pl.Buffered(1)
    return pl.pallas_call(
        functools.partial(_mla_prompt_body, tq=tq),
        grid=(B, S // tq),
        in_specs=[pl.BlockSpec((None, MLA_QW, tq), lambda b, i: (b, 0, i)),
                  pl.BlockSpec((None, S, MLA_QW), lambda b, i: (b, 0, 0), pipeline_mode=once),
                  pl.BlockSpec((None, MLA_VT, S), lambda b, i: (b, 0, 0), pipeline_mode=once)],
        out_specs=pl.BlockSpec((None, tq, MLA_W), lambda b, i: (b, i, 0)),
        out_shape=jax.ShapeDtypeStruct((B, S, MLA_W), BF16),
        scratch_shapes=[pltpu.VMEM((MLA_HEADS, tq), F32), pltpu.VMEM((MLA_HEADS, MLA_VROWS, tq), F32)],
        compiler_params=_params(("arbitrary", "arbitrary")),
        name="mla_prompt",
    )(qt, k, vt)


def _mla_sample_body(q_ref, kp_ref, vp_ref, kn_ref, vn_ref, o_ref):
    lane = lax.broadcasted_iota(jnp.int32, (1, LANES), 1)
    outs = []
    scores = []
    for h in range(MLA_HEADS):
        hs = slice(h * HEAD_PAD, (h + 1) * HEAD_PAD)
        q = q_ref[:, hs]
        scores.append((_dot_nt(q, kp_ref[:, hs]), _dot_nt(q, kn_ref[:, hs])))
    for h in range(MLA_HEADS):
        ps = slice((h // 2) * LANES, (h // 2 + 1) * LANES)
        v_lanes = (lane // MLA_V) == (h % 2)
        s1, s2 = scores[h]
        m = jnp.maximum(jnp.max(s1, axis=-1, keepdims=True), jnp.max(s2, axis=-1, keepdims=True))
        p1 = jnp.exp2(s1 - m)
        p2 = jnp.exp2(s2 - m)
        l = jnp.sum(p1, axis=-1, keepdims=True) + jnp.sum(p2, axis=-1, keepdims=True)
        v1 = vp_ref[:, ps]
        v2 = vn_ref[:, ps]
        acc = (_dot(p1.astype(BF16), jnp.where(v_lanes, v1, jnp.zeros_like(v1)))
               + _dot(p2.astype(BF16), jnp.where(v_lanes, v2, jnp.zeros_like(v2))))
        outs.append(acc * (1.0 / l))
    for j in range(MLA_HEADS // 2):
        o_ref[:, j * LANES:(j + 1) * LANES] = (outs[2 * j] + outs[2 * j + 1]).astype(BF16)


def _band_scores(q, pieces):
    lane = lax.broadcasted_iota(jnp.int32, (1, BAND_W), 1)
    raw = []
    for h in range(BAND_HEADS):
        qh = jnp.where((lane // BAND_DH) == h, q, jnp.zeros_like(q))
        raw.append([_dot_nt(qh, k) for k, _ in pieces])
    return raw


def _band_head_out(h, raw_h, pieces, biases):
    lane = lax.broadcasted_iota(jnp.int32, (1, BAND_W), 1)
    sel = (lane // BAND_DH) == h
    ss = [s + bias(h) for s, bias in zip(raw_h, biases)]
    m = functools.reduce(jnp.maximum, [jnp.max(s, axis=-1, keepdims=True) for s in ss])
    ps = [jnp.exp2(s - m) for s in ss]
    l = functools.reduce(jnp.add, [jnp.sum(p, axis=-1, keepdims=True) for p in ps])
    acc = functools.reduce(jnp.add, [
        _dot(p.astype(BF16), jnp.where(sel, v, jnp.zeros_like(v))) for p, (_, v) in zip(ps, pieces)])
    return acc * (1.0 / l)


def _band_heads(q, pieces, biases, out_dtype):
    raw = _band_scores(q, pieces)
    out = functools.reduce(jnp.add, [_band_head_out(h, raw[h], pieces, biases) for h in range(BAND_HEADS)])
    return out.astype(out_dtype)


def _band_ret_prompt_body(q_ref, kp_ref, kc_ref, vp_ref, vc_ref, bp_ref, bc_ref,
                          rq_ref, rk_ref, rv_ref, d_ref, xi_ref, zeta_ref, gam_ref, bd_ref, gain_ref,
                          o_ref, ro_ref, sfin_ref, s_sc, *, c):
    @pl.when(pl.program_id(1) == 0)
    def _():
        s_sc[...] = jnp.zeros(s_sc.shape, F32)

    tq = q_ref.shape[0]
    hq = tq // 2
    assert c == hq
    halves = []
    for j, (prev_keys, cur_keys) in enumerate([(slice(0, tq), slice(0, hq)), (slice(hq, tq), slice(0, tq))]):
        rows = slice(j * hq, (j + 1) * hq)
        pieces = [(kp_ref[prev_keys, :], vp_ref[prev_keys, :]), (kc_ref[cur_keys, :], vc_ref[cur_keys, :])]
        biases = [lambda h, r=rows, k=prev_keys: bp_ref[h, r, k], lambda h, r=rows, k=cur_keys: bc_ref[h, r, k]]
        halves.append((rows, pieces, biases, _band_scores(q_ref[rows, :], pieces)))
    state = s_sc[...]
    for rows, pieces, biases, raw in halves:
        o, state = _ret_chunk(rq_ref[rows, :], rk_ref[rows, :], rv_ref[rows, :], state,
                              d_ref, xi_ref, zeta_ref, gam_ref, bd_ref, gain_ref)
        ro_ref[rows, :] = o
        out = functools.reduce(jnp.add, [_band_head_out(h, raw[h], pieces, biases) for h in range(BAND_HEADS)])
        o_ref[rows, :] = out.astype(o_ref.dtype)
    s_sc[...] = state
    sfin_ref[...] = state


def _band_ret_prompt(q, k, v, bias_prev, bias_cur, rq, rk, rv, rt, consts, gain, *, tq, c):
    B, S, _ = q.shape
    cur = pl.BlockSpec((None, tq, BAND_W), lambda b, i: (b, i, 0))
    prev = pl.BlockSpec((None, tq, BAND_W), lambda b, i: (b, jnp.maximum(i - 1, 0), 0))
    bias = _full((BAND_HEADS, tq, tq))
    bias_prev_spec = pl.BlockSpec((None, BAND_HEADS, tq, tq), lambda b, i: (jnp.minimum(i, 1), 0, 0, 0))
    st = pl.BlockSpec((None, RET_W, RET_W), lambda b, i: (b, 0, 0))
    return pl.pallas_call(
        functools.partial(_band_ret_prompt_body, c=c),
        grid=(B, S // tq),
        in_specs=[cur, prev, cur, prev, cur, bias_prev_spec, bias, cur, cur, cur,
                  _full((RET_HEADS, c, c)), _full((c, RET_W)), _full((c, RET_W)),
                  _full((RET_W, RET_W)), _full((RET_W, RET_W)), _full((1, RET_W))],
        out_specs=[cur, cur, st],
        out_shape=[jax.ShapeDtypeStruct((B, S, BAND_W), BF16), jax.ShapeDtypeStruct((B, S, RET_W), BF16),
                   jax.ShapeDtypeStruct((B, RET_W, RET_W), F32)],
        scratch_shapes=[pltpu.VMEM((RET_W, RET_W), F32)],
        compiler_params=_params(("arbitrary", "arbitrary")),
        name="band_ret_prompt",
    )(q, k, k, v, v, bias_prev, bias_cur, rq, rk, rv,
      rt['decay'], rt['xi'], rt['zeta'], rt['gamma'], consts['bd_mask'], gain)


def _band_sample_body(q_ref, kp_ref, vp_ref, kn_ref, vn_ref, bp_ref, bn_ref, o_ref):
    o_ref[...] = _band_heads(q_ref[...],
                             [(kp_ref[...].astype(BF16), vp_ref[...].astype(BF16)), (kn_ref[...], vn_ref[...])],
                             [lambda h: bp_ref[h], lambda h: bn_ref[h]], BF16)


def _ret_chunk(q, k, v, state, d_ref, xi_ref, zeta_ref, gam_ref, bd_ref, gain_ref):
    lane = lax.broadcasted_iota(jnp.int32, (1, RET_W), 1)
    o = _dot(q, state.astype(BF16)) * xi_ref[...]
    for h in range(RET_HEADS):
        sel = (lane // RET_DV) == h
        a = _dot_nt(jnp.where(sel, q, jnp.zeros_like(q)), k) * d_ref[h]
        o = o + _dot(a.astype(BF16), jnp.where(sel, v, jnp.zeros_like(v)))
    kz = (k.astype(F32) * zeta_ref[...]).astype(BF16)
    s_new = gam_ref[...] * state + bd_ref[...] * _dot_tn(kz, v)
    inv = _group64_inv_rms(o, bd_ref[...].astype(BF16), 1.0 / RET_DV)
    return (o * inv * gain_ref[...]).astype(BF16), s_new


def _sample_mixers_body(mq_ref, mkp_ref, mvp_ref, mkn_ref, mvn_ref,
                        bq_ref, bkp_ref, bvp_ref, bkn_ref, bvn_ref, bp_ref, bn_ref,
                        rq_ref, rk_ref, rv_ref, s0_ref, d_ref, xi_ref, zeta_ref, gam_ref, bd_ref, gain_ref,
                        mo_ref, bo_ref, ro_ref, sfin_ref):
    _mla_sample_body(mq_ref, mkp_ref, mvp_ref, mkn_ref, mvn_ref, mo_ref)
    _band_sample_body(bq_ref, bkp_ref, bvp_ref, bkn_ref, bvn_ref, bp_ref, bn_ref, bo_ref)
    s0 = s0_ref[...]
    state = jnp.concatenate([
        jnp.concatenate([s0[h] if g == h else jnp.zeros((RET_DK, RET_DV), F32) for g in range(RET_HEADS)], axis=1)
        for h in range(RET_HEADS)], axis=0)
    o, s_new = _ret_chunk(rq_ref[...], rk_ref[...], rv_ref[...], state,
                          d_ref, xi_ref, zeta_ref, gam_ref, bd_ref, gain_ref)
    ro_ref[...] = o
    for h in range(RET_HEADS):
        sfin_ref[h] = s_new[h * RET_DK:(h + 1) * RET_DK, h * RET_DV:(h + 1) * RET_DV]


def _sample_mixers(mq, mkp, mvp, mkn, mvn, bq, bkp, bvp, bkn, bvn, bias_past, bias_new,
                   rq, rk, rv, s0, rt, consts, gain, *, layer):
    B, L, _ = mq.shape
    P, PB = mkp.shape[1], bkp.shape[1]
    blk = lambda n, w: pl.BlockSpec((None, n, w), lambda b: (b, 0, 0))
    return pl.pallas_call(
        _sample_mixers_body,
        grid=(B,),
        in_specs=[blk(L, MLA_QW), blk(P, MLA_QW), blk(P, MLA_W), blk(L, MLA_QW), blk(L, MLA_W),
                  blk(L, BAND_W), blk(PB, BAND_W), blk(PB, BAND_W), blk(L, BAND_W), blk(L, BAND_W),
                  _full((BAND_HEADS, L, PB)), _full((BAND_HEADS, L, L)),
                  blk(L, RET_W), blk(L, RET_W), blk(L, RET_W),
                  pl.BlockSpec((None, None, RET_HEADS, RET_DK, RET_DV), lambda b: (layer, b, 0, 0, 0)),
                  _full((RET_HEADS, L, L)), _full((L, RET_W)), _full((L, RET_W)),
                  _full((RET_W, RET_W)), _full((RET_W, RET_W)), _full((1, RET_W))],
        out_specs=[blk(L, MLA_W), blk(L, BAND_W), blk(L, RET_W),
                   pl.BlockSpec((None, RET_HEADS, RET_DK, RET_DV), lambda b: (b, 0, 0, 0))],
        out_shape=[jax.ShapeDtypeStruct((B, L, MLA_W), BF16), jax.ShapeDtypeStruct((B, L, BAND_W), BF16),
                   jax.ShapeDtypeStruct((B, L, RET_W), BF16),
                   jax.ShapeDtypeStruct((B, RET_HEADS, RET_DK, RET_DV), F32)],
        compiler_params=_params(("arbitrary",)),
        name="sample_mixers",
    )(mq, mkp, mvp, mkn, mvn, bq, bkp, bvp, bkn, bvn, bias_past, bias_new,
      rq, rk, rv, s0, rt['decay'], rt['xi'], rt['zeta'], rt['gamma'], consts['bd_mask'], gain)


def _kout_body(x_ref, ro_ref, mo_ref, bo_ref, g_ref, w_ref, y_ref):
    mix = jnp.concatenate([ro_ref[...], mo_ref[...], bo_ref[...]], axis=1) * g_ref[...]
    y_ref[...] = x_ref[...] + _dot(mix, w_ref[...])


def _kout(x2d, ro, mo, bo, gate, w_out, *, tm):
    T = x2d.shape[0]
    row = lambda w: pl.BlockSpec((tm, w), lambda i: (i, 0))
    return pl.pallas_call(
        _kout_body,
        grid=(T // tm,),
        in_specs=[row(D_MODEL), row(RET_W), row(MLA_W), row(BAND_W), row(D_MIX), _full((D_MIX, D_MODEL))],
        out_specs=row(D_MODEL),
        out_shape=jax.ShapeDtypeStruct((T, D_MODEL), F32),
        compiler_params=_params(("arbitrary",)),
        name="kout_proj",
    )(x2d, ro, mo, bo, gate, w_out)


def _constants():
    bd = np.kron(np.eye(RET_HEADS, dtype=np.float32), np.ones((RET_DK, RET_DV), np.float32))
    invn_q = np.tile(np.repeat([1.0 / MLA_NOPE, 1.0 / MLA_ROPE], HEAD_PAD // 2), MLA_HEADS)[None]
    return {'bd_mask': jnp.asarray(bd), 'bd64': jnp.asarray(bd, BF16), 'invn_q': jnp.asarray(invn_q, F32)}


def _rope_tables(pos, reps=1):
    pos = np.asarray(pos, np.float64)

    def cs(half):
        inv = ROPE_BASE ** (-np.arange(half, dtype=np.float64) / half)
        ang = pos[:, None] * inv[None, :]
        c, s = np.cos(ang), np.sin(ang)
        return np.concatenate([c, c], axis=-1), np.concatenate([-s, s], axis=-1)

    T = pos.shape[0]
    c32, s32 = cs(RET_DK // 2)
    c16, s16 = cs(MLA_ROPE // 2)
    ones = lambda w: np.ones((T, w))
    zeros = lambda w: np.zeros((T, w))
    pad = HEAD_PAD - MLA_QK
    tabs = {
        'cret': np.tile(c32, (1, RET_HEADS)), 'sret': np.tile(s32, (1, RET_HEADS)),
        'cq': np.concatenate([ones(MLA_NOPE), c16, ones(pad)], axis=-1),
        'sq': np.concatenate([zeros(MLA_NOPE), s16, zeros(pad)], axis=-1),
        'ckr': np.concatenate([c16, ones(LANES - MLA_ROPE)], axis=-1),
        'skr': np.concatenate([s16, zeros(LANES - MLA_ROPE)], axis=-1),
    }
    out = {k: jnp.asarray(np.tile(v, (reps, 1)), F32) for k, v in tabs.items()}
    if reps == 1:
        out['cq_t'], out['sq_t'] = jnp.asarray(tabs['cq'].T, F32), jnp.asarray(tabs['sq'].T, F32)
    return out


def _retention_tables(c):
    lg = np.log1p(-np.exp2(-5.0 - np.arange(RET_HEADS, dtype=np.float64)))
    idx = np.arange(c, dtype=np.float64)
    diff = idx[:, None] - idx[None, :]
    decay = np.where(diff >= 0, np.exp(lg[:, None, None] * np.maximum(diff, 0.0)), 0.0)
    per_lane = lambda t: np.repeat(t, RET_DV, axis=-1)
    xi = per_lane(np.exp(lg[None, :] * (idx[:, None] + 1.0)))
    zeta = per_lane(np.exp(lg[None, :] * (c - 1.0 - idx)[:, None]))
    gamma = np.broadcast_to(per_lane(np.exp(lg * c)[None, :]).T, (RET_W, RET_W))
    return {k: jnp.asarray(v, F32) for k, v in
            {'decay': decay, 'xi': xi, 'zeta': zeta, 'gamma': gamma}.items()}


def _band_tables_body(rp_ref, rc_ref, prev_ref, cur_ref, past_ref, new_ref, *, tq, n_new):
    w = 2 * tq
    tp = pltpu.roll(jnp.broadcast_to(rp_ref[...], (tq, w)), 0, 1, stride=1, stride_axis=0)[:, :tq]
    tc = pltpu.roll(jnp.broadcast_to(rc_ref[...], (tq, w)), 0, 1, stride=1, stride_axis=0)[:, :tq]
    qc = lax.broadcasted_iota(jnp.int32, (tq, tq), 0) // CHUNK
    kc = lax.broadcasted_iota(jnp.int32, (tq, tq), 1) // CHUNK
    prev_ref[0] = jnp.full((tq, tq), NEG_INF, F32)
    prev_ref[1] = jnp.where(kc >= qc, tp, NEG_INF)
    cur_ref[...] = jnp.where(kc <= qc, tc, NEG_INF)
    past_ref[...] = tp[:n_new, :]
    new_ref[...] = tc[:n_new, :n_new]


def _band_tables(band_bias, *, tq, n_new):
    H = band_bias.shape[0]
    band_bias = band_bias * LOG2E
    lo, mid, hi = band_bias[:, :1], band_bias[:, MAX_REL + 1:2 * MAX_REL], band_bias[:, 2 * MAX_REL:]
    rep = lambda col, n: jnp.broadcast_to(col, (H, n))
    r_cur = jnp.concatenate([band_bias[:, MAX_REL::-1], rep(lo, tq - MAX_REL - 1), rep(hi, tq - MAX_REL),
                             band_bias[:, :MAX_REL:-1]], axis=1)
    r_prev = jnp.concatenate([rep(hi, tq - MAX_REL + 1), mid[:, ::-1], rep(hi, tq)], axis=1)
    row = pl.BlockSpec((None, 1, 2 * tq), lambda h: (h, 0, 0))
    tile = lambda n, m: pl.BlockSpec((None, n, m), lambda h: (h, 0, 0))
    return pl.pallas_call(
        functools.partial(_band_tables_body, tq=tq, n_new=n_new),
        grid=(H,),
        in_specs=[row, row],
        out_specs=[pl.BlockSpec((2, None, tq, tq), lambda h: (0, h, 0, 0)), tile(tq, tq), tile(n_new, tq),
                   tile(n_new, n_new)],
        out_shape=[jax.ShapeDtypeStruct((2, H, tq, tq), F32), jax.ShapeDtypeStruct((H, tq, tq), F32),
                   jax.ShapeDtypeStruct((H, n_new, tq), F32), jax.ShapeDtypeStruct((H, n_new, n_new), F32)],
        compiler_params=_params(("arbitrary",)),
        name="band_tables",
    )(r_prev[:, None, :], r_cur[:, None, :])


def _layer_weights(l, norm_g, w_in, ret_gn_g, mla_qa_g, mla_w_uq, mla_qn_g, mla_qr_g, mla_kva_g, mla_kr_g,
                   mla_w_ukv, mla_kn_g, band_qn_g, band_kn_g, w_out):
    cuts = np.cumsum(SEG)[:-1].tolist()
    a_q, a_k, a_v, a_g, b_cq, b_ckv, b_kr, b_g, c_q, c_k, c_v, c_g = jnp.split(w_in[l], cuts, axis=-1)
    b_kr = jnp.pad(b_kr, ((0, 0), (0, LANES - MLA_ROPE)))
    w_in_p = jnp.concatenate([a_q, a_k * (RET_DK ** -0.5), a_v, b_cq, b_ckv, b_kr, c_q, c_k, c_v, a_g, b_g, c_g],
                             axis=-1).astype(BF16)
    pad = HEAD_PAD - MLA_QK
    w_uq = jnp.pad(mla_w_uq[l].reshape(Q_LORA, MLA_HEADS, MLA_QK), ((0, 0), (0, 0), (0, pad)))
    half = MLA_ROPE // 2

    def swap_rope(t):
        return jnp.concatenate([t[..., :MLA_NOPE], t[..., MLA_NOPE + half:MLA_QK], t[..., MLA_NOPE:MLA_NOPE + half],
                                t[..., MLA_QK:]], axis=-1)
    ukv = mla_w_ukv[l].reshape(KV_LORA, MLA_HEADS, MLA_NOPE + MLA_V)
    w_uk = jnp.pad(ukv[:, :, :MLA_NOPE], ((0, 0), (0, 0), (0, HEAD_PAD - MLA_NOPE)))
    zpad = jnp.zeros((pad,), F32)
    q_gain_head = jnp.concatenate([mla_qn_g[l], mla_qr_g[l], zpad]) * (MLA_QK ** -0.5 * LOG2E)
    q_gain = jnp.tile(q_gain_head, MLA_HEADS)
    q_gain_sw = jnp.tile(swap_rope(q_gain_head), MLA_HEADS)
    k_gain = jnp.tile(jnp.concatenate([mla_kn_g[l], jnp.zeros((HEAD_PAD - MLA_NOPE,), F32)]), MLA_HEADS)
    return {
        'norm_g': norm_g[l][None], 'w_in': w_in_p,
        'w_uq': w_uq.reshape(Q_LORA, MLA_QW).astype(BF16), 'qa_g': mla_qa_g[l][None],
        'w_uq_sw': swap_rope(w_uq).reshape(Q_LORA, MLA_QW).astype(BF16), 'q_gain_sw': q_gain_sw[None],
        'w_uq_t': w_uq.reshape(Q_LORA, MLA_QW).T.astype(BF16),
        'q_gain_col': jnp.broadcast_to(q_gain[:, None], (MLA_QW, LANES)),
        'kva_g': mla_kva_g[l][None],
        'kr_g': jnp.concatenate([mla_kr_g[l], jnp.zeros((LANES - MLA_ROPE,), F32)])[None],
        'q_gain': q_gain[None],
        'bq_gain': (jnp.tile(band_qn_g[l], BAND_HEADS) * (BAND_DH ** -0.5 * LOG2E))[None],
        'bk_gain': jnp.tile(band_kn_g[l], BAND_HEADS)[None],
        'w_uk': w_uk.reshape(KV_LORA, MLA_QW).astype(BF16),
        'w_uv': ukv[:, :, MLA_NOPE:].reshape(KV_LORA, MLA_W).astype(BF16),
        'w_uv_t': jnp.pad(ukv[:, :, MLA_NOPE:].transpose(1, 2, 0), ((0, 0), (0, MLA_VROWS - MLA_V), (0, 0))
                          ).reshape(MLA_VT, KV_LORA).astype(BF16),
        'k_gain': k_gain[None], 'ret_gain': ret_gn_g[l][None], 'w_out': w_out[l].astype(BF16),
    }


def _diag_blocks(s_bd):
    B = s_bd.shape[0]
    s = s_bd.reshape(B, RET_HEADS, RET_DK, RET_HEADS, RET_DV)
    return jnp.stack([s[:, h, :, h, :] for h in range(RET_HEADS)], axis=1)


def kernel(x_prompt, x_sample, state_ret, cache_mla_ckv, cache_mla_krope, cache_band_k, cache_band_v, norm_g, w_in, ret_gn_g, mla_qa_g, mla_w_uq, mla_qn_g, mla_qr_g, mla_kva_g, mla_kr_g, mla_w_ukv, mla_kn_g, band_qn_g, band_kn_g, band_bias, w_out):
    B, S, _ = x_prompt.shape
    DB, L, _ = x_sample.shape
    past_len = cache_mla_ckv.shape[2]
    n_band_past = cache_band_k.shape[2]
    n_keep_p = min(BAND_PAST, S)

    TM_P, TM_S, TM_KV, TQ, RET_C = 512, 256, 1024, 512, 256
    consts = _constants()
    tabs_p = _rope_tables(np.arange(S))
    tabs_s = _rope_tables(past_len + np.arange(L), reps=DB)
    rt_p = _retention_tables(RET_C)
    rt_s = _retention_tables(L)

    xp = x_prompt.reshape(B * S, D_MODEL)
    xs = x_sample.reshape(DB * L, D_MODEL)
    p_st, s_st = [], []
    for l in range(DEPTH):
        lw = _layer_weights(l, norm_g, w_in, ret_gn_g, mla_qa_g, mla_w_uq, mla_qn_g, mla_qr_g, mla_kva_g,
                            mla_kr_g, mla_w_ukv, mla_kn_g, band_qn_g, band_kn_g, w_out)

        rq, rk, rv, gate, qm, ckv, kr, bq, bk, bv, bks, bvs = _k1(
            xp, lw, tabs_p, consts, tm=TM_P, rows_per_seq=S, n_keep=n_keep_p, q_transposed=True)
        km, vt = _kkv(ckv, kr, lw, tm=TM_KV, seq=S)
        mla_o = _mla_prompt(qm, km.reshape(B, S, MLA_QW), vt, tq=TQ)
        assert TQ == BAND_PAST == n_band_past
        bias_prev, bias_cur, bias_past, bias_new = _band_tables(band_bias[l], tq=TQ, n_new=L)
        band_o, ret_o, ret_s = _band_ret_prompt(
            bq.reshape(B, S, BAND_W), bk.reshape(B, S, BAND_W), bv.reshape(B, S, BAND_W), bias_prev, bias_cur,
            rq.reshape(B, S, RET_W), rk.reshape(B, S, RET_W), rv.reshape(B, S, RET_W),
            rt_p, consts, lw['ret_gain'], tq=TQ, c=RET_C)
        xp = _kout(xp, ret_o.reshape(B * S, RET_W), mla_o.reshape(B * S, MLA_W), band_o.reshape(B * S, BAND_W),
                   gate, lw['w_out'], tm=TM_P)
        bks, bvs = _to_heads(bks, bvs, tm=TM_S)
        p_st.append((_diag_blocks(ret_s), ckv.reshape(B, S, KV_LORA), kr.reshape(B, S, MLA_ROPE),
                     bks.reshape(B, n_keep_p, BAND_HEADS, BAND_DH), bvs.reshape(B, n_keep_p, BAND_HEADS, BAND_DH)))

        rq, rk, rv, gate, qm, ckv, kr, bq, bk, bv, bks, bvs = _k1(
            xs, lw, tabs_s, consts, tm=TM_S, rows_per_seq=DB * L, n_keep=DB * L)
        kn, vn = _kkv(ckv, kr, lw, tm=TM_S)
        kp, vp = _kkv(cache_mla_ckv.reshape(DEPTH * DB * past_len, KV_LORA),
                      cache_mla_krope.reshape(DEPTH * DB * past_len, MLA_ROPE), lw, tm=TM_KV,
                      rows=DB * past_len, row_start=l * DB * past_len)
        mla_o, band_o, ret_o, ret_s = _sample_mixers(
            qm.reshape(DB, L, MLA_QW), kp.reshape(DB, past_len, MLA_QW), vp.reshape(DB, past_len, MLA_W),
            kn.reshape(DB, L, MLA_QW), vn.reshape(DB, L, MLA_W),
            bq.reshape(DB, L, BAND_W), cache_band_k[l].reshape(DB, n_band_past, BAND_W),
            cache_band_v[l].reshape(DB, n_band_past, BAND_W), bk.reshape(DB, L, BAND_W), bv.reshape(DB, L, BAND_W),
            bias_past, bias_new,
            rq.reshape(DB, L, RET_W), rk.reshape(DB, L, RET_W), rv.reshape(DB, L, RET_W),
            state_ret, rt_s, consts, lw['ret_gain'], layer=l)
        xs = _kout(xs, ret_o.reshape(DB * L, RET_W), mla_o.reshape(DB * L, MLA_W), band_o.reshape(DB * L, BAND_W),
                   gate, lw['w_out'], tm=TM_S)
        bks, bvs = _to_heads(bks, bvs, tm=TM_S)
        s_st.append((ret_s, ckv.reshape(DB, L, KV_LORA), kr.reshape(DB, L, MLA_ROPE),
                     bks.reshape(DB, L, BAND_HEADS, BAND_DH), bvs.reshape(DB, L, BAND_HEADS, BAND_DH)))

    stack = lambda sts, i: jnp.stack([s[i] for s in sts])
    return (xp.reshape(B, S, D_MODEL), xs.reshape(DB, L, D_MODEL),
            stack(p_st, 0), stack(p_st, 1), stack(p_st, 2), stack(p_st, 3), stack(p_st, 4),
            stack(s_st, 0), stack(s_st, 1), stack(s_st, 2), stack(s_st, 3), stack(s_st, 4))
```

```python
import functools

import jax
import jax.numpy as jnp
import numpy as np
from jax import lax
from jax.experimental import pallas as pl
from jax.experimental.pallas import tpu as pltpu

F32 = jnp.float32
BF16 = jnp.bfloat16

D_MODEL = 1024
DEPTH = 2
CHUNK = 64
EPS = 1e-6
NEG_INF = -1e30
ROPE_BASE = 10000.0
LOG2E = 1.4426950408889634
RET_HEADS, RET_DK, RET_DV = 4, 64, 64
RET_W = RET_HEADS * RET_DV
MLA_HEADS, MLA_NOPE, MLA_ROPE, MLA_V = 8, 64, 32, 64
MLA_QK = MLA_NOPE + MLA_ROPE
Q_LORA, KV_LORA = 256, 128
MLA_W = MLA_HEADS * MLA_V
BAND_HEADS, BAND_DH = 4, 64
BAND_W = BAND_HEADS * BAND_DH
BAND_PREV_CHUNKS = 8
BAND_PAST = BAND_PREV_CHUNKS * CHUNK
MAX_REL = 128
D_MIX = RET_W + MLA_W + BAND_W
SEG = (RET_HEADS * RET_DK, RET_HEADS * RET_DK, RET_W, RET_W, Q_LORA, KV_LORA, MLA_ROPE, MLA_W,
       BAND_W, BAND_W, BAND_W, BAND_W)

LANES = 128
HEAD_PAD = LANES
MLA_QW = MLA_HEADS * HEAD_PAD
MLA_VROWS = MLA_V + 16
MLA_VT = MLA_HEADS * MLA_VROWS
C_AQ, C_AK, C_AV, C_CQ, C_CKV, C_KR, C_BQ, C_BK, C_BV, C_G, C_END = (
    0, 256, 512, 768, 1024, 1152, 1280, 1536, 1792, 2048, 3072)
VMEM_LIMIT = 56 * 1024 * 1024
K1_SUB = 256


def _dot(a, b):
    return jnp.dot(a, b, preferred_element_type=F32)


def _dot_nt(a, b):
    return lax.dot_general(a, b, (((1,), (1,)), ((), ())), preferred_element_type=F32)


def _dot_tn(a, b):
    return lax.dot_general(a, b, (((0,), (0,)), ((), ())), preferred_element_type=F32)


def _group_sums(x, split, two_groups):
    lane = lax.broadcasted_iota(jnp.int32, (1, LANES), 1)
    lo = lane < split
    sums = []
    for b in range(x.shape[-1] // LANES):
        xb = x[:, b * LANES:(b + 1) * LANES]
        sq = xb * xb
        if two_groups:
            sums.append((jnp.sum(jnp.where(lo, sq, 0.0), axis=-1, keepdims=True),
                         jnp.sum(jnp.where(lo, 0.0, sq), axis=-1, keepdims=True)))
        else:
            sums.append((jnp.sum(sq, axis=-1, keepdims=True), None))
    return sums


def _group_normalize(x, sums, split, n_lo, n_hi):
    lane = lax.broadcasted_iota(jnp.int32, (1, LANES), 1)
    lo = lane < split
    outs = []
    for b, (s_lo, s_hi) in enumerate(sums):
        inv = lax.rsqrt(s_lo * (1.0 / n_lo) + EPS)
        if n_hi:
            inv = jnp.where(lo, inv, lax.rsqrt(s_hi * (1.0 / n_hi) + EPS))
        outs.append(x[:, b * LANES:(b + 1) * LANES] * inv)
    return jnp.concatenate(outs, axis=1)


def _group_rms_normed(x, split, n_lo, n_hi):
    return _group_normalize(x, _group_sums(x, split, n_hi > 0), split, n_lo, n_hi)


def _group64_inv_rms(x, bd, inv_n):
    sq = (x * x).astype(BF16)
    w = bd.shape[0]
    sums = jnp.concatenate([_dot(sq[:, c * w:(c + 1) * w], bd) for c in range(x.shape[-1] // w)], axis=1)
    return lax.rsqrt(sums * inv_n + EPS)


def _swap_halves(x, half):
    w = x.shape[-1]
    lane = lax.broadcasted_iota(jnp.int32, (1, w), 1)
    nxt = pltpu.roll(x, w - half, axis=1)
    prv = pltpu.roll(x, half, axis=1)
    return jnp.where((lane & half) == 0, nxt, prv)


def _rope(x, cos, sin_signed, half):
    return x * cos + _swap_halves(x, half) * sin_signed


def _silu(g):
    return g * (1.0 / (1.0 + jnp.exp(-g)))


def _full(shape):
    nd = len(shape)
    return pl.BlockSpec(shape, lambda *_: (0,) * nd)


def _params(sem):
    return pltpu.CompilerParams(dimension_semantics=sem, vmem_limit_bytes=VMEM_LIMIT)


def _k1_body(x_ref, ng_ref, win_ref, wuq_ref, qag_ref, kvag_ref, krg_ref, qgain_ref, bqg_ref, bkg_ref,
             cret_ref, sret_ref, cq_ref, sq_ref, ckr_ref, skr_ref, wuqr_ref, qgainr_ref, bd_ref, invnq_ref,
             rq_ref, rk_ref, rv_ref, gate_ref, qm_ref, ckv_ref, kr_ref, bq_ref, bk_ref, bv_ref,
             bks_ref, bvs_ref, *, sub, q_transposed):
    n_sub = x_ref.shape[0] // sub
    bd = bd_ref[...]
    lane_reps = sub // LANES

    def stage_a(r):
        x = x_ref[r * sub:(r + 1) * sub, :]
        h = x * lax.rsqrt(jnp.mean(x * x, axis=-1, keepdims=True) + EPS) * ng_ref[...]
        hb = h.astype(BF16)

        def seg(lo, hi):
            return _dot(hb, win_ref[:, lo:hi])

        z = {'cq': seg(C_CQ, C_CKV), 'bq': seg(C_BQ, C_BK), 'bk': seg(C_BK, C_BV)}
        cq = z['cq']
        cq = cq * lax.rsqrt(jnp.mean(cq * cq, axis=-1, keepdims=True) + EPS) * qag_ref[...]
        z['aq'], z['ak'] = seg(C_AQ, C_AK), seg(C_AK, C_AV)
        cqb = cq.astype(BF16)
        if q_transposed:
            z['qf'] = _dot_nt(wuq_ref[...], cqb)
        else:
            z['qf'] = _dot(cqb, wuq_ref[...])
            z['qf_sw'] = _dot(cqb, wuqr_ref[...])
        z['bq_inv'] = _group64_inv_rms(z['bq'], bd, 1.0 / BAND_DH)
        z['bk_inv'] = _group64_inv_rms(z['bk'], bd, 1.0 / BAND_DH)
        z['g'] = seg(C_G, C_END)
        if q_transposed:
            groups = z['qf'].reshape(2 * MLA_HEADS, HEAD_PAD // 2, sub)
            ss = jnp.sum(groups * groups, axis=1, keepdims=True)
            is_nope = lax.broadcasted_iota(jnp.int32, (2 * MLA_HEADS, 1, 1), 0) % 2 == 0
            inv = lax.rsqrt(ss * jnp.where(is_nope, 1.0 / MLA_NOPE, 1.0 / MLA_ROPE) + EPS)
            z['qn'] = (groups * inv).reshape(MLA_QW, sub)
        else:
            z['q_inv'] = _group64_inv_rms(z['qf'], bd, invnq_ref[...])
        z['av'], z['ckv'] = seg(C_AV, C_CQ), seg(C_CKV, C_KR)
        z['kr'], z['bv'] = seg(C_KR, C_BQ), seg(C_BV, C_G)
        return z

    def stage_b(r, z):
        rows = slice(r * sub, (r + 1) * sub)
        bq_ref[rows, :] = (z['bq'] * z['bq_inv'] * bqg_ref[...]).astype(BF16)
        bk = z['bk'] * z['bk_inv'] * bkg_ref[...]
        bk_ref[rows, :] = bk.astype(BF16)
        bks_ref[rows, :] = bk
        bv_ref[rows, :] = z['bv'].astype(BF16)
        bvs_ref[rows, :] = z['bv']
        cret, sret = cret_ref[rows, :], sret_ref[rows, :]
        rq_ref[rows, :] = _rope(z['aq'], cret, sret, RET_DK // 2).astype(BF16)
        rk_ref[rows, :] = _rope(z['ak'], cret, sret, RET_DK // 2).astype(BF16)
        rv_ref[rows, :] = z['av'].astype(BF16)
        gate_ref[rows, :] = _silu(z['g']).astype(BF16)
        if q_transposed:
            cq_t = jnp.concatenate([cq_ref[:, rows]] * MLA_HEADS, axis=0)
            sq_t = jnp.concatenate([sq_ref[:, rows]] * MLA_HEADS, axis=0)
            qg = z['qn'] * jnp.concatenate([qgain_ref[...]] * lane_reps, axis=1)
            half = MLA_ROPE // 2
            pieces = []
            for hd in range(MLA_HEADS):
                o = hd * HEAD_PAD
                pieces += [qg[o:o + MLA_NOPE], qg[o + MLA_NOPE + half:o + MLA_QK],
                           qg[o + MLA_NOPE:o + MLA_NOPE + half], qg[o + MLA_QK:o + HEAD_PAD]]
            qm_ref[:, rows] = (qg * cq_t + jnp.concatenate(pieces, axis=0) * sq_t).astype(BF16)
        else:
            cq_t = jnp.concatenate([cq_ref[rows, :]] * MLA_HEADS, axis=1)
            sq_t = jnp.concatenate([sq_ref[rows, :]] * MLA_HEADS, axis=1)
            qm_ref[rows, :] = (z['qf'] * (z['q_inv'] * qgain_ref[...]) * cq_t
                               + z['qf_sw'] * (z['q_inv'] * qgainr_ref[...]) * sq_t).astype(BF16)
        ckv, kr = z['ckv'], z['kr']
        ckv_ref[rows, :] = ckv * lax.rsqrt(jnp.mean(ckv * ckv, axis=-1, keepdims=True) + EPS) * kvag_ref[...]
        kr = kr * lax.rsqrt(jnp.sum(kr * kr, axis=-1, keepdims=True) * (1.0 / MLA_ROPE) + EPS) * krg_ref[...]
        kr = _rope(kr, ckr_ref[rows, :], skr_ref[rows, :], MLA_ROPE // 2)
        kr_ref[rows, :] = kr[:, :MLA_ROPE]

    z = stage_a(0)
    for r in range(n_sub):
        z_next = stage_a(r + 1) if r + 1 < n_sub else None
        stage_b(r, z)
        z = z_next


def _k1(x2d, lw, tabs, consts, *, tm, rows_per_seq, n_keep, q_transposed=False):
    T = x2d.shape[0]
    nb = rows_per_seq // tm
    nkb = n_keep // tm
    n_seq = T // rows_per_seq

    row = lambda w: pl.BlockSpec((tm, w), lambda i: (i, 0))
    tab = lambda w: pl.BlockSpec((tm, w), lambda i: (i % nb, 0))
    keep = pl.BlockSpec((tm, BAND_W), lambda i: ((i // nb) * nkb + jnp.maximum(i % nb - (nb - nkb), 0), 0))

    if q_transposed:
        tab_t = pl.BlockSpec((LANES, tm), lambda i: (0, i % nb))
        col = _full((MLA_QW, LANES))
        unused = (jnp.zeros((1, LANES), F32), _full((1, LANES)))
        q_ins = {'w_uq': (lw['w_uq_t'], _full((MLA_QW, Q_LORA))), 'w_uq_sw': unused,
                 'q_gain': (lw['q_gain_col'], col), 'q_gain_sw': unused,
                 'invn_q': unused, 'cq': (tabs['cq_t'], tab_t), 'sq': (tabs['sq_t'], tab_t)}
        q_out = ((n_seq, MLA_QW, rows_per_seq), BF16,
                 pl.BlockSpec((None, MLA_QW, tm), lambda i: (i // nb, 0, i % nb)))
    else:
        q_ins = {'w_uq': (lw['w_uq'], _full((Q_LORA, MLA_QW))), 'w_uq_sw': (lw['w_uq_sw'], _full((Q_LORA, MLA_QW))),
                 'q_gain': (lw['q_gain'], _full((1, MLA_QW))), 'q_gain_sw': (lw['q_gain_sw'], _full((1, MLA_QW))),
                 'invn_q': (consts['invn_q'], _full((1, MLA_QW))), 'cq': (tabs['cq'], tab(LANES)),
                 'sq': (tabs['sq'], tab(LANES))}
        q_out = ((T, MLA_QW), BF16, row(MLA_QW))
    ins = [
        (x2d, row(D_MODEL)), (lw['norm_g'], _full((1, D_MODEL))), (lw['w_in'], _full((D_MODEL, C_END))),
        q_ins['w_uq'], (lw['qa_g'], _full((1, Q_LORA))),
        (lw['kva_g'], _full((1, KV_LORA))), (lw['kr_g'], _full((1, LANES))),
        q_ins['q_gain'], (lw['bq_gain'], _full((1, BAND_W))),
        (lw['bk_gain'], _full((1, BAND_W))),
        (tabs['cret'], tab(RET_W)), (tabs['sret'], tab(RET_W)), q_ins['cq'],
        q_ins['sq'], (tabs['ckr'], tab(LANES)), (tabs['skr'], tab(LANES)),
        q_ins['w_uq_sw'], q_ins['q_gain_sw'],
        (consts['bd64'], _full((RET_W, RET_W))), q_ins['invn_q'],
    ]
    outs = [
        ((T, RET_W), BF16, row(RET_W)), ((T, RET_W), BF16, row(RET_W)), ((T, RET_W), BF16, row(RET_W)),
        ((T, D_MIX), BF16, row(D_MIX)), q_out,
        ((T, KV_LORA), F32, row(KV_LORA)), ((T, MLA_ROPE), F32, row(MLA_ROPE)),
        ((T, BAND_W), BF16, row(BAND_W)), ((T, BAND_W), BF16, row(BAND_W)), ((T, BAND_W), BF16, row(BAND_W)),
        ((n_seq * n_keep, BAND_W), F32, keep), ((n_seq * n_keep, BAND_W), F32, keep),
    ]
    return pl.pallas_call(
        functools.partial(_k1_body, sub=min(tm, K1_SUB), q_transposed=q_transposed),
        grid=(T // tm,),
        in_specs=[s for _, s in ins],
        out_specs=[s for _, _, s in outs],
        out_shape=[jax.ShapeDtypeStruct(sh, dt) for sh, dt, _ in outs],
        compiler_params=_params(("arbitrary",)),
        name="k1_proj",
    )(*[a for a, _ in ins])


def _to_heads_body(k_ref, v_ref, ko_ref, vo_ref):
    for hd in range(BAND_HEADS):
        lanes = slice(hd * BAND_DH, (hd + 1) * BAND_DH)
        ko_ref[:, hd, :] = k_ref[:, lanes]
        vo_ref[:, hd, :] = v_ref[:, lanes]


def _to_heads(k2d, v2d, *, tm):
    n = k2d.shape[0]
    row = pl.BlockSpec((tm, BAND_W), lambda i: (i, 0))
    head = pl.BlockSpec((tm, BAND_HEADS, BAND_DH), lambda i: (i, 0, 0))
    shape = jax.ShapeDtypeStruct((n, BAND_HEADS, BAND_DH), F32)
    return pl.pallas_call(
        _to_heads_body, grid=(n // tm,), in_specs=[row, row], out_specs=[head, head],
        out_shape=[shape, shape], compiler_params=_params(("arbitrary",)), name="band_state_heads",
    )(k2d, v2d)


def _kkv_body(ckv_ref, kr_ref, wk_ref, wv_ref, kgain_ref, *rest, transpose_v):
    if transpose_v:
        ones_ref, k_ref, v_ref = rest
    else:
        k_ref, v_ref = rest
    c = ckv_ref[...].astype(BF16)
    kn = _dot(c, wk_ref[...])
    kn = _group_rms_normed(kn, LANES, MLA_NOPE, 0) * kgain_ref[...]
    kr = kr_ref[...]
    rows = kr.shape[0]
    kr_block = jnp.concatenate([jnp.zeros((rows, MLA_NOPE), F32), kr,
                                jnp.zeros((rows, HEAD_PAD - MLA_QK), F32)], axis=1)
    k_ref[...] = (kn + jnp.concatenate([kr_block] * MLA_HEADS, axis=1)).astype(BF16)
    if transpose_v:
        v_ref[...] = (_dot_nt(wv_ref[...], c) + ones_ref[...]).astype(BF16)
    else:
        v_ref[...] = _dot(c, wv_ref[...]).astype(BF16)


def _kkv(ckv2d, kr2d, lw, *, tm, seq=None, rows=None, row_start=0):
    T = ckv2d.shape[0] if rows is None else rows
    first = row_start // tm
    row = lambda w: pl.BlockSpec((tm, w), lambda i: (i, 0))
    row_in = lambda w: pl.BlockSpec((tm, w), lambda i: (i + first, 0))
    extra_in = []
    if seq is None:
        wv, wv_spec = lw['w_uv'], _full((KV_LORA, MLA_W))
        v_spec, v_shape = row(MLA_W), (T, MLA_W)
    else:
        nb = seq // tm
        wv, wv_spec = lw['w_uv_t'], _full((MLA_VT, KV_LORA))
        v_spec = pl.BlockSpec((None, MLA_VT, tm), lambda i: (i // nb, 0, i % nb))
        v_shape = (T // seq, MLA_VT, seq)
        ones = np.zeros((MLA_HEADS, MLA_VROWS, 1), np.float32)
        ones[:, MLA_V] = 1.0
        extra_in = [(jnp.asarray(ones.reshape(MLA_VT, 1)), _full((MLA_VT, 1)))]
    return pl.pallas_call(
        functools.partial(_kkv_body, transpose_v=seq is not None),
        grid=(T // tm,),
        in_specs=[row_in(KV_LORA), row_in(MLA_ROPE), _full((KV_LORA, MLA_QW)), wv_spec, _full((1, MLA_QW))]
        + [sp for _, sp in extra_in],
        out_specs=[row(MLA_QW), v_spec],
        out_shape=[jax.ShapeDtypeStruct((T, MLA_QW), BF16), jax.ShapeDtypeStruct(v_shape, BF16)],
        compiler_params=_params(("arbitrary",)),
        name="kkv_up",
    )(ckv2d, kr2d, lw['w_uk'], wv, lw['k_gain'], *[a for a, _ in extra_in])


def _mla_prompt_body(q_ref, k_ref, vt_ref, o_ref, m_sc, acc_sc, *, tq):
    qi = pl.program_id(1)
    m_sc[...] = jnp.full(m_sc.shape, NEG_INF, F32)
    acc_sc[...] = jnp.zeros(acc_sc.shape, F32)

    def run(tiles):
        starts = [pl.multiple_of(kb * tq, tq) for kb, _ in tiles]
        stages = [(t, h) for t in range(len(tiles)) for h in range(MLA_HEADS)]
        if any(masked for _, masked in tiles):
            kc = lax.broadcasted_iota(jnp.int32, (tq, tq), 0) // CHUNK
            qc = lax.broadcasted_iota(jnp.int32, (tq, tq), 1) // CHUNK
            visible = kc <= qc

        def scores(i):
            t, h = stages[i]
            qt = q_ref[h * HEAD_PAD:(h + 1) * HEAD_PAD, :]
            k = k_ref[pl.ds(starts[t], tq), h * HEAD_PAD:(h + 1) * HEAD_PAD]
            st = _dot(k, qt)
            return jnp.where(visible, st, NEG_INF) if tiles[t][1] else st

        def softmax(i, st):
            h = stages[i][1]
            m_prev = m_sc[h:h + 1, :]
            m_new = jnp.maximum(m_prev, jnp.max(st, axis=0, keepdims=True))
            alpha = jnp.exp2(m_prev - m_new)
            p = jnp.exp2(st - m_new)
            m_sc[h:h + 1, :] = m_new
            return p.astype(BF16), alpha

        def values(i, p, alpha):
            t, h = stages[i]
            vt = vt_ref[h * MLA_VROWS:(h + 1) * MLA_VROWS, pl.ds(starts[t], tq)]
            acc_sc[h] = alpha * acc_sc[h] + _dot(vt, p)

        n = len(stages)
        st = {0: scores(0), 1: scores(1)}
        pa = {}
        for i in range(n):
            pa[i] = softmax(i, st.pop(i))
            if i >= 1:
                values(i - 1, *pa.pop(i - 1))
            if i + 2 < n:
                st[i + 2] = scores(i + 2)
        values(n - 1, *pa.pop(n - 1))

    def body(pair, carry):
        run([(2 * pair, False), (2 * pair + 1, False)])
        return carry

    lax.fori_loop(0, qi // 2, body, 0)

    @pl.when(qi % 2 == 1)
    def _():
        run([(qi - 1, False), (qi, True)])

    @pl.when(qi % 2 == 0)
    def _():
        run([(qi, True)])

    for j in range(MLA_HEADS // 2):
        pair = jnp.concatenate([acc_sc[h, :MLA_V, :] * (1.0 / acc_sc[h, MLA_V:MLA_V + 1, :])
                                for h in (2 * j, 2 * j + 1)], axis=0)
        o_ref[:, j * LANES:(j + 1) * LANES] = pair.T.astype(BF16)


def _mla_prompt(qt, k, vt, *, tq):
    B, S, _ = k.shape
    once = pl.Buffered(1)
    return pl.pallas_call(
        functools.partial(_mla_prompt_body, tq=tq),
        grid=(B, S // tq),
        in_specs=[pl.BlockSpec((None, MLA_QW, tq), lambda b, i: (b, 0, i)),
                  pl.BlockSpec((None, S, MLA_QW), lambda b, i: (b, 0, 0), pipeline_mode=once),
                  pl.BlockSpec((None, MLA_VT, S), lambda b, i: (b, 0, 0), pipeline_mode=once)],
        out_specs=pl.BlockSpec((None, tq, MLA_W), lambda b, i: (b, i, 0)),
        out_shape=jax.ShapeDtypeStruct((B, S, MLA_W), BF16),
        scratch_shapes=[pltpu.VMEM((MLA_HEADS, tq), F32), pltpu.VMEM((MLA_HEADS, MLA_VROWS, tq), F32)],
        compiler_params=_params(("arbitrary", "arbitrary")),
        name="mla_prompt",
    )(qt, k, vt)


def _mla_sample_body(q_ref, kp_ref, vp_ref, kn_ref, vn_ref, o_ref):
    lane = lax.broadcasted_iota(jnp.int32, (1, LANES), 1)
    outs = []
    scores = []
    for h in range(MLA_HEADS):
        hs = slice(h * HEAD_PAD, (h + 1) * HEAD_PAD)
        q = q_ref[:, hs]
        scores.append((_dot_nt(q, kp_ref[:, hs]), _dot_nt(q, kn_ref[:, hs])))
    for h in range(MLA_HEADS):
        ps = slice((h // 2) * LANES, (h // 2 + 1) * LANES)
        v_lanes = (lane // MLA_V) == (h % 2)
        s1, s2 = scores[h]
        m = jnp.maximum(jnp.max(s1, axis=-1, keepdims=True), jnp.max(s2, axis=-1, keepdims=True))
        p1 = jnp.exp2(s1 - m)
        p2 = jnp.exp2(s2 - m)
        l = jnp.sum(p1, axis=-1, keepdims=True) + jnp.sum(p2, axis=-1, keepdims=True)
        v1 = vp_ref[:, ps]
        v2 = vn_ref[:, ps]
        acc = (_dot(p1.astype(BF16), jnp.where(v_lanes, v1, jnp.zeros_like(v1)))
               + _dot(p2.astype(BF16), jnp.where(v_lanes, v2, jnp.zeros_like(v2))))
        outs.append(acc * (1.0 / l))
    for j in range(MLA_HEADS // 2):
        o_ref[:, j * LANES:(j + 1) * LANES] = (outs[2 * j] + outs[2 * j + 1]).astype(BF16)


def _band_scores(q, pieces):
    lane = lax.broadcasted_iota(jnp.int32, (1, BAND_W), 1)
    raw = []
    for h in range(BAND_HEADS):
        qh = jnp.where((lane // BAND_DH) == h, q, jnp.zeros_like(q))
        raw.append([_dot_nt(qh, k) for k, _ in pieces])
    return raw


def _band_head_out(h, raw_h, pieces, biases):
    lane = lax.broadcasted_iota(jnp.int32, (1, BAND_W), 1)
    sel = (lane // BAND_DH) == h
    ss = [s + bias(h) for s, bias in zip(raw_h, biases)]
    m = functools.reduce(jnp.maximum, [jnp.max(s, axis=-1, keepdims=True) for s in ss])
    ps = [jnp.exp2(s - m) for s in ss]
    l = functools.reduce(jnp.add, [jnp.sum(p, axis=-1, keepdims=True) for p in ps])
    acc = functools.reduce(jnp.add, [
        _dot(p.astype(BF16), jnp.where(sel, v, jnp.zeros_like(v))) for p, (_, v) in zip(ps, pieces)])
    return acc * (1.0 / l)


def _band_heads(q, pieces, biases, out_dtype):
    raw = _band_scores(q, pieces)
    out = functools.reduce(jnp.add, [_band_head_out(h, raw[h], pieces, biases) for h in range(BAND_HEADS)])
    return out.astype(out_dtype)


def _band_ret_prompt_body(q_ref, kp_ref, kc_ref, vp_ref, vc_ref, bp_ref, bc_ref,
                          rq_ref, rk_ref, rv_ref, d_ref, xi_ref, zeta_ref, gam_ref, bd_ref, gain_ref,
                          o_ref, ro_ref, sfin_ref, s_sc, *, c):
    @pl.when(pl.program_id(1) == 0)
    def _():
        s_sc[...] = jnp.zeros(s_sc.shape, F32)

    tq = q_ref.shape[0]
    hq = tq // 2
    assert c == hq
    halves = []
    for j, (prev_keys, cur_keys) in enumerate([(slice(0, tq), slice(0, hq)), (slice(hq, tq), slice(0, tq))]):
        rows = slice(j * hq, (j + 1) * hq)
        pieces = [(kp_ref[prev_keys, :], vp_ref[prev_keys, :]), (kc_ref[cur_keys, :], vc_ref[cur_keys, :])]
        biases = [lambda h, r=rows, k=prev_keys: bp_ref[h, r, k], lambda h, r=rows, k=cur_keys: bc_ref[h, r, k]]
        halves.append((rows, pieces, biases, _band_scores(q_ref[rows, :], pieces)))
    state = s_sc[...]
    for rows, pieces, biases, raw in halves:
        o, state = _ret_chunk(rq_ref[rows, :], rk_ref[rows, :], rv_ref[rows, :], state,
                              d_ref, xi_ref, zeta_ref, gam_ref, bd_ref, gain_ref)
        ro_ref[rows, :] = o
        out = functools.reduce(jnp.add, [_band_head_out(h, raw[h], pieces, biases) for h in range(BAND_HEADS)])
        o_ref[rows, :] = out.astype(o_ref.dtype)
    s_sc[...] = state
    sfin_ref[...] = state


def _band_ret_prompt(q, k, v, bias_prev, bias_cur, rq, rk, rv, rt, consts, gain, *, tq, c):
    B, S, _ = q.shape
    cur = pl.BlockSpec((None, tq, BAND_W), lambda b, i: (b, i, 0))
    prev = pl.BlockSpec((None, tq, BAND_W), lambda b, i: (b, jnp.maximum(i - 1, 0), 0))
    bias = _full((BAND_HEADS, tq, tq))
    bias_prev_spec = pl.BlockSpec((None, BAND_HEADS, tq, tq), lambda b, i: (jnp.minimum(i, 1), 0, 0, 0))
    st = pl.BlockSpec((None, RET_W, RET_W), lambda b, i: (b, 0, 0))
    return pl.pallas_call(
        functools.partial(_band_ret_prompt_body, c=c),
        grid=(B, S // tq),
        in_specs=[cur, prev, cur, prev, cur, bias_prev_spec, bias, cur, cur, cur,
                  _full((RET_HEADS, c, c)), _full((c, RET_W)), _full((c, RET_W)),
                  _full((RET_W, RET_W)), _full((RET_W, RET_W)), _full((1, RET_W))],
        out_specs=[cur, cur, st],
        out_shape=[jax.ShapeDtypeStruct((B, S, BAND_W), BF16), jax.ShapeDtypeStruct((B, S, RET_W), BF16),
                   jax.ShapeDtypeStruct((B, RET_W, RET_W), F32)],
        scratch_shapes=[pltpu.VMEM((RET_W, RET_W), F32)],
        compiler_params=_params(("arbitrary", "arbitrary")),
        name="band_ret_prompt",
    )(q, k, k, v, v, bias_prev, bias_cur, rq, rk, rv,
      rt['decay'], rt['xi'], rt['zeta'], rt['gamma'], consts['bd_mask'], gain)


def _band_sample_body(q_ref, kp_ref, vp_ref, kn_ref, vn_ref, bp_ref, bn_ref, o_ref):
    o_ref[...] = _band_heads(q_ref[...],
                             [(kp_ref[...].astype(BF16), vp_ref[...].astype(BF16)), (kn_ref[...], vn_ref[...])],
                             [lambda h: bp_ref[h], lambda h: bn_ref[h]], BF16)


def _ret_chunk(q, k, v, state, d_ref, xi_ref, zeta_ref, gam_ref, bd_ref, gain_ref):
    lane = lax.broadcasted_iota(jnp.int32, (1, RET_W), 1)
    o = _dot(q, state.astype(BF16)) * xi_ref[...]
    for h in range(RET_HEADS):
        sel = (lane // RET_DV) == h
        a = _dot_nt(jnp.where(sel, q, jnp.zeros_like(q)), k) * d_ref[h]
        o = o + _dot(a.astype(BF16), jnp.where(sel, v, jnp.zeros_like(v)))
    kz = (k.astype(F32) * zeta_ref[...]).astype(BF16)
    s_new = gam_ref[...] * state + bd_ref[...] * _dot_tn(kz, v)
    inv = _group64_inv_rms(o, bd_ref[...].astype(BF16), 1.0 / RET_DV)
    return (o * inv * gain_ref[...]).astype(BF16), s_new


def _sample_mixers_body(mq_ref, mkp_ref, mvp_ref, mkn_ref, mvn_ref,
                        bq_ref, bkp_ref, bvp_ref, bkn_ref, bvn_ref, bp_ref, bn_ref,
                        rq_ref, rk_ref, rv_ref, s0_ref, d_ref, xi_ref, zeta_ref, gam_ref, bd_ref, gain_ref,
                        mo_ref, bo_ref, ro_ref, sfin_ref):
    _mla_sample_body(mq_ref, mkp_ref, mvp_ref, mkn_ref, mvn_ref, mo_ref)
    _band_sample_body(bq_ref, bkp_ref, bvp_ref, bkn_ref, bvn_ref, bp_ref, bn_ref, bo_ref)
    s0 = s0_ref[...]
    state = jnp.concatenate([
        jnp.concatenate([s0[h] if g == h else jnp.zeros((RET_DK, RET_DV), F32) for g in range(RET_HEADS)], axis=1)
        for h in range(RET_HEADS)], axis=0)
    o, s_new = _ret_chunk(rq_ref[...], rk_ref[...], rv_ref[...], state,
                          d_ref, xi_ref, zeta_ref, gam_ref, bd_ref, gain_ref)
    ro_ref[...] = o
    for h in range(RET_HEADS):
        sfin_ref[h] = s_new[h * RET_DK:(h + 1) * RET_DK, h * RET_DV:(h + 1) * RET_DV]


def _sample_mixers(mq, mkp, mvp, mkn, mvn, bq, bkp, bvp, bkn, bvn, bias_past, bias_new,
                   rq, rk, rv, s0, rt, consts, gain, *, layer):
    B, L, _ = mq.shape
    P, PB = mkp.shape[1], bkp.shape[1]
    blk = lambda n, w: pl.BlockSpec((None, n, w), lambda b: (b, 0, 0))
    return pl.pallas_call(
        _sample_mixers_body,
        grid=(B,),
        in_specs=[blk(L, MLA_QW), blk(P, MLA_QW), blk(P, MLA_W), blk(L, MLA_QW), blk(L, MLA_W),
                  blk(L, BAND_W), blk(PB, BAND_W), blk(PB, BAND_W), blk(L, BAND_W), blk(L, BAND_W),
                  _full((BAND_HEADS, L, PB)), _full((BAND_HEADS, L, L)),
                  blk(L, RET_W), blk(L, RET_W), blk(L, RET_W),
                  pl.BlockSpec((None, None, RET_HEADS, RET_DK, RET_DV), lambda b: (layer, b, 0, 0, 0)),
                  _full((RET_HEADS, L, L)), _full((L, RET_W)), _full((L, RET_W)),
                  _full((RET_W, RET_W)), _full((RET_W, RET_W)), _full((1, RET_W))],
        out_specs=[blk(L, MLA_W), blk(L, BAND_W), blk(L, RET_W),
                   pl.BlockSpec((None, RET_HEADS, RET_DK, RET_DV), lambda b: (b, 0, 0, 0))],
        out_shape=[jax.ShapeDtypeStruct((B, L, MLA_W), BF16), jax.ShapeDtypeStruct((B, L, BAND_W), BF16),
                   jax.ShapeDtypeStruct((B, L, RET_W), BF16),
                   jax.ShapeDtypeStruct((B, RET_HEADS, RET_DK, RET_DV), F32)],
        compiler_params=_params(("arbitrary",)),
        name="sample_mixers",
    )(mq, mkp, mvp, mkn, mvn, bq, bkp, bvp, bkn, bvn, bias_past, bias_new,
      rq, rk, rv, s0, rt['decay'], rt['xi'], rt['zeta'], rt['gamma'], consts['bd_mask'], gain)


def _kout_body(x_ref, ro_ref, mo_ref, bo_ref, g_ref, w_ref, y_ref):
    mix = jnp.concatenate([ro_ref[...], mo_ref[...], bo_ref[...]], axis=1) * g_ref[...]
    y_ref[...] = x_ref[...] + _dot(mix, w_ref[...])


def _kout(x2d, ro, mo, bo, gate, w_out, *, tm):
    T = x2d.shape[0]
    row = lambda w: pl.BlockSpec((tm, w), lambda i: (i, 0))
    return pl.pallas_call(
        _kout_body,
        grid=(T // tm,),
        in_specs=[row(D_MODEL), row(RET_W), row(MLA_W), row(BAND_W), row(D_MIX), _full((D_MIX, D_MODEL))],
        out_specs=row(D_MODEL),
        out_shape=jax.ShapeDtypeStruct((T, D_MODEL), F32),
        compiler_params=_params(("arbitrary",)),
        name="kout_proj",
    )(x2d, ro, mo, bo, gate, w_out)


def _constants():
    bd = np.kron(np.eye(RET_HEADS, dtype=np.float32), np.ones((RET_DK, RET_DV), np.float32))
    invn_q = np.tile(np.repeat([1.0 / MLA_NOPE, 1.0 / MLA_ROPE], HEAD_PAD // 2), MLA_HEADS)[None]
    return {'bd_mask': jnp.asarray(bd), 'bd64': jnp.asarray(bd, BF16), 'invn_q': jnp.asarray(invn_q, F32)}


def _rope_tables(pos, reps=1):
    pos = np.asarray(pos, np.float64)

    def cs(half):
        inv = ROPE_BASE ** (-np.arange(half, dtype=np.float64) / half)
        ang = pos[:, None] * inv[None, :]
        c, s = np.cos(ang), np.sin(ang)
        return np.concatenate([c, c], axis=-1), np.concatenate([-s, s], axis=-1)

    T = pos.shape[0]
    c32, s32 = cs(RET_DK // 2)
    c16, s16 = cs(MLA_ROPE // 2)
    ones = lambda w: np.ones((T, w))
    zeros = lambda w: np.zeros((T, w))
    pad = HEAD_PAD - MLA_QK
    tabs = {
        'cret': np.tile(c32, (1, RET_HEADS)), 'sret': np.tile(s32, (1, RET_HEADS)),
        'cq': np.concatenate([ones(MLA_NOPE), c16, ones(pad)], axis=-1),
        'sq': np.concatenate([zeros(MLA_NOPE), s16, zeros(pad)], axis=-1),
        'ckr': np.concatenate([c16, ones(LANES - MLA_ROPE)], axis=-1),
        'skr': np.concatenate([s16, zeros(LANES - MLA_ROPE)], axis=-1),
    }
    out = {k: jnp.asarray(np.tile(v, (reps, 1)), F32) for k, v in tabs.items()}
    if reps == 1:
        out['cq_t'], out['sq_t'] = jnp.asarray(tabs['cq'].T, F32), jnp.asarray(tabs['sq'].T, F32)
    return out


def _retention_tables(c):
    lg = np.log1p(-np.exp2(-5.0 - np.arange(RET_HEADS, dtype=np.float64)))
    idx = np.arange(c, dtype=np.float64)
    diff = idx[:, None] - idx[None, :]
    decay = np.where(diff >= 0, np.exp(lg[:, None, None] * np.maximum(diff, 0.0)), 0.0)
    per_lane = lambda t: np.repeat(t, RET_DV, axis=-1)
    xi = per_lane(np.exp(lg[None, :] * (idx[:, None] + 1.0)))
    zeta = per_lane(np.exp(lg[None, :] * (c - 1.0 - idx)[:, None]))
    gamma = np.broadcast_to(per_lane(np.exp(lg * c)[None, :]).T, (RET_W, RET_W))
    return {k: jnp.asarray(v, F32) for k, v in
            {'decay': decay, 'xi': xi, 'zeta': zeta, 'gamma': gamma}.items()}


def _band_tables_body(rp_ref, rc_ref, prev_ref, cur_ref, past_ref, new_ref, *, tq, n_new):
    w = 2 * tq
    tp = pltpu.roll(jnp.broadcast_to(rp_ref[...], (tq, w)), 0, 1, stride=1, stride_axis=0)[:, :tq]
    tc = pltpu.roll(jnp.broadcast_to(rc_ref[...], (tq, w)), 0, 1, stride=1, stride_axis=0)[:, :tq]
    qc = lax.broadcasted_iota(jnp.int32, (tq, tq), 0) // CHUNK
    kc = lax.broadcasted_iota(jnp.int32, (tq, tq), 1) // CHUNK
    prev_ref[0] = jnp.full((tq, tq), NEG_INF, F32)
    prev_ref[1] = jnp.where(kc >= qc, tp, NEG_INF)
    cur_ref[...] = jnp.where(kc <= qc, tc, NEG_INF)
    past_ref[...] = tp[:n_new, :]
    new_ref[...] = tc[:n_new, :n_new]


def _band_tables(band_bias, *, tq, n_new):
    H = band_bias.shape[0]
    band_bias = band_bias * LOG2E
    lo, mid, hi = band_bias[:, :1], band_bias[:, MAX_REL + 1:2 * MAX_REL], band_bias[:, 2 * MAX_REL:]
    rep = lambda col, n: jnp.broadcast_to(col, (H, n))
    r_cur = jnp.concatenate([band_bias[:, MAX_REL::-1], rep(lo, tq - MAX_REL - 1), rep(hi, tq - MAX_REL),
                             band_bias[:, :MAX_REL:-1]], axis=1)
    r_prev = jnp.concatenate([rep(hi, tq - MAX_REL + 1), mid[:, ::-1], rep(hi, tq)], axis=1)
    row = pl.BlockSpec((None, 1, 2 * tq), lambda h: (h, 0, 0))
    tile = lambda n, m: pl.BlockSpec((None, n, m), lambda h: (h, 0, 0))
    return pl.pallas_call(
        functools.partial(_band_tables_body, tq=tq, n_new=n_new),
        grid=(H,),
        in_specs=[row, row],
        out_specs=[pl.BlockSpec((2, None, tq, tq), lambda h: (0, h, 0, 0)), tile(tq, tq), tile(n_new, tq),
                   tile(n_new, n_new)],
        out_shape=[jax.ShapeDtypeStruct((2, H, tq, tq), F32), jax.ShapeDtypeStruct((H, tq, tq), F32),
                   jax.ShapeDtypeStruct((H, n_new, tq), F32), jax.ShapeDtypeStruct((H, n_new, n_new), F32)],
        compiler_params=_params(("arbitrary",)),
        name="band_tables",
    )(r_prev[:, None, :], r_cur[:, None, :])


def _layer_weights(l, norm_g, w_in, ret_gn_g, mla_qa_g, mla_w_uq, mla_qn_g, mla_qr_g, mla_kva_g, mla_kr_g,
                   mla_w_ukv, mla_kn_g, band_qn_g, band_kn_g, w_out):
    cuts = np.cumsum(SEG)[:-1].tolist()
    a_q, a_k, a_v, a_g, b_cq, b_ckv, b_kr, b_g, c_q, c_k, c_v, c_g = jnp.split(w_in[l], cuts, axis=-1)
    b_kr = jnp.pad(b_kr, ((0, 0), (0, LANES - MLA_ROPE)))
    w_in_p = jnp.concatenate([a_q, a_k * (RET_DK ** -0.5), a_v, b_cq, b_ckv, b_kr, c_q, c_k, c_v, a_g, b_g, c_g],
                             axis=-1).astype(BF16)
    pad = HEAD_PAD - MLA_QK
    w_uq = jnp.pad(mla_w_uq[l].reshape(Q_LORA, MLA_HEADS, MLA_QK), ((0, 0), (0, 0), (0, pad)))
    half = MLA_ROPE // 2

    def swap_rope(t):
        return jnp.concatenate([t[..., :MLA_NOPE], t[..., MLA_NOPE + half:MLA_QK], t[..., MLA_NOPE:MLA_NOPE + half],
                                t[..., MLA_QK:]], axis=-1)
    ukv = mla_w_ukv[l].reshape(KV_LORA, MLA_HEADS, MLA_NOPE + MLA_V)
    w_uk = jnp.pad(ukv[:, :, :MLA_NOPE], ((0, 0), (0, 0), (0, HEAD_PAD - MLA_NOPE)))
    zpad = jnp.zeros((pad,), F32)
    q_gain_head = jnp.concatenate([mla_qn_g[l], mla_qr_g[l], zpad]) * (MLA_QK ** -0.5 * LOG2E)
    q_gain = jnp.tile(q_gain_head, MLA_HEADS)
    q_gain_sw = jnp.tile(swap_rope(q_gain_head), MLA_HEADS)
    k_gain = jnp.tile(jnp.concatenate([mla_kn_g[l], jnp.zeros((HEAD_PAD - MLA_NOPE,), F32)]), MLA_HEADS)
    return {
        'norm_g': norm_g[l][None], 'w_in': w_in_p,
        'w_uq': w_uq.reshape(Q_LORA, MLA_QW).astype(BF16), 'qa_g': mla_qa_g[l][None],
        'w_uq_sw': swap_rope(w_uq).reshape(Q_LORA, MLA_QW).astype(BF16), 'q_gain_sw': q_gain_sw[None],
        'w_uq_t': w_uq.reshape(Q_LORA, MLA_QW).T.astype(BF16),
        'q_gain_col': jnp.broadcast_to(q_gain[:, None], (MLA_QW, LANES)),
        'kva_g': mla_kva_g[l][None],
        'kr_g': jnp.concatenate([mla_kr_g[l], jnp.zeros((LANES - MLA_ROPE,), F32)])[None],
        'q_gain': q_gain[None],
        'bq_gain': (jnp.tile(band_qn_g[l], BAND_HEADS) * (BAND_DH ** -0.5 * LOG2E))[None],
        'bk_gain': jnp.tile(band_kn_g[l], BAND_HEADS)[None],
        'w_uk': w_uk.reshape(KV_LORA, MLA_QW).astype(BF16),
        'w_uv': ukv[:, :, MLA_NOPE:].reshape(KV_LORA, MLA_W).astype(BF16),
        'w_uv_t': jnp.pad(ukv[:, :, MLA_NOPE:].transpose(1, 2, 0), ((0, 0), (0, MLA_VROWS - MLA_V), (0, 0))
                          ).reshape(MLA_VT, KV_LORA).astype(BF16),
        'k_gain': k_gain[None], 'ret_gain': ret_gn_g[l][None], 'w_out': w_out[l].astype(BF16),
    }


def _diag_blocks(s_bd):
    B = s_bd.shape[0]
    s = s_bd.reshape(B, RET_HEADS, RET_DK, RET_HEADS, RET_DV)
    return jnp.stack([s[:, h, :, h, :] for h in range(RET_HEADS)], axis=1)


def kernel(x_prompt, x_sample, state_ret, cache_mla_ckv, cache_mla_krope, cache_band_k, cache_band_v, norm_g, w_in, ret_gn_g, mla_qa_g, mla_w_uq, mla_qn_g, mla_qr_g, mla_kva_g, mla_kr_g, mla_w_ukv, mla_kn_g, band_qn_g, band_kn_g, band_bias, w_out):
    B, S, _ = x_prompt.shape
    DB, L, _ = x_sample.shape
    past_len = cache_mla_ckv.shape[2]
    n_band_past = cache_band_k.shape[2]
    n_keep_p = min(BAND_PAST, S)

    TM_P, TM_S, TM_KV, TQ, RET_C = 512, 256, 1024, 512, 256
    consts = _constants()
    tabs_p = _rope_tables(np.arange(S))
    tabs_s = _rope_tables(past_len + np.arange(L), reps=DB)
    rt_p = _retention_tables(RET_C)
    rt_s = _retention_tables(L)

    xp = x_prompt.reshape(B * S, D_MODEL)
    xs = x_sample.reshape(DB * L, D_MODEL)
    p_st, s_st = [], []
    for l in range(DEPTH):
        lw = _layer_weights(l, norm_g, w_in, ret_gn_g, mla_qa_g, mla_w_uq, mla_qn_g, mla_qr_g, mla_kva_g,
                            mla_kr_g, mla_w_ukv, mla_kn_g, band_qn_g, band_kn_g, w_out)

        rq, rk, rv, gate, qm, ckv, kr, bq, bk, bv, bks, bvs = _k1(
            xp, lw, tabs_p, consts, tm=TM_P, rows_per_seq=S, n_keep=n_keep_p, q_transposed=True)
        km, vt = _kkv(ckv, kr, lw, tm=TM_KV, seq=S)
        mla_o = _mla_prompt(qm, km.reshape(B, S, MLA_QW), vt, tq=TQ)
        assert TQ == BAND_PAST == n_band_past
        bias_prev, bias_cur, bias_past, bias_new = _band_tables(band_bias[l], tq=TQ, n_new=L)
        band_o, ret_o, ret_s = _band_ret_prompt(
            bq.reshape(B, S, BAND_W), bk.reshape(B, S, BAND_W), bv.reshape(B, S, BAND_W), bias_prev, bias_cur,
            rq.reshape(B, S, RET_W), rk.reshape(B, S, RET_W), rv.reshape(B, S, RET_W),
            rt_p, consts, lw['ret_gain'], tq=TQ, c=RET_C)
        xp = _kout(xp, ret_o.reshape(B * S, RET_W), mla_o.reshape(B * S, MLA_W), band_o.reshape(B * S, BAND_W),
                   gate, lw['w_out'], tm=TM_P)
        bks, bvs = _to_heads(bks, bvs, tm=TM_S)
        p_st.append((_diag_blocks(ret_s), ckv.reshape(B, S, KV_LORA), kr.reshape(B, S, MLA_ROPE),
                     bks.reshape(B, n_keep_p, BAND_HEADS, BAND_DH), bvs.reshape(B, n_keep_p, BAND_HEADS, BAND_DH)))

        rq, rk, rv, gate, qm, ckv, kr, bq, bk, bv, bks, bvs = _k1(
            xs, lw, tabs_s, consts, tm=TM_S, rows_per_seq=DB * L, n_keep=DB * L)
        kn, vn = _kkv(ckv, kr, lw, tm=TM_S)
        kp, vp = _kkv(cache_mla_ckv.reshape(DEPTH * DB * past_len, KV_LORA),
                      cache_mla_krope.reshape(DEPTH * DB * past_len, MLA_ROPE), lw, tm=TM_KV,
                      rows=DB * past_len, row_start=l * DB * past_len)
        mla_o, band_o, ret_o, ret_s = _sample_mixers(
            qm.reshape(DB, L, MLA_QW), kp.reshape(DB, past_len, MLA_QW), vp.reshape(DB, past_len, MLA_W),
            kn.reshape(DB, L, MLA_QW), vn.reshape(DB, L, MLA_W),
            bq.reshape(DB, L, BAND_W), cache_band_k[l].reshape(DB, n_band_past, BAND_W),
            cache_band_v[l].reshape(DB, n_band_past, BAND_W), bk.reshape(DB, L, BAND_W), bv.reshape(DB, L, BAND_W),
            bias_past, bias_new,
            rq.reshape(DB, L, RET_W), rk.reshape(DB, L, RET_W), rv.reshape(DB, L, RET_W),
            state_ret, rt_s, consts, lw['ret_gain'], layer=l)
        xs = _kout(xs, ret_o.reshape(DB * L, RET_W), mla_o.reshape(DB * L, MLA_W), band_o.reshape(DB * L, BAND_W),
                   gate, lw['w_out'], tm=TM_S)
        bks, bvs = _to_heads(bks, bvs, tm=TM_S)
        s_st.append((ret_s, ckv.reshape(DB, L, KV_LORA), kr.reshape(DB, L, MLA_ROPE),
                     bks.reshape(DB, L, BAND_HEADS, BAND_DH), bvs.reshape(DB, L, BAND_HEADS, BAND_DH)))

    stack = lambda sts, i: jnp.stack([s[i] for s in sts])
    return (xp.reshape(B, S, D_MODEL), xs.reshape(DB, L, D_MODEL),
            stack(p_st, 0), stack(p_st, 1), stack(p_st, 2), stack(p_st, 3), stack(p_st, 4),
            stack(s_st, 0), stack(s_st, 1), stack(s_st, 2), stack(s_st, 3), stack(s_st, 4))
```

```python
import functools

import jax
import jax.numpy as jnp
import numpy as np
from jax import lax
from jax.experimental import pallas as pl
from jax.experimental.pallas import tpu as pltpu

F32 = jnp.float32
BF16 = jnp.bfloat16

D_MODEL = 1024
DEPTH = 2
CHUNK = 64
EPS = 1e-6
NEG_INF = -1e30
ROPE_BASE = 10000.0
LOG2E = 1.4426950408889634
RET_HEADS, RET_DK, RET_DV = 4, 64, 64
RET_W = RET_HEADS * RET_DV
MLA_HEADS, MLA_NOPE, MLA_ROPE, MLA_V = 8, 64, 32, 64
MLA_QK = MLA_NOPE + MLA_ROPE
Q_LORA, KV_LORA = 256, 128
MLA_W = MLA_HEADS * MLA_V
BAND_HEADS, BAND_DH = 4, 64
BAND_W = BAND_HEADS * BAND_DH
BAND_PREV_CHUNKS = 8
BAND_PAST = BAND_PREV_CHUNKS * CHUNK
MAX_REL = 128
D_MIX = RET_W + MLA_W + BAND_W
SEG = (RET_HEADS * RET_DK, RET_HEADS * RET_DK, RET_W, RET_W, Q_LORA, KV_LORA, MLA_ROPE, MLA_W,
       BAND_W, BAND_W, BAND_W, BAND_W)

LANES = 128
HEAD_PAD = LANES
MLA_QW = MLA_HEADS * HEAD_PAD
MLA_VROWS = MLA_V + 16
MLA_VT = MLA_HEADS * MLA_VROWS
C_AQ, C_AK, C_AV, C_CQ, C_CKV, C_KR, C_BQ, C_BK, C_BV, C_G, C_END = (
    0, 256, 512, 768, 1024, 1152, 1280, 1536, 1792, 2048, 3072)
VMEM_LIMIT = 56 * 1024 * 1024
K1_SUB = 256


def _dot(a, b):
    return jnp.dot(a, b, preferred_element_type=F32)


def _dot_nt(a, b):
    return lax.dot_general(a, b, (((1,), (1,)), ((), ())), preferred_element_type=F32)


def _dot_tn(a, b):
    return lax.dot_general(a, b, (((0,), (0,)), ((), ())), preferred_element_type=F32)


def _group_sums(x, split, two_groups):
    lane = lax.broadcasted_iota(jnp.int32, (1, LANES), 1)
    lo = lane < split
    sums = []
    for b in range(x.shape[-1] // LANES):
        xb = x[:, b * LANES:(b + 1) * LANES]
        sq = xb * xb
        if two_groups:
            sums.append((jnp.sum(jnp.where(lo, sq, 0.0), axis=-1, keepdims=True),
                         jnp.sum(jnp.where(lo, 0.0, sq), axis=-1, keepdims=True)))
        else:
            sums.append((jnp.sum(sq, axis=-1, keepdims=True), None))
    return sums


def _group_normalize(x, sums, split, n_lo, n_hi):
    lane = lax.broadcasted_iota(jnp.int32, (1, LANES), 1)
    lo = lane < split
    outs = []
    for b, (s_lo, s_hi) in enumerate(sums):
        inv = lax.rsqrt(s_lo * (1.0 / n_lo) + EPS)
        if n_hi:
            inv = jnp.where(lo, inv, lax.rsqrt(s_hi * (1.0 / n_hi) + EPS))
        outs.append(x[:, b * LANES:(b + 1) * LANES] * inv)
    return jnp.concatenate(outs, axis=1)


def _group_rms_normed(x, split, n_lo, n_hi):
    return _group_normalize(x, _group_sums(x, split, n_hi > 0), split, n_lo, n_hi)


def _group64_inv_rms(x, bd, inv_n):
    sq = (x * x).astype(BF16)
    w = bd.shape[0]
    sums = jnp.concatenate([_dot(sq[:, c * w:(c + 1) * w], bd) for c in range(x.shape[-1] // w)], axis=1)
    return lax.rsqrt(sums * inv_n + EPS)


def _swap_halves(x, half):
    w = x.shape[-1]
    lane = lax.broadcasted_iota(jnp.int32, (1, w), 1)
    nxt = pltpu.roll(x, w - half, axis=1)
    prv = pltpu.roll(x, half, axis=1)
    return jnp.where((lane & half) == 0, nxt, prv)


def _rope(x, cos, sin_signed, half):
    return x * cos + _swap_halves(x, half) * sin_signed


def _silu(g):
    return g * (1.0 / (1.0 + jnp.exp(-g)))


def _full(shape):
    nd = len(shape)
    return pl.BlockSpec(shape, lambda *_: (0,) * nd)


def _params(sem):
    return pltpu.CompilerParams(dimension_semantics=sem, vmem_limit_bytes=VMEM_LIMIT)


def _k1_body(x_ref, ng_ref, win_ref, wuq_ref, qag_ref, kvag_ref, krg_ref, qgain_ref, bqg_ref, bkg_ref,
             cret_ref, sret_ref, cq_ref, sq_ref, ckr_ref, skr_ref, wuqr_ref, qgainr_ref, bd_ref, invnq_ref,
             rq_ref, rk_ref, rv_ref, gate_ref, qm_ref, ckv_ref, kr_ref, bq_ref, bk_ref, bv_ref,
             bks_ref, bvs_ref, *, sub, q_transposed):
    n_sub = x_ref.shape[0] // sub
    bd = bd_ref[...]
    lane_reps = sub // LANES

    def stage_a(r):
        x = x_ref[r * sub:(r + 1) * sub, :]
        h = x * lax.rsqrt(jnp.mean(x * x, axis=-1, keepdims=True) + EPS) * ng_ref[...]
        hb = h.astype(BF16)

        def seg(lo, hi):
            return _dot(hb, win_ref[:, lo:hi])

        z = {'cq': seg(C_CQ, C_CKV), 'bq': seg(C_BQ, C_BK), 'bk': seg(C_BK, C_BV)}
        cq = z['cq']
        cq = cq * lax.rsqrt(jnp.mean(cq * cq, axis=-1, keepdims=True) + EPS) * qag_ref[...]
        z['aq'], z['ak'] = seg(C_AQ, C_AK), seg(C_AK, C_AV)
        cqb = cq.astype(BF16)
        if q_transposed:
            z['qf'] = _dot_nt(wuq_ref[...], cqb)
        else:
            z['qf'] = _dot(cqb, wuq_ref[...])
            z['qf_sw'] = _dot(cqb, wuqr_ref[...])
        z['bq_inv'] = _group64_inv_rms(z['bq'], bd, 1.0 / BAND_DH)
        z['bk_inv'] = _group64_inv_rms(z['bk'], bd, 1.0 / BAND_DH)
        z['g'] = seg(C_G, C_END)
        if q_transposed:
            groups = z['qf'].reshape(2 * MLA_HEADS, HEAD_PAD // 2, sub)
            ss = jnp.sum(groups * groups, axis=1, keepdims=True)
            is_nope = lax.broadcasted_iota(jnp.int32, (2 * MLA_HEADS, 1, 1), 0) % 2 == 0
            inv = lax.rsqrt(ss * jnp.where(is_nope, 1.0 / MLA_NOPE, 1.0 / MLA_ROPE) + EPS)
            z['qn'] = (groups * inv).reshape(MLA_QW, sub)
        else:
            z['q_inv'] = _group64_inv_rms(z['qf'], bd, invnq_ref[...])
        z['av'], z['ckv'] = seg(C_AV, C_CQ), seg(C_CKV, C_KR)
        z['kr'], z['bv'] = seg(C_KR, C_BQ), seg(C_BV, C_G)
        return z

    def stage_b(r, z):
        rows = slice(r * sub, (r + 1) * sub)
        bq_ref[rows, :] = (z['bq'] * z['bq_inv'] * bqg_ref[...]).astype(BF16)
        bk = z['bk'] * z['bk_inv'] * bkg_ref[...]
        bk_ref[rows, :] = bk.astype(BF16)
        bks_ref[rows, :] = bk
        bv_ref[rows, :] = z['bv'].astype(BF16)
        bvs_ref[rows, :] = z['bv']
        cret, sret = cret_ref[rows, :], sret_ref[rows, :]
        rq_ref[rows, :] = _rope(z['aq'], cret, sret, RET_DK // 2).astype(BF16)
        rk_ref[rows, :] = _rope(z['ak'], cret, sret, RET_DK // 2).astype(BF16)
        rv_ref[rows, :] = z['av'].astype(BF16)
        gate_ref[rows, :] = _silu(z['g']).astype(BF16)
        if q_transposed:
            cq_t = jnp.concatenate([cq_ref[:, rows]] * MLA_HEADS, axis=0)
            sq_t = jnp.concatenate([sq_ref[:, rows]] * MLA_HEADS, axis=0)
            qg = z['qn'] * jnp.concatenate([qgain_ref[...]] * lane_reps, axis=1)
            half = MLA_ROPE // 2
            pieces = []
            for hd in range(MLA_HEADS):
                o = hd * HEAD_PAD
                pieces += [qg[o:o + MLA_NOPE], qg[o + MLA_NOPE + half:o + MLA_QK],
                           qg[o + MLA_NOPE:o + MLA_NOPE + half], qg[o + MLA_QK:o + HEAD_PAD]]
            qm_ref[:, rows] = (qg * cq_t + jnp.concatenate(pieces, axis=0) * sq_t).astype(BF16)
        else:
            cq_t = jnp.concatenate([cq_ref[rows, :]] * MLA_HEADS, axis=1)
            sq_t = jnp.concatenate([sq_ref[rows, :]] * MLA_HEADS, axis=1)
            qm_ref[rows, :] = (z['qf'] * (z['q_inv'] * qgain_ref[...]) * cq_t
                               + z['qf_sw'] * (z['q_inv'] * qgainr_ref[...]) * sq_t).astype(BF16)
        ckv, kr = z['ckv'], z['kr']
        ckv_ref[rows, :] = ckv * lax.rsqrt(jnp.mean(ckv * ckv, axis=-1, keepdims=True) + EPS) * kvag_ref[...]
        kr = kr * lax.rsqrt(jnp.sum(kr * kr, axis=-1, keepdims=True) * (1.0 / MLA_ROPE) + EPS) * krg_ref[...]
        kr = _rope(kr, ckr_ref[rows, :], skr_ref[rows, :], MLA_ROPE // 2)
        kr_ref[rows, :] = kr[:, :MLA_ROPE]

    z = stage_a(0)
    for r in range(n_sub):
        z_next = stage_a(r + 1) if r + 1 < n_sub else None
        stage_b(r, z)
        z = z_next


def _k1(x2d, lw, tabs, consts, *, tm, rows_per_seq, n_keep, q_transposed=False):
    T = x2d.shape[0]
    nb = rows_per_seq // tm
    nkb = n_keep // tm
    n_seq = T // rows_per_seq

    row = lambda w: pl.BlockSpec((tm, w), lambda i: (i, 0))
    tab = lambda w: pl.BlockSpec((tm, w), lambda i: (i % nb, 0))
    keep = pl.BlockSpec((tm, BAND_W), lambda i: ((i // nb) * nkb + jnp.maximum(i % nb - (nb - nkb), 0), 0))

    if q_transposed:
        tab_t = pl.BlockSpec((LANES, tm), lambda i: (0, i % nb))
        col = _full((MLA_QW, LANES))
        unused = (jnp.zeros((1, LANES), F32), _full((1, LANES)))
        q_ins = {'w_uq': (lw['w_uq_t'], _full((MLA_QW, Q_LORA))), 'w_uq_sw': unused,
                 'q_gain': (lw['q_gain_col'], col), 'q_gain_sw': unused,
                 'invn_q': unused, 'cq': (tabs['cq_t'], tab_t), 'sq': (tabs['sq_t'], tab_t)}
        q_out = ((n_seq, MLA_QW, rows_per_seq), BF16,
                 pl.BlockSpec((None, MLA_QW, tm), lambda i: (i // nb, 0, i % nb)))
    else:
        q_ins = {'w_uq': (lw['w_uq'], _full((Q_LORA, MLA_QW))), 'w_uq_sw': (lw['w_uq_sw'], _full((Q_LORA, MLA_QW))),
                 'q_gain': (lw['q_gain'], _full((1, MLA_QW))), 'q_gain_sw': (lw['q_gain_sw'], _full((1, MLA_QW))),
                 'invn_q': (consts['invn_q'], _full((1, MLA_QW))), 'cq': (tabs['cq'], tab(LANES)),
                 'sq': (tabs['sq'], tab(LANES))}
        q_out = ((T, MLA_QW), BF16, row(MLA_QW))
    ins = [
        (x2d, row(D_MODEL)), (lw['norm_g'], _full((1, D_MODEL))), (lw['w_in'], _full((D_MODEL, C_END))),
        q_ins['w_uq'], (lw['qa_g'], _full((1, Q_LORA))),
        (lw['kva_g'], _full((1, KV_LORA))), (lw['kr_g'], _full((1, LANES))),
        q_ins['q_gain'], (lw['bq_gain'], _full((1, BAND_W))),
        (lw['bk_gain'], _full((1, BAND_W))),
        (tabs['cret'], tab(RET_W)), (tabs['sret'], tab(RET_W)), q_ins['cq'],
        q_ins['sq'], (tabs['ckr'], tab(LANES)), (tabs['skr'], tab(LANES)),
        q_ins['w_uq_sw'], q_ins['q_gain_sw'],
        (consts['bd64'], _full((RET_W, RET_W))), q_ins['invn_q'],
    ]
    outs = [
        ((T, RET_W), BF16, row(RET_W)), ((T, RET_W), BF16, row(RET_W)), ((T, RET_W), BF16, row(RET_W)),
        ((T, D_MIX), BF16, row(D_MIX)), q_out,
        ((T, KV_LORA), F32, row(KV_LORA)), ((T, MLA_ROPE), F32, row(MLA_ROPE)),
        ((T, BAND_W), BF16, row(BAND_W)), ((T, BAND_W), BF16, row(BAND_W)), ((T, BAND_W), BF16, row(BAND_W)),
        ((n_seq * n_keep, BAND_W), F32, keep), ((n_seq * n_keep, BAND_W), F32, keep),
    ]
    return pl.pallas_call(
        functools.partial(_k1_body, sub=min(tm, K1_SUB), q_transposed=q_transposed),
        grid=(T // tm,),
        in_specs=[s for _, s in ins],
        out_specs=[s for _, _, s in outs],
        out_shape=[jax.ShapeDtypeStruct(sh, dt) for sh, dt, _ in outs],
        compiler_params=_params(("arbitrary",)),
        name="k1_proj",
    )(*[a for a, _ in ins])


def _to_heads_body(k_ref, v_ref, ko_ref, vo_ref):
    for hd in range(BAND_HEADS):
        lanes = slice(hd * BAND_DH, (hd + 1) * BAND_DH)
        ko_ref[:, hd, :] = k_ref[:, lanes]
        vo_ref[:, hd, :] = v_ref[:, lanes]


def _to_heads(k2d, v2d, *, tm):
    n = k2d.shape[0]
    row = pl.BlockSpec((tm, BAND_W), lambda i: (i, 0))
    head = pl.BlockSpec((tm, BAND_HEADS, BAND_DH), lambda i: (i, 0, 0))
    shape = jax.ShapeDtypeStruct((n, BAND_HEADS, BAND_DH), F32)
    return pl.pallas_call(
        _to_heads_body, grid=(n // tm,), in_specs=[row, row], out_specs=[head, head],
        out_shape=[shape, shape], compiler_params=_params(("arbitrary",)), name="band_state_heads",
    )(k2d, v2d)


def _kkv_body(ckv_ref, kr_ref, wk_ref, wv_ref, kgain_ref, *rest, transpose_v):
    if transpose_v:
        ones_ref, k_ref, v_ref = rest
    else:
        k_ref, v_ref = rest
    c = ckv_ref[...].astype(BF16)
    kn = _dot(c, wk_ref[...])
    kn = _group_rms_normed(kn, LANES, MLA_NOPE, 0) * kgain_ref[...]
    kr = kr_ref[...]
    rows = kr.shape[0]
    kr_block = jnp.concatenate([jnp.zeros((rows, MLA_NOPE), F32), kr,
                                jnp.zeros((rows, HEAD_PAD - MLA_QK), F32)], axis=1)
    k_ref[...] = (kn + jnp.concatenate([kr_block] * MLA_HEADS, axis=1)).astype(BF16)
    if transpose_v:
        v_ref[...] = (_dot_nt(wv_ref[...], c) + ones_ref[...]).astype(BF16)
    else:
        v_ref[...] = _dot(c, wv_ref[...]).astype(BF16)


def _kkv(ckv2d, kr2d, lw, *, tm, seq=None, rows=None, row_start=0):
    T = ckv2d.shape[0] if rows is None else rows
    first = row_start // tm
    row = lambda w: pl.BlockSpec((tm, w), lambda i: (i, 0))
    row_in = lambda w: pl.BlockSpec((tm, w), lambda i: (i + first, 0))
    extra_in = []
    if seq is None:
        wv, wv_spec = lw['w_uv'], _full((KV_LORA, MLA_W))
        v_spec, v_shape = row(MLA_W), (T, MLA_W)
    else:
        nb = seq // tm
        wv, wv_spec = lw['w_uv_t'], _full((MLA_VT, KV_LORA))
        v_spec = pl.BlockSpec((None, MLA_VT, tm), lambda i: (i // nb, 0, i % nb))
        v_shape = (T // seq, MLA_VT, seq)
        ones = np.zeros((MLA_HEADS, MLA_VROWS, 1), np.float32)
        ones[:, MLA_V] = 1.0
        extra_in = [(jnp.asarray(ones.reshape(MLA_VT, 1)), _full((MLA_VT, 1)))]
    return pl.pallas_call(
        functools.partial(_kkv_body, transpose_v=seq is not None),
        grid=(T // tm,),
        in_specs=[row_in(KV_LORA), row_in(MLA_ROPE), _full((KV_LORA, MLA_QW)), wv_spec, _full((1, MLA_QW))]
        + [sp for _, sp in extra_in],
        out_specs=[row(MLA_QW), v_spec],
        out_shape=[jax.ShapeDtypeStruct((T, MLA_QW), BF16), jax.ShapeDtypeStruct(v_shape, BF16)],
        compiler_params=_params(("arbitrary",)),
        name="kkv_up",
    )(ckv2d, kr2d, lw['w_uk'], wv, lw['k_gain'], *[a for a, _ in extra_in])


def _mla_prompt_body(q_ref, k_ref, vt_ref, o_ref, m_sc, acc_sc, *, tq):
    qi = pl.program_id(1)
    m_sc[...] = jnp.full(m_sc.shape, NEG_INF, F32)
    acc_sc[...] = jnp.zeros(acc_sc.shape, F32)

    def run(tiles):
        starts = [pl.multiple_of(kb * tq, tq) for kb, _ in tiles]
        stages = [(t, h) for t in range(len(tiles)) for h in range(MLA_HEADS)]
        if any(masked for _, masked in tiles):
            kc = lax.broadcasted_iota(jnp.int32, (tq, tq), 0) // CHUNK
            qc = lax.broadcasted_iota(jnp.int32, (tq, tq), 1) // CHUNK
            visible = kc <= qc

        def scores(i):
            t, h = stages[i]
            qt = q_ref[h * HEAD_PAD:(h + 1) * HEAD_PAD, :]
            k = k_ref[pl.ds(starts[t], tq), h * HEAD_PAD:(h + 1) * HEAD_PAD]
            st = _dot(k, qt)
            return jnp.where(visible, st, NEG_INF) if tiles[t][1] else st

        def softmax(i, st):
            h = stages[i][1]
            m_prev = m_sc[h:h + 1, :]
            m_new = jnp.maximum(m_prev, jnp.max(st, axis=0, keepdims=True))
            alpha = jnp.exp2(m_prev - m_new)
            p = jnp.exp2(st - m_new)
            m_sc[h:h + 1, :] = m_new
            return p.astype(BF16), alpha

        def values(i, p, alpha):
            t, h = stages[i]
            vt = vt_ref[h * MLA_VROWS:(h + 1) * MLA_VROWS, pl.ds(starts[t], tq)]
            acc_sc[h] = alpha * acc_sc[h] + _dot(vt, p)

        n = len(stages)
        st = {0: scores(0), 1: scores(1)}
        pa = {}
        for i in range(n):
            pa[i] = softmax(i, st.pop(i))
            if i >= 1:
                values(i - 1, *pa.pop(i - 1))
            if i + 2 < n:
                st[i + 2] = scores(i + 2)
        values(n - 1, *pa.pop(n - 1))

    def body(pair, carry):
        run([(2 * pair, False), (2 * pair + 1, False)])
        return carry

    lax.fori_loop(0, qi // 2, body, 0)

    @pl.when(qi % 2 == 1)
    def _():
        run([(qi - 1, False), (qi, True)])

    @pl.when(qi % 2 == 0)
    def _():
        run([(qi, True)])

    for j in range(MLA_HEADS // 2):
        pair = jnp.concatenate([acc_sc[h, :MLA_V, :] * (1.0 / acc_sc[h, MLA_V:MLA_V + 1, :])
                                for h in (2 * j, 2 * j + 1)], axis=0)
        o_ref[:, j * LANES:(j + 1) * LANES] = pair.T.astype(BF16)


def _mla_prompt(qt, k, vt, *, tq):
    B, S, _ = k.shape
    once = pl.Buffered(1)
    return pl.pallas_call(
        functools.partial(_mla_prompt_body, tq=tq),
        grid=(B, S // tq),
        in_specs=[pl.BlockSpec((None, MLA_QW, tq), lambda b, i: (b, 0, i)),
                  pl.BlockSpec((None, S, MLA_QW), lambda b, i: (b, 0, 0), pipeline_mode=once),
                  pl.BlockSpec((None, MLA_VT, S), lambda b, i: (b, 0, 0), pipeline_mode=once)],
        out_specs=pl.BlockSpec((None, tq, MLA_W), lambda b, i: (b, i, 0)),
        out_shape=jax.ShapeDtypeStruct((B, S, MLA_W), BF16),
        scratch_shapes=[pltpu.VMEM((MLA_HEADS, tq), F32), pltpu.VMEM((MLA_HEADS, MLA_VROWS, tq), F32)],
        compiler_params=_params(("arbitrary", "arbitrary")),
        name="mla_prompt",
    )(qt, k, vt)


def _mla_sample_body(q_ref, kp_ref, vp_ref, kn_ref, vn_ref, o_ref):
    lane = lax.broadcasted_iota(jnp.int32, (1, LANES), 1)
    outs = []
    scores = []
    for h in range(MLA_HEADS):
        hs = slice(h * HEAD_PAD, (h + 1) * HEAD_PAD)
        q = q_ref[:, hs]
        scores.append((_dot_nt(q, kp_ref[:, hs]), _dot_nt(q, kn_ref[:, hs])))
    for h in range(MLA_HEADS):
        ps = slice((h // 2) * LANES, (h // 2 + 1) * LANES)
        v_lanes = (lane // MLA_V) == (h % 2)
        s1, s2 = scores[h]
        m = jnp.maximum(jnp.max(s1, axis=-1, keepdims=True), jnp.max(s2, axis=-1, keepdims=True))
        p1 = jnp.exp2(s1 - m)
        p2 = jnp.exp2(s2 - m)
        l = jnp.sum(p1, axis=-1, keepdims=True) + jnp.sum(p2, axis=-1, keepdims=True)
        v1 = vp_ref[:, ps]
        v2 = vn_ref[:, ps]
        acc = (_dot(p1.astype(BF16), jnp.where(v_lanes, v1, jnp.zeros_like(v1)))
               + _dot(p2.astype(BF16), jnp.where(v_lanes, v2, jnp.zeros_like(v2))))
        outs.append(acc * (1.0 / l))
    for j in range(MLA_HEADS // 2):
        o_ref[:, j * LANES:(j + 1) * LANES] = (outs[2 * j] + outs[2 * j + 1]).astype(BF16)


def _band_scores(q, pieces):
    lane = lax.broadcasted_iota(jnp.int32, (1, BAND_W), 1)
    raw = []
    for h in range(BAND_HEADS):
        qh = jnp.where((lane // BAND_DH) == h, q, jnp.zeros_like(q))
        raw.append([_dot_nt(qh, k) for k, _ in pieces])
    return raw


def _band_head_out(h, raw_h, pieces, biases):
    lane = lax.broadcasted_iota(jnp.int32, (1, BAND_W), 1)
    sel = (lane // BAND_DH) == h
    ss = [s + bias(h) for s, bias in zip(raw_h, biases)]
    m = functools.reduce(jnp.maximum, [jnp.max(s, axis=-1, keepdims=True) for s in ss])
    ps = [jnp.exp2(s - m) for s in ss]
    l = functools.reduce(jnp.add, [jnp.sum(p, axis=-1, keepdims=True) for p in ps])
    acc = functools.reduce(jnp.add, [
        _dot(p.astype(BF16), jnp.where(sel, v, jnp.zeros_like(v))) for p, (_, v) in zip(ps, pieces)])
    return acc * (1.0 / l)


def _band_heads(q, pieces, biases, out_dtype):
    raw = _band_scores(q, pieces)
    out = functools.reduce(jnp.add, [_band_head_out(h, raw[h], pieces, biases) for h in range(BAND_HEADS)])
    return out.astype(out_dtype)


def _band_ret_prompt_body(q_ref, kp_ref, kc_ref, vp_ref, vc_ref, bp_ref, bc_ref,
                          rq_ref, rk_ref, rv_ref, d_ref, xi_ref, zeta_ref, gam_ref, bd_ref, gain_ref,
                          o_ref, ro_ref, sfin_ref, s_sc, *, c):
    @pl.when(pl.program_id(1) == 0)
    def _():
        s_sc[...] = jnp.zeros(s_sc.shape, F32)

    tq = q_ref.shape[0]
    hq = tq // 2
    assert c == hq
    halves = []
    for j, (prev_keys, cur_keys) in enumerate([(slice(0, tq), slice(0, hq)), (slice(hq, tq), slice(0, tq))]):
        rows = slice(j * hq, (j + 1) * hq)
        pieces = [(kp_ref[prev_keys, :], vp_ref[prev_keys, :]), (kc_ref[cur_keys, :], vc_ref[cur_keys, :])]
        biases = [lambda h, r=rows, k=prev_keys: bp_ref[h, r, k], lambda h, r=rows, k=cur_keys: bc_ref[h, r, k]]
        halves.append((rows, pieces, biases, _band_scores(q_ref[rows, :], pieces)))
    state = s_sc[...]
    for rows, pieces, biases, raw in halves:
        o, state = _ret_chunk(rq_ref[rows, :], rk_ref[rows, :], rv_ref[rows, :], state,
                              d_ref, xi_ref, zeta_ref, gam_ref, bd_ref, gain_ref)
        ro_ref[rows, :] = o
        out = functools.reduce(jnp.add, [_band_head_out(h, raw[h], pieces, biases) for h in range(BAND_HEADS)])
        o_ref[rows, :] = out.astype(o_ref.dtype)
    s_sc[...] = state
    sfin_ref[...] = state


def _band_ret_prompt(q, k, v, bias_prev, bias_cur, rq, rk, rv, rt, consts, gain, *, tq, c):
    B, S, _ = q.shape
    cur = pl.BlockSpec((None, tq, BAND_W), lambda b, i: (b, i, 0))
    prev = pl.BlockSpec((None, tq, BAND_W), lambda b, i: (b, jnp.maximum(i - 1, 0), 0))
    bias = _full((BAND_HEADS, tq, tq))
    bias_prev_spec = pl.BlockSpec((None, BAND_HEADS, tq, tq), lambda b, i: (jnp.minimum(i, 1), 0, 0, 0))
    st = pl.BlockSpec((None, RET_W, RET_W), lambda b, i: (b, 0, 0))
    return pl.pallas_call(
        functools.partial(_band_ret_prompt_body, c=c),
        grid=(B, S // tq),
        in_specs=[cur, prev, cur, prev, cur, bias_prev_spec, bias, cur, cur, cur,
                  _full((RET_HEADS, c, c)), _full((c, RET_W)), _full((c, RET_W)),
                  _full((RET_W, RET_W)), _full((RET_W, RET_W)), _full((1, RET_W))],
        out_specs=[cur, cur, st],
        out_shape=[jax.ShapeDtypeStruct((B, S, BAND_W), BF16), jax.ShapeDtypeStruct((B, S, RET_W), BF16),
                   jax.ShapeDtypeStruct((B, RET_W, RET_W), F32)],
        scratch_shapes=[pltpu.VMEM((RET_W, RET_W), F32)],
        compiler_params=_params(("arbitrary", "arbitrary")),
        name="band_ret_prompt",
    )(q, k, k, v, v, bias_prev, bias_cur, rq, rk, rv,
      rt['decay'], rt['xi'], rt['zeta'], rt['gamma'], consts['bd_mask'], gain)


def _band_sample_body(q_ref, kp_ref, vp_ref, kn_ref, vn_ref, bp_ref, bn_ref, o_ref):
    o_ref[...] = _band_heads(q_ref[...],
                             [(kp_ref[...].astype(BF16), vp_ref[...].astype(BF16)), (kn_ref[...], vn_ref[...])],
                             [lambda h: bp_ref[h], lambda h: bn_ref[h]], BF16)


def _ret_chunk(q, k, v, state, d_ref, xi_ref, zeta_ref, gam_ref, bd_ref, gain_ref):
    lane = lax.broadcasted_iota(jnp.int32, (1, RET_W), 1)
    o = _dot(q, state.astype(BF16)) * xi_ref[...]
    for h in range(RET_HEADS):
        sel = (lane // RET_DV) == h
        a = _dot_nt(jnp.where(sel, q, jnp.zeros_like(q)), k) * d_ref[h]
        o = o + _dot(a.astype(BF16), jnp.where(sel, v, jnp.zeros_like(v)))
    kz = (k.astype(F32) * zeta_ref[...]).astype(BF16)
    s_new = gam_ref[...] * state + bd_ref[...] * _dot_tn(kz, v)
    inv = _group64_inv_rms(o, bd_ref[...].astype(BF16), 1.0 / RET_DV)
    return (o * inv * gain_ref[...]).astype(BF16), s_new


def _sample_mixers_body(mq_ref, mkp_ref, mvp_ref, mkn_ref, mvn_ref,
                        bq_ref, bkp_ref, bvp_ref, bkn_ref, bvn_ref, bp_ref, bn_ref,
                        rq_ref, rk_ref, rv_ref, s0_ref, d_ref, xi_ref, zeta_ref, gam_ref, bd_ref, gain_ref,
                        mo_ref, bo_ref, ro_ref, sfin_ref):
    _mla_sample_body(mq_ref, mkp_ref, mvp_ref, mkn_ref, mvn_ref, mo_ref)
    _band_sample_body(bq_ref, bkp_ref, bvp_ref, bkn_ref, bvn_ref, bp_ref, bn_ref, bo_ref)
    s0 = s0_ref[...]
    state = jnp.concatenate([
        jnp.concatenate([s0[h] if g == h else jnp.zeros((RET_DK, RET_DV), F32) for g in range(RET_HEADS)], axis=1)
        for h in range(RET_HEADS)], axis=0)
    o, s_new = _ret_chunk(rq_ref[...], rk_ref[...], rv_ref[...], state,
                          d_ref, xi_ref, zeta_ref, gam_ref, bd_ref, gain_ref)
    ro_ref[...] = o
    for h in range(RET_HEADS):
        sfin_ref[h] = s_new[h * RET_DK:(h + 1) * RET_DK, h * RET_DV:(h + 1) * RET_DV]


def _sample_mixers(mq, mkp, mvp, mkn, mvn, bq, bkp, bvp, bkn, bvn, bias_past, bias_new,
                   rq, rk, rv, s0, rt, consts, gain, *, layer):
    B, L, _ = mq.shape
    P, PB = mkp.shape[1], bkp.shape[1]
    blk = lambda n, w: pl.BlockSpec((None, n, w), lambda b: (b, 0, 0))
    return pl.pallas_call(
        _sample_mixers_body,
        grid=(B,),
        in_specs=[blk(L, MLA_QW), blk(P, MLA_QW), blk(P, MLA_W), blk(L, MLA_QW), blk(L, MLA_W),
                  blk(L, BAND_W), blk(PB, BAND_W), blk(PB, BAND_W), blk(L, BAND_W), blk(L, BAND_W),
                  _full((BAND_HEADS, L, PB)), _full((BAND_HEADS, L, L)),
                  blk(L, RET_W), blk(L, RET_W), blk(L, RET_W),
                  pl.BlockSpec((None, None, RET_HEADS, RET_DK, RET_DV), lambda b: (layer, b, 0, 0, 0)),
                  _full((RET_HEADS, L, L)), _full((L, RET_W)), _full((L, RET_W)),
                  _full((RET_W, RET_W)), _full((RET_W, RET_W)), _full((1, RET_W))],
        out_specs=[blk(L, MLA_W), blk(L, BAND_W), blk(L, RET_W),
                   pl.BlockSpec((None, RET_HEADS, RET_DK, RET_DV), lambda b: (b, 0, 0, 0))],
        out_shape=[jax.ShapeDtypeStruct((B, L, MLA_W), BF16), jax.ShapeDtypeStruct((B, L, BAND_W), BF16),
                   jax.ShapeDtypeStruct((B, L, RET_W), BF16),
                   jax.ShapeDtypeStruct((B, RET_HEADS, RET_DK, RET_DV), F32)],
        compiler_params=_params(("arbitrary",)),
        name="sample_mixers",
    )(mq, mkp, mvp, mkn, mvn, bq, bkp, bvp, bkn, bvn, bias_past, bias_new,
      rq, rk, rv, s0, rt['decay'], rt['xi'], rt['zeta'], rt['gamma'], consts['bd_mask'], gain)


def _kout_body(x_ref, ro_ref, mo_ref, bo_ref, g_ref, w_ref, y_ref):
    mix = jnp.concatenate([ro_ref[...], mo_ref[...], bo_ref[...]], axis=1) * g_ref[...]
    y_ref[...] = x_ref[...] + _dot(mix, w_ref[...])


def _kout(x2d, ro, mo, bo, gate, w_out, *, tm):
    T = x2d.shape[0]
    row = lambda w: pl.BlockSpec((tm, w), lambda i: (i, 0))
    return pl.pallas_call(
        _kout_body,
        grid=(T // tm,),
        in_specs=[row(D_MODEL), row(RET_W), row(MLA_W), row(BAND_W), row(D_MIX), _full((D_MIX, D_MODEL))],
        out_specs=row(D_MODEL),
        out_shape=jax.ShapeDtypeStruct((T, D_MODEL), F32),
        compiler_params=_params(("arbitrary",)),
        name="kout_proj",
    )(x2d, ro, mo, bo, gate, w_out)


def _constants():
    bd = np.kron(np.eye(RET_HEADS, dtype=np.float32), np.ones((RET_DK, RET_DV), np.float32))
    invn_q = np.tile(np.repeat([1.0 / MLA_NOPE, 1.0 / MLA_ROPE], HEAD_PAD // 2), MLA_HEADS)[None]
    return {'bd_mask': jnp.asarray(bd), 'bd64': jnp.asarray(bd, BF16), 'invn_q': jnp.asarray(invn_q, F32)}


def _rope_tables(pos, reps=1):
    pos = np.asarray(pos, np.float64)

    def cs(half):
        inv = ROPE_BASE ** (-np.arange(half, dtype=np.float64) / half)
        ang = pos[:, None] * inv[None, :]
        c, s = np.cos(ang), np.sin(ang)
        return np.concatenate([c, c], axis=-1), np.concatenate([-s, s], axis=-1)

    T = pos.shape[0]
    c32, s32 = cs(RET_DK // 2)
    c16, s16 = cs(MLA_ROPE // 2)
    ones = lambda w: np.ones((T, w))
    zeros = lambda w: np.zeros((T, w))
    pad = HEAD_PAD - MLA_QK
    tabs = {
        'cret': np.tile(c32, (1, RET_HEADS)), 'sret': np.tile(s32, (1, RET_HEADS)),
        'cq': np.concatenate([ones(MLA_NOPE), c16, ones(pad)], axis=-1),
        'sq': np.concatenate([zeros(MLA_NOPE), s16, zeros(pad)], axis=-1),
        'ckr': np.concatenate([c16, ones(LANES - MLA_ROPE)], axis=-1),
        'skr': np.concatenate([s16, zeros(LANES - MLA_ROPE)], axis=-1),
    }
    out = {k: jnp.asarray(np.tile(v, (reps, 1)), F32) for k, v in tabs.items()}
    if reps == 1:
        out['cq_t'], out['sq_t'] = jnp.asarray(tabs['cq'].T, F32), jnp.asarray(tabs['sq'].T, F32)
    return out


def _retention_tables(c):
    lg = np.log1p(-np.exp2(-5.0 - np.arange(RET_HEADS, dtype=np.float64)))
    idx = np.arange(c, dtype=np.float64)
    diff = idx[:, None] - idx[None, :]
    decay = np.where(diff >= 0, np.exp(lg[:, None, None] * np.maximum(diff, 0.0)), 0.0)
    per_lane = lambda t: np.repeat(t, RET_DV, axis=-1)
    xi = per_lane(np.exp(lg[None, :] * (idx[:, None] + 1.0)))
    zeta = per_lane(np.exp(lg[None, :] * (c - 1.0 - idx)[:, None]))
    gamma = np.broadcast_to(per_lane(np.exp(lg * c)[None, :]).T, (RET_W, RET_W))
    return {k: jnp.asarray(v, F32) for k, v in
            {'decay': decay, 'xi': xi, 'zeta': zeta, 'gamma': gamma}.items()}


def _band_tables_body(rp_ref, rc_ref, prev_ref, cur_ref, past_ref, new_ref, *, tq, n_new):
    w = 2 * tq
    tp = pltpu.roll(jnp.broadcast_to(rp_ref[...], (tq, w)), 0, 1, stride=1, stride_axis=0)[:, :tq]
    tc = pltpu.roll(jnp.broadcast_to(rc_ref[...], (tq, w)), 0, 1, stride=1, stride_axis=0)[:, :tq]
    qc = lax.broadcasted_iota(jnp.int32, (tq, tq), 0) // CHUNK
    kc = lax.broadcasted_iota(jnp.int32, (tq, tq), 1) // CHUNK
    prev_ref[0] = jnp.full((tq, tq), NEG_INF, F32)
    prev_ref[1] = jnp.where(kc >= qc, tp, NEG_INF)
    cur_ref[...] = jnp.where(kc <= qc, tc, NEG_INF)
    past_ref[...] = tp[:n_new, :]
    new_ref[...] = tc[:n_new, :n_new]


def _band_tables(band_bias, *, tq, n_new):
    H = band_bias.shape[0]
    band_bias = band_bias * LOG2E
    lo, mid, hi = band_bias[:, :1], band_bias[:, MAX_REL + 1:2 * MAX_REL], band_bias[:, 2 * MAX_REL:]
    rep = lambda col, n: jnp.broadcast_to(col, (H, n))
    r_cur = jnp.concatenate([band_bias[:, MAX_REL::-1], rep(lo, tq - MAX_REL - 1), rep(hi, tq - MAX_REL),
                             band_bias[:, :MAX_REL:-1]], axis=1)
    r_prev = jnp.concatenate([rep(hi, tq - MAX_REL + 1), mid[:, ::-1], rep(hi, tq)], axis=1)
    row = pl.BlockSpec((None, 1, 2 * tq), lambda h: (h, 0, 0))
    tile = lambda n, m: pl.BlockSpec((None, n, m), lambda h: (h, 0, 0))
    return pl.pallas_call(
        functools.partial(_band_tables_body, tq=tq, n_new=n_new),
        grid=(H,),
        in_specs=[row, row],
        out_specs=[pl.BlockSpec((2, None, tq, tq), lambda h: (0, h, 0, 0)), tile(tq, tq), tile(n_new, tq),
                   tile(n_new, n_new)],
        out_shape=[jax.ShapeDtypeStruct((2, H, tq, tq), F32), jax.ShapeDtypeStruct((H, tq, tq), F32),
                   jax.ShapeDtypeStruct((H, n_new, tq), F32), jax.ShapeDtypeStruct((H, n_new, n_new), F32)],
        compiler_params=_params(("arbitrary",)),
        name="band_tables",
    )(r_prev[:, None, :], r_cur[:, None, :])


def _layer_weights(l, norm_g, w_in, ret_gn_g, mla_qa_g, mla_w_uq, mla_qn_g, mla_qr_g, mla_kva_g, mla_kr_g,
                   mla_w_ukv, mla_kn_g, band_qn_g, band_kn_g, w_out):
    cuts = np.cumsum(SEG)[:-1].tolist()
    a_q, a_k, a_v, a_g, b_cq, b_ckv, b_kr, b_g, c_q, c_k, c_v, c_g = jnp.split(w_in[l], cuts, axis=-1)
    b_kr = jnp.pad(b_kr, ((0, 0), (0, LANES - MLA_ROPE)))
    w_in_p = jnp.concatenate([a_q, a_k * (RET_DK ** -0.5), a_v, b_cq, b_ckv, b_kr, c_q, c_k, c_v, a_g, b_g, c_g],
                             axis=-1).astype(BF16)
    pad = HEAD_PAD - MLA_QK
    w_uq = jnp.pad(mla_w_uq[l].reshape(Q_LORA, MLA_HEADS, MLA_QK), ((0, 0), (0, 0), (0, pad)))
    half = MLA_ROPE // 2

    def swap_rope(t):
        return jnp.concatenate([t[..., :MLA_NOPE], t[..., MLA_NOPE + half:MLA_QK], t[..., MLA_NOPE:MLA_NOPE + half],
                                t[..., MLA_QK:]], axis=-1)
    ukv = mla_w_ukv[l].reshape(KV_LORA, MLA_HEADS, MLA_NOPE + MLA_V)
    w_uk = jnp.pad(ukv[:, :, :MLA_NOPE], ((0, 0), (0, 0), (0, HEAD_PAD - MLA_NOPE)))
    zpad = jnp.zeros((pad,), F32)
    q_gain_head = jnp.concatenate([mla_qn_g[l], mla_qr_g[l], zpad]) * (MLA_QK ** -0.5 * LOG2E)
    q_gain = jnp.tile(q_gain_head, MLA_HEADS)
    q_gain_sw = jnp.tile(swap_rope(q_gain_head), MLA_HEADS)
    k_gain = jnp.tile(jnp.concatenate([mla_kn_g[l], jnp.zeros((HEAD_PAD - MLA_NOPE,), F32)]), MLA_HEADS)
    return {
        'norm_g': norm_g[l][None], 'w_in': w_in_p,
        'w_uq': w_uq.reshape(Q_LORA, MLA_QW).astype(BF16), 'qa_g': mla_qa_g[l][None],
        'w_uq_sw': swap_rope(w_uq).reshape(Q_LORA, MLA_QW).astype(BF16), 'q_gain_sw': q_gain_sw[None],
        'w_uq_t': w_uq.reshape(Q_LORA, MLA_QW).T.astype(BF16),
        'q_gain_col': jnp.broadcast_to(q_gain[:, None], (MLA_QW, LANES)),
        'kva_g': mla_kva_g[l][None],
        'kr_g': jnp.concatenate([mla_kr_g[l], jnp.zeros((LANES - MLA_ROPE,), F32)])[None],
        'q_gain': q_gain[None],
        'bq_gain': (jnp.tile(band_qn_g[l], BAND_HEADS) * (BAND_DH ** -0.5 * LOG2E))[None],
        'bk_gain': jnp.tile(band_kn_g[l], BAND_HEADS)[None],
        'w_uk': w_uk.reshape(KV_LORA, MLA_QW).astype(BF16),
        'w_uv': ukv[:, :, MLA_NOPE:].reshape(KV_LORA, MLA_W).astype(BF16),
        'w_uv_t': jnp.pad(ukv[:, :, MLA_NOPE:].transpose(1, 2, 0), ((0, 0), (0, MLA_VROWS - MLA_V), (0, 0))
                          ).reshape(MLA_VT, KV_LORA).astype(BF16),
        'k_gain': k_gain[None], 'ret_gain': ret_gn_g[l][None], 'w_out': w_out[l].astype(BF16),
    }


def _diag_blocks(s_bd):
    B = s_bd.shape[0]
    s = s_bd.reshape(B, RET_HEADS, RET_DK, RET_HEADS, RET_DV)
    return jnp.stack([s[:, h, :, h, :] for h in range(RET_HEADS)], axis=1)


def kernel(x_prompt, x_sample, state_ret, cache_mla_ckv, cache_mla_krope, cache_band_k, cache_band_v, norm_g, w_in, ret_gn_g, mla_qa_g, mla_w_uq, mla_qn_g, mla_qr_g, mla_kva_g, mla_kr_g, mla_w_ukv, mla_kn_g, band_qn_g, band_kn_g, band_bias, w_out):
    B, S, _ = x_prompt.shape
    DB, L, _ = x_sample.shape
    past_len = cache_mla_ckv.shape[2]
    n_band_past = cache_band_k.shape[2]
    n_keep_p = min(BAND_PAST, S)

    TM_P, TM_S, TM_KV, TQ, RET_C = 512, 256, 2048, 512, 256
    consts = _constants()
    tabs_p = _rope_tables(np.arange(S))
    tabs_s = _rope_tables(past_len + np.arange(L), reps=DB)
    rt_p = _retention_tables(RET_C)
    rt_s = _retention_tables(L)

    xp = x_prompt.reshape(B * S, D_MODEL)
    xs = x_sample.reshape(DB * L, D_MODEL)
    p_st, s_st = [], []
    for l in range(DEPTH):
        lw = _layer_weights(l, norm_g, w_in, ret_gn_g, mla_qa_g, mla_w_uq, mla_qn_g, mla_qr_g, mla_kva_g,
                            mla_kr_g, mla_w_ukv, mla_kn_g, band_qn_g, band_kn_g, w_out)

        rq, rk, rv, gate, qm, ckv, kr, bq, bk, bv, bks, bvs = _k1(
            xp, lw, tabs_p, consts, tm=TM_P, rows_per_seq=S, n_keep=n_keep_p, q_transposed=True)
        km, vt = _kkv(ckv, kr, lw, tm=TM_KV, seq=S)
        mla_o = _mla_prompt(qm, km.reshape(B, S, MLA_QW), vt, tq=TQ)
        assert TQ == BAND_PAST == n_band_past
        bias_prev, bias_cur, bias_past, bias_new = _band_tables(band_bias[l], tq=TQ, n_new=L)
        band_o, ret_o, ret_s = _band_ret_prompt(
            bq.reshape(B, S, BAND_W), bk.reshape(B, S, BAND_W), bv.reshape(B, S, BAND_W), bias_prev, bias_cur,
            rq.reshape(B, S, RET_W), rk.reshape(B, S, RET_W), rv.reshape(B, S, RET_W),
            rt_p, consts, lw['ret_gain'], tq=TQ, c=RET_C)
        xp = _kout(xp, ret_o.reshape(B * S, RET_W), mla_o.reshape(B * S, MLA_W), band_o.reshape(B * S, BAND_W),
                   gate, lw['w_out'], tm=2 * TM_P)
        bks, bvs = _to_heads(bks, bvs, tm=TM_S)
        p_st.append((_diag_blocks(ret_s), ckv.reshape(B, S, KV_LORA), kr.reshape(B, S, MLA_ROPE),
                     bks.reshape(B, n_keep_p, BAND_HEADS, BAND_DH), bvs.reshape(B, n_keep_p, BAND_HEADS, BAND_DH)))

        rq, rk, rv, gate, qm, ckv, kr, bq, bk, bv, bks, bvs = _k1(
            xs, lw, tabs_s, consts, tm=TM_S, rows_per_seq=DB * L, n_keep=DB * L)
        kn, vn = _kkv(ckv, kr, lw, tm=TM_S)
        kp, vp = _kkv(cache_mla_ckv.reshape(DEPTH * DB * past_len, KV_LORA),
                      cache_mla_krope.reshape(DEPTH * DB * past_len, MLA_ROPE), lw, tm=TM_KV,
                      rows=DB * past_len, row_start=l * DB * past_len)
        mla_o, band_o, ret_o, ret_s = _sample_mixers(
            qm.reshape(DB, L, MLA_QW), kp.reshape(DB, past_len, MLA_QW), vp.reshape(DB, past_len, MLA_W),
            kn.reshape(DB, L, MLA_QW), vn.reshape(DB, L, MLA_W),
            bq.reshape(DB, L, BAND_W), cache_band_k[l].reshape(DB, n_band_past, BAND_W),
            cache_band_v[l].reshape(DB, n_band_past, BAND_W), bk.reshape(DB, L, BAND_W), bv.reshape(DB, L, BAND_W),
            bias_past, bias_new,
            rq.reshape(DB, L, RET_W), rk.reshape(DB, L, RET_W), rv.reshape(DB, L, RET_W),
            state_ret, rt_s, consts, lw['ret_gain'], layer=l)
        xs = _kout(xs, ret_o.reshape(DB * L, RET_W), mla_o.reshape(DB * L, MLA_W), band_o.reshape(DB * L, BAND_W),
                   gate, lw['w_out'], tm=TM_S)
        bks, bvs = _to_heads(bks, bvs, tm=TM_S)
        s_st.append((ret_s, ckv.reshape(DB, L, KV_LORA), kr.reshape(DB, L, MLA_ROPE),
                     bks.reshape(DB, L, BAND_HEADS, BAND_DH), bvs.reshape(DB, L, BAND_HEADS, BAND_DH)))

    stack = lambda sts, i: jnp.stack([s[i] for s in sts])
    return (xp.reshape(B, S, D_MODEL), xs.reshape(DB, L, D_MODEL),
            stack(p_st, 0), stack(p_st, 1), stack(p_st, 2), stack(p_st, 3), stack(p_st, 4),
            stack(s_st, 0), stack(s_st, 1), stack(s_st, 2), stack(s_st, 3), stack(s_st, 4))
```

```python
import functools

import jax
import jax.numpy as jnp
import numpy as np
from jax import lax
from jax.experimental import pallas as pl
from jax.experimental.pallas import tpu as pltpu

F32 = jnp.float32
BF16 = jnp.bfloat16

D_MODEL = 1024
DEPTH = 2
CHUNK = 64
EPS = 1e-6
NEG_INF = -1e30
ROPE_BASE = 10000.0
LOG2E = 1.4426950408889634
RET_HEADS, RET_DK, RET_DV = 4, 64, 64
RET_W = RET_HEADS * RET_DV
MLA_HEADS, MLA_NOPE, MLA_ROPE, MLA_V = 8, 64, 32, 64
MLA_QK = MLA_NOPE + MLA_ROPE
Q_LORA, KV_LORA = 256, 128
MLA_W = MLA_HEADS * MLA_V
BAND_HEADS, BAND_DH = 4, 64
BAND_W = BAND_HEADS * BAND_DH
BAND_PREV_CHUNKS = 8
BAND_PAST = BAND_PREV_CHUNKS * CHUNK
MAX_REL = 128
D_MIX = RET_W + MLA_W + BAND_W
SEG = (RET_HEADS * RET_DK, RET_HEADS * RET_DK, RET_W, RET_W, Q_LORA, KV_LORA, MLA_ROPE, MLA_W,
       BAND_W, BAND_W, BAND_W, BAND_W)

LANES = 128
HEAD_PAD = LANES
MLA_QW = MLA_HEADS * HEAD_PAD
MLA_VROWS = MLA_V + 16
MLA_VT = MLA_HEADS * MLA_VROWS
C_AQ, C_AK, C_AV, C_CQ, C_CKV, C_KR, C_BQ, C_BK, C_BV, C_G, C_END = (
    0, 256, 512, 768, 1024, 1152, 1280, 1536, 1792, 2048, 3072)
VMEM_LIMIT = 56 * 1024 * 1024
K1_SUB = 256


def _dot(a, b):
    return jnp.dot(a, b, preferred_element_type=F32)


def _dot_nt(a, b):
    return lax.dot_general(a, b, (((1,), (1,)), ((), ())), preferred_element_type=F32)


def _dot_tn(a, b):
    return lax.dot_general(a, b, (((0,), (0,)), ((), ())), preferred_element_type=F32)


def _group_sums(x, split, two_groups):
    lane = lax.broadcasted_iota(jnp.int32, (1, LANES), 1)
    lo = lane < split
    sums = []
    for b in range(x.shape[-1] // LANES):
        xb = x[:, b * LANES:(b + 1) * LANES]
        sq = xb * xb
        if two_groups:
            sums.append((jnp.sum(jnp.where(lo, sq, 0.0), axis=-1, keepdims=True),
                         jnp.sum(jnp.where(lo, 0.0, sq), axis=-1, keepdims=True)))
        else:
            sums.append((jnp.sum(sq, axis=-1, keepdims=True), None))
    return sums


def _group_normalize(x, sums, split, n_lo, n_hi):
    lane = lax.broadcasted_iota(jnp.int32, (1, LANES), 1)
    lo = lane < split
    outs = []
    for b, (s_lo, s_hi) in enumerate(sums):
        inv = lax.rsqrt(s_lo * (1.0 / n_lo) + EPS)
        if n_hi:
            inv = jnp.where(lo, inv, lax.rsqrt(s_hi * (1.0 / n_hi) + EPS))
        outs.append(x[:, b * LANES:(b + 1) * LANES] * inv)
    return jnp.concatenate(outs, axis=1)


def _group_rms_normed(x, split, n_lo, n_hi):
    return _group_normalize(x, _group_sums(x, split, n_hi > 0), split, n_lo, n_hi)


def _group64_inv_rms(x, bd, inv_n):
    sq = (x * x).astype(BF16)
    w = bd.shape[0]
    sums = jnp.concatenate([_dot(sq[:, c * w:(c + 1) * w], bd) for c in range(x.shape[-1] // w)], axis=1)
    return lax.rsqrt(sums * inv_n + EPS)


def _swap_halves(x, half):
    w = x.shape[-1]
    lane = lax.broadcasted_iota(jnp.int32, (1, w), 1)
    nxt = pltpu.roll(x, w - half, axis=1)
    prv = pltpu.roll(x, half, axis=1)
    return jnp.where((lane & half) == 0, nxt, prv)


def _rope(x, cos, sin_signed, half):
    return x * cos + _swap_halves(x, half) * sin_signed


def _silu(g):
    return g * (1.0 / (1.0 + jnp.exp(-g)))


def _full(shape):
    nd = len(shape)
    return pl.BlockSpec(shape, lambda *_: (0,) * nd)


def _params(sem):
    return pltpu.CompilerParams(dimension_semantics=sem, vmem_limit_bytes=VMEM_LIMIT)


def _k1_body(x_ref, ng_ref, win_ref, wuq_ref, qag_ref, kvag_ref, krg_ref, qgain_ref, bqg_ref, bkg_ref,
             cret_ref, sret_ref, cq_ref, sq_ref, ckr_ref, skr_ref, wuqr_ref, qgainr_ref, bd_ref, invnq_ref,
             rq_ref, rk_ref, rv_ref, gate_ref, qm_ref, ckv_ref, kr_ref, bq_ref, bk_ref, bv_ref,
             bks_ref, bvs_ref, *, sub, q_transposed):
    n_sub = x_ref.shape[0] // sub
    bd = bd_ref[...]
    lane_reps = sub // LANES

    def stage_a(r):
        x = x_ref[r * sub:(r + 1) * sub, :]
        h = x * lax.rsqrt(jnp.mean(x * x, axis=-1, keepdims=True) + EPS) * ng_ref[...]
        hb = h.astype(BF16)

        def seg(lo, hi):
            return _dot(hb, win_ref[:, lo:hi])

        z = {'cq': seg(C_CQ, C_CKV), 'bq': seg(C_BQ, C_BK), 'bk': seg(C_BK, C_BV)}
        cq = z['cq']
        cq = cq * lax.rsqrt(jnp.mean(cq * cq, axis=-1, keepdims=True) + EPS) * qag_ref[...]
        z['aq'], z['ak'] = seg(C_AQ, C_AK), seg(C_AK, C_AV)
        cqb = cq.astype(BF16)
        if q_transposed:
            z['qf'] = _dot_nt(wuq_ref[...], cqb)
        else:
            z['qf'] = _dot(cqb, wuq_ref[...])
            z['qf_sw'] = _dot(cqb, wuqr_ref[...])
        z['bq_inv'] = _group64_inv_rms(z['bq'], bd, 1.0 / BAND_DH)
        z['bk_inv'] = _group64_inv_rms(z['bk'], bd, 1.0 / BAND_DH)
        z['g'] = seg(C_G, C_END)
        if q_transposed:
            groups = z['qf'].reshape(2 * MLA_HEADS, HEAD_PAD // 2, sub)
            ss = jnp.sum(groups * groups, axis=1, keepdims=True)
            is_nope = lax.broadcasted_iota(jnp.int32, (2 * MLA_HEADS, 1, 1), 0) % 2 == 0
            inv = lax.rsqrt(ss * jnp.where(is_nope, 1.0 / MLA_NOPE, 1.0 / MLA_ROPE) + EPS)
            z['qn'] = (groups * inv).reshape(MLA_QW, sub)
        else:
            z['q_inv'] = _group64_inv_rms(z['qf'], bd, invnq_ref[...])
        z['av'], z['ckv'] = seg(C_AV, C_CQ), seg(C_CKV, C_KR)
        z['kr'], z['bv'] = seg(C_KR, C_BQ), seg(C_BV, C_G)
        return z

    def stage_b(r, z):
        rows = slice(r * sub, (r + 1) * sub)
        bq_ref[rows, :] = (z['bq'] * z['bq_inv'] * bqg_ref[...]).astype(BF16)
        bk = z['bk'] * z['bk_inv'] * bkg_ref[...]
        bk_ref[rows, :] = bk.astype(BF16)
        bks_ref[rows, :] = bk
        bv_ref[rows, :] = z['bv'].astype(BF16)
        bvs_ref[rows, :] = z['bv']
        cret, sret = cret_ref[rows, :], sret_ref[rows, :]
        rq_ref[rows, :] = _rope(z['aq'], cret, sret, RET_DK // 2).astype(BF16)
        rk_ref[rows, :] = _rope(z['ak'], cret, sret, RET_DK // 2).astype(BF16)
        rv_ref[rows, :] = z['av'].astype(BF16)
        gate_ref[rows, :] = _silu(z['g']).astype(BF16)
        if q_transposed:
            cq_t = jnp.concatenate([cq_ref[:, rows]] * MLA_HEADS, axis=0)
            sq_t = jnp.concatenate([sq_ref[:, rows]] * MLA_HEADS, axis=0)
            qg = z['qn'] * jnp.concatenate([qgain_ref[...]] * lane_reps, axis=1)
            half = MLA_ROPE // 2
            pieces = []
            for hd in range(MLA_HEADS):
                o = hd * HEAD_PAD
                pieces += [qg[o:o + MLA_NOPE], qg[o + MLA_NOPE + half:o + MLA_QK],
                           qg[o + MLA_NOPE:o + MLA_NOPE + half], qg[o + MLA_QK:o + HEAD_PAD]]
            qm_ref[:, rows] = (qg * cq_t + jnp.concatenate(pieces, axis=0) * sq_t).astype(BF16)
        else:
            cq_t = jnp.concatenate([cq_ref[rows, :]] * MLA_HEADS, axis=1)
            sq_t = jnp.concatenate([sq_ref[rows, :]] * MLA_HEADS, axis=1)
            qm_ref[rows, :] = (z['qf'] * (z['q_inv'] * qgain_ref[...]) * cq_t
                               + z['qf_sw'] * (z['q_inv'] * qgainr_ref[...]) * sq_t).astype(BF16)
        ckv, kr = z['ckv'], z['kr']
        ckv_ref[rows, :] = ckv * lax.rsqrt(jnp.mean(ckv * ckv, axis=-1, keepdims=True) + EPS) * kvag_ref[...]
        kr = kr * lax.rsqrt(jnp.sum(kr * kr, axis=-1, keepdims=True) * (1.0 / MLA_ROPE) + EPS) * krg_ref[...]
        kr = _rope(kr, ckr_ref[rows, :], skr_ref[rows, :], MLA_ROPE // 2)
        kr_ref[rows, :] = kr[:, :MLA_ROPE]

    z = stage_a(0)
    for r in range(n_sub):
        z_next = stage_a(r + 1) if r + 1 < n_sub else None
        stage_b(r, z)
        z = z_next


def _k1(x2d, lw, tabs, consts, *, tm, rows_per_seq, n_keep, q_transposed=False):
    T = x2d.shape[0]
    nb = rows_per_seq // tm
    nkb = n_keep // tm
    n_seq = T // rows_per_seq

    row = lambda w: pl.BlockSpec((tm, w), lambda i: (i, 0))
    tab = lambda w: pl.BlockSpec((tm, w), lambda i: (i % nb, 0))
    keep = pl.BlockSpec((tm, BAND_W), lambda i: ((i // nb) * nkb + jnp.maximum(i % nb - (nb - nkb), 0), 0))

    if q_transposed:
        tab_t = pl.BlockSpec((LANES, tm), lambda i: (0, i % nb))
        col = _full((MLA_QW, LANES))
        unused = (jnp.zeros((1, LANES), F32), _full((1, LANES)))
        q_ins = {'w_uq': (lw['w_uq_t'], _full((MLA_QW, Q_LORA))), 'w_uq_sw': unused,
                 'q_gain': (lw['q_gain_col'], col), 'q_gain_sw': unused,
                 'invn_q': unused, 'cq': (tabs['cq_t'], tab_t), 'sq': (tabs['sq_t'], tab_t)}
        q_out = ((n_seq, MLA_QW, rows_per_seq), BF16,
                 pl.BlockSpec((None, MLA_QW, tm), lambda i: (i // nb, 0, i % nb)))
    else:
        q_ins = {'w_uq': (lw['w_uq'], _full((Q_LORA, MLA_QW))), 'w_uq_sw': (lw['w_uq_sw'], _full((Q_LORA, MLA_QW))),
                 'q_gain': (lw['q_gain'], _full((1, MLA_QW))), 'q_gain_sw': (lw['q_gain_sw'], _full((1, MLA_QW))),
                 'invn_q': (consts['invn_q'], _full((1, MLA_QW))), 'cq': (tabs['cq'], tab(LANES)),
                 'sq': (tabs['sq'], tab(LANES))}
        q_out = ((T, MLA_QW), BF16, row(MLA_QW))
    ins = [
        (x2d, row(D_MODEL)), (lw['norm_g'], _full((1, D_MODEL))), (lw['w_in'], _full((D_MODEL, C_END))),
        q_ins['w_uq'], (lw['qa_g'], _full((1, Q_LORA))),
        (lw['kva_g'], _full((1, KV_LORA))), (lw['kr_g'], _full((1, LANES))),
        q_ins['q_gain'], (lw['bq_gain'], _full((1, BAND_W))),
        (lw['bk_gain'], _full((1, BAND_W))),
        (tabs['cret'], tab(RET_W)), (tabs['sret'], tab(RET_W)), q_ins['cq'],
        q_ins['sq'], (tabs['ckr'], tab(LANES)), (tabs['skr'], tab(LANES)),
        q_ins['w_uq_sw'], q_ins['q_gain_sw'],
        (consts['bd64'], _full((RET_W, RET_W))), q_ins['invn_q'],
    ]
    outs = [
        ((T, RET_W), BF16, row(RET_W)), ((T, RET_W), BF16, row(RET_W)), ((T, RET_W), BF16, row(RET_W)),
        ((T, D_MIX), BF16, row(D_MIX)), q_out,
        ((T, KV_LORA), F32, row(KV_LORA)), ((T, MLA_ROPE), F32, row(MLA_ROPE)),
        ((T, BAND_W), BF16, row(BAND_W)), ((T, BAND_W), BF16, row(BAND_W)), ((T, BAND_W), BF16, row(BAND_W)),
        ((n_seq * n_keep, BAND_W), F32, keep), ((n_seq * n_keep, BAND_W), F32, keep),
    ]
    return pl.pallas_call(
        functools.partial(_k1_body, sub=min(tm, K1_SUB), q_transposed=q_transposed),
        grid=(T // tm,),
        in_specs=[s for _, s in ins],
        out_specs=[s for _, _, s in outs],
        out_shape=[jax.ShapeDtypeStruct(sh, dt) for sh, dt, _ in outs],
        compiler_params=_params(("arbitrary",)),
        name="k1_proj",
    )(*[a for a, _ in ins])


def _to_heads_body(k_ref, v_ref, ko_ref, vo_ref):
    for hd in range(BAND_HEADS):
        lanes = slice(hd * BAND_DH, (hd + 1) * BAND_DH)
        ko_ref[:, hd, :] = k_ref[:, lanes]
        vo_ref[:, hd, :] = v_ref[:, lanes]


def _to_heads(k2d, v2d, *, tm):
    n = k2d.shape[0]
    row = pl.BlockSpec((tm, BAND_W), lambda i: (i, 0))
    head = pl.BlockSpec((tm, BAND_HEADS, BAND_DH), lambda i: (i, 0, 0))
    shape = jax.ShapeDtypeStruct((n, BAND_HEADS, BAND_DH), F32)
    return pl.pallas_call(
        _to_heads_body, grid=(n // tm,), in_specs=[row, row], out_specs=[head, head],
        out_shape=[shape, shape], compiler_params=_params(("arbitrary",)), name="band_state_heads",
    )(k2d, v2d)


def _kkv_body(ckv_ref, kr_ref, wk_ref, wv_ref, kgain_ref, *rest, transpose_v):
    if transpose_v:
        ones_ref, k_ref, v_ref = rest
    else:
        k_ref, v_ref = rest
    c = ckv_ref[...].astype(BF16)
    kn = _dot(c, wk_ref[...])
    kn = _group_rms_normed(kn, LANES, MLA_NOPE, 0) * kgain_ref[...]
    kr = kr_ref[...]
    rows = kr.shape[0]
    kr_block = jnp.concatenate([jnp.zeros((rows, MLA_NOPE), F32), kr,
                                jnp.zeros((rows, HEAD_PAD - MLA_QK), F32)], axis=1)
    k_ref[...] = (kn + jnp.concatenate([kr_block] * MLA_HEADS, axis=1)).astype(BF16)
    if transpose_v:
        v_ref[...] = (_dot_nt(wv_ref[...], c) + ones_ref[...]).astype(BF16)
    else:
        v_ref[...] = _dot(c, wv_ref[...]).astype(BF16)


def _kkv(ckv2d, kr2d, lw, *, tm, seq=None, rows=None, row_start=0):
    T = ckv2d.shape[0] if rows is None else rows
    first = row_start // tm
    row = lambda w: pl.BlockSpec((tm, w), lambda i: (i, 0))
    row_in = lambda w: pl.BlockSpec((tm, w), lambda i: (i + first, 0))
    extra_in = []
    if seq is None:
        wv, wv_spec = lw['w_uv'], _full((KV_LORA, MLA_W))
        v_spec, v_shape = row(MLA_W), (T, MLA_W)
    else:
        nb = seq // tm
        wv, wv_spec = lw['w_uv_t'], _full((MLA_VT, KV_LORA))
        v_spec = pl.BlockSpec((None, MLA_VT, tm), lambda i: (i // nb, 0, i % nb))
        v_shape = (T // seq, MLA_VT, seq)
        ones = np.zeros((MLA_HEADS, MLA_VROWS, 1), np.float32)
        ones[:, MLA_V] = 1.0
        extra_in = [(jnp.asarray(ones.reshape(MLA_VT, 1)), _full((MLA_VT, 1)))]
    return pl.pallas_call(
        functools.partial(_kkv_body, transpose_v=seq is not None),
        grid=(T // tm,),
        in_specs=[row_in(KV_LORA), row_in(MLA_ROPE), _full((KV_LORA, MLA_QW)), wv_spec, _full((1, MLA_QW))]
        + [sp for _, sp in extra_in],
        out_specs=[row(MLA_QW), v_spec],
        out_shape=[jax.ShapeDtypeStruct((T, MLA_QW), BF16), jax.ShapeDtypeStruct(v_shape, BF16)],
        compiler_params=_params(("arbitrary",)),
        name="kkv_up",
    )(ckv2d, kr2d, lw['w_uk'], wv, lw['k_gain'], *[a for a, _ in extra_in])


def _mla_prompt_body(q_ref, k_ref, vt_ref, o_ref, m_sc, acc_sc, *, tq):
    qi = pl.program_id(1)
    m_sc[...] = jnp.full(m_sc.shape, NEG_INF, F32)
    acc_sc[...] = jnp.zeros(acc_sc.shape, F32)

    def run(tiles):
        starts = [pl.multiple_of(kb * tq, tq) for kb, _ in tiles]
        stages = [(t, h) for t in range(len(tiles)) for h in range(MLA_HEADS)]
        if any(masked for _, masked in tiles):
            kc = lax.broadcasted_iota(jnp.int32, (tq, tq), 0) // CHUNK
            qc = lax.broadcasted_iota(jnp.int32, (tq, tq), 1) // CHUNK
            visible = kc <= qc

        def scores(i):
            t, h = stages[i]
            qt = q_ref[h * HEAD_PAD:(h + 1) * HEAD_PAD, :]
            k = k_ref[pl.ds(starts[t], tq), h * HEAD_PAD:(h + 1) * HEAD_PAD]
            st = _dot(k, qt)
            return jnp.where(visible, st, NEG_INF) if tiles[t][1] else st

        def softmax(i, st):
            h = stages[i][1]
            m_prev = m_sc[h:h + 1, :]
            m_new = jnp.maximum(m_prev, jnp.max(st, axis=0, keepdims=True))
            alpha = jnp.exp2(m_prev - m_new)
            p = jnp.exp2(st - m_new)
            m_sc[h:h + 1, :] = m_new
            return p.astype(BF16), alpha

        def values(i, p, alpha):
            t, h = stages[i]
            vt = vt_ref[h * MLA_VROWS:(h + 1) * MLA_VROWS, pl.ds(starts[t], tq)]
            acc_sc[h] = alpha * acc_sc[h] + _dot(vt, p)

        n = len(stages)
        st = {0: scores(0), 1: scores(1)}
        pa = {}
        for i in range(n):
            pa[i] = softmax(i, st.pop(i))
            if i >= 1:
                values(i - 1, *pa.pop(i - 1))
            if i + 2 < n:
                st[i + 2] = scores(i + 2)
        values(n - 1, *pa.pop(n - 1))

    def body(pair, carry):
        run([(2 * pair, False), (2 * pair + 1, False)])
        return carry

    lax.fori_loop(0, qi // 2, body, 0)

    @pl.when(qi % 2 == 1)
    def _():
        run([(qi - 1, False), (qi, True)])

    @pl.when(qi % 2 == 0)
    def _():
        run([(qi, True)])

    for j in range(MLA_HEADS // 2):
        pair = jnp.concatenate([acc_sc[h, :MLA_V, :] * (1.0 / acc_sc[h, MLA_V:MLA_V + 1, :])
                                for h in (2 * j, 2 * j + 1)], axis=0)
        o_ref[:, j * LANES:(j + 1) * LANES] = pair.T.astype(BF16)


def _mla_prompt(qt, k, vt, *, tq):
    B, S, _ = k.shape
    once = pl.Buffered(1)
    return pl.pallas_call(
        functools.partial(_mla_prompt_body, tq=tq),
        grid=(B, S // tq),
        in_specs=[pl.BlockSpec((None, MLA_QW, tq), lambda b, i: (b, 0, i)),
                  pl.BlockSpec((None, S, MLA_QW), lambda b, i: (b, 0, 0), pipeline_mode=once),
                  pl.BlockSpec((None, MLA_VT, S), lambda b, i: (b, 0, 0), pipeline_mode=once)],
        out_specs=pl.BlockSpec((None, tq, MLA_W), lambda b, i: (b, i, 0)),
        out_shape=jax.ShapeDtypeStruct((B, S, MLA_W), BF16),
        scratch_shapes=[pltpu.VMEM((MLA_HEADS, tq), F32), pltpu.VMEM((MLA_HEADS, MLA_VROWS, tq), F32)],
        compiler_params=_params(("arbitrary", "arbitrary")),
        name="mla_prompt",
    )(qt, k, vt)


def _mla_sample_body(q_ref, kp_ref, vp_ref, kn_ref, vn_ref, o_ref):
    lane = lax.broadcasted_iota(jnp.int32, (1, LANES), 1)
    outs = []
    scores = []
    for h in range(MLA_HEADS):
        hs = slice(h * HEAD_PAD, (h + 1) * HEAD_PAD)
        q = q_ref[:, hs]
        scores.append((_dot_nt(q, kp_ref[:, hs]), _dot_nt(q, kn_ref[:, hs])))
    for h in range(MLA_HEADS):
        ps = slice((h // 2) * LANES, (h // 2 + 1) * LANES)
        v_lanes = (lane // MLA_V) == (h % 2)
        s1, s2 = scores[h]
        m = jnp.maximum(jnp.max(s1, axis=-1, keepdims=True), jnp.max(s2, axis=-1, keepdims=True))
        p1 = jnp.exp2(s1 - m)
        p2 = jnp.exp2(s2 - m)
        l = jnp.sum(p1, axis=-1, keepdims=True) + jnp.sum(p2, axis=-1, keepdims=True)
        v1 = vp_ref[:, ps]
        v2 = vn_ref[:, ps]
        acc = (_dot(p1.astype(BF16), jnp.where(v_lanes, v1, jnp.zeros_like(v1)))
               + _dot(p2.astype(BF16), jnp.where(v_lanes, v2, jnp.zeros_like(v2))))
        outs.append(acc * (1.0 / l))
    for j in range(MLA_HEADS // 2):
        o_ref[:, j * LANES:(j + 1) * LANES] = (outs[2 * j] + outs[2 * j + 1]).astype(BF16)


def _band_scores(q, pieces):
    lane = lax.broadcasted_iota(jnp.int32, (1, BAND_W), 1)
    raw = []
    for h in range(BAND_HEADS):
        qh = jnp.where((lane // BAND_DH) == h, q, jnp.zeros_like(q))
        raw.append([_dot_nt(qh, k) for k, _ in pieces])
    return raw


def _band_head_out(h, raw_h, pieces, biases):
    lane = lax.broadcasted_iota(jnp.int32, (1, BAND_W), 1)
    sel = (lane // BAND_DH) == h
    ss = [s + bias(h) for s, bias in zip(raw_h, biases)]
    m = functools.reduce(jnp.maximum, [jnp.max(s, axis=-1, keepdims=True) for s in ss])
    ps = [jnp.exp2(s - m) for s in ss]
    l = functools.reduce(jnp.add, [jnp.sum(p, axis=-1, keepdims=True) for p in ps])
    acc = functools.reduce(jnp.add, [
        _dot(p.astype(BF16), jnp.where(sel, v, jnp.zeros_like(v))) for p, (_, v) in zip(ps, pieces)])
    return acc * (1.0 / l)


def _band_heads(q, pieces, biases, out_dtype):
    raw = _band_scores(q, pieces)
    out = functools.reduce(jnp.add, [_band_head_out(h, raw[h], pieces, biases) for h in range(BAND_HEADS)])
    return out.astype(out_dtype)


def _band_ret_prompt_body(q_ref, kp_ref, kc_ref, vp_ref, vc_ref, bp_ref, bc_ref,
                          rq_ref, rk_ref, rv_ref, d_ref, xi_ref, zeta_ref, gam_ref, bd_ref, gain_ref,
                          o_ref, ro_ref, sfin_ref, s_sc, *, c):
    @pl.when(pl.program_id(1) == 0)
    def _():
        s_sc[...] = jnp.zeros(s_sc.shape, F32)

    tq = q_ref.shape[0]
    hq = tq // 2
    assert c == hq
    halves = []
    for j, (prev_keys, cur_keys) in enumerate([(slice(0, tq), slice(0, hq)), (slice(hq, tq), slice(0, tq))]):
        rows = slice(j * hq, (j + 1) * hq)
        pieces = [(kp_ref[prev_keys, :], vp_ref[prev_keys, :]), (kc_ref[cur_keys, :], vc_ref[cur_keys, :])]
        biases = [lambda h, r=rows, k=prev_keys: bp_ref[h, r, k], lambda h, r=rows, k=cur_keys: bc_ref[h, r, k]]
        halves.append((rows, pieces, biases, _band_scores(q_ref[rows, :], pieces)))
    state = s_sc[...]
    for rows, pieces, biases, raw in halves:
        o, state = _ret_chunk(rq_ref[rows, :], rk_ref[rows, :], rv_ref[rows, :], state,
                              d_ref, xi_ref, zeta_ref, gam_ref, bd_ref, gain_ref)
        ro_ref[rows, :] = o
        out = functools.reduce(jnp.add, [_band_head_out(h, raw[h], pieces, biases) for h in range(BAND_HEADS)])
        o_ref[rows, :] = out.astype(o_ref.dtype)
    s_sc[...] = state
    sfin_ref[...] = state


def _band_ret_prompt(q, k, v, bias_prev, bias_cur, rq, rk, rv, rt, consts, gain, *, tq, c):
    B, S, _ = q.shape
    cur = pl.BlockSpec((None, tq, BAND_W), lambda b, i: (b, i, 0))
    prev = pl.BlockSpec((None, tq, BAND_W), lambda b, i: (b, jnp.maximum(i - 1, 0), 0))
    bias = _full((BAND_HEADS, tq, tq))
    bias_prev_spec = pl.BlockSpec((None, BAND_HEADS, tq, tq), lambda b, i: (jnp.minimum(i, 1), 0, 0, 0))
    st = pl.BlockSpec((None, RET_W, RET_W), lambda b, i: (b, 0, 0))
    return pl.pallas_call(
        functools.partial(_band_ret_prompt_body, c=c),
        grid=(B, S // tq),
        in_specs=[cur, prev, cur, prev, cur, bias_prev_spec, bias, cur, cur, cur,
                  _full((RET_HEADS, c, c)), _full((c, RET_W)), _full((c, RET_W)),
                  _full((RET_W, RET_W)), _full((RET_W, RET_W)), _full((1, RET_W))],
        out_specs=[cur, cur, st],
        out_shape=[jax.ShapeDtypeStruct((B, S, BAND_W), BF16), jax.ShapeDtypeStruct((B, S, RET_W), BF16),
                   jax.ShapeDtypeStruct((B, RET_W, RET_W), F32)],
        scratch_shapes=[pltpu.VMEM((RET_W, RET_W), F32)],
        compiler_params=_params(("arbitrary", "arbitrary")),
        name="band_ret_prompt",
    )(q, k, k, v, v, bias_prev, bias_cur, rq, rk, rv,
      rt['decay'], rt['xi'], rt['zeta'], rt['gamma'], consts['bd_mask'], gain)


def _band_sample_body(q_ref, kp_ref, vp_ref, kn_ref, vn_ref, bp_ref, bn_ref, o_ref):
    o_ref[...] = _band_heads(q_ref[...],
                             [(kp_ref[...].astype(BF16), vp_ref[...].astype(BF16)), (kn_ref[...], vn_ref[...])],
                             [lambda h: bp_ref[h], lambda h: bn_ref[h]], BF16)


def _ret_chunk(q, k, v, state, d_ref, xi_ref, zeta_ref, gam_ref, bd_ref, gain_ref):
    lane = lax.broadcasted_iota(jnp.int32, (1, RET_W), 1)
    o = _dot(q, state.astype(BF16)) * xi_ref[...]
    for h in range(RET_HEADS):
        sel = (lane // RET_DV) == h
        a = _dot_nt(jnp.where(sel, q, jnp.zeros_like(q)), k) * d_ref[h]
        o = o + _dot(a.astype(BF16), jnp.where(sel, v, jnp.zeros_like(v)))
    kz = (k.astype(F32) * zeta_ref[...]).astype(BF16)
    s_new = gam_ref[...] * state + bd_ref[...] * _dot_tn(kz, v)
    inv = _group64_inv_rms(o, bd_ref[...].astype(BF16), 1.0 / RET_DV)
    return (o * inv * gain_ref[...]).astype(BF16), s_new


def _sample_mixers_body(mq_ref, mkp_ref, mvp_ref, mkn_ref, mvn_ref,
                        bq_ref, bkp_ref, bvp_ref, bkn_ref, bvn_ref, bp_ref, bn_ref,
                        rq_ref, rk_ref, rv_ref, s0_ref, d_ref, xi_ref, zeta_ref, gam_ref, bd_ref, gain_ref,
                        mo_ref, bo_ref, ro_ref, sfin_ref):
    _mla_sample_body(mq_ref, mkp_ref, mvp_ref, mkn_ref, mvn_ref, mo_ref)
    _band_sample_body(bq_ref, bkp_ref, bvp_ref, bkn_ref, bvn_ref, bp_ref, bn_ref, bo_ref)
    s0 = s0_ref[...]
    state = jnp.concatenate([
        jnp.concatenate([s0[h] if g == h else jnp.zeros((RET_DK, RET_DV), F32) for g in range(RET_HEADS)], axis=1)
        for h in range(RET_HEADS)], axis=0)
    o, s_new = _ret_chunk(rq_ref[...], rk_ref[...], rv_ref[...], state,
                          d_ref, xi_ref, zeta_ref, gam_ref, bd_ref, gain_ref)
    ro_ref[...] = o
    for h in range(RET_HEADS):
        sfin_ref[h] = s_new[h * RET_DK:(h + 1) * RET_DK, h * RET_DV:(h + 1) * RET_DV]


def _sample_mixers(mq, mkp, mvp, mkn, mvn, bq, bkp, bvp, bkn, bvn, bias_past, bias_new,
                   rq, rk, rv, s0, rt, consts, gain, *, layer):
    B, L, _ = mq.shape
    P, PB = mkp.shape[1], bkp.shape[1]
    blk = lambda n, w: pl.BlockSpec((None, n, w), lambda b: (b, 0, 0))
    return pl.pallas_call(
        _sample_mixers_body,
        grid=(B,),
        in_specs=[blk(L, MLA_QW), blk(P, MLA_QW), blk(P, MLA_W), blk(L, MLA_QW), blk(L, MLA_W),
                  blk(L, BAND_W), blk(PB, BAND_W), blk(PB, BAND_W), blk(L, BAND_W), blk(L, BAND_W),
                  _full((BAND_HEADS, L, PB)), _full((BAND_HEADS, L, L)),
                  blk(L, RET_W), blk(L, RET_W), blk(L, RET_W),
                  pl.BlockSpec((None, None, RET_HEADS, RET_DK, RET_DV), lambda b: (layer, b, 0, 0, 0)),
                  _full((RET_HEADS, L, L)), _full((L, RET_W)), _full((L, RET_W)),
                  _full((RET_W, RET_W)), _full((RET_W, RET_W)), _full((1, RET_W))],
        out_specs=[blk(L, MLA_W), blk(L, BAND_W), blk(L, RET_W),
                   pl.BlockSpec((None, RET_HEADS, RET_DK, RET_DV), lambda b: (b, 0, 0, 0))],
        out_shape=[jax.ShapeDtypeStruct((B, L, MLA_W), BF16), jax.ShapeDtypeStruct((B, L, BAND_W), BF16),
                   jax.ShapeDtypeStruct((B, L, RET_W), BF16),
                   jax.ShapeDtypeStruct((B, RET_HEADS, RET_DK, RET_DV), F32)],
        compiler_params=_params(("arbitrary",)),
        name="sample_mixers",
    )(mq, mkp, mvp, mkn, mvn, bq, bkp, bvp, bkn, bvn, bias_past, bias_new,
      rq, rk, rv, s0, rt['decay'], rt['xi'], rt['zeta'], rt['gamma'], consts['bd_mask'], gain)


def _kout_body(x_ref, ro_ref, mo_ref, bo_ref, g_ref, w_ref, y_ref):
    mix = jnp.concatenate([ro_ref[...], mo_ref[...], bo_ref[...]], axis=1) * g_ref[...]
    y_ref[...] = x_ref[...] + _dot(mix, w_ref[...])


def _kout(x2d, ro, mo, bo, gate, w_out, *, tm):
    T = x2d.shape[0]
    row = lambda w: pl.BlockSpec((tm, w), lambda i: (i, 0))
    return pl.pallas_call(
        _kout_body,
        grid=(T // tm,),
        in_specs=[row(D_MODEL), row(RET_W), row(MLA_W), row(BAND_W), row(D_MIX), _full((D_MIX, D_MODEL))],
        out_specs=row(D_MODEL),
        out_shape=jax.ShapeDtypeStruct((T, D_MODEL), F32),
        compiler_params=_params(("arbitrary",)),
        name="kout_proj",
    )(x2d, ro, mo, bo, gate, w_out)


def _constants():
    bd = np.kron(np.eye(RET_HEADS, dtype=np.float32), np.ones((RET_DK, RET_DV), np.float32))
    invn_q = np.tile(np.repeat([1.0 / MLA_NOPE, 1.0 / MLA_ROPE], HEAD_PAD // 2), MLA_HEADS)[None]
    return {'bd_mask': jnp.asarray(bd), 'bd64': jnp.asarray(bd, BF16), 'invn_q': jnp.asarray(invn_q, F32)}


def _rope_tables(pos, reps=1):
    pos = np.asarray(pos, np.float64)

    def cs(half):
        inv = ROPE_BASE ** (-np.arange(half, dtype=np.float64) / half)
        ang = pos[:, None] * inv[None, :]
        c, s = np.cos(ang), np.sin(ang)
        return np.concatenate([c, c], axis=-1), np.concatenate([-s, s], axis=-1)

    T = pos.shape[0]
    c32, s32 = cs(RET_DK // 2)
    c16, s16 = cs(MLA_ROPE // 2)
    ones = lambda w: np.ones((T, w))
    zeros = lambda w: np.zeros((T, w))
    pad = HEAD_PAD - MLA_QK
    tabs = {
        'cret': np.tile(c32, (1, RET_HEADS)), 'sret': np.tile(s32, (1, RET_HEADS)),
        'cq': np.concatenate([ones(MLA_NOPE), c16, ones(pad)], axis=-1),
        'sq': np.concatenate([zeros(MLA_NOPE), s16, zeros(pad)], axis=-1),
        'ckr': np.concatenate([c16, ones(LANES - MLA_ROPE)], axis=-1),
        'skr': np.concatenate([s16, zeros(LANES - MLA_ROPE)], axis=-1),
    }
    out = {k: jnp.asarray(np.tile(v, (reps, 1)), F32) for k, v in tabs.items()}
    if reps == 1:
        out['cq_t'], out['sq_t'] = jnp.asarray(tabs['cq'].T, F32), jnp.asarray(tabs['sq'].T, F32)
    return out


def _retention_tables(c):
    lg = np.log1p(-np.exp2(-5.0 - np.arange(RET_HEADS, dtype=np.float64)))
    idx = np.arange(c, dtype=np.float64)
    diff = idx[:, None] - idx[None, :]
    decay = np.where(diff >= 0, np.exp(lg[:, None, None] * np.maximum(diff, 0.0)), 0.0)
    per_lane = lambda t: np.repeat(t, RET_DV, axis=-1)
    xi = per_lane(np.exp(lg[None, :] * (idx[:, None] + 1.0)))
    zeta = per_lane(np.exp(lg[None, :] * (c - 1.0 - idx)[:, None]))
    gamma = np.broadcast_to(per_lane(np.exp(lg * c)[None, :]).T, (RET_W, RET_W))
    return {k: jnp.asarray(v, F32) for k, v in
            {'decay': decay, 'xi': xi, 'zeta': zeta, 'gamma': gamma}.items()}


def _band_tables_body(rp_ref, rc_ref, prev_ref, cur_ref, past_ref, new_ref, *, tq, n_new):
    w = 2 * tq
    tp = pltpu.roll(jnp.broadcast_to(rp_ref[...], (tq, w)), 0, 1, stride=1, stride_axis=0)[:, :tq]
    tc = pltpu.roll(jnp.broadcast_to(rc_ref[...], (tq, w)), 0, 1, stride=1, stride_axis=0)[:, :tq]
    qc = lax.broadcasted_iota(jnp.int32, (tq, tq), 0) // CHUNK
    kc = lax.broadcasted_iota(jnp.int32, (tq, tq), 1) // CHUNK
    prev_ref[0] = jnp.full((tq, tq), NEG_INF, F32)
    prev_ref[1] = jnp.where(kc >= qc, tp, NEG_INF)
    cur_ref[...] = jnp.where(kc <= qc, tc, NEG_INF)
    past_ref[...] = tp[:n_new, :]
    new_ref[...] = tc[:n_new, :n_new]


def _band_tables(band_bias, *, tq, n_new):
    H = band_bias.shape[0]
    band_bias = band_bias * LOG2E
    lo, mid, hi = band_bias[:, :1], band_bias[:, MAX_REL + 1:2 * MAX_REL], band_bias[:, 2 * MAX_REL:]
    rep = lambda col, n: jnp.broadcast_to(col, (H, n))
    r_cur = jnp.concatenate([band_bias[:, MAX_REL::-1], rep(lo, tq - MAX_REL - 1), rep(hi, tq - MAX_REL),
                             band_bias[:, :MAX_REL:-1]], axis=1)
    r_prev = jnp.concatenate([rep(hi, tq - MAX_REL + 1), mid[:, ::-1], rep(hi, tq)], axis=1)
    row = pl.BlockSpec((None, 1, 2 * tq), lambda h: (h, 0, 0))
    tile = lambda n, m: pl.BlockSpec((None, n, m), lambda h: (h, 0, 0))
    return pl.pallas_call(
        functools.partial(_band_tables_body, tq=tq, n_new=n_new),
        grid=(H,),
        in_specs=[row, row],
        out_specs=[pl.BlockSpec((2, None, tq, tq), lambda h: (0, h, 0, 0)), tile(tq, tq), tile(n_new, tq),
                   tile(n_new, n_new)],
        out_shape=[jax.ShapeDtypeStruct((2, H, tq, tq), F32), jax.ShapeDtypeStruct((H, tq, tq), F32),
                   jax.ShapeDtypeStruct((H, n_new, tq), F32), jax.ShapeDtypeStruct((H, n_new, n_new), F32)],
        compiler_params=_params(("arbitrary",)),
        name="band_tables",
    )(r_prev[:, None, :], r_cur[:, None, :])


def _layer_weights(l, norm_g, w_in, ret_gn_g, mla_qa_g, mla_w_uq, mla_qn_g, mla_qr_g, mla_kva_g, mla_kr_g,
                   mla_w_ukv, mla_kn_g, band_qn_g, band_kn_g, w_out):
    cuts = np.cumsum(SEG)[:-1].tolist()
    a_q, a_k, a_v, a_g, b_cq, b_ckv, b_kr, b_g, c_q, c_k, c_v, c_g = jnp.split(w_in[l], cuts, axis=-1)
    b_kr = jnp.pad(b_kr, ((0, 0), (0, LANES - MLA_ROPE)))
    w_in_p = jnp.concatenate([a_q, a_k * (RET_DK ** -0.5), a_v, b_cq, b_ckv, b_kr, c_q, c_k, c_v, a_g, b_g, c_g],
                             axis=-1).astype(BF16)
    pad = HEAD_PAD - MLA_QK
    w_uq = jnp.pad(mla_w_uq[l].reshape(Q_LORA, MLA_HEADS, MLA_QK), ((0, 0), (0, 0), (0, pad)))
    half = MLA_ROPE // 2

    def swap_rope(t):
        return jnp.concatenate([t[..., :MLA_NOPE], t[..., MLA_NOPE + half:MLA_QK], t[..., MLA_NOPE:MLA_NOPE + half],
                                t[..., MLA_QK:]], axis=-1)
    ukv = mla_w_ukv[l].reshape(KV_LORA, MLA_HEADS, MLA_NOPE + MLA_V)
    w_uk = jnp.pad(ukv[:, :, :MLA_NOPE], ((0, 0), (0, 0), (0, HEAD_PAD - MLA_NOPE)))
    zpad = jnp.zeros((pad,), F32)
    q_gain_head = jnp.concatenate([mla_qn_g[l], mla_qr_g[l], zpad]) * (MLA_QK ** -0.5 * LOG2E)
    q_gain = jnp.tile(q_gain_head, MLA_HEADS)
    q_gain_sw = jnp.tile(swap_rope(q_gain_head), MLA_HEADS)
    k_gain = jnp.tile(jnp.concatenate([mla_kn_g[l], jnp.zeros((HEAD_PAD - MLA_NOPE,), F32)]), MLA_HEADS)
    return {
        'norm_g': norm_g[l][None], 'w_in': w_in_p,
        'w_uq': w_uq.reshape(Q_LORA, MLA_QW).astype(BF16), 'qa_g': mla_qa_g[l][None],
        'w_uq_sw': swap_rope(w_uq).reshape(Q_LORA, MLA_QW).astype(BF16), 'q_gain_sw': q_gain_sw[None],
        'w_uq_t': w_uq.reshape(Q_LORA, MLA_QW).T.astype(BF16),
        'q_gain_col': jnp.broadcast_to(q_gain[:, None], (MLA_QW, LANES)),
        'kva_g': mla_kva_g[l][None],
        'kr_g': jnp.concatenate([mla_kr_g[l], jnp.zeros((LANES - MLA_ROPE,), F32)])[None],
        'q_gain': q_gain[None],
        'bq_gain': (jnp.tile(band_qn_g[l], BAND_HEADS) * (BAND_DH ** -0.5 * LOG2E))[None],
        'bk_gain': jnp.tile(band_kn_g[l], BAND_HEADS)[None],
        'w_uk': w_uk.reshape(KV_LORA, MLA_QW).astype(BF16),
        'w_uv': ukv[:, :, MLA_NOPE:].reshape(KV_LORA, MLA_W).astype(BF16),
        'w_uv_t': jnp.pad(ukv[:, :, MLA_NOPE:].transpose(1, 2, 0), ((0, 0), (0, MLA_VROWS - MLA_V), (0, 0))
                          ).reshape(MLA_VT, KV_LORA).astype(BF16),
        'k_gain': k_gain[None], 'ret_gain': ret_gn_g[l][None], 'w_out': w_out[l].astype(BF16),
    }


def _diag_blocks(s_bd):
    B = s_bd.shape[0]
    s = s_bd.reshape(B, RET_HEADS, RET_DK, RET_HEADS, RET_DV)
    return jnp.stack([s[:, h, :, h, :] for h in range(RET_HEADS)], axis=1)


def kernel(x_prompt, x_sample, state_ret, cache_mla_ckv, cache_mla_krope, cache_band_k, cache_band_v, norm_g, w_in, ret_gn_g, mla_qa_g, mla_w_uq, mla_qn_g, mla_qr_g, mla_kva_g, mla_kr_g, mla_w_ukv, mla_kn_g, band_qn_g, band_kn_g, band_bias, w_out):
    B, S, _ = x_prompt.shape
    DB, L, _ = x_sample.shape
    past_len = cache_mla_ckv.shape[2]
    n_band_past = cache_band_k.shape[2]
    n_keep_p = min(BAND_PAST, S)

    TM_P, TM_S, TM_KV, TQ, RET_C = 512, 512, 2048, 512, 256
    consts = _constants()
    tabs_p = _rope_tables(np.arange(S))
    tabs_s = _rope_tables(past_len + np.arange(L), reps=DB)
    rt_p = _retention_tables(RET_C)
    rt_s = _retention_tables(L)

    xp = x_prompt.reshape(B * S, D_MODEL)
    xs = x_sample.reshape(DB * L, D_MODEL)
    p_st, s_st = [], []
    for l in range(DEPTH):
        lw = _layer_weights(l, norm_g, w_in, ret_gn_g, mla_qa_g, mla_w_uq, mla_qn_g, mla_qr_g, mla_kva_g,
                            mla_kr_g, mla_w_ukv, mla_kn_g, band_qn_g, band_kn_g, w_out)

        rq, rk, rv, gate, qm, ckv, kr, bq, bk, bv, bks, bvs = _k1(
            xp, lw, tabs_p, consts, tm=TM_P, rows_per_seq=S, n_keep=n_keep_p, q_transposed=True)
        km, vt = _kkv(ckv, kr, lw, tm=TM_KV, seq=S)
        mla_o = _mla_prompt(qm, km.reshape(B, S, MLA_QW), vt, tq=TQ)
        assert TQ == BAND_PAST == n_band_past
        bias_prev, bias_cur, bias_past, bias_new = _band_tables(band_bias[l], tq=TQ, n_new=L)
        band_o, ret_o, ret_s = _band_ret_prompt(
            bq.reshape(B, S, BAND_W), bk.reshape(B, S, BAND_W), bv.reshape(B, S, BAND_W), bias_prev, bias_cur,
            rq.reshape(B, S, RET_W), rk.reshape(B, S, RET_W), rv.reshape(B, S, RET_W),
            rt_p, consts, lw['ret_gain'], tq=TQ, c=RET_C)
        xp = _kout(xp, ret_o.reshape(B * S, RET_W), mla_o.reshape(B * S, MLA_W), band_o.reshape(B * S, BAND_W),
                   gate, lw['w_out'], tm=2 * TM_P)
        bks, bvs = _to_heads(bks, bvs, tm=TM_S)
        p_st.append((_diag_blocks(ret_s), ckv.reshape(B, S, KV_LORA), kr.reshape(B, S, MLA_ROPE),
                     bks.reshape(B, n_keep_p, BAND_HEADS, BAND_DH), bvs.reshape(B, n_keep_p, BAND_HEADS, BAND_DH)))

        rq, rk, rv, gate, qm, ckv, kr, bq, bk, bv, bks, bvs = _k1(
            xs, lw, tabs_s, consts, tm=TM_S, rows_per_seq=DB * L, n_keep=DB * L)
        kn, vn = _kkv(ckv, kr, lw, tm=TM_S)
        kp, vp = _kkv(cache_mla_ckv.reshape(DEPTH * DB * past_len, KV_LORA),
                      cache_mla_krope.reshape(DEPTH * DB * past_len, MLA_ROPE), lw, tm=TM_KV,
                      rows=DB * past_len, row_start=l * DB * past_len)
        mla_o, band_o, ret_o, ret_s = _sample_mixers(
            qm.reshape(DB, L, MLA_QW), kp.reshape(DB, past_len, MLA_QW), vp.reshape(DB, past_len, MLA_W),
            kn.reshape(DB, L, MLA_QW), vn.reshape(DB, L, MLA_W),
            bq.reshape(DB, L, BAND_W), cache_band_k[l].reshape(DB, n_band_past, BAND_W),
            cache_band_v[l].reshape(DB, n_band_past, BAND_W), bk.reshape(DB, L, BAND_W), bv.reshape(DB, L, BAND_W),
            bias_past, bias_new,
            rq.reshape(DB, L, RET_W), rk.reshape(DB, L, RET_W), rv.reshape(DB, L, RET_W),
            state_ret, rt_s, consts, lw['ret_gain'], layer=l)
        xs = _kout(xs, ret_o.reshape(DB * L, RET_W), mla_o.reshape(DB * L, MLA_W), band_o.reshape(DB * L, BAND_W),
                   gate, lw['w_out'], tm=TM_S)
        bks, bvs = _to_heads(bks, bvs, tm=TM_S)
        s_st.append((ret_s, ckv.reshape(DB, L, KV_LORA), kr.reshape(DB, L, MLA_ROPE),
                     bks.reshape(DB, L, BAND_HEADS, BAND_DH), bvs.reshape(DB, L, BAND_HEADS, BAND_DH)))

    stack = lambda sts, i: jnp.stack([s[i] for s in sts])
    return (xp.reshape(B, S, D_MODEL), xs.reshape(DB, L, D_MODEL),
            stack(p_st, 0), stack(p_st, 1), stack(p_st, 2), stack(p_st, 3), stack(p_st, 4),
            stack(s_st, 0), stack(s_st, 1), stack(s_st, 2), stack(s_st, 3), stack(s_st, 4))
```

```python
import functools

import jax
import jax.numpy as jnp
import numpy as np
from jax import lax
from jax.experimental import pallas as pl
from jax.experimental.pallas import tpu as pltpu

F32 = jnp.float32
BF16 = jnp.bfloat16

D_MODEL = 1024
DEPTH = 2
CHUNK = 64
EPS = 1e-6
NEG_INF = -1e30
ROPE_BASE = 10000.0
LOG2E = 1.4426950408889634
RET_HEADS, RET_DK, RET_DV = 4, 64, 64
RET_W = RET_HEADS * RET_DV
MLA_HEADS, MLA_NOPE, MLA_ROPE, MLA_V = 8, 64, 32, 64
MLA_QK = MLA_NOPE + MLA_ROPE
Q_LORA, KV_LORA = 256, 128
MLA_W = MLA_HEADS * MLA_V
BAND_HEADS, BAND_DH = 4, 64
BAND_W = BAND_HEADS * BAND_DH
BAND_PREV_CHUNKS = 8
BAND_PAST = BAND_PREV_CHUNKS * CHUNK
MAX_REL = 128
D_MIX = RET_W + MLA_W + BAND_W
SEG = (RET_HEADS * RET_DK, RET_HEADS * RET_DK, RET_W, RET_W, Q_LORA, KV_LORA, MLA_ROPE, MLA_W,
       BAND_W, BAND_W, BAND_W, BAND_W)

LANES = 128
HEAD_PAD = LANES
MLA_QW = MLA_HEADS * HEAD_PAD
MLA_VROWS = MLA_V + 16
MLA_VT = MLA_HEADS * MLA_VROWS
C_AQ, C_AK, C_AV, C_CQ, C_CKV, C_KR, C_BQ, C_BK, C_BV, C_G, C_END = (
    0, 256, 512, 768, 1024, 1152, 1280, 1536, 1792, 2048, 3072)
VMEM_LIMIT = 56 * 1024 * 1024
K1_SUB = 256


def _dot(a, b):
    return jnp.dot(a, b, preferred_element_type=F32)


def _dot_nt(a, b):
    return lax.dot_general(a, b, (((1,), (1,)), ((), ())), preferred_element_type=F32)


def _dot_tn(a, b):
    return lax.dot_general(a, b, (((0,), (0,)), ((), ())), preferred_element_type=F32)


def _group_sums(x, split, two_groups):
    lane = lax.broadcasted_iota(jnp.int32, (1, LANES), 1)
    lo = lane < split
    sums = []
    for b in range(x.shape[-1] // LANES):
        xb = x[:, b * LANES:(b + 1) * LANES]
        sq = xb * xb
        if two_groups:
            sums.append((jnp.sum(jnp.where(lo, sq, 0.0), axis=-1, keepdims=True),
                         jnp.sum(jnp.where(lo, 0.0, sq), axis=-1, keepdims=True)))
        else:
            sums.append((jnp.sum(sq, axis=-1, keepdims=True), None))
    return sums


def _group_normalize(x, sums, split, n_lo, n_hi):
    lane = lax.broadcasted_iota(jnp.int32, (1, LANES), 1)
    lo = lane < split
    outs = []
    for b, (s_lo, s_hi) in enumerate(sums):
        inv = lax.rsqrt(s_lo * (1.0 / n_lo) + EPS)
        if n_hi:
            inv = jnp.where(lo, inv, lax.rsqrt(s_hi * (1.0 / n_hi) + EPS))
        outs.append(x[:, b * LANES:(b + 1) * LANES] * inv)
    return jnp.concatenate(outs, axis=1)


def _group_rms_normed(x, split, n_lo, n_hi):
    return _group_normalize(x, _group_sums(x, split, n_hi > 0), split, n_lo, n_hi)


def _group64_inv_rms(x, bd, inv_n):
    sq = (x * x).astype(BF16)
    w = bd.shape[0]
    sums = jnp.concatenate([_dot(sq[:, c * w:(c + 1) * w], bd) for c in range(x.shape[-1] // w)], axis=1)
    return lax.rsqrt(sums * inv_n + EPS)


def _swap_halves(x, half):
    w = x.shape[-1]
    lane = lax.broadcasted_iota(jnp.int32, (1, w), 1)
    nxt = pltpu.roll(x, w - half, axis=1)
    prv = pltpu.roll(x, half, axis=1)
    return jnp.where((lane & half) == 0, nxt, prv)


def _rope(x, cos, sin_signed, half):
    return x * cos + _swap_halves(x, half) * sin_signed


def _silu(g):
    return g * (1.0 / (1.0 + jnp.exp(-g)))


def _full(shape):
    nd = len(shape)
    return pl.BlockSpec(shape, lambda *_: (0,) * nd)


def _params(sem):
    return pltpu.CompilerParams(dimension_semantics=sem, vmem_limit_bytes=VMEM_LIMIT)


def _k1_body(x_ref, ng_ref, win_ref, wuq_ref, qag_ref, kvag_ref, krg_ref, qgain_ref, bqg_ref, bkg_ref,
             cret_ref, sret_ref, cq_ref, sq_ref, ckr_ref, skr_ref, wuqr_ref, qgainr_ref, bd_ref, invnq_ref,
             rq_ref, rk_ref, rv_ref, gate_ref, qm_ref, ckv_ref, kr_ref, bq_ref, bk_ref, bv_ref,
             bks_ref, bvs_ref, *, sub, q_transposed):
    n_sub = x_ref.shape[0] // sub
    bd = bd_ref[...]
    lane_reps = sub // LANES

    def stage_a(r):
        x = x_ref[r * sub:(r + 1) * sub, :]
        h = x * lax.rsqrt(jnp.mean(x * x, axis=-1, keepdims=True) + EPS) * ng_ref[...]
        hb = h.astype(BF16)

        def seg(lo, hi):
            return _dot(hb, win_ref[:, lo:hi])

        z = {'cq': seg(C_CQ, C_CKV), 'bq': seg(C_BQ, C_BK), 'bk': seg(C_BK, C_BV)}
        cq = z['cq']
        cq = cq * lax.rsqrt(jnp.mean(cq * cq, axis=-1, keepdims=True) + EPS) * qag_ref[...]
        z['aq'], z['ak'] = seg(C_AQ, C_AK), seg(C_AK, C_AV)
        cqb = cq.astype(BF16)
        if q_transposed:
            z['qf'] = _dot_nt(wuq_ref[...], cqb)
        else:
            z['qf'] = _dot(cqb, wuq_ref[...])
            z['qf_sw'] = _dot(cqb, wuqr_ref[...])
        z['bq_inv'] = _group64_inv_rms(z['bq'], bd, 1.0 / BAND_DH)
        z['bk_inv'] = _group64_inv_rms(z['bk'], bd, 1.0 / BAND_DH)
        z['g'] = seg(C_G, C_END)
        if q_transposed:
            groups = z['qf'].reshape(2 * MLA_HEADS, HEAD_PAD // 2, sub)
            ss = jnp.sum(groups * groups, axis=1, keepdims=True)
            is_nope = lax.broadcasted_iota(jnp.int32, (2 * MLA_HEADS, 1, 1), 0) % 2 == 0
            inv = lax.rsqrt(ss * jnp.where(is_nope, 1.0 / MLA_NOPE, 1.0 / MLA_ROPE) + EPS)
            z['qn'] = (groups * inv).reshape(MLA_QW, sub)
        else:
            z['q_inv'] = _group64_inv_rms(z['qf'], bd, invnq_ref[...])
        z['av'], z['ckv'] = seg(C_AV, C_CQ), seg(C_CKV, C_KR)
        z['kr'], z['bv'] = seg(C_KR, C_BQ), seg(C_BV, C_G)
        return z

    def stage_b(r, z):
        rows = slice(r * sub, (r + 1) * sub)
        bq_ref[rows, :] = (z['bq'] * z['bq_inv'] * bqg_ref[...]).astype(BF16)
        bk = z['bk'] * z['bk_inv'] * bkg_ref[...]
        bk_ref[rows, :] = bk.astype(BF16)
        bks_ref[rows, :] = bk
        bv_ref[rows, :] = z['bv'].astype(BF16)
        bvs_ref[rows, :] = z['bv']
        cret, sret = cret_ref[rows, :], sret_ref[rows, :]
        rq_ref[rows, :] = _rope(z['aq'], cret, sret, RET_DK // 2).astype(BF16)
        rk_ref[rows, :] = _rope(z['ak'], cret, sret, RET_DK // 2).astype(BF16)
        rv_ref[rows, :] = z['av'].astype(BF16)
        gate_ref[rows, :] = _silu(z['g']).astype(BF16)
        if q_transposed:
            cq_t = jnp.concatenate([cq_ref[:, rows]] * MLA_HEADS, axis=0)
            sq_t = jnp.concatenate([sq_ref[:, rows]] * MLA_HEADS, axis=0)
            qg = z['qn'] * jnp.concatenate([qgain_ref[...]] * lane_reps, axis=1)
            half = MLA_ROPE // 2
            pieces = []
            for hd in range(MLA_HEADS):
                o = hd * HEAD_PAD
                pieces += [qg[o:o + MLA_NOPE], qg[o + MLA_NOPE + half:o + MLA_QK],
                           qg[o + MLA_NOPE:o + MLA_NOPE + half], qg[o + MLA_QK:o + HEAD_PAD]]
            qm_ref[:, rows] = (qg * cq_t + jnp.concatenate(pieces, axis=0) * sq_t).astype(BF16)
        else:
            cq_t = jnp.concatenate([cq_ref[rows, :]] * MLA_HEADS, axis=1)
            sq_t = jnp.concatenate([sq_ref[rows, :]] * MLA_HEADS, axis=1)
            qm_ref[rows, :] = (z['qf'] * (z['q_inv'] * qgain_ref[...]) * cq_t
                               + z['qf_sw'] * (z['q_inv'] * qgainr_ref[...]) * sq_t).astype(BF16)
        ckv, kr = z['ckv'], z['kr']
        ckv_ref[rows, :] = ckv * lax.rsqrt(jnp.mean(ckv * ckv, axis=-1, keepdims=True) + EPS) * kvag_ref[...]
        kr = kr * lax.rsqrt(jnp.sum(kr * kr, axis=-1, keepdims=True) * (1.0 / MLA_ROPE) + EPS) * krg_ref[...]
        kr = _rope(kr, ckr_ref[rows, :], skr_ref[rows, :], MLA_ROPE // 2)
        kr_ref[rows, :] = kr[:, :MLA_ROPE]

    z = stage_a(0)
    for r in range(n_sub):
        z_next = stage_a(r + 1) if r + 1 < n_sub else None
        stage_b(r, z)
        z = z_next


def _k1(x2d, lw, tabs, consts, *, tm, rows_per_seq, n_keep, q_transposed=False):
    T = x2d.shape[0]
    nb = rows_per_seq // tm
    nkb = n_keep // tm
    n_seq = T // rows_per_seq

    row = lambda w: pl.BlockSpec((tm, w), lambda i: (i, 0))
    tab = lambda w: pl.BlockSpec((tm, w), lambda i: (i % nb, 0))
    keep = pl.BlockSpec((tm, BAND_W), lambda i: ((i // nb) * nkb + jnp.maximum(i % nb - (nb - nkb), 0), 0))

    if q_transposed:
        tab_t = pl.BlockSpec((LANES, tm), lambda i: (0, i % nb))
        col = _full((MLA_QW, LANES))
        unused = (jnp.zeros((1, LANES), F32), _full((1, LANES)))
        q_ins = {'w_uq': (lw['w_uq_t'], _full((MLA_QW, Q_LORA))), 'w_uq_sw': unused,
                 'q_gain': (lw['q_gain_col'], col), 'q_gain_sw': unused,
                 'invn_q': unused, 'cq': (tabs['cq_t'], tab_t), 'sq': (tabs['sq_t'], tab_t)}
        q_out = ((n_seq, MLA_QW, rows_per_seq), BF16,
                 pl.BlockSpec((None, MLA_QW, tm), lambda i: (i // nb, 0, i % nb)))
    else:
        q_ins = {'w_uq': (lw['w_uq'], _full((Q_LORA, MLA_QW))), 'w_uq_sw': (lw['w_uq_sw'], _full((Q_LORA, MLA_QW))),
                 'q_gain': (lw['q_gain'], _full((1, MLA_QW))), 'q_gain_sw': (lw['q_gain_sw'], _full((1, MLA_QW))),
                 'invn_q': (consts['invn_q'], _full((1, MLA_QW))), 'cq': (tabs['cq'], tab(LANES)),
                 'sq': (tabs['sq'], tab(LANES))}
        q_out = ((T, MLA_QW), BF16, row(MLA_QW))
    ins = [
        (x2d, row(D_MODEL)), (lw['norm_g'], _full((1, D_MODEL))), (lw['w_in'], _full((D_MODEL, C_END))),
        q_ins['w_uq'], (lw['qa_g'], _full((1, Q_LORA))),
        (lw['kva_g'], _full((1, KV_LORA))), (lw['kr_g'], _full((1, LANES))),
        q_ins['q_gain'], (lw['bq_gain'], _full((1, BAND_W))),
        (lw['bk_gain'], _full((1, BAND_W))),
        (tabs['cret'], tab(RET_W)), (tabs['sret'], tab(RET_W)), q_ins['cq'],
        q_ins['sq'], (tabs['ckr'], tab(LANES)), (tabs['skr'], tab(LANES)),
        q_ins['w_uq_sw'], q_ins['q_gain_sw'],
        (consts['bd64'], _full((RET_W, RET_W))), q_ins['invn_q'],
    ]
    outs = [
        ((T, RET_W), BF16, row(RET_W)), ((T, RET_W), BF16, row(RET_W)), ((T, RET_W), BF16, row(RET_W)),
        ((T, D_MIX), BF16, row(D_MIX)), q_out,
        ((T, KV_LORA), F32, row(KV_LORA)), ((T, MLA_ROPE), F32, row(MLA_ROPE)),
        ((T, BAND_W), BF16, row(BAND_W)), ((T, BAND_W), BF16, row(BAND_W)), ((T, BAND_W), BF16, row(BAND_W)),
        ((n_seq * n_keep, BAND_W), F32, keep), ((n_seq * n_keep, BAND_W), F32, keep),
    ]
    return pl.pallas_call(
        functools.partial(_k1_body, sub=min(tm, K1_SUB), q_transposed=q_transposed),
        grid=(T // tm,),
        in_specs=[s for _, s in ins],
        out_specs=[s for _, _, s in outs],
        out_shape=[jax.ShapeDtypeStruct(sh, dt) for sh, dt, _ in outs],
        compiler_params=_params(("arbitrary",)),
        name="k1_proj",
    )(*[a for a, _ in ins])


def _to_heads_body(k_ref, v_ref, ko_ref, vo_ref):
    for hd in range(BAND_HEADS):
        lanes = slice(hd * BAND_DH, (hd + 1) * BAND_DH)
        ko_ref[:, hd, :] = k_ref[:, lanes]
        vo_ref[:, hd, :] = v_ref[:, lanes]


def _to_heads(k2d, v2d, *, tm):
    n = k2d.shape[0]
    row = pl.BlockSpec((tm, BAND_W), lambda i: (i, 0))
    head = pl.BlockSpec((tm, BAND_HEADS, BAND_DH), lambda i: (i, 0, 0))
    shape = jax.ShapeDtypeStruct((n, BAND_HEADS, BAND_DH), F32)
    return pl.pallas_call(
        _to_heads_body, grid=(n // tm,), in_specs=[row, row], out_specs=[head, head],
        out_shape=[shape, shape], compiler_params=_params(("arbitrary",)), name="band_state_heads",
    )(k2d, v2d)


def _kkv_body(ckv_ref, kr_ref, wk_ref, wv_ref, kgain_ref, *rest, transpose_v):
    if transpose_v:
        ones_ref, k_ref, v_ref = rest
    else:
        k_ref, v_ref = rest
    c = ckv_ref[...].astype(BF16)
    kn = _dot(c, wk_ref[...])
    kn = _group_rms_normed(kn, LANES, MLA_NOPE, 0) * kgain_ref[...]
    kr = kr_ref[...]
    rows = kr.shape[0]
    kr_block = jnp.concatenate([jnp.zeros((rows, MLA_NOPE), F32), kr,
                                jnp.zeros((rows, HEAD_PAD - MLA_QK), F32)], axis=1)
    k_ref[...] = (kn + jnp.concatenate([kr_block] * MLA_HEADS, axis=1)).astype(BF16)
    if transpose_v:
        v_ref[...] = (_dot_nt(wv_ref[...], c) + ones_ref[...]).astype(BF16)
    else:
        v_ref[...] = _dot(c, wv_ref[...]).astype(BF16)


def _kkv(ckv2d, kr2d, lw, *, tm, seq=None, rows=None, row_start=0):
    T = ckv2d.shape[0] if rows is None else rows
    first = row_start // tm
    row = lambda w: pl.BlockSpec((tm, w), lambda i: (i, 0))
    row_in = lambda w: pl.BlockSpec((tm, w), lambda i: (i + first, 0))
    extra_in = []
    if seq is None:
        wv, wv_spec = lw['w_uv'], _full((KV_LORA, MLA_W))
        v_spec, v_shape = row(MLA_W), (T, MLA_W)
    else:
        nb = seq // tm
        wv, wv_spec = lw['w_uv_t'], _full((MLA_VT, KV_LORA))
        v_spec = pl.BlockSpec((None, MLA_VT, tm), lambda i: (i // nb, 0, i % nb))
        v_shape = (T // seq, MLA_VT, seq)
        ones = np.zeros((MLA_HEADS, MLA_VROWS, 1), np.float32)
        ones[:, MLA_V] = 1.0
        extra_in = [(jnp.asarray(ones.reshape(MLA_VT, 1)), _full((MLA_VT, 1)))]
    return pl.pallas_call(
        functools.partial(_kkv_body, transpose_v=seq is not None),
        grid=(T // tm,),
        in_specs=[row_in(KV_LORA), row_in(MLA_ROPE), _full((KV_LORA, MLA_QW)), wv_spec, _full((1, MLA_QW))]
        + [sp for _, sp in extra_in],
        out_specs=[row(MLA_QW), v_spec],
        out_shape=[jax.ShapeDtypeStruct((T, MLA_QW), BF16), jax.ShapeDtypeStruct(v_shape, BF16)],
        compiler_params=_params(("arbitrary",)),
        name="kkv_up",
    )(ckv2d, kr2d, lw['w_uk'], wv, lw['k_gain'], *[a for a, _ in extra_in])


def _mla_prompt_body(q_ref, k_ref, vt_ref, o_ref, m_sc, acc_sc, *, tq):
    qi = pl.program_id(1)
    m_sc[...] = jnp.full(m_sc.shape, NEG_INF, F32)
    acc_sc[...] = jnp.zeros(acc_sc.shape, F32)

    def run(tiles):
        starts = [pl.multiple_of(kb * tq, tq) for kb, _ in tiles]
        stages = [(t, h) for t in range(len(tiles)) for h in range(MLA_HEADS)]
        if any(masked for _, masked in tiles):
            kc = lax.broadcasted_iota(jnp.int32, (tq, tq), 0) // CHUNK
            qc = lax.broadcasted_iota(jnp.int32, (tq, tq), 1) // CHUNK
            visible = kc <= qc

        def scores(i):
            t, h = stages[i]
            qt = q_ref[h * HEAD_PAD:(h + 1) * HEAD_PAD, :]
            k = k_ref[pl.ds(starts[t], tq), h * HEAD_PAD:(h + 1) * HEAD_PAD]
            st = _dot(k, qt)
            return jnp.where(visible, st, NEG_INF) if tiles[t][1] else st

        def softmax(i, st):
            h = stages[i][1]
            m_prev = m_sc[h:h + 1, :]
            m_new = jnp.maximum(m_prev, jnp.max(st, axis=0, keepdims=True))
            alpha = jnp.exp2(m_prev - m_new)
            p = jnp.exp2(st - m_new)
            m_sc[h:h + 1, :] = m_new
            return p.astype(BF16), alpha

        def values(i, p, alpha):
            t, h = stages[i]
            vt = vt_ref[h * MLA_VROWS:(h + 1) * MLA_VROWS, pl.ds(starts[t], tq)]
            acc_sc[h] = alpha * acc_sc[h] + _dot(vt, p)

        n = len(stages)
        st = {0: scores(0), 1: scores(1)}
        pa = {}
        for i in range(n):
            pa[i] = softmax(i, st.pop(i))
            if i >= 1:
                values(i - 1, *pa.pop(i - 1))
            if i + 2 < n:
                st[i + 2] = scores(i + 2)
        values(n - 1, *pa.pop(n - 1))

    def body(pair, carry):
        run([(2 * pair, False), (2 * pair + 1, False)])
        return carry

    lax.fori_loop(0, qi // 2, body, 0)

    @pl.when(qi % 2 == 1)
    def _():
        run([(qi - 1, False), (qi, True)])

    @pl.when(qi % 2 == 0)
    def _():
        run([(qi, True)])

    for j in range(MLA_HEADS // 2):
        pair = jnp.concatenate([acc_sc[h, :MLA_V, :] * (1.0 / acc_sc[h, MLA_V:MLA_V + 1, :])
                                for h in (2 * j, 2 * j + 1)], axis=0)
        o_ref[:, j * LANES:(j + 1) * LANES] = pair.T.astype(BF16)


def _mla_prompt(qt, k, vt, *, tq):
    B, S, _ = k.shape
    once = pl.Buffered(1)
    return pl.pallas_call(
        functools.partial(_mla_prompt_body, tq=tq),
        grid=(B, S // tq),
        in_specs=[pl.BlockSpec((None, MLA_QW, tq), lambda b, i: (b, 0, i)),
                  pl.BlockSpec((None, S, MLA_QW), lambda b, i: (b, 0, 0), pipeline_mode=once),
                  pl.BlockSpec((None, MLA_VT, S), lambda b, i: (b, 0, 0), pipeline_mode=once)],
        out_specs=pl.BlockSpec((None, tq, MLA_W), lambda b, i: (b, i, 0)),
        out_shape=jax.ShapeDtypeStruct((B, S, MLA_W), BF16),
        scratch_shapes=[pltpu.VMEM((MLA_HEADS, tq), F32), pltpu.VMEM((MLA_HEADS, MLA_VROWS, tq), F32)],
        compiler_params=_params(("arbitrary", "arbitrary")),
        name="mla_prompt",
    )(qt, k, vt)


def _mla_sample_body(q_ref, kp_ref, vp_ref, kn_ref, vn_ref, o_ref):
    lane = lax.broadcasted_iota(jnp.int32, (1, LANES), 1)
    outs = []
    scores = []
    for h in range(MLA_HEADS):
        hs = slice(h * HEAD_PAD, (h + 1) * HEAD_PAD)
        q = q_ref[:, hs]
        scores.append((_dot_nt(q, kp_ref[:, hs]), _dot_nt(q, kn_ref[:, hs])))
    for h in range(MLA_HEADS):
        ps = slice((h // 2) * LANES, (h // 2 + 1) * LANES)
        v_lanes = (lane // MLA_V) == (h % 2)
        s1, s2 = scores[h]
        m = jnp.maximum(jnp.max(s1, axis=-1, keepdims=True), jnp.max(s2, axis=-1, keepdims=True))
        p1 = jnp.exp2(s1 - m)
        p2 = jnp.exp2(s2 - m)
        l = jnp.sum(p1, axis=-1, keepdims=True) + jnp.sum(p2, axis=-1, keepdims=True)
        v1 = vp_ref[:, ps]
        v2 = vn_ref[:, ps]
        acc = (_dot(p1.astype(BF16), jnp.where(v_lanes, v1, jnp.zeros_like(v1)))
               + _dot(p2.astype(BF16), jnp.where(v_lanes, v2, jnp.zeros_like(v2))))
        outs.append(acc * (1.0 / l))
    for j in range(MLA_HEADS // 2):
        o_ref[:, j * LANES:(j + 1) * LANES] = (outs[2 * j] + outs[2 * j + 1]).astype(BF16)


def _band_scores(q, pieces):
    lane = lax.broadcasted_iota(jnp.int32, (1, BAND_W), 1)
    raw = []
    for h in range(BAND_HEADS):
        qh = jnp.where((lane // BAND_DH) == h, q, jnp.zeros_like(q))
        raw.append([_dot_nt(qh, k) for k, _ in pieces])
    return raw


def _band_head_out(h, raw_h, pieces, biases):
    lane = lax.broadcasted_iota(jnp.int32, (1, BAND_W), 1)
    sel = (lane // BAND_DH) == h
    ss = [s + bias(h) for s, bias in zip(raw_h, biases)]
    m = functools.reduce(jnp.maximum, [jnp.max(s, axis=-1, keepdims=True) for s in ss])
    ps = [jnp.exp2(s - m) for s in ss]
    l = functools.reduce(jnp.add, [jnp.sum(p, axis=-1, keepdims=True) for p in ps])
    acc = functools.reduce(jnp.add, [
        _dot(p.astype(BF16), jnp.where(sel, v, jnp.zeros_like(v))) for p, (_, v) in zip(ps, pieces)])
    return acc * (1.0 / l)


def _band_heads(q, pieces, biases, out_dtype):
    raw = _band_scores(q, pieces)
    out = functools.reduce(jnp.add, [_band_head_out(h, raw[h], pieces, biases) for h in range(BAND_HEADS)])
    return out.astype(out_dtype)


def _band_ret_prompt_body(q_ref, kp_ref, kc_ref, vp_ref, vc_ref, bp_ref, bc_ref,
                          rq_ref, rk_ref, rv_ref, d_ref, xi_ref, zeta_ref, gam_ref, bd_ref, gain_ref,
                          o_ref, ro_ref, sfin_ref, s_sc, *, c):
    @pl.when(pl.program_id(1) == 0)
    def _():
        s_sc[...] = jnp.zeros(s_sc.shape, F32)

    tq = q_ref.shape[0]
    hq = tq // 2
    assert c == hq
    halves = []
    for j, (prev_keys, cur_keys) in enumerate([(slice(0, tq), slice(0, hq)), (slice(hq, tq), slice(0, tq))]):
        rows = slice(j * hq, (j + 1) * hq)
        pieces = [(kp_ref[prev_keys, :], vp_ref[prev_keys, :]), (kc_ref[cur_keys, :], vc_ref[cur_keys, :])]
        biases = [lambda h, r=rows, k=prev_keys: bp_ref[h, r, k], lambda h, r=rows, k=cur_keys: bc_ref[h, r, k]]
        halves.append((rows, pieces, biases, _band_scores(q_ref[rows, :], pieces)))
    state = s_sc[...]
    for rows, pieces, biases, raw in halves:
        o, state = _ret_chunk(rq_ref[rows, :], rk_ref[rows, :], rv_ref[rows, :], state,
                              d_ref, xi_ref, zeta_ref, gam_ref, bd_ref, gain_ref)
        ro_ref[rows, :] = o
        out = functools.reduce(jnp.add, [_band_head_out(h, raw[h], pieces, biases) for h in range(BAND_HEADS)])
        o_ref[rows, :] = out.astype(o_ref.dtype)
    s_sc[...] = state
    sfin_ref[...] = state


def _band_ret_prompt(q, k, v, bias_prev, bias_cur, rq, rk, rv, rt, consts, gain, *, tq, c):
    B, S, _ = q.shape
    cur = pl.BlockSpec((None, tq, BAND_W), lambda b, i: (b, i, 0))
    prev = pl.BlockSpec((None, tq, BAND_W), lambda b, i: (b, jnp.maximum(i - 1, 0), 0))
    bias = _full((BAND_HEADS, tq, tq))
    bias_prev_spec = pl.BlockSpec((None, BAND_HEADS, tq, tq), lambda b, i: (jnp.minimum(i, 1), 0, 0, 0))
    st = pl.BlockSpec((None, RET_W, RET_W), lambda b, i: (b, 0, 0))
    return pl.pallas_call(
        functools.partial(_band_ret_prompt_body, c=c),
        grid=(B, S // tq),
        in_specs=[cur, prev, cur, prev, cur, bias_prev_spec, bias, cur, cur, cur,
                  _full((RET_HEADS, c, c)), _full((c, RET_W)), _full((c, RET_W)),
                  _full((RET_W, RET_W)), _full((RET_W, RET_W)), _full((1, RET_W))],
        out_specs=[cur, cur, st],
        out_shape=[jax.ShapeDtypeStruct((B, S, BAND_W), BF16), jax.ShapeDtypeStruct((B, S, RET_W), BF16),
                   jax.ShapeDtypeStruct((B, RET_W, RET_W), F32)],
        scratch_shapes=[pltpu.VMEM((RET_W, RET_W), F32)],
        compiler_params=_params(("arbitrary", "arbitrary")),
        name="band_ret_prompt",
    )(q, k, k, v, v, bias_prev, bias_cur, rq, rk, rv,
      rt['decay'], rt['xi'], rt['zeta'], rt['gamma'], consts['bd_mask'], gain)


def _band_sample_body(q_ref, kp_ref, vp_ref, kn_ref, vn_ref, bp_ref, bn_ref, o_ref):
    o_ref[...] = _band_heads(q_ref[...],
                             [(kp_ref[...].astype(BF16), vp_ref[...].astype(BF16)), (kn_ref[...], vn_ref[...])],
                             [lambda h: bp_ref[h], lambda h: bn_ref[h]], BF16)


def _ret_chunk(q, k, v, state, d_ref, xi_ref, zeta_ref, gam_ref, bd_ref, gain_ref):
    lane = lax.broadcasted_iota(jnp.int32, (1, RET_W), 1)
    o = _dot(q, state.astype(BF16)) * xi_ref[...]
    for h in range(RET_HEADS):
        sel = (lane // RET_DV) == h
        a = _dot_nt(jnp.where(sel, q, jnp.zeros_like(q)), k) * d_ref[h]
        o = o + _dot(a.astype(BF16), jnp.where(sel, v, jnp.zeros_like(v)))
    kz = (k.astype(F32) * zeta_ref[...]).astype(BF16)
    s_new = gam_ref[...] * state + bd_ref[...] * _dot_tn(kz, v)
    inv = _group64_inv_rms(o, bd_ref[...].astype(BF16), 1.0 / RET_DV)
    return (o * inv * gain_ref[...]).astype(BF16), s_new


def _sample_mixers_body(mq_ref, mkp_ref, mvp_ref, mkn_ref, mvn_ref,
                        bq_ref, bkp_ref, bvp_ref, bkn_ref, bvn_ref, bp_ref, bn_ref,
                        rq_ref, rk_ref, rv_ref, s0_ref, d_ref, xi_ref, zeta_ref, gam_ref, bd_ref, gain_ref,
                        mo_ref, bo_ref, ro_ref, sfin_ref):
    _mla_sample_body(mq_ref, mkp_ref, mvp_ref, mkn_ref, mvn_ref, mo_ref)
    _band_sample_body(bq_ref, bkp_ref, bvp_ref, bkn_ref, bvn_ref, bp_ref, bn_ref, bo_ref)
    s0 = s0_ref[...]
    state = jnp.concatenate([
        jnp.concatenate([s0[h] if g == h else jnp.zeros((RET_DK, RET_DV), F32) for g in range(RET_HEADS)], axis=1)
        for h in range(RET_HEADS)], axis=0)
    o, s_new = _ret_chunk(rq_ref[...], rk_ref[...], rv_ref[...], state,
                          d_ref, xi_ref, zeta_ref, gam_ref, bd_ref, gain_ref)
    ro_ref[...] = o
    for h in range(RET_HEADS):
        sfin_ref[h] = s_new[h * RET_DK:(h + 1) * RET_DK, h * RET_DV:(h + 1) * RET_DV]


def _sample_mixers(mq, mkp, mvp, mkn, mvn, bq, bkp, bvp, bkn, bvn, bias_past, bias_new,
                   rq, rk, rv, s0, rt, consts, gain, *, layer):
    B, L, _ = mq.shape
    P, PB = mkp.shape[1], bkp.shape[1]
    blk = lambda n, w: pl.BlockSpec((None, n, w), lambda b: (b, 0, 0))
    return pl.pallas_call(
        _sample_mixers_body,
        grid=(B,),
        in_specs=[blk(L, MLA_QW), blk(P, MLA_QW), blk(P, MLA_W), blk(L, MLA_QW), blk(L, MLA_W),
                  blk(L, BAND_W), blk(PB, BAND_W), blk(PB, BAND_W), blk(L, BAND_W), blk(L, BAND_W),
                  _full((BAND_HEADS, L, PB)), _full((BAND_HEADS, L, L)),
                  blk(L, RET_W), blk(L, RET_W), blk(L, RET_W),
                  pl.BlockSpec((None, None, RET_HEADS, RET_DK, RET_DV), lambda b: (layer, b, 0, 0, 0)),
                  _full((RET_HEADS, L, L)), _full((L, RET_W)), _full((L, RET_W)),
                  _full((RET_W, RET_W)), _full((RET_W, RET_W)), _full((1, RET_W))],
        out_specs=[blk(L, MLA_W), blk(L, BAND_W), blk(L, RET_W),
                   pl.BlockSpec((None, RET_HEADS, RET_DK, RET_DV), lambda b: (b, 0, 0, 0))],
        out_shape=[jax.ShapeDtypeStruct((B, L, MLA_W), BF16), jax.ShapeDtypeStruct((B, L, BAND_W), BF16),
                   jax.ShapeDtypeStruct((B, L, RET_W), BF16),
                   jax.ShapeDtypeStruct((B, RET_HEADS, RET_DK, RET_DV), F32)],
        compiler_params=pltpu.CompilerParams(
            dimension_semantics=("arbitrary",), vmem_limit_bytes=VMEM_LIMIT,
            allow_input_fusion=[i in (6, 7) for i in range(22)]),
        name="sample_mixers",
    )(mq, mkp, mvp, mkn, mvn, bq, bkp, bvp, bkn, bvn, bias_past, bias_new,
      rq, rk, rv, s0, rt['decay'], rt['xi'], rt['zeta'], rt['gamma'], consts['bd_mask'], gain)


def _kout_body(x_ref, ro_ref, mo_ref, bo_ref, g_ref, w_ref, y_ref):
    mix = jnp.concatenate([ro_ref[...], mo_ref[...], bo_ref[...]], axis=1) * g_ref[...]
    y_ref[...] = x_ref[...] + _dot(mix, w_ref[...])


def _kout(x2d, ro, mo, bo, gate, w_out, *, tm):
    T = x2d.shape[0]
    row = lambda w: pl.BlockSpec((tm, w), lambda i: (i, 0))
    return pl.pallas_call(
        _kout_body,
        grid=(T // tm,),
        in_specs=[row(D_MODEL), row(RET_W), row(MLA_W), row(BAND_W), row(D_MIX), _full((D_MIX, D_MODEL))],
        out_specs=row(D_MODEL),
        out_shape=jax.ShapeDtypeStruct((T, D_MODEL), F32),
        compiler_params=_params(("arbitrary",)),
        name="kout_proj",
    )(x2d, ro, mo, bo, gate, w_out)


def _constants():
    bd = np.kron(np.eye(RET_HEADS, dtype=np.float32), np.ones((RET_DK, RET_DV), np.float32))
    invn_q = np.tile(np.repeat([1.0 / MLA_NOPE, 1.0 / MLA_ROPE], HEAD_PAD // 2), MLA_HEADS)[None]
    return {'bd_mask': jnp.asarray(bd), 'bd64': jnp.asarray(bd, BF16), 'invn_q': jnp.asarray(invn_q, F32)}


def _rope_tables(pos, reps=1):
    pos = np.asarray(pos, np.float64)

    def cs(half):
        inv = ROPE_BASE ** (-np.arange(half, dtype=np.float64) / half)
        ang = pos[:, None] * inv[None, :]
        c, s = np.cos(ang), np.sin(ang)
        return np.concatenate([c, c], axis=-1), np.concatenate([-s, s], axis=-1)

    T = pos.shape[0]
    c32, s32 = cs(RET_DK // 2)
    c16, s16 = cs(MLA_ROPE // 2)
    ones = lambda w: np.ones((T, w))
    zeros = lambda w: np.zeros((T, w))
    pad = HEAD_PAD - MLA_QK
    tabs = {
        'cret': np.tile(c32, (1, RET_HEADS)), 'sret': np.tile(s32, (1, RET_HEADS)),
        'cq': np.concatenate([ones(MLA_NOPE), c16, ones(pad)], axis=-1),
        'sq': np.concatenate([zeros(MLA_NOPE), s16, zeros(pad)], axis=-1),
        'ckr': np.concatenate([c16, ones(LANES - MLA_ROPE)], axis=-1),
        'skr': np.concatenate([s16, zeros(LANES - MLA_ROPE)], axis=-1),
    }
    out = {k: jnp.asarray(np.tile(v, (reps, 1)), F32) for k, v in tabs.items()}
    if reps == 1:
        out['cq_t'], out['sq_t'] = jnp.asarray(tabs['cq'].T, F32), jnp.asarray(tabs['sq'].T, F32)
    return out


def _retention_tables(c):
    lg = np.log1p(-np.exp2(-5.0 - np.arange(RET_HEADS, dtype=np.float64)))
    idx = np.arange(c, dtype=np.float64)
    diff = idx[:, None] - idx[None, :]
    decay = np.where(diff >= 0, np.exp(lg[:, None, None] * np.maximum(diff, 0.0)), 0.0)
    per_lane = lambda t: np.repeat(t, RET_DV, axis=-1)
    xi = per_lane(np.exp(lg[None, :] * (idx[:, None] + 1.0)))
    zeta = per_lane(np.exp(lg[None, :] * (c - 1.0 - idx)[:, None]))
    gamma = np.broadcast_to(per_lane(np.exp(lg * c)[None, :]).T, (RET_W, RET_W))
    return {k: jnp.asarray(v, F32) for k, v in
            {'decay': decay, 'xi': xi, 'zeta': zeta, 'gamma': gamma}.items()}


def _band_tables_body(rp_ref, rc_ref, prev_ref, cur_ref, past_ref, new_ref, *, tq, n_new):
    w = 2 * tq
    tp = pltpu.roll(jnp.broadcast_to(rp_ref[...], (tq, w)), 0, 1, stride=1, stride_axis=0)[:, :tq]
    tc = pltpu.roll(jnp.broadcast_to(rc_ref[...], (tq, w)), 0, 1, stride=1, stride_axis=0)[:, :tq]
    qc = lax.broadcasted_iota(jnp.int32, (tq, tq), 0) // CHUNK
    kc = lax.broadcasted_iota(jnp.int32, (tq, tq), 1) // CHUNK
    prev_ref[0] = jnp.full((tq, tq), NEG_INF, F32)
    prev_ref[1] = jnp.where(kc >= qc, tp, NEG_INF)
    cur_ref[...] = jnp.where(kc <= qc, tc, NEG_INF)
    past_ref[...] = tp[:n_new, :]
    new_ref[...] = tc[:n_new, :n_new]


def _band_tables(band_bias, *, tq, n_new):
    H = band_bias.shape[0]
    band_bias = band_bias * LOG2E
    lo, mid, hi = band_bias[:, :1], band_bias[:, MAX_REL + 1:2 * MAX_REL], band_bias[:, 2 * MAX_REL:]
    rep = lambda col, n: jnp.broadcast_to(col, (H, n))
    r_cur = jnp.concatenate([band_bias[:, MAX_REL::-1], rep(lo, tq - MAX_REL - 1), rep(hi, tq - MAX_REL),
                             band_bias[:, :MAX_REL:-1]], axis=1)
    r_prev = jnp.concatenate([rep(hi, tq - MAX_REL + 1), mid[:, ::-1], rep(hi, tq)], axis=1)
    row = pl.BlockSpec((None, 1, 2 * tq), lambda h: (h, 0, 0))
    tile = lambda n, m: pl.BlockSpec((None, n, m), lambda h: (h, 0, 0))
    return pl.pallas_call(
        functools.partial(_band_tables_body, tq=tq, n_new=n_new),
        grid=(H,),
        in_specs=[row, row],
        out_specs=[pl.BlockSpec((2, None, tq, tq), lambda h: (0, h, 0, 0)), tile(tq, tq), tile(n_new, tq),
                   tile(n_new, n_new)],
        out_shape=[jax.ShapeDtypeStruct((2, H, tq, tq), F32), jax.ShapeDtypeStruct((H, tq, tq), F32),
                   jax.ShapeDtypeStruct((H, n_new, tq), F32), jax.ShapeDtypeStruct((H, n_new, n_new), F32)],
        compiler_params=_params(("arbitrary",)),
        name="band_tables",
    )(r_prev[:, None, :], r_cur[:, None, :])


def _layer_weights(l, norm_g, w_in, ret_gn_g, mla_qa_g, mla_w_uq, mla_qn_g, mla_qr_g, mla_kva_g, mla_kr_g,
                   mla_w_ukv, mla_kn_g, band_qn_g, band_kn_g, w_out):
    cuts = np.cumsum(SEG)[:-1].tolist()
    a_q, a_k, a_v, a_g, b_cq, b_ckv, b_kr, b_g, c_q, c_k, c_v, c_g = jnp.split(w_in[l], cuts, axis=-1)
    b_kr = jnp.pad(b_kr, ((0, 0), (0, LANES - MLA_ROPE)))
    w_in_p = jnp.concatenate([a_q, a_k * (RET_DK ** -0.5), a_v, b_cq, b_ckv, b_kr, c_q, c_k, c_v, a_g, b_g, c_g],
                             axis=-1).astype(BF16)
    pad = HEAD_PAD - MLA_QK
    w_uq = jnp.pad(mla_w_uq[l].reshape(Q_LORA, MLA_HEADS, MLA_QK), ((0, 0), (0, 0), (0, pad)))
    half = MLA_ROPE // 2

    def swap_rope(t):
        return jnp.concatenate([t[..., :MLA_NOPE], t[..., MLA_NOPE + half:MLA_QK], t[..., MLA_NOPE:MLA_NOPE + half],
                                t[..., MLA_QK:]], axis=-1)
    ukv = mla_w_ukv[l].reshape(KV_LORA, MLA_HEADS, MLA_NOPE + MLA_V)
    w_uk = jnp.pad(ukv[:, :, :MLA_NOPE], ((0, 0), (0, 0), (0, HEAD_PAD - MLA_NOPE)))
    zpad = jnp.zeros((pad,), F32)
    q_gain_head = jnp.concatenate([mla_qn_g[l], mla_qr_g[l], zpad]) * (MLA_QK ** -0.5 * LOG2E)
    q_gain = jnp.tile(q_gain_head, MLA_HEADS)
    q_gain_sw = jnp.tile(swap_rope(q_gain_head), MLA_HEADS)
    k_gain = jnp.tile(jnp.concatenate([mla_kn_g[l], jnp.zeros((HEAD_PAD - MLA_NOPE,), F32)]), MLA_HEADS)
    return {
        'norm_g': norm_g[l][None], 'w_in': w_in_p,
        'w_uq': w_uq.reshape(Q_LORA, MLA_QW).astype(BF16), 'qa_g': mla_qa_g[l][None],
        'w_uq_sw': swap_rope(w_uq).reshape(Q_LORA, MLA_QW).astype(BF16), 'q_gain_sw': q_gain_sw[None],
        'w_uq_t': w_uq.reshape(Q_LORA, MLA_QW).T.astype(BF16),
        'q_gain_col': jnp.broadcast_to(q_gain[:, None], (MLA_QW, LANES)),
        'kva_g': mla_kva_g[l][None],
        'kr_g': jnp.concatenate([mla_kr_g[l], jnp.zeros((LANES - MLA_ROPE,), F32)])[None],
        'q_gain': q_gain[None],
        'bq_gain': (jnp.tile(band_qn_g[l], BAND_HEADS) * (BAND_DH ** -0.5 * LOG2E))[None],
        'bk_gain': jnp.tile(band_kn_g[l], BAND_HEADS)[None],
        'w_uk': w_uk.reshape(KV_LORA, MLA_QW).astype(BF16),
        'w_uv': ukv[:, :, MLA_NOPE:].reshape(KV_LORA, MLA_W).astype(BF16),
        'w_uv_t': jnp.pad(ukv[:, :, MLA_NOPE:].transpose(1, 2, 0), ((0, 0), (0, MLA_VROWS - MLA_V), (0, 0))
                          ).reshape(MLA_VT, KV_LORA).astype(BF16),
        'k_gain': k_gain[None], 'ret_gain': ret_gn_g[l][None], 'w_out': w_out[l].astype(BF16),
    }


def _diag_blocks(s_bd):
    B = s_bd.shape[0]
    s = s_bd.reshape(B, RET_HEADS, RET_DK, RET_HEADS, RET_DV)
    return jnp.stack([s[:, h, :, h, :] for h in range(RET_HEADS)], axis=1)


def kernel(x_prompt, x_sample, state_ret, cache_mla_ckv, cache_mla_krope, cache_band_k, cache_band_v, norm_g, w_in, ret_gn_g, mla_qa_g, mla_w_uq, mla_qn_g, mla_qr_g, mla_kva_g, mla_kr_g, mla_w_ukv, mla_kn_g, band_qn_g, band_kn_g, band_bias, w_out):
    B, S, _ = x_prompt.shape
    DB, L, _ = x_sample.shape
    past_len = cache_mla_ckv.shape[2]
    n_band_past = cache_band_k.shape[2]
    n_keep_p = min(BAND_PAST, S)

    TM_P, TM_S, TM_KV, TQ, RET_C = 512, 512, 2048, 512, 256
    consts = _constants()
    tabs_p = _rope_tables(np.arange(S))
    tabs_s = _rope_tables(past_len + np.arange(L), reps=DB)
    rt_p = _retention_tables(RET_C)
    rt_s = _retention_tables(L)

    xp = x_prompt.reshape(B * S, D_MODEL)
    xs = x_sample.reshape(DB * L, D_MODEL)
    p_st, s_st = [], []
    for l in range(DEPTH):
        lw = _layer_weights(l, norm_g, w_in, ret_gn_g, mla_qa_g, mla_w_uq, mla_qn_g, mla_qr_g, mla_kva_g,
                            mla_kr_g, mla_w_ukv, mla_kn_g, band_qn_g, band_kn_g, w_out)

        rq, rk, rv, gate, qm, ckv, kr, bq, bk, bv, bks, bvs = _k1(
            xp, lw, tabs_p, consts, tm=TM_P, rows_per_seq=S, n_keep=n_keep_p, q_transposed=True)
        km, vt = _kkv(ckv, kr, lw, tm=TM_KV, seq=S)
        mla_o = _mla_prompt(qm, km.reshape(B, S, MLA_QW), vt, tq=TQ)
        assert TQ == BAND_PAST == n_band_past
        bias_prev, bias_cur, bias_past, bias_new = _band_tables(band_bias[l], tq=TQ, n_new=L)
        band_o, ret_o, ret_s = _band_ret_prompt(
            bq.reshape(B, S, BAND_W), bk.reshape(B, S, BAND_W), bv.reshape(B, S, BAND_W), bias_prev, bias_cur,
            rq.reshape(B, S, RET_W), rk.reshape(B, S, RET_W), rv.reshape(B, S, RET_W),
            rt_p, consts, lw['ret_gain'], tq=TQ, c=RET_C)
        xp = _kout(xp, ret_o.reshape(B * S, RET_W), mla_o.reshape(B * S, MLA_W), band_o.reshape(B * S, BAND_W),
                   gate, lw['w_out'], tm=2 * TM_P)
        bks, bvs = _to_heads(bks, bvs, tm=TM_S)
        p_st.append((_diag_blocks(ret_s), ckv.reshape(B, S, KV_LORA), kr.reshape(B, S, MLA_ROPE),
                     bks.reshape(B, n_keep_p, BAND_HEADS, BAND_DH), bvs.reshape(B, n_keep_p, BAND_HEADS, BAND_DH)))

        rq, rk, rv, gate, qm, ckv, kr, bq, bk, bv, bks, bvs = _k1(
            xs, lw, tabs_s, consts, tm=TM_S, rows_per_seq=DB * L, n_keep=DB * L)
        kn, vn = _kkv(ckv, kr, lw, tm=TM_S)
        kp, vp = _kkv(cache_mla_ckv.reshape(DEPTH * DB * past_len, KV_LORA),
                      cache_mla_krope.reshape(DEPTH * DB * past_len, MLA_ROPE), lw, tm=TM_KV,
                      rows=DB * past_len, row_start=l * DB * past_len)
        mla_o, band_o, ret_o, ret_s = _sample_mixers(
            qm.reshape(DB, L, MLA_QW), kp.reshape(DB, past_len, MLA_QW), vp.reshape(DB, past_len, MLA_W),
            kn.reshape(DB, L, MLA_QW), vn.reshape(DB, L, MLA_W),
            bq.reshape(DB, L, BAND_W), cache_band_k[l].reshape(DB, n_band_past, BAND_W),
            cache_band_v[l].reshape(DB, n_band_past, BAND_W), bk.reshape(DB, L, BAND_W), bv.reshape(DB, L, BAND_W),
            bias_past, bias_new,
            rq.reshape(DB, L, RET_W), rk.reshape(DB, L, RET_W), rv.reshape(DB, L, RET_W),
            state_ret, rt_s, consts, lw['ret_gain'], layer=l)
        xs = _kout(xs, ret_o.reshape(DB * L, RET_W), mla_o.reshape(DB * L, MLA_W), band_o.reshape(DB * L, BAND_W),
                   gate, lw['w_out'], tm=TM_S)
        bks, bvs = _to_heads(bks, bvs, tm=TM_S)
        s_st.append((ret_s, ckv.reshape(DB, L, KV_LORA), kr.reshape(DB, L, MLA_ROPE),
                     bks.reshape(DB, L, BAND_HEADS, BAND_DH), bvs.reshape(DB, L, BAND_HEADS, BAND_DH)))

    stack = lambda sts, i: jnp.stack([s[i] for s in sts])
    return (xp.reshape(B, S, D_MODEL), xs.reshape(DB, L, D_MODEL),
            stack(p_st, 0), stack(p_st, 1), stack(p_st, 2), stack(p_st, 3), stack(p_st, 4),
            stack(s_st, 0), stack(s_st, 1), stack(s_st, 2), stack(s_st, 3), stack(s_st, 4))
```
